```python
import math
import jax, jax.numpy as jnp
from jax import lax
import numpy as np

D_MODEL = 1024
BATCH = 8
SEQ = 4096
DEPTH = 2

N_A_LAYERS = DEPTH // 2
N_B_LAYERS = DEPTH - N_A_LAYERS
A_HEADS = 8
A_HEAD_DIM = 128
A_WIDTH = A_HEADS * A_HEAD_DIM
A_CONV = 4
A_CHUNK = 64
B_HEADS = 16
B_GROUPS = 2
B_HPG = B_HEADS // B_GROUPS
B_HEAD_DIM = 64
B_WIDTH = B_HEADS * B_HEAD_DIM
N_BRANCH = 3
L_CMP = 32
CMP_STRIDE = 16
CMP_HIDDEN = 256
L_SLC = 64
N_SEL = 16
WINDOW = 512
Q_BLOCK = 64
NUM_BUCKETS = 32
MAX_DISTANCE = 128

EPS = 1e-6
NEG_INF = -1e30
SEL_BOOST = 1e9

kernel_name = "yoco_deltanet_nsa_hybrid"


def rmsnorm(x, g):
    xf = x.astype(jnp.float32)
    y = xf * lax.rsqrt(jnp.mean(xf * xf, axis=-1, keepdims=True) + EPS)
    return (y * g.astype(jnp.float32)).astype(x.dtype)


def l2norm(x):
    xf = x.astype(jnp.float32)
    return xf * lax.rsqrt(jnp.sum(xf * xf, axis=-1, keepdims=True) + EPS)


def masked_softmax(logits, mask):
    logits = jnp.where(mask, logits.astype(jnp.float32), NEG_INF)
    m = jnp.max(logits, axis=-1, keepdims=True)
    e = jnp.exp(logits - m) * mask
    return e / jnp.maximum(jnp.sum(e, axis=-1, keepdims=True), 1e-30)


def ada_modulation(c, w, b):
    mod = jax.nn.silu(c) @ w + b
    shift, scale, gate = jnp.split(mod, 3, axis=-1)
    return shift[:, None], scale[:, None], gate[:, None]


def t5_bucket(dist):
    n = jnp.maximum(dist, 0)
    max_exact = NUM_BUCKETS // 2
    nf = jnp.maximum(n, 1).astype(jnp.float32)
    large = max_exact + (jnp.log(nf / max_exact) / math.log(MAX_DISTANCE / max_exact)
                         * (NUM_BUCKETS - max_exact)).astype(jnp.int32)
    large = jnp.minimum(large, NUM_BUCKETS - 1)
    return jnp.where(n < max_exact, n, large)


def causal_depthwise_conv(x, w):
    K = w.shape[0]
    T = x.shape[1]
    xp = jnp.pad(x, ((0, 0), (K - 1, 0), (0, 0)))
    y = xp[:, 0:T] * w[0]
    for k in range(1, K):
        y = y + xp[:, k:k + T] * w[k]
    return y


def chunk_gated_delta_rule(q, k, v, g, beta):
    Bn, T, H, dk = q.shape
    dv = v.shape[-1]
    C = A_CHUNK
    N = T // C

    def chunks(t):
        return t.reshape(Bn, N, C, H, -1).transpose(0, 3, 1, 2, 4)

    q, k, v = chunks(q), chunks(k), chunks(v)
    g = g.reshape(Bn, N, C, H).transpose(0, 3, 1, 2)
    beta = beta.reshape(Bn, N, C, H).transpose(0, 3, 1, 2)
    gc = jnp.cumsum(g, axis=-1)
    incl = np.tril(np.ones((C, C), dtype=bool))
    strict = np.tril(np.ones((C, C), dtype=bool), -1)
    diff = gc[..., :, None] - gc[..., None, :]
    decay = jnp.where(incl, jnp.exp(jnp.where(incl, diff, 0.0)), 0.0)
    kb = k * beta[..., None]
    m = jnp.where(strict, jnp.einsum('bhnid,bhnjd->bhnij', kb, k) * decay, 0.0)
    a_mat = m + jnp.eye(C, dtype=m.dtype)
    rhs = jnp.concatenate([v * beta[..., None], kb * jnp.exp(gc)[..., None]], axis=-1)
    uw = lax.linalg.triangular_solve(a_mat, rhs, left_side=True, lower=True, unit_diagonal=True)
    u, w = uw[..., :dv], uw[..., dv:]
    attn = jnp.einsum('bhnid,bhnjd->bhnij', q, k) * decay
    q_dec = q * jnp.exp(gc)[..., None]
    k_dec = k * jnp.exp(gc[..., -1:] - gc)[..., None]
    g_last = jnp.exp(gc[..., -1])

    def step(S, xs):
        q_n, u_n, w_n, a_n, kd_n, gl_n = xs
        v_new = u_n - jnp.einsum('bhck,bhkv->bhcv', w_n, S)
        o_n = jnp.einsum('bhck,bhkv->bhcv', q_n, S) + jnp.einsum('bhij,bhjv->bhiv', a_n, v_new)
        S = S * gl_n[..., None, None] + jnp.einsum('bhck,bhcv->bhkv', kd_n, v_new)
        return S, o_n

    xs = tuple(jnp.moveaxis(t, 2, 0) for t in (q_dec, u, w, attn, k_dec, g_last))
    S0 = jnp.zeros((Bn, H, dk, dv), jnp.float32)
    _, o = lax.scan(step, S0, xs)
    return o.transpose(1, 0, 3, 2, 4).reshape(Bn, T, H, dv)


def gated_deltanet_layer(h, in_w, conv_w, A_log, dt_bias, onorm_g, out_w):
    Bn, T, _ = h.shape
    proj = h @ in_w
    qkv = jax.nn.silu(causal_depthwise_conv(proj[..., :3 * A_WIDTH], conv_w))
    z = proj[..., 3 * A_WIDTH:4 * A_WIDTH].reshape(Bn, T, A_HEADS, A_HEAD_DIM)
    b = proj[..., 4 * A_WIDTH:4 * A_WIDTH + A_HEADS]
    a = proj[..., 4 * A_WIDTH + A_HEADS:]
    q, k, v = [t.reshape(Bn, T, A_HEADS, A_HEAD_DIM) for t in jnp.split(qkv, 3, axis=-1)]
    q = l2norm(q) * (A_HEAD_DIM ** -0.5)
    k = l2norm(k)
    beta = jax.nn.sigmoid(b.astype(jnp.float32))
    g = -jnp.exp(A_log.astype(jnp.float32)) * jax.nn.softplus(a.astype(jnp.float32) + dt_bias.astype(jnp.float32))
    o = chunk_gated_delta_rule(q, k, v.astype(jnp.float32), g, beta)
    o = rmsnorm(o, onorm_g) * jax.nn.silu(z.astype(jnp.float32))
    return o.astype(h.dtype).reshape(Bn, T, A_WIDTH) @ out_w


def nsa_shared_kv(stream, kv_norm_g, kv_w, cmp_pos_k, cmp_pos_v, cmp_k_w1, cmp_k_w2, cmp_v_w1, cmp_v_w2):
    Bn, T, _ = stream.shape
    s = rmsnorm(stream, kv_norm_g)
    kv = (s @ kv_w).reshape(Bn, T, 6, B_GROUPS, B_HEAD_DIM).transpose(2, 0, 3, 1, 4)
    n_cmp = (T - L_CMP) // CMP_STRIDE + 1
    idx = np.arange(n_cmp)[:, None] * CMP_STRIDE + np.arange(L_CMP)[None, :]

    def compress(tok, pos, w1, w2):
        blk = tok[:, :, idx] + pos
        return jax.nn.silu(blk.reshape(Bn, B_GROUPS, n_cmp, L_CMP * B_HEAD_DIM) @ w1) @ w2

    k_cmp = compress(kv[0], cmp_pos_k, cmp_k_w1, cmp_k_w2)
    v_cmp = compress(kv[1], cmp_pos_v, cmp_v_w1, cmp_v_w2)
    n_slc = T // L_SLC
    k_slc = kv[2].reshape(Bn, B_GROUPS, n_slc, L_SLC, B_HEAD_DIM)
    v_slc = kv[3].reshape(Bn, B_GROUPS, n_slc, L_SLC, B_HEAD_DIM)
    return k_cmp, v_cmp, k_slc, v_slc, kv[4], kv[5]


def nsa_layer(h, shared, rel_bias, in_w, out_w):
    k_cmp, v_cmp, k_slc, v_slc, k_win, v_win = shared
    Bn, T, _ = h.shape
    proj = h @ in_w
    z = proj[..., B_WIDTH:4 * B_WIDTH].reshape(Bn, T, N_BRANCH, B_HEADS, B_HEAD_DIM)
    gates = jax.nn.sigmoid(proj[..., 4 * B_WIDTH:].astype(jnp.float32)).reshape(Bn, T, N_BRANCH, B_HEADS)
    n_q = T // Q_BLOCK
    q = (proj[..., :B_WIDTH] * (B_HEAD_DIM ** -0.5)).reshape(
        Bn, n_q, Q_BLOCK, B_GROUPS, B_HPG, B_HEAD_DIM).transpose(1, 0, 3, 4, 2, 5)
    n_cmp = k_cmp.shape[2]
    n_slc = k_slc.shape[2]
    n_sel = min(N_SEL, n_slc)
    cmp_end = jnp.arange(n_cmp) * CMP_STRIDE + (L_CMP - 1)
    cells = np.arange(n_cmp)[:, None] + np.arange(L_CMP // CMP_STRIDE)[None, :]
    overlap = jnp.asarray((cells[:, None, :] // (L_SLC // CMP_STRIDE) == np.arange(n_slc)[None, :, None])
                          .sum(-1).astype(np.float32))
    rb_group = rel_bias.reshape(NUM_BUCKETS, B_GROUPS, B_HPG)
    k_win_p = jnp.pad(k_win, ((0, 0), (0, 0), (WINDOW, 0), (0, 0)))
    v_win_p = jnp.pad(v_win, ((0, 0), (0, 0), (WINDOW, 0), (0, 0)))
    b_idx = jnp.arange(Bn)[:, None, None, None]
    g_idx = jnp.arange(B_GROUPS)[None, :, None, None]
    blk_ids = jnp.arange(n_slc)

    def dense_bias(dist):
        return rel_bias[t5_bucket(dist)].transpose(2, 0, 1).reshape(B_GROUPS, B_HPG, *dist.shape)

    def attend_block(args):
        qb, qi = args
        t = qi * Q_BLOCK + jnp.arange(Q_BLOCK)
        d_c = t[:, None] - cmp_end[None, :]
        s_c = jnp.einsum('bghqd,bgcd->bghqc', qb, k_cmp).astype(jnp.float32) + dense_bias(d_c)
        p_c = masked_softmax(s_c, d_c >= 0)
        o_c = jnp.einsum('bghqc,bgcd->bghqd', p_c.astype(v_cmp.dtype), v_cmp)
        imp = jnp.einsum('bghqc,cs->bgqs', p_c, overlap)
        cur = (t // L_SLC)[:, None]
        forced = (blk_ids == 0) | (blk_ids == cur) | (blk_ids == cur - 1)
        imp = jnp.where(forced, SEL_BOOST, jnp.where(blk_ids > cur, -SEL_BOOST, imp))
        _, sel = lax.top_k(imp, n_sel)
        k_sel = k_slc[b_idx, g_idx, sel]
        v_sel = v_slc[b_idx, g_idx, sel]
        d_s = t[:, None, None] - (sel[..., None] * L_SLC + jnp.arange(L_SLC))
        bias_s = jnp.moveaxis(rb_group[t5_bucket(d_s), g_idx[..., None]], -1, 2)
        s_s = jnp.einsum('bghqd,bgqnld->bghqnl', qb, k_sel).astype(jnp.float32) + bias_s
        p_s = masked_softmax(s_s.reshape(Bn, B_GROUPS, B_HPG, Q_BLOCK, n_sel * L_SLC),
                             (d_s >= 0).reshape(Bn, B_GROUPS, 1, Q_BLOCK, n_sel * L_SLC))
        o_s = jnp.einsum('bghqk,bgqkd->bghqd', p_s.astype(v_sel.dtype),
                         v_sel.reshape(Bn, B_GROUPS, Q_BLOCK, n_sel * L_SLC, B_HEAD_DIM))
        q0 = qi * Q_BLOCK
        k_w = lax.dynamic_slice_in_dim(k_win_p, q0, WINDOW + Q_BLOCK, axis=2)
        v_w = lax.dynamic_slice_in_dim(v_win_p, q0, WINDOW + Q_BLOCK, axis=2)
        kpos = q0 - WINDOW + jnp.arange(WINDOW + Q_BLOCK)
        d_w = t[:, None] - kpos[None, :]
        mask_w = (d_w >= 0) & (d_w < WINDOW) & (kpos[None, :] >= 0)
        s_w = jnp.einsum('bghqd,bgkd->bghqk', qb, k_w).astype(jnp.float32) + dense_bias(d_w)
        p_w = masked_softmax(s_w, mask_w)
        o_w = jnp.einsum('bghqk,bgkd->bghqd', p_w.astype(v_w.dtype), v_w)
        return jnp.stack([o_c, o_s, o_w], axis=0)

    o = lax.map(attend_block, (q, jnp.arange(n_q)))
    o = o.transpose(2, 0, 5, 1, 3, 4, 6).reshape(Bn, T, N_BRANCH, B_HEADS, B_HEAD_DIM)
    y = jnp.sum(gates[..., None].astype(o.dtype) * o * jax.nn.silu(z), axis=2)
    return y.reshape(Bn, T, B_WIDTH) @ out_w


def setup_inputs(seed: int = 0) -> dict:
    key = jax.random.key(seed)
    ks = jax.random.split(key, 24)
    f32 = jnp.float32

    def nrm(k, shape, scale):
        return jax.random.normal(k, shape, f32) * scale

    a_cols = 4 * A_WIDTH + 2 * A_HEADS
    b_cols = 4 * B_WIDTH + N_BRANCH * B_HEADS
    dt = jnp.exp(jax.random.uniform(ks[9], (N_A_LAYERS, A_HEADS), f32, math.log(1e-3), math.log(1e-1)))
    return {
        "x": nrm(ks[0], (BATCH, SEQ, D_MODEL), 1.0),
        "c": nrm(ks[1], (BATCH, D_MODEL), 1.0),
        "rel_bias": nrm(ks[2], (NUM_BUCKETS, B_HEADS), 0.3),
        "ada_w": nrm(ks[3], (DEPTH, D_MODEL, 3 * D_MODEL), 0.5 * D_MODEL ** -0.5),
        "ada_b": nrm(ks[4], (DEPTH, 3 * D_MODEL), 0.02),
        "norm_g": 1.0 + nrm(ks[5], (DEPTH, D_MODEL), 0.02),
        "a_in_w": nrm(ks[6], (N_A_LAYERS, D_MODEL, a_cols), D_MODEL ** -0.5),
        "a_conv_w": nrm(ks[7], (N_A_LAYERS, A_CONV, 3 * A_WIDTH), A_CONV ** -0.5),
        "a_A_log": jnp.log(jax.random.uniform(ks[8], (N_A_LAYERS, A_HEADS), f32, 1.0, 16.0)),
        "a_dt_bias": dt + jnp.log(-jnp.expm1(-dt)),
        "a_onorm_g": 1.0 + nrm(ks[10], (N_A_LAYERS, A_HEAD_DIM), 0.02),
        "a_out_w": nrm(ks[11], (N_A_LAYERS, A_WIDTH, D_MODEL), A_WIDTH ** -0.5),
        "kv_norm_g": 1.0 + nrm(ks[12], (D_MODEL,), 0.02),
        "kv_w": nrm(ks[13], (D_MODEL, 6 * B_GROUPS * B_HEAD_DIM), D_MODEL ** -0.5),
        "cmp_pos_k": nrm(ks[14], (L_CMP, B_HEAD_DIM), 0.1),
        "cmp_pos_v": nrm(ks[15], (L_CMP, B_HEAD_DIM), 0.1),
        "cmp_k_w1": nrm(ks[16], (L_CMP * B_HEAD_DIM, CMP_HIDDEN), (L_CMP * B_HEAD_DIM) ** -0.5),
        "cmp_k_w2": nrm(ks[17], (CMP_HIDDEN, B_HEAD_DIM), CMP_HIDDEN ** -0.5),
        "cmp_v_w1": nrm(ks[18], (L_CMP * B_HEAD_DIM, CMP_HIDDEN), (L_CMP * B_HEAD_DIM) ** -0.5),
        "cmp_v_w2": nrm(ks[19], (CMP_HIDDEN, B_HEAD_DIM), CMP_HIDDEN ** -0.5),
        "b_in_w": nrm(ks[20], (N_B_LAYERS, D_MODEL, b_cols), D_MODEL ** -0.5),
        "b_out_w": nrm(ks[21], (N_B_LAYERS, B_WIDTH, D_MODEL), B_WIDTH ** -0.5),
        "final_g": 1.0 + nrm(ks[22], (D_MODEL,), 0.02),
    }


def reference(x, c, rel_bias, ada_w, ada_b, norm_g, a_in_w, a_conv_w, a_A_log, a_dt_bias, a_onorm_g, a_out_w,
              kv_norm_g, kv_w, cmp_pos_k, cmp_pos_v, cmp_k_w1, cmp_k_w2, cmp_v_w1, cmp_v_w2,
              b_in_w, b_out_w, final_g):
    shared = None
    for l in range(DEPTH):
        shift, scale, gate = ada_modulation(c, ada_w[l], ada_b[l])
        h = rmsnorm(x, norm_g[l]) * (1.0 + scale) + shift
        if l < N_A_LAYERS:
            out = gated_deltanet_layer(h, a_in_w[l], a_conv_w[l], a_A_log[l], a_dt_bias[l], a_onorm_g[l], a_out_w[l])
        else:
            j = l - N_A_LAYERS
            out = nsa_layer(h, shared, rel_bias, b_in_w[j], b_out_w[j])
        x = x + gate * out
        if l == N_A_LAYERS - 1:
            shared = nsa_shared_kv(x, kv_norm_g, kv_w, cmp_pos_k, cmp_pos_v, cmp_k_w1, cmp_k_w2, cmp_v_w1, cmp_v_w2)
    return rmsnorm(x, final_g)
```

```python
import functools
import math

import numpy as np
import jax
import jax.numpy as jnp
from jax import lax
from jax.experimental import pallas as pl
from jax.experimental.pallas import tpu as pltpu

F32 = jnp.float32
BF16 = jnp.bfloat16
HIGHEST = lax.Precision.HIGHEST

A_HEADS = 8
A_HEAD_DIM = 128
A_WIDTH = A_HEADS * A_HEAD_DIM
A_CONV = 4
A_CHUNK = 64
B_HEADS = 16
B_GROUPS = 2
B_HPG = B_HEADS // B_GROUPS
B_HEAD_DIM = 64
B_WIDTH = B_HEADS * B_HEAD_DIM
N_BRANCH = 3
L_CMP = 32
CMP_STRIDE = 16
L_SLC = 64
N_SEL = 16
WINDOW = 512
Q_BLOCK = 64
NUM_BUCKETS = 32
MAX_DISTANCE = 128
EPS = 1e-6
NEG_INF = -1e30
SEL_BOOST = 1e9

LANES = 128
SUBLANES = 8
VMEM_LIMIT_BYTES = 56 * 1024 * 1024

ROW_TILE = 256
GDN_HEADS_PER_STEP = 4
SEL_KEY_TILE = 512
KV_PAD = WINDOW
CMP_NEAR = 16
BLK16 = 16


def _cparams(*sem):
    return pltpu.CompilerParams(dimension_semantics=sem, vmem_limit_bytes=VMEM_LIMIT_BYTES)


def _sigmoid(x):
    return 1.0 / (1.0 + jnp.exp(-x))


def _silu(x):
    return x * _sigmoid(x)


def _dot(a, b):
    return jnp.dot(a.astype(BF16), b.astype(BF16), preferred_element_type=F32)


def _dot_nt(a, b):
    return lax.dot_general(a.astype(BF16), b.astype(BF16), (((1,), (1,)), ((), ())),
                           preferred_element_type=F32)


def _dot_f32(a, b):
    return jnp.dot(a, b, precision=HIGHEST, preferred_element_type=F32)


def _rms(x, g):
    ms = jnp.mean(x * x, axis=-1, keepdims=True)
    return x * lax.rsqrt(ms + EPS) * g


def _ada_kernel(c_ref, w_ref, b_ref, o_ref):
    o_ref[...] = _dot_f32(_silu(c_ref[...]), w_ref[...]) + b_ref[...]


def _ada_modulation(c, ada_w, ada_b):
    depth, d, d3 = ada_w.shape
    bn = c.shape[0]
    return pl.pallas_call(
        _ada_kernel,
        grid=(depth, d3 // d),
        in_specs=[pl.BlockSpec((bn, d), lambda l, j: (0, 0)),
                  pl.BlockSpec((None, d, d), lambda l, j: (l, 0, j)),
                  pl.BlockSpec((None, 1, d), lambda l, j: (l, 0, j))],
        out_specs=pl.BlockSpec((None, bn, d), lambda l, j: (l, 0, j)),
        out_shape=jax.ShapeDtypeStruct((depth, bn, d3), F32),
        compiler_params=_cparams("arbitrary", "arbitrary"),
    )(c, ada_w, ada_b.reshape(depth, 1, d3))


def _in_proj_a_kernel(x_ref, g_ref, scale_ref, shift_ref, wqkv_ref, wz_ref, wba_ref,
                      qkv_ref, z_ref, ba_ref):
    h = _rms(x_ref[...], g_ref[...]) * (1.0 + scale_ref[...]) + shift_ref[...]
    hb = h.astype(BF16)
    qkv_ref[...] = jnp.dot(hb, wqkv_ref[...], preferred_element_type=F32)
    z_ref[...] = jnp.dot(hb, wz_ref[...], preferred_element_type=F32)
    ba_ref[...] = jnp.dot(hb, wba_ref[...], preferred_element_type=F32)


def _in_proj_a(x, g, scale, shift, wqkv, wz, wba):
    bn, t, d = x.shape
    tm = ROW_TILE
    row = lambda b, i: (b, i, 0)
    per_b = lambda b, i: (b, 0, 0)
    const = lambda b, i: (0, 0)
    nba = wba.shape[1]
    return pl.pallas_call(
        _in_proj_a_kernel,
        grid=(bn, t // tm),
        in_specs=[pl.BlockSpec((None, tm, d), row),
                  pl.BlockSpec((1, d), const),
                  pl.BlockSpec((None, 1, d), per_b),
                  pl.BlockSpec((None, 1, d), per_b),
                  pl.BlockSpec(wqkv.shape, const),
                  pl.BlockSpec(wz.shape, const),
                  pl.BlockSpec(wba.shape, const)],
        out_specs=[pl.BlockSpec((None, tm, 3 * A_WIDTH), row),
                   pl.BlockSpec((None, tm, A_WIDTH), row),
                   pl.BlockSpec((None, tm, nba), row)],
        out_shape=[jax.ShapeDtypeStruct((bn, t, 3 * A_WIDTH), F32),
                   jax.ShapeDtypeStruct((bn, t, A_WIDTH), F32),
                   jax.ShapeDtypeStruct((bn, t, nba), F32)],
        compiler_params=_cparams("arbitrary", "arbitrary"),
    )(x, g, scale, shift, wqkv, wz, wba)


def _cumsum_rows(x):
    n = x.shape[0]
    row = lax.broadcasted_iota(jnp.int32, x.shape, 0)
    s = 1
    while s < n:
        x = x + jnp.where(row >= s, pltpu.roll(x, s, axis=0), 0.0)
        s *= 2
    return x


def _unit_lower_inverse(m):
    c = m.shape[0]
    row = lax.broadcasted_iota(jnp.int32, (c, c), 0)
    col = lax.broadcasted_iota(jnp.int32, (c, c), 1)
    eye = (row == col).astype(F32)
    same_blk = (row & -BLK16) == (col & -BLK16)
    d = jnp.where(same_blk, m, 0.0)
    mo = m - d
    d2 = _dot_f32(d, d)
    d4 = _dot_f32(d2, d2)
    d8 = _dot_f32(d4, d4)
    td = eye - d
    td = td + _dot_f32(td, d2)
    td = td + _dot_f32(td, d4)
    td = td + _dot_f32(td, d8)
    n = _dot_f32(td, mo)
    n2 = _dot_f32(n, n)
    r = eye - n
    r = r + _dot_f32(r, n2)
    return _dot_f32(r, td)


def _gdn_kernel(q_ref, k_ref, v_ref, z_ref, ba_ref, cwq_ref, cwk_ref, cwv_ref, alog_ref, dtb_ref,
                ong_ref, o_ref, buf_ref, s_ref, *, hb):
    c = A_CHUNK
    dh = A_HEAD_DIM
    halo = SUBLANES
    n = pl.program_id(2)

    @pl.when(n == 0)
    def _():
        buf_ref[:, 0:halo, :] = jnp.zeros((3, halo, hb * dh), F32)
        s_ref[...] = jnp.zeros(s_ref.shape, F32)

    def conv_silu(idx, x_ref, cw_ref):
        buf_ref[idx, halo:halo + c, :] = x_ref[...]
        cw = cw_ref[...]
        off = halo - (A_CONV - 1)
        y = buf_ref[idx, off:off + c, :] * cw[0:1, :]
        for kk in range(1, A_CONV):
            y = y + buf_ref[idx, off + kk:off + kk + c, :] * cw[kk:kk + 1, :]
        buf_ref[idx, 0:halo, :] = x_ref[c - halo:c, :]
        return _silu(y)

    q_all = conv_silu(0, q_ref, cwq_ref)
    k_all = conv_silu(1, k_ref, cwk_ref)
    v_all = conv_silu(2, v_ref, cwv_ref)

    ba = ba_ref[...]
    beta_t = _sigmoid(ba)
    xa = ba + dtb_ref[...]
    softplus = jnp.maximum(xa, 0.0) + jnp.log(1.0 + jnp.exp(-jnp.abs(xa)))
    g_t = -jnp.exp(alog_ref[...]) * softplus
    gc_t = _cumsum_rows(g_t)
    gc_tt = gc_t.T
    egc_t = jnp.exp(gc_t)

    row = lax.broadcasted_iota(jnp.int32, (c, c), 0)
    col = lax.broadcasted_iota(jnp.int32, (c, c), 1)
    incl = row >= col
    strict = row > col

    for i in range(hb):
        sl = slice(i * dh, (i + 1) * dh)
        qh, kh, vh = q_all[:, sl], k_all[:, sl], v_all[:, sl]
        qn = qh * lax.rsqrt(jnp.sum(qh * qh, axis=-1, keepdims=True) + EPS) * (dh ** -0.5)
        kn = kh * lax.rsqrt(jnp.sum(kh * kh, axis=-1, keepdims=True) + EPS)
        beta = beta_t[:, i:i + 1]
        la = LANES // 2 + i
        gcol = gc_t[:, la:la + 1]
        grow = gc_tt[la:la + 1, :]
        eg = egc_t[:, la:la + 1]
        glast = gc_t[c - 1:c, la:la + 1]
        decay = jnp.where(incl, jnp.exp(jnp.where(incl, gcol - grow, 0.0)), 0.0)
        kb = kn * beta
        m = jnp.where(strict, _dot_nt(kb, kn) * decay, 0.0)
        tinv = _unit_lower_inverse(m)
        rhs = jnp.concatenate([vh * beta, kb * eg], axis=1)
        uw = _dot(tinv, rhs)
        u, w = uw[:, :dh], uw[:, dh:]
        attn = _dot_nt(qn, kn) * decay
        s_old = s_ref[i]
        sb = s_old.astype(BF16)
        v_new = u - _dot(w, sb)
        o = _dot(qn * eg, sb) + _dot(attn, v_new)
        k_dec = kn * jnp.exp(glast - gcol)
        s_ref[i] = s_old * jnp.exp(glast) + _dot(k_dec.T, v_new)
        o_ref[:, sl] = _rms(o, ong_ref[...]) * _silu(z_ref[:, sl])


def _gdn(qkv, z, ba, conv_w, alog_row, dtb_row, onorm_g):
    bn, t, _ = qkv.shape
    hb = GDN_HEADS_PER_STEP
    nhg = A_HEADS // hb
    c = A_CHUNK
    w = hb * A_HEAD_DIM
    kern = functools.partial(_gdn_kernel, hb=hb)

    def qkv_spec(which):
        return pl.BlockSpec((None, c, w), lambda b, hg, n: (b, n, which * nhg + hg))

    def cw_spec(which):
        return pl.BlockSpec((A_CONV, w), lambda b, hg, n: (0, which * nhg + hg))

    hg_row = pl.BlockSpec((None, 1, LANES), lambda b, hg, n: (hg, 0, 0))
    return pl.pallas_call(
        kern,
        grid=(bn, nhg, t // c),
        in_specs=[qkv_spec(0), qkv_spec(1), qkv_spec(2),
                  pl.BlockSpec((None, c, w), lambda b, hg, n: (b, n, hg)),
                  pl.BlockSpec((None, c, LANES), lambda b, hg, n: (b, n, hg)),
                  cw_spec(0), cw_spec(1), cw_spec(2),
                  hg_row, hg_row,
                  pl.BlockSpec((1, A_HEAD_DIM), lambda b, hg, n: (0, 0))],
        out_specs=pl.BlockSpec((None, c, w), lambda b, hg, n: (b, n, hg)),
        out_shape=jax.ShapeDtypeStruct((bn, t, A_WIDTH), F32),
        scratch_shapes=[pltpu.VMEM((3, SUBLANES + c, w), F32),
                        pltpu.VMEM((hb, A_HEAD_DIM, A_HEAD_DIM), F32)],
        compiler_params=_cparams("arbitrary", "arbitrary", "arbitrary"),
    )(qkv, qkv, qkv, z, ba, conv_w, conv_w, conv_w, alog_row, dtb_row, onorm_g)


def _post_a_kernel(x_ref, og_ref, gate_ref, wo_ref, kvg_ref, wkvc_ref, wkvr_ref, g1_ref, scale_ref,
                   shift_ref, wq_ref, wz_ref, wg_ref,
                   x1_ref, kvc_ref, kvr_ref, q_ref, z_ref, gates_ref):
    out = _dot(og_ref[...], wo_ref[...])
    x1 = x_ref[...] + gate_ref[...] * out
    x1_ref[...] = x1
    sb = _rms(x1, kvg_ref[...]).astype(BF16)
    kvc_ref[...] = jnp.dot(sb, wkvc_ref[...], preferred_element_type=F32)
    kvr_ref[...] = jnp.dot(sb, wkvr_ref[...], preferred_element_type=F32).astype(BF16)
    h = _rms(x1, g1_ref[...]) * (1.0 + scale_ref[...]) + shift_ref[...]
    hb = h.astype(BF16)
    q = jnp.dot(hb, wq_ref[...], preferred_element_type=F32) * (B_HEAD_DIM ** -0.5)
    q_ref[...] = q.astype(BF16)
    z_ref[...] = jnp.dot(hb, wz_ref[...], preferred_element_type=F32)
    gates_ref[...] = _sigmoid(jnp.dot(hb, wg_ref[...], preferred_element_type=F32))


def _post_a(x, og, gate0, wo, kvg, wkvc, wkvr, g1, scale1, shift1, wq, wz, wg):
    bn, t, d = x.shape
    tm = ROW_TILE
    row = lambda b, i: (b, i, 0)
    per_b = lambda b, i: (b, 0, 0)
    const = lambda b, i: (0, 0)
    full = lambda a: pl.BlockSpec(a.shape, const)
    vec = pl.BlockSpec((1, d), const)
    bvec = pl.BlockSpec((None, 1, d), per_b)
    outs = [(d, F32), (wkvc.shape[1], F32), (wkvr.shape[1], BF16), (wq.shape[1], BF16),
            (wz.shape[1], F32), (wg.shape[1], F32)]
    return pl.pallas_call(
        _post_a_kernel,
        grid=(bn, t // tm),
        in_specs=[pl.BlockSpec((None, tm, d), row), pl.BlockSpec((None, tm, A_WIDTH), row), bvec,
                  full(wo), vec, full(wkvc), full(wkvr), vec, bvec, bvec, full(wq), full(wz), full(wg)],
        out_specs=[pl.BlockSpec((None, tm, n), row) for n, _ in outs],
        out_shape=[jax.ShapeDtypeStruct((bn, t, n), dt) for n, dt in outs],
        compiler_params=_cparams("arbitrary", "arbitrary"),
    )(x, og, gate0, wo, kvg, wkvc, wkvr, g1, scale1, shift1, wq, wz, wg)


def _compress_kernel(hv_ref, ptop_ref, pbot_ref, w1t_ref, w1b_ref, w2_ref, o_ref):
    hv = hv_ref[...]
    a = _dot(hv + ptop_ref[...], w1t_ref[...])
    b = _dot(hv + pbot_ref[...], w1b_ref[...])
    nrow = a.shape[0]
    hid = a + pltpu.roll(b, nrow - 1, axis=0)
    o_ref[...] = _dot(_silu(hid), w2_ref[...]).astype(o_ref.dtype)


def _compress(halves, ptop, pbot, w1t, w1b, w2):
    bn, four, nh, wd = halves.shape
    hid = w1t.shape[-1]
    per_kind = lambda b, j: (j // B_GROUPS, 0, 0)
    return pl.pallas_call(
        _compress_kernel,
        grid=(bn, four),
        in_specs=[pl.BlockSpec((None, None, nh, wd), lambda b, j: (b, j, 0, 0)),
                  pl.BlockSpec((None, 1, wd), per_kind),
                  pl.BlockSpec((None, 1, wd), per_kind),
                  pl.BlockSpec((None, wd, hid), per_kind),
                  pl.BlockSpec((None, wd, hid), per_kind),
                  pl.BlockSpec((None, hid, B_HEAD_DIM), per_kind)],
        out_specs=pl.BlockSpec((None, None, nh, B_HEAD_DIM), lambda b, j: (b, j, 0, 0)),
        out_shape=jax.ShapeDtypeStruct((bn, four, nh, B_HEAD_DIM), BF16),
        compiler_params=_cparams("arbitrary", "arbitrary"),
    )(halves, ptop, pbot, w1t, w1b, w2)


def _t5_bucket_np(dist):
    n = np.maximum(dist, 0)
    max_exact = NUM_BUCKETS // 2
    nf = np.maximum(n, 1).astype(np.float64)
    val = np.log(nf / max_exact) / math.log(MAX_DISTANCE / max_exact) * (NUM_BUCKETS - max_exact)
    frac = np.abs(val - np.round(val))
    safe = (frac > 1e-6) | (n <= max_exact) | (n >= MAX_DISTANCE)
    assert bool(np.all(safe)), "bucket boundary too close to an integer distance"
    large = np.minimum(max_exact + np.floor(np.maximum(val, 0.0)).astype(np.int64), NUM_BUCKETS - 1)
    return np.where(n < max_exact, n, large)


def _bias_onehot():
    r = np.arange(Q_BLOCK)[:, None]
    tiles = []
    j = np.arange(WINDOW + Q_BLOCK)[None, :]
    d = r + WINDOW - j
    tiles.append((d, (d >= 0) & (d < WINDOW)))
    j = np.arange(3 * L_SLC)[None, :]
    d = r + 2 * L_SLC - j
    tiles.append((d, d >= 0))
    j = np.arange(CMP_NEAR)[None, :]
    d = r - CMP_STRIDE * (j - 12) - (L_CMP - 1)
    tiles.append((d, d >= 0))
    cols = []
    for d, valid in tiles:
        bucket = np.where(valid, _t5_bucket_np(d), NUM_BUCKETS).reshape(-1)
        oh = np.zeros((2 * NUM_BUCKETS, bucket.size), np.float32)
        oh[bucket, np.arange(bucket.size)] = 1.0
        cols.append(oh)
    widths = [c.shape[1] for c in cols]
    return np.concatenate(cols, axis=1), widths


def _bias_kernel(rb_ref, oh_ref, o_ref):
    rb = rb_ref[...]
    lane = lax.broadcasted_iota(jnp.int32, rb.shape, 1)
    rbs = rb - rb[:, NUM_BUCKETS - 1:NUM_BUCKETS]
    rbs = jnp.where(lane < NUM_BUCKETS, rbs, jnp.where(lane == NUM_BUCKETS, NEG_INF, 0.0))
    o_ref[...] = _dot_f32(rbs, oh_ref[...])


def _bias_tables(rel_bias):
    oh, widths = _bias_onehot()
    ncol = oh.shape[1]
    nt = 8
    assert ncol % (nt * LANES) == 0
    tc = ncol // nt
    rb = jnp.concatenate([rel_bias.T, jnp.zeros((B_HEADS, NUM_BUCKETS), F32)], axis=1)
    flat = pl.pallas_call(
        _bias_kernel,
        grid=(nt,),
        in_specs=[pl.BlockSpec((B_HEADS, 2 * NUM_BUCKETS), lambda i: (0, 0)),
                  pl.BlockSpec((2 * NUM_BUCKETS, tc), lambda i: (0, i))],
        out_specs=pl.BlockSpec((B_HEADS, tc), lambda i: (0, i)),
        out_shape=jax.ShapeDtypeStruct((B_HEADS, ncol), F32),
        compiler_params=_cparams("arbitrary"),
    )(rb, jnp.asarray(oh))
    out, start = [], 0
    for wd in widths:
        tile = flat[:, start:start + wd].reshape(B_GROUPS, B_HPG * Q_BLOCK, wd // Q_BLOCK)
        out.append(tile)
        start += wd
    return out


def _nsa_kernel(q_ref, kc_ref, vc_ref, ks_ref, vs_ref, kw_ref, vw_ref, tcmp_ref, tsel_ref, twin_ref,
                ov_ref, oc_ref, os_ref, ow_ref):
    qb = Q_BLOCK
    dh = B_HEAD_DIM
    hpg = B_HPG
    rows = hpg * qb
    qi = pl.program_id(2)
    q0 = qi * qb
    qt = q_ref[...]
    q = jnp.concatenate([qt[:, h * dh:(h + 1) * dh] for h in range(hpg)], axis=0)

    def to_tokens(o):
        return jnp.concatenate([o[h * qb:(h + 1) * qb, :] for h in range(hpg)], axis=1)

    kc = kc_ref[...]
    ncp = kc.shape[0]
    s = _dot_nt(q, kc)
    first_near = (qb // CMP_STRIDE) * qi - 12
    cid = lax.broadcasted_iota(jnp.int32, (2 * CMP_NEAR, ncp), 1)
    jrow = lax.broadcasted_iota(jnp.int32, (2 * CMP_NEAR, ncp), 0) & (CMP_NEAR - 1)
    shift_eye = jnp.where(cid - first_near == jrow, 1.0, 0.0).astype(BF16)
    s = s + jnp.dot(tcmp_ref[...], shift_eye, preferred_element_type=F32)
    cvis = lax.broadcasted_iota(jnp.int32, (1, ncp), 1) < first_near + CMP_NEAR
    s = jnp.where(cvis, s, NEG_INF)
    live = s > 0.1 * NEG_INF
    m = jnp.max(s, axis=-1, keepdims=True)
    e = jnp.where(live, jnp.exp(s - m), 0.0)
    p = e / jnp.maximum(jnp.sum(e, axis=-1, keepdims=True), 1e-30)
    oc_ref[...] = to_tokens(_dot(p, vc_ref[...]))

    psum = p[0:qb, :]
    for h in range(1, hpg):
        psum = psum + p[h * qb:(h + 1) * qb, :]
    imp_t = _dot_f32(psum, ov_ref[...]).T
    nblk = imp_t.shape[0]
    blk = lax.broadcasted_iota(jnp.int32, (nblk, qb), 0)
    forced = (blk == 0) | (blk == qi) | (blk == qi - 1)
    val = jnp.where(forced, SEL_BOOST, jnp.where(blk > qi, -SEL_BOOST, imp_t))
    rank = jnp.zeros((nblk, qb), jnp.int32)
    for j in range(nblk):
        vj = val[j:j + 1, :]
        ahead = (vj > val) | ((vj == val) & (blk > j))
        rank = rank + ahead.astype(jnp.int32)
    sel_t = (rank < N_SEL) & (blk <= qi)
    far_t = jnp.where(sel_t & (blk <= qi - 3), 0.0, NEG_INF)
    near_t = jnp.where(sel_t & (blk >= qi - 2), 0.0, NEG_INF)

    def q_with_mask(mask_t):
        mk = mask_t.T.astype(BF16)
        return jnp.concatenate([q, jnp.concatenate([mk] * hpg, axis=0)], axis=1)

    q_far = q_with_mask(far_t)
    q_near = q_with_mask(near_t)

    kt_sz = SEL_KEY_TILE
    n_far_keys = jnp.maximum(qi - 2, 0) * L_SLC
    n_tiles = (n_far_keys + kt_sz - 1) // kt_sz

    def far_step(i, carry):
        m_i, l_i, acc = carry
        start = pl.multiple_of(KV_PAD + i * kt_sz, kt_sz)
        s_f = _dot_nt(q_far, ks_ref[pl.ds(start, kt_sz), :])
        m_n = jnp.maximum(m_i, jnp.max(s_f, axis=-1, keepdims=True))
        alpha = jnp.exp(m_i - m_n)
        e_f = jnp.exp(s_f - m_n)
        l_n = alpha * l_i + jnp.sum(e_f, axis=-1, keepdims=True)
        acc_n = alpha * acc + _dot(e_f, vs_ref[pl.ds(start, kt_sz), :])
        return m_n, l_n, acc_n

    init = (jnp.full((rows, 1), NEG_INF, F32), jnp.zeros((rows, 1), F32), jnp.zeros((rows, dh), F32))
    m_i, l_i, acc = lax.fori_loop(0, n_tiles, far_step, init)

    nk = 3 * L_SLC
    start = pl.multiple_of(KV_PAD + q0 - 2 * L_SLC, L_SLC)
    s_n = _dot_nt(q_near, ks_ref[pl.ds(start, nk), :]) + tsel_ref[...]
    exists = lax.broadcasted_iota(jnp.int32, (1, nk), 1) >= 2 * L_SLC - q0
    s_n = jnp.where(exists, s_n, NEG_INF)
    m_n = jnp.maximum(m_i, jnp.max(s_n, axis=-1, keepdims=True))
    alpha = jnp.exp(m_i - m_n)
    e_n = jnp.exp(s_n - m_n)
    l_n = alpha * l_i + jnp.sum(e_n, axis=-1, keepdims=True)
    acc = alpha * acc + _dot(e_n, vs_ref[pl.ds(start, nk), :])
    os_ref[...] = to_tokens(acc / l_n)

    nw = WINDOW + qb
    start = pl.multiple_of(q0, qb)
    s_w = _dot_nt(q, kw_ref[pl.ds(start, nw), :]) + twin_ref[...]
    exists = lax.broadcasted_iota(jnp.int32, (1, nw), 1) >= WINDOW - q0
    s_w = jnp.where(exists, s_w, NEG_INF)
    m_w = jnp.max(s_w, axis=-1, keepdims=True)
    e_w = jnp.exp(s_w - m_w)
    l_w = jnp.sum(e_w, axis=-1, keepdims=True)
    ow_ref[...] = to_tokens(_dot(e_w, vw_ref[pl.ds(start, nw), :]) / l_w)


def _nsa(q, kcv, ks, vs, kw, vw, tcmp, tsel, twin, ov):
    bn, t, _ = q.shape
    qb = Q_BLOCK
    gw = B_HPG * B_HEAD_DIM
    ncp = kcv.shape[2]
    tp = ks.shape[2]
    rows = B_HPG * qb
    per_bg = lambda b, g, i: (b, g, 0, 0)
    per_g = lambda b, g, i: (g, 0, 0)
    out_spec = pl.BlockSpec((None, qb, gw), lambda b, g, i: (b, i, g))
    out_sd = jax.ShapeDtypeStruct((bn, t, B_WIDTH), F32)
    return pl.pallas_call(
        _nsa_kernel,
        grid=(bn, B_GROUPS, t // qb),
        in_specs=[pl.BlockSpec((None, qb, gw), lambda b, g, i: (b, i, g)),
                  pl.BlockSpec((None, None, ncp, B_HEAD_DIM), lambda b, g, i: (b, g, 0, 0)),
                  pl.BlockSpec((None, None, ncp, B_HEAD_DIM), lambda b, g, i: (b, B_GROUPS + g, 0, 0)),
                  pl.BlockSpec((None, None, tp, ks.shape[3]), per_bg),
                  pl.BlockSpec((None, None, tp, B_HEAD_DIM), per_bg),
                  pl.BlockSpec((None, None, tp, B_HEAD_DIM), per_bg),
                  pl.BlockSpec((None, None, tp, B_HEAD_DIM), per_bg),
                  pl.BlockSpec((None, rows, tcmp.shape[2]), per_g),
                  pl.BlockSpec((None, rows, tsel.shape[2]), per_g),
                  pl.BlockSpec((None, rows, twin.shape[2]), per_g),
                  pl.BlockSpec(ov.shape, lambda b, g, i: (0, 0))],
        out_specs=[out_spec, out_spec, out_spec],
        out_shape=[out_sd, out_sd, out_sd],
        compiler_params=_cparams("arbitrary", "arbitrary", "arbitrary"),
    )(q, kcv, kcv, ks, vs, kw, vw, tcmp, tsel, twin, ov)


def _final_kernel(oc_ref, os_ref, ow_ref, z_ref, gates_ref, ex_ref, x1_ref, gate_ref, wo_ref, fg_ref, o_ref):
    gt = gates_ref[...]
    g_hi = gt.astype(BF16)
    g_lo = (gt - g_hi.astype(F32)).astype(BF16)
    ghl = jnp.concatenate([g_hi, g_lo], axis=1)
    y = None
    for br, o_ref_br in enumerate((oc_ref, os_ref, ow_ref)):
        gexp = jnp.dot(ghl, ex_ref[br], preferred_element_type=F32)
        term = gexp * o_ref_br[...] * _silu(z_ref[:, br * B_WIDTH:(br + 1) * B_WIDTH])
        y = term if y is None else y + term
    x2 = x1_ref[...] + gate_ref[...] * _dot(y, wo_ref[...])
    o_ref[...] = _rms(x2, fg_ref[...])


def _final(oc, osel, ow, z, gates, x1, gate1, wo, fg):
    bn, t, d = x1.shape
    tm = ROW_TILE
    ng = gates.shape[2]
    row = lambda b, i: (b, i, 0)
    ex = np.zeros((N_BRANCH, 2 * ng, B_WIDTH), np.float32)
    for br in range(N_BRANCH):
        for h in range(B_HEADS):
            ex[br, br * B_HEADS + h, h * B_HEAD_DIM:(h + 1) * B_HEAD_DIM] = 1.0
            ex[br, ng + br * B_HEADS + h, h * B_HEAD_DIM:(h + 1) * B_HEAD_DIM] = 1.0
    ex = jnp.asarray(ex, BF16)
    return pl.pallas_call(
        _final_kernel,
        grid=(bn, t // tm),
        in_specs=[pl.BlockSpec((None, tm, B_WIDTH), row)] * 3
        + [pl.BlockSpec((None, tm, N_BRANCH * B_WIDTH), row),
           pl.BlockSpec((None, tm, ng), row),
           pl.BlockSpec(ex.shape, lambda b, i: (0, 0, 0)),
           pl.BlockSpec((None, tm, d), row),
           pl.BlockSpec((None, 1, d), lambda b, i: (b, 0, 0)),
           pl.BlockSpec(wo.shape, lambda b, i: (0, 0)),
           pl.BlockSpec((1, d), lambda b, i: (0, 0))],
        out_specs=pl.BlockSpec((None, tm, d), row),
        out_shape=jax.ShapeDtypeStruct((bn, t, d), F32),
        compiler_params=_cparams("arbitrary", "arbitrary"),
    )(oc, osel, ow, z, gates, ex, x1, gate1, wo, fg)


def _overlap_matrix(ncp, n_cmp, n_slc, nblk):
    cells = np.arange(n_cmp)[:, None] + np.arange(L_CMP // CMP_STRIDE)[None, :]
    ov = (cells[:, None, :] // (L_SLC // CMP_STRIDE) == np.arange(n_slc)[None, :, None]).sum(-1)
    out = np.zeros((ncp, nblk), np.float32)
    out[:n_cmp, :n_slc] = ov
    return out


def _block_onehot(t, nblk):
    oh = np.zeros((KV_PAD + t, nblk), np.float32)
    oh[KV_PAD + np.arange(t), np.arange(t) // L_SLC] = 1.0
    return oh


def kernel(x, c, rel_bias, ada_w, ada_b, norm_g, a_in_w, a_conv_w, a_A_log, a_dt_bias, a_onorm_g, a_out_w,
           kv_norm_g, kv_w, cmp_pos_k, cmp_pos_v, cmp_k_w1, cmp_k_w2, cmp_v_w1, cmp_v_w2,
           b_in_w, b_out_w, final_g):
    bn, t, d = x.shape
    assert ada_w.shape[0] == 2 and a_in_w.shape[0] == 1 and b_in_w.shape[0] == 1
    assert t % max(ROW_TILE, SEL_KEY_TILE) == 0
    n_slc = t // L_SLC
    nblk = 64
    assert n_slc <= nblk
    n_cmp = (t - L_CMP) // CMP_STRIDE + 1
    ncp = t // CMP_STRIDE

    mod = _ada_modulation(c, ada_w, ada_b)
    shift = mod[:, :, None, :d]
    scale = mod[:, :, None, d:2 * d]
    gate = mod[:, :, None, 2 * d:]

    hb = GDN_HEADS_PER_STEP
    nhg = A_HEADS // hb
    w_in = a_in_w[0]
    wqkv = w_in[:, :3 * A_WIDTH].astype(BF16)
    wz = w_in[:, 3 * A_WIDTH:4 * A_WIDTH].astype(BF16)
    wb = w_in[:, 4 * A_WIDTH:4 * A_WIDTH + A_HEADS]
    wa = w_in[:, 4 * A_WIDTH + A_HEADS:]
    half = LANES // 2
    wba = jnp.zeros((d, nhg, LANES), F32)
    wba = wba.at[:, :, :hb].set(wb.reshape(d, nhg, hb)).at[:, :, half:half + hb].set(wa.reshape(d, nhg, hb))
    wba = wba.reshape(d, nhg * LANES).astype(BF16)
    lane_rows = lambda v: jnp.zeros((nhg, 1, LANES), F32).at[:, 0, half:half + hb].set(v.reshape(nhg, hb))
    qkv, z_a, ba = _in_proj_a(x, norm_g[0:1], scale[0], shift[0], wqkv, wz, wba)
    og = _gdn(qkv, z_a, ba, a_conv_w[0], lane_rows(a_A_log[0]), lane_rows(a_dt_bias[0]), a_onorm_g[0:1])

    ndh = B_GROUPS * B_HEAD_DIM
    wkvc = kv_w[:, :2 * ndh].astype(BF16)
    wkvr = kv_w[:, 2 * ndh:].astype(BF16)
    w_b = b_in_w[0]
    wq = w_b[:, :B_WIDTH].astype(BF16)
    wzb = w_b[:, B_WIDTH:4 * B_WIDTH].astype(BF16)
    wg = jnp.zeros((d, LANES), F32).at[:, :N_BRANCH * B_HEADS].set(w_b[:, 4 * B_WIDTH:]).astype(BF16)
    x1, kvc, kvr, q, z_b, gates = _post_a(x, og, gate[0], a_out_w[0].astype(BF16), kv_norm_g[None, :], wkvc, wkvr,
                                          norm_g[1:2], scale[1], shift[1], wq, wzb, wg)

    halves = kvc.reshape(bn, ncp, CMP_STRIDE, 2 * B_GROUPS, B_HEAD_DIM).transpose(0, 3, 1, 2, 4)
    halves = halves.reshape(bn, 2 * B_GROUPS, ncp, CMP_STRIDE * B_HEAD_DIM)
    pos = jnp.stack([cmp_pos_k, cmp_pos_v])
    hw = CMP_STRIDE * B_HEAD_DIM
    ptop = pos[:, :CMP_STRIDE].reshape(2, 1, hw)
    pbot = pos[:, CMP_STRIDE:].reshape(2, 1, hw)
    w1 = jnp.stack([cmp_k_w1, cmp_v_w1]).astype(BF16)
    w2 = jnp.stack([cmp_k_w2, cmp_v_w2]).astype(BF16)
    kcv = _compress(halves, ptop, pbot, w1[:, :hw], w1[:, hw:], w2)

    kvr4 = kvr.reshape(bn, t, 4, B_GROUPS, B_HEAD_DIM).transpose(2, 0, 3, 1, 4)
    kvr4 = jnp.pad(kvr4, ((0, 0), (0, 0), (0, 0), (KV_PAD, 0), (0, 0)))
    onehot = jnp.broadcast_to(jnp.asarray(_block_onehot(t, nblk), BF16), (bn, B_GROUPS, KV_PAD + t, nblk))
    ks = jnp.concatenate([kvr4[0], onehot], axis=-1)

    twin, tsel, tcmp = _bias_tables(rel_bias)
    tc_hi = tcmp.astype(BF16)
    tc_lo = (tcmp - tc_hi.astype(F32)).astype(BF16)
    tcmp2 = jnp.concatenate([tc_hi, tc_lo], axis=-1)
    ov = jnp.asarray(_overlap_matrix(ncp, n_cmp, n_slc, nblk))

    oc, osel, ow = _nsa(q, kcv, ks, kvr4[1], kvr4[2], kvr4[3], tcmp2, tsel, twin, ov)

    return _final(oc, osel, ow, z_b, gates, x1, gate[1], b_out_w[0].astype(BF16), final_g[None, :])
```

```python
import functools
import math

import numpy as np
import jax
import jax.numpy as jnp
from jax import lax
from jax.experimental import pallas as pl
from jax.experimental.pallas import tpu as pltpu

F32 = jnp.float32
BF16 = jnp.bfloat16
HIGHEST = lax.Precision.HIGHEST

A_HEADS = 8
A_HEAD_DIM = 128
A_WIDTH = A_HEADS * A_HEAD_DIM
A_CONV = 4
A_CHUNK = 64
B_HEADS = 16
B_GROUPS = 2
B_HPG = B_HEADS // B_GROUPS
B_HEAD_DIM = 64
B_WIDTH = B_HEADS * B_HEAD_DIM
N_BRANCH = 3
L_CMP = 32
CMP_STRIDE = 16
L_SLC = 64
N_SEL = 16
WINDOW = 512
Q_BLOCK = 64
NUM_BUCKETS = 32
MAX_DISTANCE = 128
EPS = 1e-6
NEG_INF = -1e30
SEL_BOOST = 1e9

LANES = 128
SUBLANES = 8
VMEM_LIMIT_BYTES = 56 * 1024 * 1024

ROW_TILE = 256
GDN_HEADS_PER_STEP = 8
SEL_KEY_TILE = 512
KV_PAD = WINDOW
CMP_NEAR = 16
BLK16 = 16


def _cparams(*sem):
    return pltpu.CompilerParams(dimension_semantics=sem, vmem_limit_bytes=VMEM_LIMIT_BYTES)


def _sigmoid(x):
    return 1.0 / (1.0 + jnp.exp(-x))


def _silu(x):
    return x * _sigmoid(x)


def _dot(a, b):
    return jnp.dot(a.astype(BF16), b.astype(BF16), preferred_element_type=F32)


def _dot_nt(a, b):
    return lax.dot_general(a.astype(BF16), b.astype(BF16), (((1,), (1,)), ((), ())),
                           preferred_element_type=F32)


def _dot_f32(a, b):
    return jnp.dot(a, b, precision=HIGHEST, preferred_element_type=F32)


def _rms(x, g):
    ms = jnp.mean(x * x, axis=-1, keepdims=True)
    return x * lax.rsqrt(ms + EPS) * g


def _ada_kernel(c_ref, w_ref, b_ref, o_ref):
    o_ref[...] = _dot_f32(_silu(c_ref[...]), w_ref[...]) + b_ref[...]


def _ada_modulation(c, ada_w, ada_b):
    depth, d, d3 = ada_w.shape
    bn = c.shape[0]
    return pl.pallas_call(
        _ada_kernel,
        grid=(depth, d3 // d),
        in_specs=[pl.BlockSpec((bn, d), lambda l, j: (0, 0)),
                  pl.BlockSpec((None, d, d), lambda l, j: (l, 0, j)),
                  pl.BlockSpec((None, 1, d), lambda l, j: (l, 0, j))],
        out_specs=pl.BlockSpec((None, bn, d), lambda l, j: (l, 0, j)),
        out_shape=jax.ShapeDtypeStruct((depth, bn, d3), F32),
        compiler_params=_cparams("arbitrary", "arbitrary"),
    )(c, ada_w, ada_b.reshape(depth, 1, d3))


def _in_proj_a_kernel(x_ref, g_ref, scale_ref, shift_ref, wqkv_ref, wz_ref, wba_ref,
                      qkv_ref, z_ref, ba_ref):
    h = _rms(x_ref[...], g_ref[...]) * (1.0 + scale_ref[...]) + shift_ref[...]
    hb = h.astype(BF16)
    qkv_ref[...] = jnp.dot(hb, wqkv_ref[...], preferred_element_type=F32)
    z_ref[...] = jnp.dot(hb, wz_ref[...], preferred_element_type=F32)
    ba_ref[...] = jnp.dot(hb, wba_ref[...], preferred_element_type=F32)


def _in_proj_a(x, g, scale, shift, wqkv, wz, wba):
    bn, t, d = x.shape
    tm = ROW_TILE
    row = lambda b, i: (b, i, 0)
    per_b = lambda b, i: (b, 0, 0)
    const = lambda b, i: (0, 0)
    nba = wba.shape[1]
    return pl.pallas_call(
        _in_proj_a_kernel,
        grid=(bn, t // tm),
        in_specs=[pl.BlockSpec((None, tm, d), row),
                  pl.BlockSpec((1, d), const),
                  pl.BlockSpec((None, 1, d), per_b),
                  pl.BlockSpec((None, 1, d), per_b),
                  pl.BlockSpec(wqkv.shape, const),
                  pl.BlockSpec(wz.shape, const),
                  pl.BlockSpec(wba.shape, const)],
        out_specs=[pl.BlockSpec((None, tm, 3 * A_WIDTH), row),
                   pl.BlockSpec((None, tm, A_WIDTH), row),
                   pl.BlockSpec((None, tm, nba), row)],
        out_shape=[jax.ShapeDtypeStruct((bn, t, 3 * A_WIDTH), F32),
                   jax.ShapeDtypeStruct((bn, t, A_WIDTH), F32),
                   jax.ShapeDtypeStruct((bn, t, nba), F32)],
        compiler_params=_cparams("arbitrary", "arbitrary"),
    )(x, g, scale, shift, wqkv, wz, wba)


def _cumsum_rows(x):
    n = x.shape[0]
    row = lax.broadcasted_iota(jnp.int32, x.shape, 0)
    s = 1
    while s < n:
        x = x + jnp.where(row >= s, pltpu.roll(x, s, axis=0), 0.0)
        s *= 2
    return x


def _unit_lower_inverse(ms):
    c = ms[0].shape[0]
    row = lax.broadcasted_iota(jnp.int32, (c, c), 0)
    col = lax.broadcasted_iota(jnp.int32, (c, c), 1)
    eye = (row == col).astype(F32)
    same_blk = (row & -BLK16) == (col & -BLK16)
    d = [jnp.where(same_blk, m, 0.0) for m in ms]
    mo = [m - x for m, x in zip(ms, d)]
    d2 = [_dot(x, x) for x in d]
    td = [eye - x for x in d]
    d4 = [_dot(x, x) for x in d2]
    td = [t + _dot(t, x) for t, x in zip(td, d2)]
    d8 = [_dot(x, x) for x in d4]
    td = [t + _dot(t, x) for t, x in zip(td, d4)]
    td = [t + _dot(t, x) for t, x in zip(td, d8)]
    n = [_dot(t, x) for t, x in zip(td, mo)]
    n2 = [_dot(x, x) for x in n]
    r = [eye - x for x in n]
    r = [a + _dot(a, x) for a, x in zip(r, n2)]
    return [_dot(a, t) for a, t in zip(r, td)]


def _gdn_kernel(q_ref, k_ref, v_ref, z_ref, ba_ref, cwq_ref, cwk_ref, cwv_ref, alog_ref, dtb_ref,
                ong_ref, o_ref, buf_ref, s_ref, *, hb):
    c = A_CHUNK
    dh = A_HEAD_DIM
    halo = SUBLANES
    n = pl.program_id(2)

    @pl.when(n == 0)
    def _():
        buf_ref[:, 0:halo, :] = jnp.zeros((3, halo, hb * dh), F32)
        s_ref[...] = jnp.zeros(s_ref.shape, F32)

    def conv_silu(idx, x_ref, cw_ref):
        buf_ref[idx, halo:halo + c, :] = x_ref[...]
        cw = cw_ref[...]
        off = halo - (A_CONV - 1)
        y = buf_ref[idx, off:off + c, :] * cw[0:1, :]
        for kk in range(1, A_CONV):
            y = y + buf_ref[idx, off + kk:off + kk + c, :] * cw[kk:kk + 1, :]
        buf_ref[idx, 0:halo, :] = x_ref[c - halo:c, :]
        return _silu(y)

    q_all = conv_silu(0, q_ref, cwq_ref)
    k_all = conv_silu(1, k_ref, cwk_ref)
    v_all = conv_silu(2, v_ref, cwv_ref)

    ba = ba_ref[...]
    beta_t = _sigmoid(ba)
    xa = ba + dtb_ref[...]
    softplus = jnp.maximum(xa, 0.0) + jnp.log(1.0 + jnp.exp(-jnp.abs(xa)))
    g_t = -jnp.exp(alog_ref[...]) * softplus
    gc_t = _cumsum_rows(g_t)
    gc_tt = gc_t.T
    egc_t = jnp.exp(gc_t)
    ekd_t = jnp.exp(gc_t[c - 1:c, :] - gc_t)
    egl_t = jnp.exp(gc_t[c - 1:c, :])

    row = lax.broadcasted_iota(jnp.int32, (c, c), 0)
    col = lax.broadcasted_iota(jnp.int32, (c, c), 1)
    incl = row >= col
    strict = row > col
    heads = range(hb)
    la = LANES // 2

    def head(x, i):
        return x[:, i * dh:(i + 1) * dh]

    def lane(x, i):
        return x[:, la + i:la + i + 1]

    def l2n(x):
        return x * lax.rsqrt(jnp.sum(x * x, axis=-1, keepdims=True) + EPS)

    qn = [l2n(head(q_all, i)) * (dh ** -0.5) for i in heads]
    kn = [l2n(head(k_all, i)) for i in heads]
    knb = [x.astype(BF16) for x in kn]
    beta = [beta_t[:, i:i + 1] for i in heads]
    kb = [kn[i] * beta[i] for i in heads]
    decay = [jnp.where(incl, jnp.exp(jnp.where(incl, lane(gc_t, i) - gc_tt[la + i:la + i + 1, :], 0.0)), 0.0)
             for i in heads]
    m = [jnp.where(strict, _dot_nt(kb[i], knb[i]) * decay[i], 0.0) for i in heads]
    attn = [(_dot_nt(qn[i], knb[i]) * decay[i]).astype(BF16) for i in heads]
    rhs = [jnp.concatenate([head(v_all, i) * beta[i], kb[i] * lane(egc_t, i)], axis=1).astype(BF16)
           for i in heads]
    qdec = [(qn[i] * lane(egc_t, i)).astype(BF16) for i in heads]
    kdec_t = [(kn[i] * lane(ekd_t, i)).T.astype(BF16) for i in heads]
    tinv = _unit_lower_inverse(m)
    uw = [_dot(tinv[i], rhs[i]) for i in heads]
    s_old = [s_ref[i] for i in heads]
    sb = [x.astype(BF16) for x in s_old]
    v_new = [uw[i][:, :dh] - _dot(uw[i][:, dh:], sb[i]) for i in heads]
    vnb = [x.astype(BF16) for x in v_new]
    for i in heads:
        s_ref[i] = s_old[i] * lane(egl_t, i) + jnp.dot(kdec_t[i], vnb[i], preferred_element_type=F32)
    o = [jnp.dot(qdec[i], sb[i], preferred_element_type=F32)
         + jnp.dot(attn[i], vnb[i], preferred_element_type=F32) for i in heads]
    for i in heads:
        sl = slice(i * dh, (i + 1) * dh)
        o_ref[:, sl] = _rms(o[i], ong_ref[...]) * _silu(z_ref[:, sl])


def _gdn(qkv, z, ba, conv_w, alog_row, dtb_row, onorm_g):
    bn, t, _ = qkv.shape
    hb = GDN_HEADS_PER_STEP
    nhg = A_HEADS // hb
    c = A_CHUNK
    w = hb * A_HEAD_DIM
    kern = functools.partial(_gdn_kernel, hb=hb)

    def qkv_spec(which):
        return pl.BlockSpec((None, c, w), lambda b, hg, n: (b, n, which * nhg + hg))

    def cw_spec(which):
        return pl.BlockSpec((A_CONV, w), lambda b, hg, n: (0, which * nhg + hg))

    hg_row = pl.BlockSpec((None, 1, LANES), lambda b, hg, n: (hg, 0, 0))
    return pl.pallas_call(
        kern,
        grid=(bn, nhg, t // c),
        in_specs=[qkv_spec(0), qkv_spec(1), qkv_spec(2),
                  pl.BlockSpec((None, c, w), lambda b, hg, n: (b, n, hg)),
                  pl.BlockSpec((None, c, LANES), lambda b, hg, n: (b, n, hg)),
                  cw_spec(0), cw_spec(1), cw_spec(2),
                  hg_row, hg_row,
                  pl.BlockSpec((1, A_HEAD_DIM), lambda b, hg, n: (0, 0))],
        out_specs=pl.BlockSpec((None, c, w), lambda b, hg, n: (b, n, hg)),
        out_shape=jax.ShapeDtypeStruct((bn, t, A_WIDTH), F32),
        scratch_shapes=[pltpu.VMEM((3, SUBLANES + c, w), F32),
                        pltpu.VMEM((hb, A_HEAD_DIM, A_HEAD_DIM), F32)],
        compiler_params=_cparams("arbitrary", "arbitrary", "arbitrary"),
    )(qkv, qkv, qkv, z, ba, conv_w, conv_w, conv_w, alog_row, dtb_row, onorm_g)


def _post_a_kernel(x_ref, og_ref, gate_ref, wo_ref, kvg_ref, wkvc_ref, wkvr_ref, g1_ref, scale_ref,
                   shift_ref, wq_ref, wz_ref, wg_ref,
                   x1_ref, kvc_ref, kvr_ref, q_ref, z_ref, gates_ref):
    out = _dot(og_ref[...], wo_ref[...])
    x1 = x_ref[...] + gate_ref[...] * out
    x1_ref[...] = x1
    sb = _rms(x1, kvg_ref[...]).astype(BF16)
    kvc_ref[...] = jnp.dot(sb, wkvc_ref[...], preferred_element_type=F32)
    kvr_ref[...] = jnp.dot(sb, wkvr_ref[...], preferred_element_type=F32).astype(BF16)
    h = _rms(x1, g1_ref[...]) * (1.0 + scale_ref[...]) + shift_ref[...]
    hb = h.astype(BF16)
    q = jnp.dot(hb, wq_ref[...], preferred_element_type=F32) * (B_HEAD_DIM ** -0.5)
    q_ref[...] = q.astype(BF16)
    z_ref[...] = jnp.dot(hb, wz_ref[...], preferred_element_type=F32)
    gates_ref[...] = _sigmoid(jnp.dot(hb, wg_ref[...], preferred_element_type=F32))


def _post_a(x, og, gate0, wo, kvg, wkvc, wkvr, g1, scale1, shift1, wq, wz, wg):
    bn, t, d = x.shape
    tm = ROW_TILE
    row = lambda b, i: (b, i, 0)
    per_b = lambda b, i: (b, 0, 0)
    const = lambda b, i: (0, 0)
    full = lambda a: pl.BlockSpec(a.shape, const)
    vec = pl.BlockSpec((1, d), const)
    bvec = pl.BlockSpec((None, 1, d), per_b)
    outs = [(d, F32), (wkvc.shape[1], F32), (wkvr.shape[1], BF16), (wq.shape[1], BF16),
            (wz.shape[1], F32), (wg.shape[1], F32)]
    return pl.pallas_call(
        _post_a_kernel,
        grid=(bn, t // tm),
        in_specs=[pl.BlockSpec((None, tm, d), row), pl.BlockSpec((None, tm, A_WIDTH), row), bvec,
                  full(wo), vec, full(wkvc), full(wkvr), vec, bvec, bvec, full(wq), full(wz), full(wg)],
        out_specs=[pl.BlockSpec((None, tm, n), row) for n, _ in outs],
        out_shape=[jax.ShapeDtypeStruct((bn, t, n), dt) for n, dt in outs],
        compiler_params=_cparams("arbitrary", "arbitrary"),
    )(x, og, gate0, wo, kvg, wkvc, wkvr, g1, scale1, shift1, wq, wz, wg)


def _compress_kernel(hv_ref, ptop_ref, pbot_ref, w1t_ref, w1b_ref, w2_ref, o_ref):
    hv = hv_ref[...]
    a = _dot(hv + ptop_ref[...], w1t_ref[...])
    b = _dot(hv + pbot_ref[...], w1b_ref[...])
    nrow = a.shape[0]
    hid = a + pltpu.roll(b, nrow - 1, axis=0)
    o_ref[...] = _dot(_silu(hid), w2_ref[...]).astype(o_ref.dtype)


def _compress(halves, ptop, pbot, w1t, w1b, w2):
    bn, four, nh, wd = halves.shape
    hid = w1t.shape[-1]
    per_kind = lambda b, j: (j // B_GROUPS, 0, 0)
    return pl.pallas_call(
        _compress_kernel,
        grid=(bn, four),
        in_specs=[pl.BlockSpec((None, None, nh, wd), lambda b, j: (b, j, 0, 0)),
                  pl.BlockSpec((None, 1, wd), per_kind),
                  pl.BlockSpec((None, 1, wd), per_kind),
                  pl.BlockSpec((None, wd, hid), per_kind),
                  pl.BlockSpec((None, wd, hid), per_kind),
                  pl.BlockSpec((None, hid, B_HEAD_DIM), per_kind)],
        out_specs=pl.BlockSpec((None, None, nh, B_HEAD_DIM), lambda b, j: (b, j, 0, 0)),
        out_shape=jax.ShapeDtypeStruct((bn, four, nh, B_HEAD_DIM), BF16),
        compiler_params=_cparams("arbitrary", "arbitrary"),
    )(halves, ptop, pbot, w1t, w1b, w2)


def _t5_bucket_np(dist):
    n = np.maximum(dist, 0)
    max_exact = NUM_BUCKETS // 2
    nf = np.maximum(n, 1).astype(np.float64)
    val = np.log(nf / max_exact) / math.log(MAX_DISTANCE / max_exact) * (NUM_BUCKETS - max_exact)
    frac = np.abs(val - np.round(val))
    safe = (frac > 1e-6) | (n <= max_exact) | (n >= MAX_DISTANCE)
    assert bool(np.all(safe)), "bucket boundary too close to an integer distance"
    large = np.minimum(max_exact + np.floor(np.maximum(val, 0.0)).astype(np.int64), NUM_BUCKETS - 1)
    return np.where(n < max_exact, n, large)


def _bias_onehot():
    r = np.arange(Q_BLOCK)[:, None]
    tiles = []
    j = np.arange(WINDOW + Q_BLOCK)[None, :]
    d = r + WINDOW - j
    tiles.append((d, (d >= 0) & (d < WINDOW)))
    j = np.arange(3 * L_SLC)[None, :]
    d = r + 2 * L_SLC - j
    tiles.append((d, d >= 0))
    j = np.arange(CMP_NEAR)[None, :]
    d = r - CMP_STRIDE * (j - 12) - (L_CMP - 1)
    tiles.append((d, d >= 0))
    cols = []
    for d, valid in tiles:
        bucket = np.where(valid, _t5_bucket_np(d), NUM_BUCKETS).reshape(-1)
        oh = np.zeros((2 * NUM_BUCKETS, bucket.size), np.float32)
        oh[bucket, np.arange(bucket.size)] = 1.0
        cols.append(oh)
    widths = [c.shape[1] for c in cols]
    return np.concatenate(cols, axis=1), widths


def _bias_kernel(rb_ref, oh_ref, o_ref):
    rb = rb_ref[...]
    lane = lax.broadcasted_iota(jnp.int32, rb.shape, 1)
    rbs = rb - rb[:, NUM_BUCKETS - 1:NUM_BUCKETS]
    rbs = jnp.where(lane < NUM_BUCKETS, rbs, jnp.where(lane == NUM_BUCKETS, NEG_INF, 0.0))
    o_ref[...] = _dot_f32(rbs, oh_ref[...])


def _bias_tables(rel_bias):
    oh, widths = _bias_onehot()
    ncol = oh.shape[1]
    nt = 8
    assert ncol % (nt * LANES) == 0
    tc = ncol // nt
    rb = jnp.concatenate([rel_bias.T, jnp.zeros((B_HEADS, NUM_BUCKETS), F32)], axis=1)
    flat = pl.pallas_call(
        _bias_kernel,
        grid=(nt,),
        in_specs=[pl.BlockSpec((B_HEADS, 2 * NUM_BUCKETS), lambda i: (0, 0)),
                  pl.BlockSpec((2 * NUM_BUCKETS, tc), lambda i: (0, i))],
        out_specs=pl.BlockSpec((B_HEADS, tc), lambda i: (0, i)),
        out_shape=jax.ShapeDtypeStruct((B_HEADS, ncol), F32),
        compiler_params=_cparams("arbitrary"),
    )(rb, jnp.asarray(oh))
    out, start = [], 0
    for wd in widths:
        tile = flat[:, start:start + wd].reshape(B_GROUPS, B_HPG * Q_BLOCK, wd // Q_BLOCK)
        out.append(tile)
        start += wd
    return out


def _nsa_kernel(q_ref, kc_ref, vc_ref, ks_ref, vs_ref, kw_ref, vw_ref, tcmp_ref, tsel_ref, twin_ref,
                ov_ref, oc_ref, os_ref, ow_ref):
    qb = Q_BLOCK
    dh = B_HEAD_DIM
    hpg = B_HPG
    rows = hpg * qb
    qi = pl.program_id(2)
    q0 = qi * qb
    qt = q_ref[...]
    q = jnp.concatenate([qt[:, h * dh:(h + 1) * dh] for h in range(hpg)], axis=0)

    def to_tokens(o):
        return jnp.concatenate([o[h * qb:(h + 1) * qb, :] for h in range(hpg)], axis=1)

    kc = kc_ref[...]
    ncp = kc.shape[0]
    s = _dot_nt(q, kc)
    first_near = (qb // CMP_STRIDE) * qi - 12
    cid = lax.broadcasted_iota(jnp.int32, (2 * CMP_NEAR, ncp), 1)
    jrow = lax.broadcasted_iota(jnp.int32, (2 * CMP_NEAR, ncp), 0) & (CMP_NEAR - 1)
    shift_eye = jnp.where(cid - first_near == jrow, 1.0, 0.0).astype(BF16)
    s = s + jnp.dot(tcmp_ref[...], shift_eye, preferred_element_type=F32)
    cvis = lax.broadcasted_iota(jnp.int32, (1, ncp), 1) < first_near + CMP_NEAR
    s = jnp.where(cvis, s, NEG_INF)
    live = s > 0.1 * NEG_INF
    m = jnp.max(s, axis=-1, keepdims=True)
    e = jnp.where(live, jnp.exp(s - m), 0.0)
    p = e / jnp.maximum(jnp.sum(e, axis=-1, keepdims=True), 1e-30)
    oc_ref[...] = to_tokens(_dot(p, vc_ref[...]))

    psum = p[0:qb, :]
    for h in range(1, hpg):
        psum = psum + p[h * qb:(h + 1) * qb, :]
    imp_t = _dot_f32(psum, ov_ref[...]).T
    nblk = imp_t.shape[0]
    blk = lax.broadcasted_iota(jnp.int32, (nblk, qb), 0)
    forced = (blk == 0) | (blk == qi) | (blk == qi - 1)
    val = jnp.where(forced, SEL_BOOST, jnp.where(blk > qi, -SEL_BOOST, imp_t))
    rank = jnp.zeros((nblk, qb), jnp.int32)
    for j in range(nblk):
        vj = val[j:j + 1, :]
        ahead = (vj > val) | ((vj == val) & (blk > j))
        rank = rank + ahead.astype(jnp.int32)
    sel_t = (rank < N_SEL) & (blk <= qi)
    far_t = jnp.where(sel_t & (blk <= qi - 3), 0.0, NEG_INF)
    near_t = jnp.where(sel_t & (blk >= qi - 2), 0.0, NEG_INF)

    def q_with_mask(mask_t):
        mk = mask_t.T.astype(BF16)
        return jnp.concatenate([q, jnp.concatenate([mk] * hpg, axis=0)], axis=1)

    q_far = q_with_mask(far_t)
    q_near = q_with_mask(near_t)

    kt_sz = SEL_KEY_TILE
    n_far_keys = jnp.maximum(qi - 2, 0) * L_SLC
    n_tiles = (n_far_keys + kt_sz - 1) // kt_sz

    def far_step(i, carry):
        m_i, l_i, acc = carry
        start = pl.multiple_of(KV_PAD + i * kt_sz, kt_sz)
        s_f = _dot_nt(q_far, ks_ref[pl.ds(start, kt_sz), :])
        m_n = jnp.maximum(m_i, jnp.max(s_f, axis=-1, keepdims=True))
        alpha = jnp.exp(m_i - m_n)
        e_f = jnp.exp(s_f - m_n)
        l_n = alpha * l_i + jnp.sum(e_f, axis=-1, keepdims=True)
        acc_n = alpha * acc + _dot(e_f, vs_ref[pl.ds(start, kt_sz), :])
        return m_n, l_n, acc_n

    init = (jnp.full((rows, 1), NEG_INF, F32), jnp.zeros((rows, 1), F32), jnp.zeros((rows, dh), F32))
    m_i, l_i, acc = lax.fori_loop(0, n_tiles, far_step, init)

    nk = 3 * L_SLC
    start = pl.multiple_of(KV_PAD + q0 - 2 * L_SLC, L_SLC)
    s_n = _dot_nt(q_near, ks_ref[pl.ds(start, nk), :]) + tsel_ref[...]
    exists = lax.broadcasted_iota(jnp.int32, (1, nk), 1) >= 2 * L_SLC - q0
    s_n = jnp.where(exists, s_n, NEG_INF)
    m_n = jnp.maximum(m_i, jnp.max(s_n, axis=-1, keepdims=True))
    alpha = jnp.exp(m_i - m_n)
    e_n = jnp.exp(s_n - m_n)
    l_n = alpha * l_i + jnp.sum(e_n, axis=-1, keepdims=True)
    acc = alpha * acc + _dot(e_n, vs_ref[pl.ds(start, nk), :])
    os_ref[...] = to_tokens(acc / l_n)

    nw = WINDOW + qb
    start = pl.multiple_of(q0, qb)
    s_w = _dot_nt(q, kw_ref[pl.ds(start, nw), :]) + twin_ref[...]
    exists = lax.broadcasted_iota(jnp.int32, (1, nw), 1) >= WINDOW - q0
    s_w = jnp.where(exists, s_w, NEG_INF)
    m_w = jnp.max(s_w, axis=-1, keepdims=True)
    e_w = jnp.exp(s_w - m_w)
    l_w = jnp.sum(e_w, axis=-1, keepdims=True)
    ow_ref[...] = to_tokens(_dot(e_w, vw_ref[pl.ds(start, nw), :]) / l_w)


def _nsa(q, kcv, ks, vs, kw, vw, tcmp, tsel, twin, ov):
    bn, t, _ = q.shape
    qb = Q_BLOCK
    gw = B_HPG * B_HEAD_DIM
    ncp = kcv.shape[2]
    tp = ks.shape[2]
    rows = B_HPG * qb
    per_bg = lambda b, g, i: (b, g, 0, 0)
    per_g = lambda b, g, i: (g, 0, 0)
    out_spec = pl.BlockSpec((None, qb, gw), lambda b, g, i: (b, i, g))
    out_sd = jax.ShapeDtypeStruct((bn, t, B_WIDTH), F32)
    return pl.pallas_call(
        _nsa_kernel,
        grid=(bn, B_GROUPS, t // qb),
        in_specs=[pl.BlockSpec((None, qb, gw), lambda b, g, i: (b, i, g)),
                  pl.BlockSpec((None, None, ncp, B_HEAD_DIM), lambda b, g, i: (b, g, 0, 0)),
                  pl.BlockSpec((None, None, ncp, B_HEAD_DIM), lambda b, g, i: (b, B_GROUPS + g, 0, 0)),
                  pl.BlockSpec((None, None, tp, ks.shape[3]), per_bg),
                  pl.BlockSpec((None, None, tp, B_HEAD_DIM), per_bg),
                  pl.BlockSpec((None, None, tp, B_HEAD_DIM), per_bg),
                  pl.BlockSpec((None, None, tp, B_HEAD_DIM), per_bg),
                  pl.BlockSpec((None, rows, tcmp.shape[2]), per_g),
                  pl.BlockSpec((None, rows, tsel.shape[2]), per_g),
                  pl.BlockSpec((None, rows, twin.shape[2]), per_g),
                  pl.BlockSpec(ov.shape, lambda b, g, i: (0, 0))],
        out_specs=[out_spec, out_spec, out_spec],
        out_shape=[out_sd, out_sd, out_sd],
        compiler_params=_cparams("arbitrary", "arbitrary", "arbitrary"),
    )(q, kcv, kcv, ks, vs, kw, vw, tcmp, tsel, twin, ov)


def _final_kernel(oc_ref, os_ref, ow_ref, z_ref, gates_ref, ex_ref, x1_ref, gate_ref, wo_ref, fg_ref, o_ref):
    gt = gates_ref[...]
    g_hi = gt.astype(BF16)
    g_lo = (gt - g_hi.astype(F32)).astype(BF16)
    ghl = jnp.concatenate([g_hi, g_lo], axis=1)
    y = None
    for br, o_ref_br in enumerate((oc_ref, os_ref, ow_ref)):
        gexp = jnp.dot(ghl, ex_ref[br], preferred_element_type=F32)
        term = gexp * o_ref_br[...] * _silu(z_ref[:, br * B_WIDTH:(br + 1) * B_WIDTH])
        y = term if y is None else y + term
    x2 = x1_ref[...] + gate_ref[...] * _dot(y, wo_ref[...])
    o_ref[...] = _rms(x2, fg_ref[...])


def _final(oc, osel, ow, z, gates, x1, gate1, wo, fg):
    bn, t, d = x1.shape
    tm = ROW_TILE
    ng = gates.shape[2]
    row = lambda b, i: (b, i, 0)
    ex = np.zeros((N_BRANCH, 2 * ng, B_WIDTH), np.float32)
    for br in range(N_BRANCH):
        for h in range(B_HEADS):
            ex[br, br * B_HEADS + h, h * B_HEAD_DIM:(h + 1) * B_HEAD_DIM] = 1.0
            ex[br, ng + br * B_HEADS + h, h * B_HEAD_DIM:(h + 1) * B_HEAD_DIM] = 1.0
    ex = jnp.asarray(ex, BF16)
    return pl.pallas_call(
        _final_kernel,
        grid=(bn, t // tm),
        in_specs=[pl.BlockSpec((None, tm, B_WIDTH), row)] * 3
        + [pl.BlockSpec((None, tm, N_BRANCH * B_WIDTH), row),
           pl.BlockSpec((None, tm, ng), row),
           pl.BlockSpec(ex.shape, lambda b, i: (0, 0, 0)),
           pl.BlockSpec((None, tm, d), row),
           pl.BlockSpec((None, 1, d), lambda b, i: (b, 0, 0)),
           pl.BlockSpec(wo.shape, lambda b, i: (0, 0)),
           pl.BlockSpec((1, d), lambda b, i: (0, 0))],
        out_specs=pl.BlockSpec((None, tm, d), row),
        out_shape=jax.ShapeDtypeStruct((bn, t, d), F32),
        compiler_params=_cparams("arbitrary", "arbitrary"),
    )(oc, osel, ow, z, gates, ex, x1, gate1, wo, fg)


def _overlap_matrix(ncp, n_cmp, n_slc, nblk):
    cells = np.arange(n_cmp)[:, None] + np.arange(L_CMP // CMP_STRIDE)[None, :]
    ov = (cells[:, None, :] // (L_SLC // CMP_STRIDE) == np.arange(n_slc)[None, :, None]).sum(-1)
    out = np.zeros((ncp, nblk), np.float32)
    out[:n_cmp, :n_slc] = ov
    return out


def _block_onehot(t, nblk):
    oh = np.zeros((KV_PAD + t, nblk), np.float32)
    oh[KV_PAD + np.arange(t), np.arange(t) // L_SLC] = 1.0
    return oh


def kernel(x, c, rel_bias, ada_w, ada_b, norm_g, a_in_w, a_conv_w, a_A_log, a_dt_bias, a_onorm_g, a_out_w,
           kv_norm_g, kv_w, cmp_pos_k, cmp_pos_v, cmp_k_w1, cmp_k_w2, cmp_v_w1, cmp_v_w2,
           b_in_w, b_out_w, final_g):
    bn, t, d = x.shape
    assert ada_w.shape[0] == 2 and a_in_w.shape[0] == 1 and b_in_w.shape[0] == 1
    assert t % max(ROW_TILE, SEL_KEY_TILE) == 0
    n_slc = t // L_SLC
    nblk = 64
    assert n_slc <= nblk
    n_cmp = (t - L_CMP) // CMP_STRIDE + 1
    ncp = t // CMP_STRIDE

    mod = _ada_modulation(c, ada_w, ada_b)
    shift = mod[:, :, None, :d]
    scale = mod[:, :, None, d:2 * d]
    gate = mod[:, :, None, 2 * d:]

    hb = GDN_HEADS_PER_STEP
    nhg = A_HEADS // hb
    w_in = a_in_w[0]
    wqkv = w_in[:, :3 * A_WIDTH].astype(BF16)
    wz = w_in[:, 3 * A_WIDTH:4 * A_WIDTH].astype(BF16)
    wb = w_in[:, 4 * A_WIDTH:4 * A_WIDTH + A_HEADS]
    wa = w_in[:, 4 * A_WIDTH + A_HEADS:]
    half = LANES // 2
    wba = jnp.zeros((d, nhg, LANES), F32)
    wba = wba.at[:, :, :hb].set(wb.reshape(d, nhg, hb)).at[:, :, half:half + hb].set(wa.reshape(d, nhg, hb))
    wba = wba.reshape(d, nhg * LANES).astype(BF16)
    lane_rows = lambda v: jnp.zeros((nhg, 1, LANES), F32).at[:, 0, half:half + hb].set(v.reshape(nhg, hb))
    qkv, z_a, ba = _in_proj_a(x, norm_g[0:1], scale[0], shift[0], wqkv, wz, wba)
    og = _gdn(qkv, z_a, ba, a_conv_w[0], lane_rows(a_A_log[0]), lane_rows(a_dt_bias[0]), a_onorm_g[0:1])

    ndh = B_GROUPS * B_HEAD_DIM
    wkvc = kv_w[:, :2 * ndh].astype(BF16)
    wkvr = kv_w[:, 2 * ndh:].astype(BF16)
    w_b = b_in_w[0]
    wq = w_b[:, :B_WIDTH].astype(BF16)
    wzb = w_b[:, B_WIDTH:4 * B_WIDTH].astype(BF16)
    wg = jnp.zeros((d, LANES), F32).at[:, :N_BRANCH * B_HEADS].set(w_b[:, 4 * B_WIDTH:]).astype(BF16)
    x1, kvc, kvr, q, z_b, gates = _post_a(x, og, gate[0], a_out_w[0].astype(BF16), kv_norm_g[None, :], wkvc, wkvr,
                                          norm_g[1:2], scale[1], shift[1], wq, wzb, wg)

    halves = kvc.reshape(bn, ncp, CMP_STRIDE, 2 * B_GROUPS, B_HEAD_DIM).transpose(0, 3, 1, 2, 4)
    halves = halves.reshape(bn, 2 * B_GROUPS, ncp, CMP_STRIDE * B_HEAD_DIM)
    pos = jnp.stack([cmp_pos_k, cmp_pos_v])
    hw = CMP_STRIDE * B_HEAD_DIM
    ptop = pos[:, :CMP_STRIDE].reshape(2, 1, hw)
    pbot = pos[:, CMP_STRIDE:].reshape(2, 1, hw)
    w1 = jnp.stack([cmp_k_w1, cmp_v_w1]).astype(BF16)
    w2 = jnp.stack([cmp_k_w2, cmp_v_w2]).astype(BF16)
    kcv = _compress(halves, ptop, pbot, w1[:, :hw], w1[:, hw:], w2)

    kvr4 = kvr.reshape(bn, t, 4, B_GROUPS, B_HEAD_DIM).transpose(2, 0, 3, 1, 4)
    kvr4 = jnp.pad(kvr4, ((0, 0), (0, 0), (0, 0), (KV_PAD, 0), (0, 0)))
    onehot = jnp.broadcast_to(jnp.asarray(_block_onehot(t, nblk), BF16), (bn, B_GROUPS, KV_PAD + t, nblk))
    ks = jnp.concatenate([kvr4[0], onehot], axis=-1)

    twin, tsel, tcmp = _bias_tables(rel_bias)
    tc_hi = tcmp.astype(BF16)
    tc_lo = (tcmp - tc_hi.astype(F32)).astype(BF16)
    tcmp2 = jnp.concatenate([tc_hi, tc_lo], axis=-1)
    ov = jnp.asarray(_overlap_matrix(ncp, n_cmp, n_slc, nblk))

    oc, osel, ow = _nsa(q, kcv, ks, kvr4[1], kvr4[2], kvr4[3], tcmp2, tsel, twin, ov)

    return _final(oc, osel, ow, z_b, gates, x1, gate[1], b_out_w[0].astype(BF16), final_g[None, :])
```

```python
import functools
import math

import numpy as np
import jax
import jax.numpy as jnp
from jax import lax
from jax.experimental import pallas as pl
from jax.experimental.pallas import tpu as pltpu

F32 = jnp.float32
BF16 = jnp.bfloat16
HIGHEST = lax.Precision.HIGHEST

A_HEADS = 8
A_HEAD_DIM = 128
A_WIDTH = A_HEADS * A_HEAD_DIM
A_CONV = 4
A_CHUNK = 64
B_HEADS = 16
B_GROUPS = 2
B_HPG = B_HEADS // B_GROUPS
B_HEAD_DIM = 64
B_WIDTH = B_HEADS * B_HEAD_DIM
N_BRANCH = 3
L_CMP = 32
CMP_STRIDE = 16
L_SLC = 64
N_SEL = 16
WINDOW = 512
Q_BLOCK = 64
NUM_BUCKETS = 32
MAX_DISTANCE = 128
EPS = 1e-6
NEG_INF = -1e30
SEL_BOOST = 1e9

LANES = 128
SUBLANES = 8
VMEM_LIMIT_BYTES = 56 * 1024 * 1024

ROW_TILE = 256
GDN_HEADS_PER_STEP = 8
SEL_KEY_TILE = 512
KV_PAD = WINDOW
NSA_Q_TILE = 128
NSA_SUB = NSA_Q_TILE // Q_BLOCK
SLC_SHIFT = L_SLC.bit_length() - 1
assert 1 << SLC_SHIFT == L_SLC and L_SLC == Q_BLOCK
CMP_LEAD = 12
CMP_NEAR = 32
assert CMP_NEAR >= CMP_LEAD + NSA_Q_TILE // CMP_STRIDE and NSA_Q_TILE % Q_BLOCK == 0
BLK16 = 16


def _cparams(*sem):
    return pltpu.CompilerParams(dimension_semantics=sem, vmem_limit_bytes=VMEM_LIMIT_BYTES)


def _sigmoid(x):
    return 1.0 / (1.0 + jnp.exp(-x))


def _silu(x):
    return x * _sigmoid(x)


def _dot(a, b):
    return jnp.dot(a.astype(BF16), b.astype(BF16), preferred_element_type=F32)


def _dot_nt(a, b):
    return lax.dot_general(a.astype(BF16), b.astype(BF16), (((1,), (1,)), ((), ())),
                           preferred_element_type=F32)


def _dot_f32(a, b):
    return jnp.dot(a, b, precision=HIGHEST, preferred_element_type=F32)


def _rms(x, g):
    ms = jnp.mean(x * x, axis=-1, keepdims=True)
    return x * lax.rsqrt(ms + EPS) * g


def _ada_kernel(c_ref, w_ref, b_ref, o_ref):
    o_ref[...] = _dot_f32(_silu(c_ref[...]), w_ref[...]) + b_ref[...]


def _ada_modulation(c, ada_w, ada_b):
    depth, d, d3 = ada_w.shape
    bn = c.shape[0]
    return pl.pallas_call(
        _ada_kernel,
        grid=(depth, d3 // d),
        in_specs=[pl.BlockSpec((bn, d), lambda l, j: (0, 0)),
                  pl.BlockSpec((None, d, d), lambda l, j: (l, 0, j)),
                  pl.BlockSpec((None, 1, d), lambda l, j: (l, 0, j))],
        out_specs=pl.BlockSpec((None, bn, d), lambda l, j: (l, 0, j)),
        out_shape=jax.ShapeDtypeStruct((depth, bn, d3), F32),
        compiler_params=_cparams("arbitrary", "arbitrary"),
    )(c, ada_w, ada_b.reshape(depth, 1, d3))


def _in_proj_a_kernel(x_ref, g_ref, scale_ref, shift_ref, wqkv_ref, wz_ref, wba_ref,
                      qkv_ref, z_ref, ba_ref):
    h = _rms(x_ref[...], g_ref[...]) * (1.0 + scale_ref[...]) + shift_ref[...]
    hb = h.astype(BF16)
    qkv_ref[...] = jnp.dot(hb, wqkv_ref[...], preferred_element_type=F32)
    z_ref[...] = jnp.dot(hb, wz_ref[...], preferred_element_type=F32)
    ba_ref[...] = jnp.dot(hb, wba_ref[...], preferred_element_type=F32)


def _in_proj_a(x, g, scale, shift, wqkv, wz, wba):
    bn, t, d = x.shape
    tm = ROW_TILE
    row = lambda b, i: (b, i, 0)
    per_b = lambda b, i: (b, 0, 0)
    const = lambda b, i: (0, 0)
    nba = wba.shape[1]
    return pl.pallas_call(
        _in_proj_a_kernel,
        grid=(bn, t // tm),
        in_specs=[pl.BlockSpec((None, tm, d), row),
                  pl.BlockSpec((1, d), const),
                  pl.BlockSpec((None, 1, d), per_b),
                  pl.BlockSpec((None, 1, d), per_b),
                  pl.BlockSpec(wqkv.shape, const),
                  pl.BlockSpec(wz.shape, const),
                  pl.BlockSpec(wba.shape, const)],
        out_specs=[pl.BlockSpec((None, tm, 3 * A_WIDTH), row),
                   pl.BlockSpec((None, tm, A_WIDTH), row),
                   pl.BlockSpec((None, tm, nba), row)],
        out_shape=[jax.ShapeDtypeStruct((bn, t, 3 * A_WIDTH), F32),
                   jax.ShapeDtypeStruct((bn, t, A_WIDTH), F32),
                   jax.ShapeDtypeStruct((bn, t, nba), F32)],
        compiler_params=_cparams("arbitrary", "arbitrary"),
    )(x, g, scale, shift, wqkv, wz, wba)


def _cumsum_rows(x):
    n = x.shape[0]
    row = lax.broadcasted_iota(jnp.int32, x.shape, 0)
    s = 1
    while s < n:
        x = x + jnp.where(row >= s, pltpu.roll(x, s, axis=0), 0.0)
        s *= 2
    return x


def _unit_lower_inverse(ms):
    c = ms[0].shape[0]
    row = lax.broadcasted_iota(jnp.int32, (c, c), 0)
    col = lax.broadcasted_iota(jnp.int32, (c, c), 1)
    eye = (row == col).astype(F32)
    same_blk = (row & -BLK16) == (col & -BLK16)
    d = [jnp.where(same_blk, m, 0.0) for m in ms]
    mo = [m - x for m, x in zip(ms, d)]
    d2 = [_dot(x, x) for x in d]
    td = [eye - x for x in d]
    d4 = [_dot(x, x) for x in d2]
    td = [t + _dot(t, x) for t, x in zip(td, d2)]
    d8 = [_dot(x, x) for x in d4]
    td = [t + _dot(t, x) for t, x in zip(td, d4)]
    td = [t + _dot(t, x) for t, x in zip(td, d8)]
    n = [_dot(t, x) for t, x in zip(td, mo)]
    n2 = [_dot(x, x) for x in n]
    r = [eye - x for x in n]
    r = [a + _dot(a, x) for a, x in zip(r, n2)]
    return [_dot(a, t) for a, t in zip(r, td)]


def _gdn_kernel(q_ref, k_ref, v_ref, z_ref, ba_ref, cwq_ref, cwk_ref, cwv_ref, alog_ref, dtb_ref,
                ong_ref, o_ref, buf_ref, s_ref, *, hb):
    c = A_CHUNK
    dh = A_HEAD_DIM
    halo = SUBLANES
    n = pl.program_id(2)

    @pl.when(n == 0)
    def _():
        buf_ref[:, 0:halo, :] = jnp.zeros((3, halo, hb * dh), F32)
        s_ref[...] = jnp.zeros(s_ref.shape, F32)

    def conv_silu(idx, x_ref, cw_ref):
        buf_ref[idx, halo:halo + c, :] = x_ref[...]
        cw = cw_ref[...]
        off = halo - (A_CONV - 1)
        y = buf_ref[idx, off:off + c, :] * cw[0:1, :]
        for kk in range(1, A_CONV):
            y = y + buf_ref[idx, off + kk:off + kk + c, :] * cw[kk:kk + 1, :]
        buf_ref[idx, 0:halo, :] = x_ref[c - halo:c, :]
        return _silu(y)

    q_all = conv_silu(0, q_ref, cwq_ref)
    k_all = conv_silu(1, k_ref, cwk_ref)
    v_all = conv_silu(2, v_ref, cwv_ref)

    ba = ba_ref[...]
    beta_t = _sigmoid(ba)
    xa = ba + dtb_ref[...]
    softplus = jnp.maximum(xa, 0.0) + jnp.log(1.0 + jnp.exp(-jnp.abs(xa)))
    g_t = -jnp.exp(alog_ref[...]) * softplus
    gc_t = _cumsum_rows(g_t)
    gc_tt = gc_t.T
    egc_t = jnp.exp(gc_t)
    ekd_t = jnp.exp(gc_t[c - 1:c, :] - gc_t)
    egl_t = jnp.exp(gc_t[c - 1:c, :])

    row = lax.broadcasted_iota(jnp.int32, (c, c), 0)
    col = lax.broadcasted_iota(jnp.int32, (c, c), 1)
    incl = row >= col
    strict = row > col
    heads = range(hb)
    la = LANES // 2

    def head(x, i):
        return x[:, i * dh:(i + 1) * dh]

    def lane(x, i):
        return x[:, la + i:la + i + 1]

    def l2n(x):
        return x * lax.rsqrt(jnp.sum(x * x, axis=-1, keepdims=True) + EPS)

    qn = [l2n(head(q_all, i)) * (dh ** -0.5) for i in heads]
    kn = [l2n(head(k_all, i)) for i in heads]
    knb = [x.astype(BF16) for x in kn]
    beta = [beta_t[:, i:i + 1] for i in heads]
    kb = [kn[i] * beta[i] for i in heads]
    decay = [jnp.where(incl, jnp.exp(jnp.where(incl, lane(gc_t, i) - gc_tt[la + i:la + i + 1, :], 0.0)), 0.0)
             for i in heads]
    m = [jnp.where(strict, _dot_nt(kb[i], knb[i]) * decay[i], 0.0) for i in heads]
    attn = [(_dot_nt(qn[i], knb[i]) * decay[i]).astype(BF16) for i in heads]
    rhs = [jnp.concatenate([head(v_all, i) * beta[i], kb[i] * lane(egc_t, i)], axis=1).astype(BF16)
           for i in heads]
    qdec = [(qn[i] * lane(egc_t, i)).astype(BF16) for i in heads]
    kdec_t = [(kn[i] * lane(ekd_t, i)).T.astype(BF16) for i in heads]
    tinv = _unit_lower_inverse(m)
    uw = [_dot(tinv[i], rhs[i]) for i in heads]
    s_old = [s_ref[i] for i in heads]
    sb = [x.astype(BF16) for x in s_old]
    v_new = [uw[i][:, :dh] - _dot(uw[i][:, dh:], sb[i]) for i in heads]
    vnb = [x.astype(BF16) for x in v_new]
    for i in heads:
        s_ref[i] = s_old[i] * lane(egl_t, i) + jnp.dot(kdec_t[i], vnb[i], preferred_element_type=F32)
    o = [jnp.dot(qdec[i], sb[i], preferred_element_type=F32)
         + jnp.dot(attn[i], vnb[i], preferred_element_type=F32) for i in heads]
    for i in heads:
        sl = slice(i * dh, (i + 1) * dh)
        o_ref[:, sl] = _rms(o[i], ong_ref[...]) * _silu(z_ref[:, sl])


def _gdn(qkv, z, ba, conv_w, alog_row, dtb_row, onorm_g):
    bn, t, _ = qkv.shape
    hb = GDN_HEADS_PER_STEP
    nhg = A_HEADS // hb
    c = A_CHUNK
    w = hb * A_HEAD_DIM
    kern = functools.partial(_gdn_kernel, hb=hb)

    def qkv_spec(which):
        return pl.BlockSpec((None, c, w), lambda b, hg, n: (b, n, which * nhg + hg))

    def cw_spec(which):
        return pl.BlockSpec((A_CONV, w), lambda b, hg, n: (0, which * nhg + hg))

    hg_row = pl.BlockSpec((None, 1, LANES), lambda b, hg, n: (hg, 0, 0))
    return pl.pallas_call(
        kern,
        grid=(bn, nhg, t // c),
        in_specs=[qkv_spec(0), qkv_spec(1), qkv_spec(2),
                  pl.BlockSpec((None, c, w), lambda b, hg, n: (b, n, hg)),
                  pl.BlockSpec((None, c, LANES), lambda b, hg, n: (b, n, hg)),
                  cw_spec(0), cw_spec(1), cw_spec(2),
                  hg_row, hg_row,
                  pl.BlockSpec((1, A_HEAD_DIM), lambda b, hg, n: (0, 0))],
        out_specs=pl.BlockSpec((None, c, w), lambda b, hg, n: (b, n, hg)),
        out_shape=jax.ShapeDtypeStruct((bn, t, A_WIDTH), F32),
        scratch_shapes=[pltpu.VMEM((3, SUBLANES + c, w), F32),
                        pltpu.VMEM((hb, A_HEAD_DIM, A_HEAD_DIM), F32)],
        compiler_params=_cparams("arbitrary", "arbitrary", "arbitrary"),
    )(qkv, qkv, qkv, z, ba, conv_w, conv_w, conv_w, alog_row, dtb_row, onorm_g)


def _post_a_kernel(x_ref, og_ref, gate_ref, wo_ref, kvg_ref, wkvc_ref, wkvr_ref, g1_ref, scale_ref,
                   shift_ref, wq_ref, wz_ref, wg_ref,
                   x1_ref, kvc_ref, kvr_ref, q_ref, z_ref, gates_ref):
    out = _dot(og_ref[...], wo_ref[...])
    x1 = x_ref[...] + gate_ref[...] * out
    x1_ref[...] = x1
    sb = _rms(x1, kvg_ref[...]).astype(BF16)
    kvc_ref[...] = jnp.dot(sb, wkvc_ref[...], preferred_element_type=F32)
    kvr_ref[...] = jnp.dot(sb, wkvr_ref[...], preferred_element_type=F32).astype(BF16)
    h = _rms(x1, g1_ref[...]) * (1.0 + scale_ref[...]) + shift_ref[...]
    hb = h.astype(BF16)
    q = jnp.dot(hb, wq_ref[...], preferred_element_type=F32) * (B_HEAD_DIM ** -0.5)
    q_ref[...] = q.astype(BF16)
    z_ref[...] = jnp.dot(hb, wz_ref[...], preferred_element_type=F32)
    gates_ref[...] = _sigmoid(jnp.dot(hb, wg_ref[...], preferred_element_type=F32))


def _post_a(x, og, gate0, wo, kvg, wkvc, wkvr, g1, scale1, shift1, wq, wz, wg):
    bn, t, d = x.shape
    tm = ROW_TILE
    row = lambda b, i: (b, i, 0)
    per_b = lambda b, i: (b, 0, 0)
    const = lambda b, i: (0, 0)
    full = lambda a: pl.BlockSpec(a.shape, const)
    vec = pl.BlockSpec((1, d), const)
    bvec = pl.BlockSpec((None, 1, d), per_b)
    outs = [(d, F32), (wkvc.shape[1], F32), (wkvr.shape[1], BF16), (wq.shape[1], BF16),
            (wz.shape[1], F32), (wg.shape[1], F32)]
    return pl.pallas_call(
        _post_a_kernel,
        grid=(bn, t // tm),
        in_specs=[pl.BlockSpec((None, tm, d), row), pl.BlockSpec((None, tm, A_WIDTH), row), bvec,
                  full(wo), vec, full(wkvc), full(wkvr), vec, bvec, bvec, full(wq), full(wz), full(wg)],
        out_specs=[pl.BlockSpec((None, tm, n), row) for n, _ in outs],
        out_shape=[jax.ShapeDtypeStruct((bn, t, n), dt) for n, dt in outs],
        compiler_params=_cparams("arbitrary", "arbitrary"),
    )(x, og, gate0, wo, kvg, wkvc, wkvr, g1, scale1, shift1, wq, wz, wg)


def _compress_kernel(hv_ref, ptop_ref, pbot_ref, w1t_ref, w1b_ref, w2_ref, o_ref):
    hv = hv_ref[...]
    a = _dot(hv + ptop_ref[...], w1t_ref[...])
    b = _dot(hv + pbot_ref[...], w1b_ref[...])
    nrow = a.shape[0]
    hid = a + pltpu.roll(b, nrow - 1, axis=0)
    o_ref[...] = _dot(_silu(hid), w2_ref[...]).astype(o_ref.dtype)


def _compress(halves, ptop, pbot, w1t, w1b, w2):
    bn, four, nh, wd = halves.shape
    hid = w1t.shape[-1]
    per_kind = lambda b, j: (j // B_GROUPS, 0, 0)
    return pl.pallas_call(
        _compress_kernel,
        grid=(bn, four),
        in_specs=[pl.BlockSpec((None, None, nh, wd), lambda b, j: (b, j, 0, 0)),
                  pl.BlockSpec((None, 1, wd), per_kind),
                  pl.BlockSpec((None, 1, wd), per_kind),
                  pl.BlockSpec((None, wd, hid), per_kind),
                  pl.BlockSpec((None, wd, hid), per_kind),
                  pl.BlockSpec((None, hid, B_HEAD_DIM), per_kind)],
        out_specs=pl.BlockSpec((None, None, nh, B_HEAD_DIM), lambda b, j: (b, j, 0, 0)),
        out_shape=jax.ShapeDtypeStruct((bn, four, nh, B_HEAD_DIM), BF16),
        compiler_params=_cparams("arbitrary", "arbitrary"),
    )(halves, ptop, pbot, w1t, w1b, w2)


def _t5_bucket_np(dist):
    n = np.maximum(dist, 0)
    max_exact = NUM_BUCKETS // 2
    nf = np.maximum(n, 1).astype(np.float64)
    val = np.log(nf / max_exact) / math.log(MAX_DISTANCE / max_exact) * (NUM_BUCKETS - max_exact)
    frac = np.abs(val - np.round(val))
    safe = (frac > 1e-6) | (n <= max_exact) | (n >= MAX_DISTANCE)
    assert bool(np.all(safe)), "bucket boundary too close to an integer distance"
    large = np.minimum(max_exact + np.floor(np.maximum(val, 0.0)).astype(np.int64), NUM_BUCKETS - 1)
    return np.where(n < max_exact, n, large)


def _bias_onehot():
    r = np.arange(NSA_Q_TILE)[:, None]
    tiles = []
    j = np.arange(WINDOW + NSA_Q_TILE)[None, :]
    d = r + WINDOW - j
    tiles.append((d, (d >= 0) & (d < WINDOW)))
    j = np.arange((NSA_SUB + 2) * L_SLC)[None, :]
    d = r + 2 * L_SLC - j
    tiles.append((d, d >= 0))
    j = np.arange(CMP_NEAR)[None, :]
    d = r - CMP_STRIDE * (j - CMP_LEAD) - (L_CMP - 1)
    tiles.append((d, d >= 0))
    cols = [np.where(valid, _t5_bucket_np(d), NUM_BUCKETS).reshape(-1) for d, valid in tiles]
    widths = [c.size for c in cols]
    return np.concatenate(cols).astype(np.int32)[None, :], widths


def _bias_kernel(rb_ref, bk_ref, o_ref):
    rb = rb_ref[...]
    lane = lax.broadcasted_iota(jnp.int32, rb.shape, 1)
    rbs = rb - rb[:, NUM_BUCKETS - 1:NUM_BUCKETS]
    rbs = jnp.where(lane < NUM_BUCKETS, rbs, jnp.where(lane == NUM_BUCKETS, NEG_INF, 0.0))
    bk = bk_ref[...]
    onehot = jnp.where(lax.broadcasted_iota(jnp.int32, (2 * NUM_BUCKETS, bk.shape[1]), 0) == bk, 1.0, 0.0)
    o_ref[...] = _dot_f32(rbs, onehot)


def _bias_tables(rel_bias):
    bk, widths = _bias_onehot()
    ncol = bk.shape[1]
    nt = 8
    assert ncol % (nt * LANES) == 0
    tc = ncol // nt
    rb = jnp.concatenate([rel_bias.T, jnp.zeros((B_HEADS, NUM_BUCKETS), F32)], axis=1)
    flat = pl.pallas_call(
        _bias_kernel,
        grid=(nt,),
        in_specs=[pl.BlockSpec((B_HEADS, 2 * NUM_BUCKETS), lambda i: (0, 0)),
                  pl.BlockSpec((1, tc), lambda i: (0, i))],
        out_specs=pl.BlockSpec((B_HEADS, tc), lambda i: (0, i)),
        out_shape=jax.ShapeDtypeStruct((B_HEADS, ncol), F32),
        compiler_params=_cparams("arbitrary"),
    )(rb, jnp.asarray(bk))
    out, start = [], 0
    for wd in widths:
        tile = flat[:, start:start + wd].reshape(B_GROUPS, B_HPG * NSA_Q_TILE, wd // NSA_Q_TILE)
        out.append(tile)
        start += wd
    return out


def _nsa_kernel(q_ref, kc_ref, vc_ref, ks_ref, vs_ref, kw_ref, vw_ref, tcmp_ref, tsel_ref, twin_ref,
                ov_ref, oc_ref, os_ref, ow_ref):
    tq = NSA_Q_TILE
    dh = B_HEAD_DIM
    hpg = B_HPG
    rows = hpg * tq
    ti = pl.program_id(2)
    q0 = ti * tq
    blk0 = ti * NSA_SUB
    qt = q_ref[...]
    q = jnp.concatenate([qt[:, h * dh:(h + 1) * dh] for h in range(hpg)], axis=0)

    def to_tokens(o):
        return jnp.concatenate([o[h * tq:(h + 1) * tq, :] for h in range(hpg)], axis=1)

    def finish(pv):
        return to_tokens(pv[:, :dh] * (1.0 / pv[:, dh:dh + 1]))

    kc = kc_ref[...]
    ncp = kc.shape[0]
    s = _dot_nt(q, kc)
    first_near = (tq // CMP_STRIDE) * ti - CMP_LEAD
    cid = lax.broadcasted_iota(jnp.int32, (2 * CMP_NEAR, ncp), 1)
    jrow = lax.broadcasted_iota(jnp.int32, (2 * CMP_NEAR, ncp), 0) & (CMP_NEAR - 1)
    shift_eye = jnp.where(cid - first_near == jrow, 1.0, 0.0).astype(BF16)
    s = s + jnp.dot(tcmp_ref[...], shift_eye, preferred_element_type=F32)
    cvis = lax.broadcasted_iota(jnp.int32, (1, ncp), 1) < first_near + CMP_NEAR
    s = jnp.where(cvis, s, NEG_INF)
    live = s > 0.1 * NEG_INF
    m = jnp.max(s, axis=-1, keepdims=True)
    e = jnp.where(live, jnp.exp(s - m), 0.0)
    p = e * (1.0 / jnp.maximum(jnp.sum(e, axis=-1, keepdims=True), 1e-30))
    oc_ref[...] = to_tokens(_dot(p, vc_ref[...]))

    psum = p[0:tq, :]
    for h in range(1, hpg):
        psum = psum + p[h * tq:(h + 1) * tq, :]
    p_hi = psum.astype(BF16)
    p_r1 = psum - p_hi.astype(F32)
    p_mid = p_r1.astype(BF16)
    p_lo = (p_r1 - p_mid.astype(F32)).astype(BF16)
    p3 = jnp.concatenate([p_hi, p_mid, p_lo], axis=1)
    imp_t = _dot_nt(ov_ref[...], p3)
    nblk = imp_t.shape[0]
    blk = lax.broadcasted_iota(jnp.int32, (nblk, tq), 0)
    cur = blk0 + (lax.broadcasted_iota(jnp.int32, (nblk, tq), 1) >> SLC_SHIFT)
    forced = (blk == 0) | (blk == cur) | (blk == cur - 1)
    val = jnp.where(forced, SEL_BOOST, jnp.where(blk > cur, -SEL_BOOST, imp_t))
    nslab = nblk // SUBLANES
    slabs = [val[SUBLANES * r:SUBLANES * (r + 1), :] for r in range(nslab)]
    sub = lax.broadcasted_iota(jnp.int32, (SUBLANES, tq), 0)
    ranks = [jnp.zeros((SUBLANES, tq), jnp.int32) for _ in range(nslab)]
    for j in range(nblk):
        vj = jnp.broadcast_to(val[j:j + 1, :], (SUBLANES, tq))
        for r in range(nslab):
            lo = SUBLANES * r
            if lo > j:
                ahead = vj >= slabs[r]
            elif lo + SUBLANES - 1 <= j:
                ahead = vj > slabs[r]
            else:
                ahead = (vj > slabs[r]) | ((vj == slabs[r]) & (sub > j - lo))
            ranks[r] = ranks[r] + ahead.astype(jnp.int32)
    rank = jnp.concatenate(ranks, axis=0)
    sel_t = (rank < N_SEL) & (blk <= cur)
    far_t = jnp.where(sel_t & (blk <= blk0 - 3), 0.0, NEG_INF)
    near_t = jnp.where(sel_t & (blk >= blk0 - 2), 0.0, NEG_INF)

    def q_with_mask(mask_t):
        mk = mask_t.T.astype(BF16)
        return jnp.concatenate([q, jnp.concatenate([mk] * hpg, axis=0)], axis=1)

    q_far = q_with_mask(far_t)
    q_near = q_with_mask(near_t)

    kt_sz = SEL_KEY_TILE
    n_far_keys = jnp.maximum(blk0 - 2, 0) * L_SLC
    n_tiles = (n_far_keys + kt_sz - 1) // kt_sz

    def far_step(i, carry):
        m_i, acc = carry
        start = pl.multiple_of(KV_PAD + i * kt_sz, kt_sz)
        s_f = _dot_nt(q_far, ks_ref[pl.ds(start, kt_sz), :])
        m_n = jnp.maximum(m_i, jnp.max(s_f, axis=-1, keepdims=True))
        e_f = jnp.exp(s_f - m_n)
        acc_n = jnp.exp(m_i - m_n) * acc + _dot(e_f, vs_ref[pl.ds(start, kt_sz), :])
        return m_n, acc_n

    init = (jnp.full((rows, 1), NEG_INF, F32), jnp.zeros((rows, 2 * dh), F32))
    m_i, acc = lax.fori_loop(0, n_tiles, far_step, init)

    nk = (NSA_SUB + 2) * L_SLC
    start = pl.multiple_of(KV_PAD + q0 - 2 * L_SLC, L_SLC)
    s_n = _dot_nt(q_near, ks_ref[pl.ds(start, nk), :]) + tsel_ref[...]
    m_n = jnp.maximum(m_i, jnp.max(s_n, axis=-1, keepdims=True))
    e_n = jnp.exp(s_n - m_n)
    acc = jnp.exp(m_i - m_n) * acc + _dot(e_n, vs_ref[pl.ds(start, nk), :])
    os_ref[...] = finish(acc)

    nw = WINDOW + tq
    start = pl.multiple_of(q0, tq)
    pad_col = jnp.where(lax.broadcasted_iota(jnp.int32, (rows, dh), 1) == 0, NEG_INF, 0.0).astype(BF16)
    q_win = jnp.concatenate([q, pad_col], axis=1)
    s_w = _dot_nt(q_win, kw_ref[pl.ds(start, nw), :]) + twin_ref[...]
    m_w = jnp.max(s_w, axis=-1, keepdims=True)
    e_w = jnp.exp(s_w - m_w)
    ow_ref[...] = finish(_dot(e_w, vw_ref[pl.ds(start, nw), :]))


def _nsa(q, kcv, ks, vs, kw, vw, tcmp, tsel, twin, ov):
    bn, t, _ = q.shape
    tq = NSA_Q_TILE
    gw = B_HPG * B_HEAD_DIM
    ncp = kcv.shape[2]
    tp = ks.shape[2]
    rows = B_HPG * tq
    per_bg = pl.BlockSpec((None, None, tp, ks.shape[3]), lambda b, g, i: (b, g, 0, 0))
    per_g = lambda b, g, i: (g, 0, 0)
    out_spec = pl.BlockSpec((None, tq, gw), lambda b, g, i: (b, i, g))
    out_sd = jax.ShapeDtypeStruct((bn, t, B_WIDTH), F32)
    return pl.pallas_call(
        _nsa_kernel,
        grid=(bn, B_GROUPS, t // tq),
        in_specs=[pl.BlockSpec((None, tq, gw), lambda b, g, i: (b, i, g)),
                  pl.BlockSpec((None, None, ncp, B_HEAD_DIM), lambda b, g, i: (b, g, 0, 0)),
                  pl.BlockSpec((None, None, ncp, B_HEAD_DIM), lambda b, g, i: (b, B_GROUPS + g, 0, 0)),
                  per_bg, per_bg, per_bg, per_bg,
                  pl.BlockSpec((None, rows, tcmp.shape[2]), per_g),
                  pl.BlockSpec((None, rows, tsel.shape[2]), per_g),
                  pl.BlockSpec((None, rows, twin.shape[2]), per_g),
                  pl.BlockSpec(ov.shape, lambda b, g, i: (0, 0))],
        out_specs=[out_spec, out_spec, out_spec],
        out_shape=[out_sd, out_sd, out_sd],
        compiler_params=_cparams("arbitrary", "arbitrary", "arbitrary"),
    )(q, kcv, kcv, ks, vs, kw, vw, tcmp, tsel, twin, ov)


def _final_kernel(oc_ref, os_ref, ow_ref, z_ref, gates_ref, ex_ref, x1_ref, gate_ref, wo_ref, fg_ref, o_ref):
    gt = gates_ref[...]
    g_hi = gt.astype(BF16)
    g_lo = (gt - g_hi.astype(F32)).astype(BF16)
    ghl = jnp.concatenate([g_hi, g_lo], axis=1)
    y = None
    for br, o_ref_br in enumerate((oc_ref, os_ref, ow_ref)):
        gexp = jnp.dot(ghl, ex_ref[br], preferred_element_type=F32)
        term = gexp * o_ref_br[...] * _silu(z_ref[:, br * B_WIDTH:(br + 1) * B_WIDTH])
        y = term if y is None else y + term
    x2 = x1_ref[...] + gate_ref[...] * _dot(y, wo_ref[...])
    o_ref[...] = _rms(x2, fg_ref[...])


def _final(oc, osel, ow, z, gates, x1, gate1, wo, fg):
    bn, t, d = x1.shape
    tm = ROW_TILE
    ng = gates.shape[2]
    row = lambda b, i: (b, i, 0)
    ex = np.zeros((N_BRANCH, 2 * ng, B_WIDTH), np.float32)
    for br in range(N_BRANCH):
        for h in range(B_HEADS):
            ex[br, br * B_HEADS + h, h * B_HEAD_DIM:(h + 1) * B_HEAD_DIM] = 1.0
            ex[br, ng + br * B_HEADS + h, h * B_HEAD_DIM:(h + 1) * B_HEAD_DIM] = 1.0
    ex = jnp.asarray(ex, BF16)
    return pl.pallas_call(
        _final_kernel,
        grid=(bn, t // tm),
        in_specs=[pl.BlockSpec((None, tm, B_WIDTH), row)] * 3
        + [pl.BlockSpec((None, tm, N_BRANCH * B_WIDTH), row),
           pl.BlockSpec((None, tm, ng), row),
           pl.BlockSpec(ex.shape, lambda b, i: (0, 0, 0)),
           pl.BlockSpec((None, tm, d), row),
           pl.BlockSpec((None, 1, d), lambda b, i: (b, 0, 0)),
           pl.BlockSpec(wo.shape, lambda b, i: (0, 0)),
           pl.BlockSpec((1, d), lambda b, i: (0, 0))],
        out_specs=pl.BlockSpec((None, tm, d), row),
        out_shape=jax.ShapeDtypeStruct((bn, t, d), F32),
        compiler_params=_cparams("arbitrary", "arbitrary"),
    )(oc, osel, ow, z, gates, ex, x1, gate1, wo, fg)


def _overlap_matrix(ncp, n_cmp, n_slc, nblk):
    cells = np.arange(n_cmp)[:, None] + np.arange(L_CMP // CMP_STRIDE)[None, :]
    ov = (cells[:, None, :] // (L_SLC // CMP_STRIDE) == np.arange(n_slc)[None, :, None]).sum(-1)
    out = np.zeros((ncp, nblk), np.float32)
    out[:n_cmp, :n_slc] = ov
    return out


def _block_onehot(t, nblk):
    oh = np.zeros((KV_PAD + t, nblk), np.float32)
    oh[KV_PAD + np.arange(t), np.arange(t) // L_SLC] = 1.0
    oh[:KV_PAD, nblk - 1] = 1.0
    return oh


def kernel(x, c, rel_bias, ada_w, ada_b, norm_g, a_in_w, a_conv_w, a_A_log, a_dt_bias, a_onorm_g, a_out_w,
           kv_norm_g, kv_w, cmp_pos_k, cmp_pos_v, cmp_k_w1, cmp_k_w2, cmp_v_w1, cmp_v_w2,
           b_in_w, b_out_w, final_g):
    bn, t, d = x.shape
    assert ada_w.shape[0] == 2 and a_in_w.shape[0] == 1 and b_in_w.shape[0] == 1
    assert t % max(ROW_TILE, SEL_KEY_TILE) == 0
    n_slc = t // L_SLC
    nblk = 64
    assert n_slc <= nblk
    n_cmp = (t - L_CMP) // CMP_STRIDE + 1
    ncp = t // CMP_STRIDE

    mod = _ada_modulation(c, ada_w, ada_b)
    shift = mod[:, :, None, :d]
    scale = mod[:, :, None, d:2 * d]
    gate = mod[:, :, None, 2 * d:]

    hb = GDN_HEADS_PER_STEP
    nhg = A_HEADS // hb
    w_in = a_in_w[0]
    wqkv = w_in[:, :3 * A_WIDTH].astype(BF16)
    wz = w_in[:, 3 * A_WIDTH:4 * A_WIDTH].astype(BF16)
    wb = w_in[:, 4 * A_WIDTH:4 * A_WIDTH + A_HEADS]
    wa = w_in[:, 4 * A_WIDTH + A_HEADS:]
    half = LANES // 2
    wba = jnp.zeros((d, nhg, LANES), F32)
    wba = wba.at[:, :, :hb].set(wb.reshape(d, nhg, hb)).at[:, :, half:half + hb].set(wa.reshape(d, nhg, hb))
    wba = wba.reshape(d, nhg * LANES).astype(BF16)
    lane_rows = lambda v: jnp.zeros((nhg, 1, LANES), F32).at[:, 0, half:half + hb].set(v.reshape(nhg, hb))
    qkv, z_a, ba = _in_proj_a(x, norm_g[0:1], scale[0], shift[0], wqkv, wz, wba)
    og = _gdn(qkv, z_a, ba, a_conv_w[0], lane_rows(a_A_log[0]), lane_rows(a_dt_bias[0]), a_onorm_g[0:1])

    ndh = B_GROUPS * B_HEAD_DIM
    wkvc = kv_w[:, :2 * ndh].astype(BF16)
    wkvr = kv_w[:, 2 * ndh:].astype(BF16)
    w_b = b_in_w[0]
    wq = w_b[:, :B_WIDTH].astype(BF16)
    wzb = w_b[:, B_WIDTH:4 * B_WIDTH].astype(BF16)
    wg = jnp.zeros((d, LANES), F32).at[:, :N_BRANCH * B_HEADS].set(w_b[:, 4 * B_WIDTH:]).astype(BF16)
    x1, kvc, kvr, q, z_b, gates = _post_a(x, og, gate[0], a_out_w[0].astype(BF16), kv_norm_g[None, :], wkvc, wkvr,
                                          norm_g[1:2], scale[1], shift[1], wq, wzb, wg)

    halves = kvc.reshape(bn, ncp, CMP_STRIDE, 2 * B_GROUPS, B_HEAD_DIM).transpose(0, 3, 1, 2, 4)
    halves = halves.reshape(bn, 2 * B_GROUPS, ncp, CMP_STRIDE * B_HEAD_DIM)
    pos = jnp.stack([cmp_pos_k, cmp_pos_v])
    hw = CMP_STRIDE * B_HEAD_DIM
    ptop = pos[:, :CMP_STRIDE].reshape(2, 1, hw)
    pbot = pos[:, CMP_STRIDE:].reshape(2, 1, hw)
    w1 = jnp.stack([cmp_k_w1, cmp_v_w1]).astype(BF16)
    w2 = jnp.stack([cmp_k_w2, cmp_v_w2]).astype(BF16)
    kcv = _compress(halves, ptop, pbot, w1[:, :hw], w1[:, hw:], w2)

    kvr4 = kvr.reshape(bn, t, 4, B_GROUPS, B_HEAD_DIM).transpose(2, 0, 3, 1, 4)
    kvr4 = jnp.pad(kvr4, ((0, 0), (0, 0), (0, 0), (KV_PAD, 0), (0, 0)))
    tp = KV_PAD + t

    def beside(x, extra):
        return jnp.concatenate([x, jnp.broadcast_to(jnp.asarray(extra, BF16), x.shape[:2] + extra.shape)], axis=-1)

    first_col = np.zeros((tp, B_HEAD_DIM), np.float32)
    ones_col = first_col.copy()
    ones_col[:, 0] = 1.0
    pad_col = first_col.copy()
    pad_col[:KV_PAD, 0] = 1.0
    ks = beside(kvr4[0], _block_onehot(t, nblk))
    vs = beside(kvr4[1], ones_col)
    kw = beside(kvr4[2], pad_col)
    vw = beside(kvr4[3], ones_col)

    twin, tsel, tcmp = _bias_tables(rel_bias)
    tc_hi = tcmp.astype(BF16)
    tc_lo = (tcmp - tc_hi.astype(F32)).astype(BF16)
    tcmp2 = jnp.concatenate([tc_hi, tc_lo], axis=-1)
    ov_t = _overlap_matrix(ncp, n_cmp, n_slc, nblk).T
    ov3 = jnp.asarray(np.concatenate([ov_t] * 3, axis=1), BF16)

    oc, osel, ow = _nsa(q, kcv, ks, vs, kw, vw, tcmp2, tsel, twin, ov3)

    return _final(oc, osel, ow, z_b, gates, x1, gate[1], b_out_w[0].astype(BF16), final_g[None, :])
```

```python
import functools
import math

import numpy as np
import jax
import jax.numpy as jnp
from jax import lax
from jax.experimental import pallas as pl
from jax.experimental.pallas import tpu as pltpu

F32 = jnp.float32
BF16 = jnp.bfloat16
HIGHEST = lax.Precision.HIGHEST

A_HEADS = 8
A_HEAD_DIM = 128
A_WIDTH = A_HEADS * A_HEAD_DIM
A_CONV = 4
A_CHUNK = 64
B_HEADS = 16
B_GROUPS = 2
B_HPG = B_HEADS // B_GROUPS
B_HEAD_DIM = 64
B_WIDTH = B_HEADS * B_HEAD_DIM
N_BRANCH = 3
L_CMP = 32
CMP_STRIDE = 16
L_SLC = 64
N_SEL = 16
WINDOW = 512
Q_BLOCK = 64
NUM_BUCKETS = 32
MAX_DISTANCE = 128
EPS = 1e-6
NEG_INF = -1e30
SEL_BOOST = 1e9

LANES = 128
SUBLANES = 8
VMEM_LIMIT_BYTES = 56 * 1024 * 1024

ROW_TILE = 256
GDN_HEADS_PER_STEP = 8
SEL_KEY_TILE = 1024
KV_PAD = WINDOW
NSA_Q_TILE = 128
NSA_SUB = NSA_Q_TILE // Q_BLOCK
SLC_SHIFT = L_SLC.bit_length() - 1
assert 1 << SLC_SHIFT == L_SLC and L_SLC == Q_BLOCK
CMP_LEAD = 12
CMP_NEAR = 32
assert CMP_NEAR >= CMP_LEAD + NSA_Q_TILE // CMP_STRIDE and NSA_Q_TILE % Q_BLOCK == 0
BLK16 = 16


def _cparams(*sem):
    return pltpu.CompilerParams(dimension_semantics=sem, vmem_limit_bytes=VMEM_LIMIT_BYTES)


def _sigmoid(x):
    return 1.0 / (1.0 + jnp.exp(-x))


def _silu(x):
    return x * _sigmoid(x)


def _dot(a, b):
    return jnp.dot(a.astype(BF16), b.astype(BF16), preferred_element_type=F32)


def _dot_nt(a, b):
    return lax.dot_general(a.astype(BF16), b.astype(BF16), (((1,), (1,)), ((), ())),
                           preferred_element_type=F32)


def _dot_f32(a, b):
    return jnp.dot(a, b, precision=HIGHEST, preferred_element_type=F32)


def _rms(x, g):
    ms = jnp.mean(x * x, axis=-1, keepdims=True)
    return x * lax.rsqrt(ms + EPS) * g


def _ada_kernel(c_ref, w_ref, b_ref, o_ref):
    o_ref[...] = _dot_f32(_silu(c_ref[...]), w_ref[...]) + b_ref[...]


def _ada_modulation(c, ada_w, ada_b):
    depth, d, d3 = ada_w.shape
    bn = c.shape[0]
    return pl.pallas_call(
        _ada_kernel,
        grid=(depth, d3 // d),
        in_specs=[pl.BlockSpec((bn, d), lambda l, j: (0, 0)),
                  pl.BlockSpec((None, d, d), lambda l, j: (l, 0, j)),
                  pl.BlockSpec((None, 1, d), lambda l, j: (l, 0, j))],
        out_specs=pl.BlockSpec((None, bn, d), lambda l, j: (l, 0, j)),
        out_shape=jax.ShapeDtypeStruct((depth, bn, d3), F32),
        compiler_params=_cparams("arbitrary", "arbitrary"),
    )(c, ada_w, ada_b.reshape(depth, 1, d3))


def _in_proj_a_kernel(x_ref, g_ref, scale_ref, shift_ref, wqkv_ref, wz_ref, wba_ref,
                      qkv_ref, z_ref, ba_ref):
    h = _rms(x_ref[...], g_ref[...]) * (1.0 + scale_ref[...]) + shift_ref[...]
    hb = h.astype(BF16)
    qkv_ref[...] = jnp.dot(hb, wqkv_ref[...], preferred_element_type=F32)
    z_ref[...] = jnp.dot(hb, wz_ref[...], preferred_element_type=F32)
    ba_ref[...] = jnp.dot(hb, wba_ref[...], preferred_element_type=F32)


def _in_proj_a(x, g, scale, shift, wqkv, wz, wba):
    bn, t, d = x.shape
    tm = ROW_TILE
    row = lambda b, i: (b, i, 0)
    per_b = lambda b, i: (b, 0, 0)
    const = lambda b, i: (0, 0)
    nba = wba.shape[1]
    return pl.pallas_call(
        _in_proj_a_kernel,
        grid=(bn, t // tm),
        in_specs=[pl.BlockSpec((None, tm, d), row),
                  pl.BlockSpec((1, d), const),
                  pl.BlockSpec((None, 1, d), per_b),
                  pl.BlockSpec((None, 1, d), per_b),
                  pl.BlockSpec(wqkv.shape, const),
                  pl.BlockSpec(wz.shape, const),
                  pl.BlockSpec(wba.shape, const)],
        out_specs=[pl.BlockSpec((None, tm, 3 * A_WIDTH), row),
                   pl.BlockSpec((None, tm, A_WIDTH), row),
                   pl.BlockSpec((None, tm, nba), row)],
        out_shape=[jax.ShapeDtypeStruct((bn, t, 3 * A_WIDTH), F32),
                   jax.ShapeDtypeStruct((bn, t, A_WIDTH), F32),
                   jax.ShapeDtypeStruct((bn, t, nba), F32)],
        compiler_params=_cparams("arbitrary", "arbitrary"),
    )(x, g, scale, shift, wqkv, wz, wba)


def _cumsum_rows(x):
    n = x.shape[0]
    row = lax.broadcasted_iota(jnp.int32, x.shape, 0)
    s = 1
    while s < n:
        x = x + jnp.where(row >= s, pltpu.roll(x, s, axis=0), 0.0)
        s *= 2
    return x


def _unit_lower_inverse(ms):
    c = ms[0].shape[0]
    row = lax.broadcasted_iota(jnp.int32, (c, c), 0)
    col = lax.broadcasted_iota(jnp.int32, (c, c), 1)
    eye = (row == col).astype(F32)
    same_blk = (row & -BLK16) == (col & -BLK16)
    d = [jnp.where(same_blk, m, 0.0) for m in ms]
    mo = [m - x for m, x in zip(ms, d)]
    d2 = [_dot(x, x) for x in d]
    td = [eye - x for x in d]
    d4 = [_dot(x, x) for x in d2]
    td = [t + _dot(t, x) for t, x in zip(td, d2)]
    d8 = [_dot(x, x) for x in d4]
    td = [t + _dot(t, x) for t, x in zip(td, d4)]
    td = [t + _dot(t, x) for t, x in zip(td, d8)]
    n = [_dot(t, x) for t, x in zip(td, mo)]
    n2 = [_dot(x, x) for x in n]
    r = [eye - x for x in n]
    r = [a + _dot(a, x) for a, x in zip(r, n2)]
    return [_dot(a, t) for a, t in zip(r, td)]


def _gdn_kernel(q_ref, k_ref, v_ref, z_ref, ba_ref, cwq_ref, cwk_ref, cwv_ref, alog_ref, dtb_ref,
                ong_ref, o_ref, buf_ref, s_ref, *, hb):
    c = A_CHUNK
    dh = A_HEAD_DIM
    halo = SUBLANES
    n = pl.program_id(2)

    @pl.when(n == 0)
    def _():
        buf_ref[:, 0:halo, :] = jnp.zeros((3, halo, hb * dh), F32)
        s_ref[...] = jnp.zeros(s_ref.shape, F32)

    def conv_silu(idx, x_ref, cw_ref):
        buf_ref[idx, halo:halo + c, :] = x_ref[...]
        cw = cw_ref[...]
        off = halo - (A_CONV - 1)
        y = buf_ref[idx, off:off + c, :] * cw[0:1, :]
        for kk in range(1, A_CONV):
            y = y + buf_ref[idx, off + kk:off + kk + c, :] * cw[kk:kk + 1, :]
        buf_ref[idx, 0:halo, :] = x_ref[c - halo:c, :]
        return _silu(y)

    q_all = conv_silu(0, q_ref, cwq_ref)
    k_all = conv_silu(1, k_ref, cwk_ref)
    v_all = conv_silu(2, v_ref, cwv_ref)

    ba = ba_ref[...]
    beta_t = _sigmoid(ba)
    xa = ba + dtb_ref[...]
    softplus = jnp.maximum(xa, 0.0) + jnp.log(1.0 + jnp.exp(-jnp.abs(xa)))
    g_t = -jnp.exp(alog_ref[...]) * softplus
    gc_t = _cumsum_rows(g_t)
    gc_tt = gc_t.T
    egc_t = jnp.exp(gc_t)
    ekd_t = jnp.exp(gc_t[c - 1:c, :] - gc_t)
    egl_t = jnp.exp(gc_t[c - 1:c, :])

    row = lax.broadcasted_iota(jnp.int32, (c, c), 0)
    col = lax.broadcasted_iota(jnp.int32, (c, c), 1)
    incl = row >= col
    strict = row > col
    heads = range(hb)
    la = LANES // 2

    def head(x, i):
        return x[:, i * dh:(i + 1) * dh]

    def lane(x, i):
        return x[:, la + i:la + i + 1]

    def l2n(x):
        return x * lax.rsqrt(jnp.sum(x * x, axis=-1, keepdims=True) + EPS)

    qn = [l2n(head(q_all, i)) * (dh ** -0.5) for i in heads]
    kn = [l2n(head(k_all, i)) for i in heads]
    knb = [x.astype(BF16) for x in kn]
    beta = [beta_t[:, i:i + 1] for i in heads]
    kb = [kn[i] * beta[i] for i in heads]
    decay = [jnp.where(incl, jnp.exp(jnp.where(incl, lane(gc_t, i) - gc_tt[la + i:la + i + 1, :], 0.0)), 0.0)
             for i in heads]
    m = [jnp.where(strict, _dot_nt(kb[i], knb[i]) * decay[i], 0.0) for i in heads]
    attn = [(_dot_nt(qn[i], knb[i]) * decay[i]).astype(BF16) for i in heads]
    rhs = [jnp.concatenate([head(v_all, i) * beta[i], kb[i] * lane(egc_t, i)], axis=1).astype(BF16)
           for i in heads]
    qdec = [(qn[i] * lane(egc_t, i)).astype(BF16) for i in heads]
    kdec_t = [(kn[i] * lane(ekd_t, i)).T.astype(BF16) for i in heads]
    tinv = _unit_lower_inverse(m)
    uw = [_dot(tinv[i], rhs[i]) for i in heads]
    s_old = [s_ref[i] for i in heads]
    sb = [x.astype(BF16) for x in s_old]
    v_new = [uw[i][:, :dh] - _dot(uw[i][:, dh:], sb[i]) for i in heads]
    vnb = [x.astype(BF16) for x in v_new]
    for i in heads:
        s_ref[i] = s_old[i] * lane(egl_t, i) + jnp.dot(kdec_t[i], vnb[i], preferred_element_type=F32)
    o = [jnp.dot(qdec[i], sb[i], preferred_element_type=F32)
         + jnp.dot(attn[i], vnb[i], preferred_element_type=F32) for i in heads]
    for i in heads:
        sl = slice(i * dh, (i + 1) * dh)
        o_ref[:, sl] = _rms(o[i], ong_ref[...]) * _silu(z_ref[:, sl])


def _gdn(qkv, z, ba, conv_w, alog_row, dtb_row, onorm_g):
    bn, t, _ = qkv.shape
    hb = GDN_HEADS_PER_STEP
    nhg = A_HEADS // hb
    c = A_CHUNK
    w = hb * A_HEAD_DIM
    kern = functools.partial(_gdn_kernel, hb=hb)

    def qkv_spec(which):
        return pl.BlockSpec((None, c, w), lambda b, hg, n: (b, n, which * nhg + hg))

    def cw_spec(which):
        return pl.BlockSpec((A_CONV, w), lambda b, hg, n: (0, which * nhg + hg))

    hg_row = pl.BlockSpec((None, 1, LANES), lambda b, hg, n: (hg, 0, 0))
    return pl.pallas_call(
        kern,
        grid=(bn, nhg, t // c),
        in_specs=[qkv_spec(0), qkv_spec(1), qkv_spec(2),
                  pl.BlockSpec((None, c, w), lambda b, hg, n: (b, n, hg)),
                  pl.BlockSpec((None, c, LANES), lambda b, hg, n: (b, n, hg)),
                  cw_spec(0), cw_spec(1), cw_spec(2),
                  hg_row, hg_row,
                  pl.BlockSpec((1, A_HEAD_DIM), lambda b, hg, n: (0, 0))],
        out_specs=pl.BlockSpec((None, c, w), lambda b, hg, n: (b, n, hg)),
        out_shape=jax.ShapeDtypeStruct((bn, t, A_WIDTH), F32),
        scratch_shapes=[pltpu.VMEM((3, SUBLANES + c, w), F32),
                        pltpu.VMEM((hb, A_HEAD_DIM, A_HEAD_DIM), F32)],
        compiler_params=_cparams("arbitrary", "arbitrary", "arbitrary"),
    )(qkv, qkv, qkv, z, ba, conv_w, conv_w, conv_w, alog_row, dtb_row, onorm_g)


def _post_a_kernel(x_ref, og_ref, gate_ref, wo_ref, kvg_ref, wkvc_ref, wkvr_ref, g1_ref, scale_ref,
                   shift_ref, wq_ref, wz_ref, wg_ref,
                   x1_ref, kvc_ref, kvr_ref, q_ref, z_ref, gates_ref):
    out = _dot(og_ref[...], wo_ref[...])
    x1 = x_ref[...] + gate_ref[...] * out
    x1_ref[...] = x1
    sb = _rms(x1, kvg_ref[...]).astype(BF16)
    kvc_ref[...] = jnp.dot(sb, wkvc_ref[...], preferred_element_type=F32)
    kvr_ref[...] = jnp.dot(sb, wkvr_ref[...], preferred_element_type=F32).astype(BF16)
    h = _rms(x1, g1_ref[...]) * (1.0 + scale_ref[...]) + shift_ref[...]
    hb = h.astype(BF16)
    q = jnp.dot(hb, wq_ref[...], preferred_element_type=F32) * (B_HEAD_DIM ** -0.5)
    q_ref[...] = q.astype(BF16)
    z_ref[...] = jnp.dot(hb, wz_ref[...], preferred_element_type=F32)
    gates_ref[...] = _sigmoid(jnp.dot(hb, wg_ref[...], preferred_element_type=F32))


def _post_a(x, og, gate0, wo, kvg, wkvc, wkvr, g1, scale1, shift1, wq, wz, wg):
    bn, t, d = x.shape
    tm = ROW_TILE
    row = lambda b, i: (b, i, 0)
    per_b = lambda b, i: (b, 0, 0)
    const = lambda b, i: (0, 0)
    full = lambda a: pl.BlockSpec(a.shape, const)
    vec = pl.BlockSpec((1, d), const)
    bvec = pl.BlockSpec((None, 1, d), per_b)
    outs = [(d, F32), (wkvc.shape[1], F32), (wkvr.shape[1], BF16), (wq.shape[1], BF16),
            (wz.shape[1], F32), (wg.shape[1], F32)]
    return pl.pallas_call(
        _post_a_kernel,
        grid=(bn, t // tm),
        in_specs=[pl.BlockSpec((None, tm, d), row), pl.BlockSpec((None, tm, A_WIDTH), row), bvec,
                  full(wo), vec, full(wkvc), full(wkvr), vec, bvec, bvec, full(wq), full(wz), full(wg)],
        out_specs=[pl.BlockSpec((None, tm, n), row) for n, _ in outs],
        out_shape=[jax.ShapeDtypeStruct((bn, t, n), dt) for n, dt in outs],
        compiler_params=_cparams("arbitrary", "arbitrary"),
    )(x, og, gate0, wo, kvg, wkvc, wkvr, g1, scale1, shift1, wq, wz, wg)


def _compress_kernel(hv_ref, ptop_ref, pbot_ref, w1t_ref, w1b_ref, w2_ref, o_ref):
    hv = hv_ref[...]
    a = _dot(hv + ptop_ref[...], w1t_ref[...])
    b = _dot(hv + pbot_ref[...], w1b_ref[...])
    nrow = a.shape[0]
    hid = a + pltpu.roll(b, nrow - 1, axis=0)
    o_ref[...] = _dot(_silu(hid), w2_ref[...]).astype(o_ref.dtype)


def _compress(halves, ptop, pbot, w1t, w1b, w2):
    bn, four, nh, wd = halves.shape
    hid = w1t.shape[-1]
    per_kind = lambda b, j: (j // B_GROUPS, 0, 0)
    return pl.pallas_call(
        _compress_kernel,
        grid=(bn, four),
        in_specs=[pl.BlockSpec((None, None, nh, wd), lambda b, j: (b, j, 0, 0)),
                  pl.BlockSpec((None, 1, wd), per_kind),
                  pl.BlockSpec((None, 1, wd), per_kind),
                  pl.BlockSpec((None, wd, hid), per_kind),
                  pl.BlockSpec((None, wd, hid), per_kind),
                  pl.BlockSpec((None, hid, B_HEAD_DIM), per_kind)],
        out_specs=pl.BlockSpec((None, None, nh, B_HEAD_DIM), lambda b, j: (b, j, 0, 0)),
        out_shape=jax.ShapeDtypeStruct((bn, four, nh, B_HEAD_DIM), BF16),
        compiler_params=_cparams("arbitrary", "arbitrary"),
    )(halves, ptop, pbot, w1t, w1b, w2)


def _t5_bucket_np(dist):
    n = np.maximum(dist, 0)
    max_exact = NUM_BUCKETS // 2
    nf = np.maximum(n, 1).astype(np.float64)
    val = np.log(nf / max_exact) / math.log(MAX_DISTANCE / max_exact) * (NUM_BUCKETS - max_exact)
    frac = np.abs(val - np.round(val))
    safe = (frac > 1e-6) | (n <= max_exact) | (n >= MAX_DISTANCE)
    assert bool(np.all(safe)), "bucket boundary too close to an integer distance"
    large = np.minimum(max_exact + np.floor(np.maximum(val, 0.0)).astype(np.int64), NUM_BUCKETS - 1)
    return np.where(n < max_exact, n, large)


def _bias_onehot():
    r = np.arange(NSA_Q_TILE)[:, None]
    tiles = []
    j = np.arange(WINDOW + NSA_Q_TILE)[None, :]
    d = r + WINDOW - j
    tiles.append((d, (d >= 0) & (d < WINDOW)))
    j = np.arange((NSA_SUB + 2) * L_SLC)[None, :]
    d = r + 2 * L_SLC - j
    tiles.append((d, d >= 0))
    j = np.arange(CMP_NEAR)[None, :]
    d = r - CMP_STRIDE * (j - CMP_LEAD) - (L_CMP - 1)
    tiles.append((d, d >= 0))
    cols = [np.where(valid, _t5_bucket_np(d), NUM_BUCKETS).reshape(-1) for d, valid in tiles]
    widths = [c.size for c in cols]
    return np.concatenate(cols).astype(np.int32)[None, :], widths


def _bias_kernel(rb_ref, bk_ref, o_ref):
    rb = rb_ref[...]
    lane = lax.broadcasted_iota(jnp.int32, rb.shape, 1)
    rbs = rb - rb[:, NUM_BUCKETS - 1:NUM_BUCKETS]
    rbs = jnp.where(lane < NUM_BUCKETS, rbs, jnp.where(lane == NUM_BUCKETS, NEG_INF, 0.0))
    bk = bk_ref[...]
    onehot = jnp.where(lax.broadcasted_iota(jnp.int32, (2 * NUM_BUCKETS, bk.shape[1]), 0) == bk, 1.0, 0.0)
    o_ref[...] = _dot_f32(rbs, onehot)


def _bias_tables(rel_bias):
    bk, widths = _bias_onehot()
    ncol = bk.shape[1]
    nt = 8
    assert ncol % (nt * LANES) == 0
    tc = ncol // nt
    rb = jnp.concatenate([rel_bias.T, jnp.zeros((B_HEADS, NUM_BUCKETS), F32)], axis=1)
    flat = pl.pallas_call(
        _bias_kernel,
        grid=(nt,),
        in_specs=[pl.BlockSpec((B_HEADS, 2 * NUM_BUCKETS), lambda i: (0, 0)),
                  pl.BlockSpec((1, tc), lambda i: (0, i))],
        out_specs=pl.BlockSpec((B_HEADS, tc), lambda i: (0, i)),
        out_shape=jax.ShapeDtypeStruct((B_HEADS, ncol), F32),
        compiler_params=_cparams("arbitrary"),
    )(rb, jnp.asarray(bk))
    out, start = [], 0
    for wd in widths:
        tile = flat[:, start:start + wd].reshape(B_GROUPS, B_HPG * NSA_Q_TILE, wd // NSA_Q_TILE)
        out.append(tile)
        start += wd
    return out


def _nsa_kernel(q_ref, kc_ref, vc_ref, ks_ref, vs_ref, kw_ref, vw_ref, tcmp_ref, tsel_ref, twin_ref,
                ov_ref, oc_ref, os_ref, ow_ref):
    tq = NSA_Q_TILE
    dh = B_HEAD_DIM
    hpg = B_HPG
    rows = hpg * tq
    ti = pl.program_id(2)
    q0 = ti * tq
    blk0 = ti * NSA_SUB
    qt = q_ref[...]
    q = jnp.concatenate([qt[:, h * dh:(h + 1) * dh] for h in range(hpg)], axis=0)

    def to_tokens(o):
        return jnp.concatenate([o[h * tq:(h + 1) * tq, :] for h in range(hpg)], axis=1)

    def finish(pv):
        return to_tokens(pv[:, :dh] * (1.0 / pv[:, dh:dh + 1]))

    kc = kc_ref[...]
    ncp = kc.shape[0]
    s = _dot_nt(q, kc)
    first_near = (tq // CMP_STRIDE) * ti - CMP_LEAD
    cid = lax.broadcasted_iota(jnp.int32, (2 * CMP_NEAR, ncp), 1)
    jrow = lax.broadcasted_iota(jnp.int32, (2 * CMP_NEAR, ncp), 0) & (CMP_NEAR - 1)
    shift_eye = jnp.where(cid - first_near == jrow, 1.0, 0.0).astype(BF16)
    s = s + jnp.dot(tcmp_ref[...], shift_eye, preferred_element_type=F32)
    cvis = lax.broadcasted_iota(jnp.int32, (1, ncp), 1) < first_near + CMP_NEAR
    s = jnp.where(cvis, s, NEG_INF)
    live = s > 0.1 * NEG_INF
    m = jnp.max(s, axis=-1, keepdims=True)
    e = jnp.where(live, jnp.exp(s - m), 0.0)
    p = e * (1.0 / jnp.maximum(jnp.sum(e, axis=-1, keepdims=True), 1e-30))
    oc_ref[...] = to_tokens(_dot(p, vc_ref[...]))

    nw = WINDOW + tq
    start = pl.multiple_of(q0, tq)
    pad_col = jnp.where(lax.broadcasted_iota(jnp.int32, (rows, dh), 1) == 0, NEG_INF, 0.0).astype(BF16)
    q_win = jnp.concatenate([q, pad_col], axis=1)
    s_w = _dot_nt(q_win, kw_ref[pl.ds(start, nw), :]) + twin_ref[...]
    m_w = jnp.max(s_w, axis=-1, keepdims=True)
    e_w = jnp.exp(s_w - m_w)
    ow_ref[...] = finish(_dot(e_w, vw_ref[pl.ds(start, nw), :]))

    psum = p[0:tq, :]
    for h in range(1, hpg):
        psum = psum + p[h * tq:(h + 1) * tq, :]
    p_hi = psum.astype(BF16)
    p_r1 = psum - p_hi.astype(F32)
    p_mid = p_r1.astype(BF16)
    p_lo = (p_r1 - p_mid.astype(F32)).astype(BF16)
    p3 = jnp.concatenate([p_hi, p_mid, p_lo], axis=1)
    imp_t = _dot_nt(ov_ref[...], p3)
    nblk = imp_t.shape[0]
    blk = lax.broadcasted_iota(jnp.int32, (nblk, tq), 0)
    cur = blk0 + (lax.broadcasted_iota(jnp.int32, (nblk, tq), 1) >> SLC_SHIFT)
    forced = (blk == 0) | (blk == cur) | (blk == cur - 1)
    val = jnp.where(forced, SEL_BOOST, jnp.where(blk > cur, -SEL_BOOST, imp_t))
    nslab = nblk // SUBLANES
    slabs = [val[SUBLANES * r:SUBLANES * (r + 1), :] for r in range(nslab)]
    sub = lax.broadcasted_iota(jnp.int32, (SUBLANES, tq), 0)
    n_acc = 4
    ranks = [[jnp.zeros((SUBLANES, tq), jnp.int32) for _ in range(n_acc)] for _ in range(nslab)]
    for j in range(nblk):
        vj = jnp.broadcast_to(val[j:j + 1, :], (SUBLANES, tq))
        for r in range(nslab):
            lo = SUBLANES * r
            if lo > j:
                ahead = vj >= slabs[r]
            elif lo + SUBLANES - 1 <= j:
                ahead = vj > slabs[r]
            else:
                ahead = (vj > slabs[r]) | ((vj == slabs[r]) & (sub > j - lo))
            ranks[r][j % n_acc] = ranks[r][j % n_acc] + ahead.astype(jnp.int32)
    rank = jnp.concatenate([(a[0] + a[1]) + (a[2] + a[3]) for a in ranks], axis=0)
    sel_t = (rank < N_SEL) & (blk <= cur)
    far_t = jnp.where(sel_t & (blk <= blk0 - 3), 0.0, NEG_INF)
    near_t = jnp.where(sel_t & (blk >= blk0 - 2), 0.0, NEG_INF)

    def q_with_mask(mask_t):
        mk = mask_t.T.astype(BF16)
        return jnp.concatenate([q, jnp.concatenate([mk] * hpg, axis=0)], axis=1)

    q_far = q_with_mask(far_t)
    q_near = q_with_mask(near_t)

    kt_sz = SEL_KEY_TILE
    n_far_keys = jnp.maximum(blk0 - 2, 0) * L_SLC
    n_tiles = (n_far_keys + kt_sz - 1) // kt_sz

    def far_step(i, carry):
        m_i, acc = carry
        start = pl.multiple_of(KV_PAD + i * kt_sz, kt_sz)
        s_f = _dot_nt(q_far, ks_ref[pl.ds(start, kt_sz), :])
        m_n = jnp.maximum(m_i, jnp.max(s_f, axis=-1, keepdims=True))
        e_f = jnp.exp(s_f - m_n)
        acc_n = jnp.exp(m_i - m_n) * acc + _dot(e_f, vs_ref[pl.ds(start, kt_sz), :])
        return m_n, acc_n

    init = (jnp.full((rows, 1), NEG_INF, F32), jnp.zeros((rows, 2 * dh), F32))
    m_i, acc = lax.fori_loop(0, n_tiles, far_step, init)

    nk = (NSA_SUB + 2) * L_SLC
    start = pl.multiple_of(KV_PAD + q0 - 2 * L_SLC, L_SLC)
    s_n = _dot_nt(q_near, ks_ref[pl.ds(start, nk), :]) + tsel_ref[...]
    m_n = jnp.maximum(m_i, jnp.max(s_n, axis=-1, keepdims=True))
    e_n = jnp.exp(s_n - m_n)
    acc = jnp.exp(m_i - m_n) * acc + _dot(e_n, vs_ref[pl.ds(start, nk), :])
    os_ref[...] = finish(acc)


def _nsa(q, kcv, ks, vs, kw, vw, tcmp, tsel, twin, ov):
    bn, t, _ = q.shape
    tq = NSA_Q_TILE
    gw = B_HPG * B_HEAD_DIM
    ncp = kcv.shape[2]
    tp = ks.shape[2]
    rows = B_HPG * tq
    per_bg = pl.BlockSpec((None, None, tp, ks.shape[3]), lambda b, g, i: (b, g, 0, 0))
    per_g = lambda b, g, i: (g, 0, 0)
    out_spec = pl.BlockSpec((None, tq, gw), lambda b, g, i: (b, i, g))
    out_sd = jax.ShapeDtypeStruct((bn, t, B_WIDTH), F32)
    return pl.pallas_call(
        _nsa_kernel,
        grid=(bn, B_GROUPS, t // tq),
        in_specs=[pl.BlockSpec((None, tq, gw), lambda b, g, i: (b, i, g)),
                  pl.BlockSpec((None, None, ncp, B_HEAD_DIM), lambda b, g, i: (b, g, 0, 0)),
                  pl.BlockSpec((None, None, ncp, B_HEAD_DIM), lambda b, g, i: (b, B_GROUPS + g, 0, 0)),
                  per_bg, per_bg, per_bg, per_bg,
                  pl.BlockSpec((None, rows, tcmp.shape[2]), per_g),
                  pl.BlockSpec((None, rows, tsel.shape[2]), per_g),
                  pl.BlockSpec((None, rows, twin.shape[2]), per_g),
                  pl.BlockSpec(ov.shape, lambda b, g, i: (0, 0))],
        out_specs=[out_spec, out_spec, out_spec],
        out_shape=[out_sd, out_sd, out_sd],
        compiler_params=_cparams("arbitrary", "arbitrary", "arbitrary"),
    )(q, kcv, kcv, ks, vs, kw, vw, tcmp, tsel, twin, ov)


def _final_kernel(oc_ref, os_ref, ow_ref, z_ref, gates_ref, ex_ref, x1_ref, gate_ref, wo_ref, fg_ref, o_ref):
    gt = gates_ref[...]
    g_hi = gt.astype(BF16)
    g_lo = (gt - g_hi.astype(F32)).astype(BF16)
    ghl = jnp.concatenate([g_hi, g_lo], axis=1)
    y = None
    for br, o_ref_br in enumerate((oc_ref, os_ref, ow_ref)):
        gexp = jnp.dot(ghl, ex_ref[br], preferred_element_type=F32)
        term = gexp * o_ref_br[...] * _silu(z_ref[:, br * B_WIDTH:(br + 1) * B_WIDTH])
        y = term if y is None else y + term
    x2 = x1_ref[...] + gate_ref[...] * _dot(y, wo_ref[...])
    o_ref[...] = _rms(x2, fg_ref[...])


def _final(oc, osel, ow, z, gates, x1, gate1, wo, fg):
    bn, t, d = x1.shape
    tm = ROW_TILE
    ng = gates.shape[2]
    row = lambda b, i: (b, i, 0)
    ex = np.zeros((N_BRANCH, 2 * ng, B_WIDTH), np.float32)
    for br in range(N_BRANCH):
        for h in range(B_HEADS):
            ex[br, br * B_HEADS + h, h * B_HEAD_DIM:(h + 1) * B_HEAD_DIM] = 1.0
            ex[br, ng + br * B_HEADS + h, h * B_HEAD_DIM:(h + 1) * B_HEAD_DIM] = 1.0
    ex = jnp.asarray(ex, BF16)
    return pl.pallas_call(
        _final_kernel,
        grid=(bn, t // tm),
        in_specs=[pl.BlockSpec((None, tm, B_WIDTH), row)] * 3
        + [pl.BlockSpec((None, tm, N_BRANCH * B_WIDTH), row),
           pl.BlockSpec((None, tm, ng), row),
           pl.BlockSpec(ex.shape, lambda b, i: (0, 0, 0)),
           pl.BlockSpec((None, tm, d), row),
           pl.BlockSpec((None, 1, d), lambda b, i: (b, 0, 0)),
           pl.BlockSpec(wo.shape, lambda b, i: (0, 0)),
           pl.BlockSpec((1, d), lambda b, i: (0, 0))],
        out_specs=pl.BlockSpec((None, tm, d), row),
        out_shape=jax.ShapeDtypeStruct((bn, t, d), F32),
        compiler_params=_cparams("arbitrary", "arbitrary"),
    )(oc, osel, ow, z, gates, ex, x1, gate1, wo, fg)


def _overlap_matrix(ncp, n_cmp, n_slc, nblk):
    cells = np.arange(n_cmp)[:, None] + np.arange(L_CMP // CMP_STRIDE)[None, :]
    ov = (cells[:, None, :] // (L_SLC // CMP_STRIDE) == np.arange(n_slc)[None, :, None]).sum(-1)
    out = np.zeros((ncp, nblk), np.float32)
    out[:n_cmp, :n_slc] = ov
    return out


def _block_onehot(t, nblk):
    oh = np.zeros((KV_PAD + t, nblk), np.float32)
    oh[KV_PAD + np.arange(t), np.arange(t) // L_SLC] = 1.0
    oh[:KV_PAD, nblk - 1] = 1.0
    return oh


def kernel(x, c, rel_bias, ada_w, ada_b, norm_g, a_in_w, a_conv_w, a_A_log, a_dt_bias, a_onorm_g, a_out_w,
           kv_norm_g, kv_w, cmp_pos_k, cmp_pos_v, cmp_k_w1, cmp_k_w2, cmp_v_w1, cmp_v_w2,
           b_in_w, b_out_w, final_g):
    bn, t, d = x.shape
    assert ada_w.shape[0] == 2 and a_in_w.shape[0] == 1 and b_in_w.shape[0] == 1
    assert t % max(ROW_TILE, SEL_KEY_TILE) == 0
    n_slc = t // L_SLC
    nblk = 64
    assert n_slc <= nblk
    n_cmp = (t - L_CMP) // CMP_STRIDE + 1
    ncp = t // CMP_STRIDE

    mod = _ada_modulation(c, ada_w, ada_b)
    shift = mod[:, :, None, :d]
    scale = mod[:, :, None, d:2 * d]
    gate = mod[:, :, None, 2 * d:]

    hb = GDN_HEADS_PER_STEP
    nhg = A_HEADS // hb
    w_in = a_in_w[0]
    wqkv = w_in[:, :3 * A_WIDTH].astype(BF16)
    wz = w_in[:, 3 * A_WIDTH:4 * A_WIDTH].astype(BF16)
    wb = w_in[:, 4 * A_WIDTH:4 * A_WIDTH + A_HEADS]
    wa = w_in[:, 4 * A_WIDTH + A_HEADS:]
    half = LANES // 2
    wba = jnp.zeros((d, nhg, LANES), F32)
    wba = wba.at[:, :, :hb].set(wb.reshape(d, nhg, hb)).at[:, :, half:half + hb].set(wa.reshape(d, nhg, hb))
    wba = wba.reshape(d, nhg * LANES).astype(BF16)
    lane_rows = lambda v: jnp.zeros((nhg, 1, LANES), F32).at[:, 0, half:half + hb].set(v.reshape(nhg, hb))
    qkv, z_a, ba = _in_proj_a(x, norm_g[0:1], scale[0], shift[0], wqkv, wz, wba)
    og = _gdn(qkv, z_a, ba, a_conv_w[0], lane_rows(a_A_log[0]), lane_rows(a_dt_bias[0]), a_onorm_g[0:1])

    ndh = B_GROUPS * B_HEAD_DIM
    wkvc = kv_w[:, :2 * ndh].astype(BF16)
    wkvr = kv_w[:, 2 * ndh:].astype(BF16)
    w_b = b_in_w[0]
    wq = w_b[:, :B_WIDTH].astype(BF16)
    wzb = w_b[:, B_WIDTH:4 * B_WIDTH].astype(BF16)
    wg = jnp.zeros((d, LANES), F32).at[:, :N_BRANCH * B_HEADS].set(w_b[:, 4 * B_WIDTH:]).astype(BF16)
    x1, kvc, kvr, q, z_b, gates = _post_a(x, og, gate[0], a_out_w[0].astype(BF16), kv_norm_g[None, :], wkvc, wkvr,
                                          norm_g[1:2], scale[1], shift[1], wq, wzb, wg)

    halves = kvc.reshape(bn, ncp, CMP_STRIDE, 2 * B_GROUPS, B_HEAD_DIM).transpose(0, 3, 1, 2, 4)
    halves = halves.reshape(bn, 2 * B_GROUPS, ncp, CMP_STRIDE * B_HEAD_DIM)
    pos = jnp.stack([cmp_pos_k, cmp_pos_v])
    hw = CMP_STRIDE * B_HEAD_DIM
    ptop = pos[:, :CMP_STRIDE].reshape(2, 1, hw)
    pbot = pos[:, CMP_STRIDE:].reshape(2, 1, hw)
    w1 = jnp.stack([cmp_k_w1, cmp_v_w1]).astype(BF16)
    w2 = jnp.stack([cmp_k_w2, cmp_v_w2]).astype(BF16)
    kcv = _compress(halves, ptop, pbot, w1[:, :hw], w1[:, hw:], w2)

    kvr4 = kvr.reshape(bn, t, 4, B_GROUPS, B_HEAD_DIM).transpose(2, 0, 3, 1, 4)
    kvr4 = jnp.pad(kvr4, ((0, 0), (0, 0), (0, 0), (KV_PAD, 0), (0, 0)))
    tp = KV_PAD + t

    def beside(x, extra):
        return jnp.concatenate([x, jnp.broadcast_to(jnp.asarray(extra, BF16), x.shape[:2] + extra.shape)], axis=-1)

    first_col = np.zeros((tp, B_HEAD_DIM), np.float32)
    ones_col = first_col.copy()
    ones_col[:, 0] = 1.0
    pad_col = first_col.copy()
    pad_col[:KV_PAD, 0] = 1.0
    ks = beside(kvr4[0], _block_onehot(t, nblk))
    vs = beside(kvr4[1], ones_col)
    kw = beside(kvr4[2], pad_col)
    vw = beside(kvr4[3], ones_col)

    twin, tsel, tcmp = _bias_tables(rel_bias)
    tc_hi = tcmp.astype(BF16)
    tc_lo = (tcmp - tc_hi.astype(F32)).astype(BF16)
    tcmp2 = jnp.concatenate([tc_hi, tc_lo], axis=-1)
    ov_t = _overlap_matrix(ncp, n_cmp, n_slc, nblk).T
    ov3 = jnp.asarray(np.concatenate([ov_t] * 3, axis=1), BF16)

    oc, osel, ow = _nsa(q, kcv, ks, vs, kw, vw, tcmp2, tsel, twin, ov3)

    return _final(oc, osel, ow, z_b, gates, x1, gate[1], b_out_w[0].astype(BF16), final_g[None, :])
```

```python
import functools
import math

import numpy as np
import jax
import jax.numpy as jnp
from jax import lax
from jax.experimental import pallas as pl
from jax.experimental.pallas import tpu as pltpu

F32 = jnp.float32
BF16 = jnp.bfloat16
HIGHEST = lax.Precision.HIGHEST

A_HEADS = 8
A_HEAD_DIM = 128
A_WIDTH = A_HEADS * A_HEAD_DIM
A_CONV = 4
A_CHUNK = 64
B_HEADS = 16
B_GROUPS = 2
B_HPG = B_HEADS // B_GROUPS
B_HEAD_DIM = 64
B_WIDTH = B_HEADS * B_HEAD_DIM
N_BRANCH = 3
L_CMP = 32
CMP_STRIDE = 16
L_SLC = 64
N_SEL = 16
WINDOW = 512
Q_BLOCK = 64
NUM_BUCKETS = 32
MAX_DISTANCE = 128
EPS = 1e-6
NEG_INF = -1e30
SEL_BOOST = 1e9

LANES = 128
SUBLANES = 8
VMEM_LIMIT_BYTES = 56 * 1024 * 1024

ROW_TILE = 256
GDN_HEADS_PER_STEP = 8
SEL_KEY_TILE = 512
KV_PAD = WINDOW
NSA_Q_TILE = 128
NSA_SUB = NSA_Q_TILE // Q_BLOCK
SLC_SHIFT = L_SLC.bit_length() - 1
assert 1 << SLC_SHIFT == L_SLC and L_SLC == Q_BLOCK
CMP_LEAD = 12
CMP_NEAR = 32
assert CMP_NEAR >= CMP_LEAD + NSA_Q_TILE // CMP_STRIDE and NSA_Q_TILE % Q_BLOCK == 0
BLK16 = 16


def _cparams(*sem):
    return pltpu.CompilerParams(dimension_semantics=sem, vmem_limit_bytes=VMEM_LIMIT_BYTES)


def _sigmoid(x):
    return 1.0 / (1.0 + jnp.exp(-x))


def _silu(x):
    return x * _sigmoid(x)


def _dot(a, b):
    return jnp.dot(a.astype(BF16), b.astype(BF16), preferred_element_type=F32)


def _dot_nt(a, b):
    return lax.dot_general(a.astype(BF16), b.astype(BF16), (((1,), (1,)), ((), ())),
                           preferred_element_type=F32)


def _dot_f32(a, b):
    return jnp.dot(a, b, precision=HIGHEST, preferred_element_type=F32)


def _rms(x, g):
    ms = jnp.mean(x * x, axis=-1, keepdims=True)
    return x * lax.rsqrt(ms + EPS) * g


def _ada_kernel(c_ref, w_ref, b_ref, o_ref):
    o_ref[...] = _dot_f32(_silu(c_ref[...]), w_ref[...]) + b_ref[...]


def _ada_modulation(c, ada_w, ada_b):
    depth, d, d3 = ada_w.shape
    bn = c.shape[0]
    return pl.pallas_call(
        _ada_kernel,
        grid=(depth, d3 // d),
        in_specs=[pl.BlockSpec((bn, d), lambda l, j: (0, 0)),
                  pl.BlockSpec((None, d, d), lambda l, j: (l, 0, j)),
                  pl.BlockSpec((None, 1, d), lambda l, j: (l, 0, j))],
        out_specs=pl.BlockSpec((None, bn, d), lambda l, j: (l, 0, j)),
        out_shape=jax.ShapeDtypeStruct((depth, bn, d3), F32),
        compiler_params=_cparams("arbitrary", "arbitrary"),
    )(c, ada_w, ada_b.reshape(depth, 1, d3))


def _in_proj_a_kernel(x_ref, g_ref, scale_ref, shift_ref, wqkv_ref, wz_ref, wba_ref,
                      qkv_ref, z_ref, ba_ref):
    h = _rms(x_ref[...], g_ref[...]) * (1.0 + scale_ref[...]) + shift_ref[...]
    hb = h.astype(BF16)
    qkv_ref[...] = jnp.dot(hb, wqkv_ref[...], preferred_element_type=F32)
    z_ref[...] = jnp.dot(hb, wz_ref[...], preferred_element_type=F32)
    ba_ref[...] = jnp.dot(hb, wba_ref[...], preferred_element_type=F32)


def _in_proj_a(x, g, scale, shift, wqkv, wz, wba):
    bn, t, d = x.shape
    tm = ROW_TILE
    row = lambda b, i: (b, i, 0)
    per_b = lambda b, i: (b, 0, 0)
    const = lambda b, i: (0, 0)
    nba = wba.shape[1]
    return pl.pallas_call(
        _in_proj_a_kernel,
        grid=(bn, t // tm),
        in_specs=[pl.BlockSpec((None, tm, d), row),
                  pl.BlockSpec((1, d), const),
                  pl.BlockSpec((None, 1, d), per_b),
                  pl.BlockSpec((None, 1, d), per_b),
                  pl.BlockSpec(wqkv.shape, const),
                  pl.BlockSpec(wz.shape, const),
                  pl.BlockSpec(wba.shape, const)],
        out_specs=[pl.BlockSpec((None, tm, 3 * A_WIDTH), row),
                   pl.BlockSpec((None, tm, A_WIDTH), row),
                   pl.BlockSpec((None, tm, nba), row)],
        out_shape=[jax.ShapeDtypeStruct((bn, t, 3 * A_WIDTH), F32),
                   jax.ShapeDtypeStruct((bn, t, A_WIDTH), F32),
                   jax.ShapeDtypeStruct((bn, t, nba), F32)],
        compiler_params=_cparams("arbitrary", "arbitrary"),
    )(x, g, scale, shift, wqkv, wz, wba)


def _cumsum_rows(x):
    n = x.shape[0]
    row = lax.broadcasted_iota(jnp.int32, x.shape, 0)
    s = 1
    while s < n:
        x = x + jnp.where(row >= s, pltpu.roll(x, s, axis=0), 0.0)
        s *= 2
    return x


def _unit_lower_inverse(ms):
    c = ms[0].shape[0]
    row = lax.broadcasted_iota(jnp.int32, (c, c), 0)
    col = lax.broadcasted_iota(jnp.int32, (c, c), 1)
    eye = (row == col).astype(F32)
    same_blk = (row & -BLK16) == (col & -BLK16)
    d = [jnp.where(same_blk, m, 0.0) for m in ms]
    mo = [m - x for m, x in zip(ms, d)]
    d2 = [_dot(x, x) for x in d]
    td = [eye - x for x in d]
    d4 = [_dot(x, x) for x in d2]
    td = [t + _dot(t, x) for t, x in zip(td, d2)]
    d8 = [_dot(x, x) for x in d4]
    td = [t + _dot(t, x) for t, x in zip(td, d4)]
    td = [t + _dot(t, x) for t, x in zip(td, d8)]
    n = [_dot(t, x) for t, x in zip(td, mo)]
    n2 = [_dot(x, x) for x in n]
    r = [eye - x for x in n]
    r = [a + _dot(a, x) for a, x in zip(r, n2)]
    return [_dot(a, t) for a, t in zip(r, td)]


def _gdn_kernel(q_ref, k_ref, v_ref, z_ref, ba_ref, cwq_ref, cwk_ref, cwv_ref, alog_ref, dtb_ref,
                ong_ref, o_ref, buf_ref, s_ref, *, hb):
    c = A_CHUNK
    dh = A_HEAD_DIM
    halo = SUBLANES
    n = pl.program_id(2)

    @pl.when(n == 0)
    def _():
        buf_ref[:, 0:halo, :] = jnp.zeros((3, halo, hb * dh), F32)
        s_ref[...] = jnp.zeros(s_ref.shape, F32)

    def conv_silu(idx, x_ref, cw_ref):
        buf_ref[idx, halo:halo + c, :] = x_ref[...]
        cw = cw_ref[...]
        off = halo - (A_CONV - 1)
        y = buf_ref[idx, off:off + c, :] * cw[0:1, :]
        for kk in range(1, A_CONV):
            y = y + buf_ref[idx, off + kk:off + kk + c, :] * cw[kk:kk + 1, :]
        buf_ref[idx, 0:halo, :] = x_ref[c - halo:c, :]
        return _silu(y)

    q_all = conv_silu(0, q_ref, cwq_ref)
    k_all = conv_silu(1, k_ref, cwk_ref)
    v_all = conv_silu(2, v_ref, cwv_ref)

    ba = ba_ref[...]
    beta_t = _sigmoid(ba)
    xa = ba + dtb_ref[...]
    softplus = jnp.maximum(xa, 0.0) + jnp.log(1.0 + jnp.exp(-jnp.abs(xa)))
    g_t = -jnp.exp(alog_ref[...]) * softplus
    gc_t = _cumsum_rows(g_t)
    gc_tt = gc_t.T
    egc_t = jnp.exp(gc_t)
    ekd_t = jnp.exp(gc_t[c - 1:c, :] - gc_t)
    egl_t = jnp.exp(gc_t[c - 1:c, :])

    row = lax.broadcasted_iota(jnp.int32, (c, c), 0)
    col = lax.broadcasted_iota(jnp.int32, (c, c), 1)
    incl = row >= col
    strict = row > col
    heads = range(hb)
    la = LANES // 2

    def head(x, i):
        return x[:, i * dh:(i + 1) * dh]

    def lane(x, i):
        return x[:, la + i:la + i + 1]

    def l2n(x):
        return x * lax.rsqrt(jnp.sum(x * x, axis=-1, keepdims=True) + EPS)

    qn = [l2n(head(q_all, i)) * (dh ** -0.5) for i in heads]
    kn = [l2n(head(k_all, i)) for i in heads]
    knb = [x.astype(BF16) for x in kn]
    beta = [beta_t[:, i:i + 1] for i in heads]
    kb = [kn[i] * beta[i] for i in heads]
    decay = [jnp.where(incl, jnp.exp(jnp.where(incl, lane(gc_t, i) - gc_tt[la + i:la + i + 1, :], 0.0)), 0.0)
             for i in heads]
    m = [jnp.where(strict, _dot_nt(kb[i], knb[i]) * decay[i], 0.0) for i in heads]
    attn = [(_dot_nt(qn[i], knb[i]) * decay[i]).astype(BF16) for i in heads]
    rhs = [jnp.concatenate([head(v_all, i) * beta[i], kb[i] * lane(egc_t, i)], axis=1).astype(BF16)
           for i in heads]
    qdec = [(qn[i] * lane(egc_t, i)).astype(BF16) for i in heads]
    kdec_t = [(kn[i] * lane(ekd_t, i)).T.astype(BF16) for i in heads]
    tinv = _unit_lower_inverse(m)
    uw = [_dot(tinv[i], rhs[i]) for i in heads]
    s_old = [s_ref[i] for i in heads]
    sb = [x.astype(BF16) for x in s_old]
    v_new = [uw[i][:, :dh] - _dot(uw[i][:, dh:], sb[i]) for i in heads]
    vnb = [x.astype(BF16) for x in v_new]
    for i in heads:
        s_ref[i] = s_old[i] * lane(egl_t, i) + jnp.dot(kdec_t[i], vnb[i], preferred_element_type=F32)
    o = [jnp.dot(qdec[i], sb[i], preferred_element_type=F32)
         + jnp.dot(attn[i], vnb[i], preferred_element_type=F32) for i in heads]
    for i in heads:
        sl = slice(i * dh, (i + 1) * dh)
        o_ref[:, sl] = _rms(o[i], ong_ref[...]) * _silu(z_ref[:, sl])


def _gdn(qkv, z, ba, conv_w, alog_row, dtb_row, onorm_g):
    bn, t, _ = qkv.shape
    hb = GDN_HEADS_PER_STEP
    nhg = A_HEADS // hb
    c = A_CHUNK
    w = hb * A_HEAD_DIM
    kern = functools.partial(_gdn_kernel, hb=hb)

    def qkv_spec(which):
        return pl.BlockSpec((None, c, w), lambda b, hg, n: (b, n, which * nhg + hg))

    def cw_spec(which):
        return pl.BlockSpec((A_CONV, w), lambda b, hg, n: (0, which * nhg + hg))

    hg_row = pl.BlockSpec((None, 1, LANES), lambda b, hg, n: (hg, 0, 0))
    return pl.pallas_call(
        kern,
        grid=(bn, nhg, t // c),
        in_specs=[qkv_spec(0), qkv_spec(1), qkv_spec(2),
                  pl.BlockSpec((None, c, w), lambda b, hg, n: (b, n, hg)),
                  pl.BlockSpec((None, c, LANES), lambda b, hg, n: (b, n, hg)),
                  cw_spec(0), cw_spec(1), cw_spec(2),
                  hg_row, hg_row,
                  pl.BlockSpec((1, A_HEAD_DIM), lambda b, hg, n: (0, 0))],
        out_specs=pl.BlockSpec((None, c, w), lambda b, hg, n: (b, n, hg)),
        out_shape=jax.ShapeDtypeStruct((bn, t, A_WIDTH), F32),
        scratch_shapes=[pltpu.VMEM((3, SUBLANES + c, w), F32),
                        pltpu.VMEM((hb, A_HEAD_DIM, A_HEAD_DIM), F32)],
        compiler_params=_cparams("arbitrary", "arbitrary", "arbitrary"),
    )(qkv, qkv, qkv, z, ba, conv_w, conv_w, conv_w, alog_row, dtb_row, onorm_g)


def _post_a_kernel(x_ref, og_ref, gate_ref, wo_ref, kvg_ref, wkvc_ref, wkvr_ref, g1_ref, scale_ref,
                   shift_ref, wq_ref, wz_ref, wg_ref,
                   x1_ref, kvc_ref, kvr_ref, q_ref, z_ref, gates_ref):
    out = _dot(og_ref[...], wo_ref[...])
    x1 = x_ref[...] + gate_ref[...] * out
    x1_ref[...] = x1
    sb = _rms(x1, kvg_ref[...]).astype(BF16)
    kvc_ref[...] = jnp.dot(sb, wkvc_ref[...], preferred_element_type=F32)
    kvr_ref[...] = jnp.dot(sb, wkvr_ref[...], preferred_element_type=F32).astype(BF16)
    h = _rms(x1, g1_ref[...]) * (1.0 + scale_ref[...]) + shift_ref[...]
    hb = h.astype(BF16)
    q = jnp.dot(hb, wq_ref[...], preferred_element_type=F32) * (B_HEAD_DIM ** -0.5)
    q_ref[...] = q.astype(BF16)
    z_ref[...] = jnp.dot(hb, wz_ref[...], preferred_element_type=F32)
    gates_ref[...] = _sigmoid(jnp.dot(hb, wg_ref[...], preferred_element_type=F32))


def _post_a(x, og, gate0, wo, kvg, wkvc, wkvr, g1, scale1, shift1, wq, wz, wg):
    bn, t, d = x.shape
    tm = ROW_TILE
    row = lambda b, i: (b, i, 0)
    per_b = lambda b, i: (b, 0, 0)
    const = lambda b, i: (0, 0)
    full = lambda a: pl.BlockSpec(a.shape, const)
    vec = pl.BlockSpec((1, d), const)
    bvec = pl.BlockSpec((None, 1, d), per_b)
    outs = [(d, F32), (wkvc.shape[1], F32), (wkvr.shape[1], BF16), (wq.shape[1], BF16),
            (wz.shape[1], F32), (wg.shape[1], F32)]
    return pl.pallas_call(
        _post_a_kernel,
        grid=(bn, t // tm),
        in_specs=[pl.BlockSpec((None, tm, d), row), pl.BlockSpec((None, tm, A_WIDTH), row), bvec,
                  full(wo), vec, full(wkvc), full(wkvr), vec, bvec, bvec, full(wq), full(wz), full(wg)],
        out_specs=[pl.BlockSpec((None, tm, n), row) for n, _ in outs],
        out_shape=[jax.ShapeDtypeStruct((bn, t, n), dt) for n, dt in outs],
        compiler_params=_cparams("arbitrary", "arbitrary"),
    )(x, og, gate0, wo, kvg, wkvc, wkvr, g1, scale1, shift1, wq, wz, wg)


def _compress_kernel(hv_ref, ptop_ref, pbot_ref, w1t_ref, w1b_ref, w2_ref, o_ref):
    hv = hv_ref[...]
    a = _dot(hv + ptop_ref[...], w1t_ref[...])
    b = _dot(hv + pbot_ref[...], w1b_ref[...])
    nrow = a.shape[0]
    hid = a + pltpu.roll(b, nrow - 1, axis=0)
    o_ref[...] = _dot(_silu(hid), w2_ref[...]).astype(o_ref.dtype)


def _compress(halves, ptop, pbot, w1t, w1b, w2):
    bn, four, nh, wd = halves.shape
    hid = w1t.shape[-1]
    per_kind = lambda b, j: (j // B_GROUPS, 0, 0)
    return pl.pallas_call(
        _compress_kernel,
        grid=(bn, four),
        in_specs=[pl.BlockSpec((None, None, nh, wd), lambda b, j: (b, j, 0, 0)),
                  pl.BlockSpec((None, 1, wd), per_kind),
                  pl.BlockSpec((None, 1, wd), per_kind),
                  pl.BlockSpec((None, wd, hid), per_kind),
                  pl.BlockSpec((None, wd, hid), per_kind),
                  pl.BlockSpec((None, hid, B_HEAD_DIM), per_kind)],
        out_specs=pl.BlockSpec((None, None, nh, B_HEAD_DIM), lambda b, j: (b, j, 0, 0)),
        out_shape=jax.ShapeDtypeStruct((bn, four, nh, B_HEAD_DIM), BF16),
        compiler_params=_cparams("arbitrary", "arbitrary"),
    )(halves, ptop, pbot, w1t, w1b, w2)


def _t5_bucket_np(dist):
    n = np.maximum(dist, 0)
    max_exact = NUM_BUCKETS // 2
    nf = np.maximum(n, 1).astype(np.float64)
    val = np.log(nf / max_exact) / math.log(MAX_DISTANCE / max_exact) * (NUM_BUCKETS - max_exact)
    frac = np.abs(val - np.round(val))
    safe = (frac > 1e-6) | (n <= max_exact) | (n >= MAX_DISTANCE)
    assert bool(np.all(safe)), "bucket boundary too close to an integer distance"
    large = np.minimum(max_exact + np.floor(np.maximum(val, 0.0)).astype(np.int64), NUM_BUCKETS - 1)
    return np.where(n < max_exact, n, large)


def _bias_onehot():
    r = np.arange(NSA_Q_TILE)[:, None]
    tiles = []
    j = np.arange(WINDOW + NSA_Q_TILE)[None, :]
    d = r + WINDOW - j
    tiles.append((d, (d >= 0) & (d < WINDOW)))
    j = np.arange((NSA_SUB + 2) * L_SLC)[None, :]
    d = r + 2 * L_SLC - j
    tiles.append((d, d >= 0))
    j = np.arange(CMP_NEAR)[None, :]
    d = r - CMP_STRIDE * (j - CMP_LEAD) - (L_CMP - 1)
    tiles.append((d, d >= 0))
    cols = [np.where(valid, _t5_bucket_np(d), NUM_BUCKETS).reshape(-1) for d, valid in tiles]
    widths = [c.size for c in cols]
    return np.concatenate(cols).astype(np.int32)[None, :], widths


def _bias_kernel(rb_ref, bk_ref, o_ref):
    rb = rb_ref[...]
    lane = lax.broadcasted_iota(jnp.int32, rb.shape, 1)
    rbs = rb - rb[:, NUM_BUCKETS - 1:NUM_BUCKETS]
    rbs = jnp.where(lane < NUM_BUCKETS, rbs, jnp.where(lane == NUM_BUCKETS, NEG_INF, 0.0))
    bk = bk_ref[...]
    onehot = jnp.where(lax.broadcasted_iota(jnp.int32, (2 * NUM_BUCKETS, bk.shape[1]), 0) == bk, 1.0, 0.0)
    o_ref[...] = _dot_f32(rbs, onehot)


def _bias_tables(rel_bias):
    bk, widths = _bias_onehot()
    ncol = bk.shape[1]
    nt = 8
    assert ncol % (nt * LANES) == 0
    tc = ncol // nt
    rb = jnp.concatenate([rel_bias.T, jnp.zeros((B_HEADS, NUM_BUCKETS), F32)], axis=1)
    flat = pl.pallas_call(
        _bias_kernel,
        grid=(nt,),
        in_specs=[pl.BlockSpec((B_HEADS, 2 * NUM_BUCKETS), lambda i: (0, 0)),
                  pl.BlockSpec((1, tc), lambda i: (0, i))],
        out_specs=pl.BlockSpec((B_HEADS, tc), lambda i: (0, i)),
        out_shape=jax.ShapeDtypeStruct((B_HEADS, ncol), F32),
        compiler_params=_cparams("arbitrary"),
    )(rb, jnp.asarray(bk))
    out, start = [], 0
    for wd in widths:
        tile = flat[:, start:start + wd].reshape(B_GROUPS, B_HPG * NSA_Q_TILE, wd // NSA_Q_TILE)
        out.append(tile)
        start += wd
    return out


def _nsa_kernel(q_ref, kc_ref, vc_ref, ks_ref, vs_ref, kw_ref, vw_ref, tcmp_ref, tsel_ref, twin_ref,
                ov_ref, oc_ref, os_ref, ow_ref, sa_ref, sb_ref):
    tq = NSA_Q_TILE
    dh = B_HEAD_DIM
    hpg = B_HPG
    rows = hpg * tq
    ti = pl.program_id(2)
    q0 = ti * tq
    blk0 = ti * NSA_SUB
    qt = q_ref[...]
    q = jnp.concatenate([qt[:, h * dh:(h + 1) * dh] for h in range(hpg)], axis=0)

    def to_tokens(o):
        return jnp.concatenate([o[h * tq:(h + 1) * tq, :] for h in range(hpg)], axis=1)

    def finish(pv):
        return to_tokens(pv[:, :dh] * (1.0 / pv[:, dh:dh + 1]))

    kc = kc_ref[...]
    ncp = kc.shape[0]
    nw = WINDOW + tq
    win0 = pl.multiple_of(q0, tq)
    first_near = (tq // CMP_STRIDE) * ti - CMP_LEAD
    cid = lax.broadcasted_iota(jnp.int32, (2 * CMP_NEAR, ncp), 1)
    jrow = lax.broadcasted_iota(jnp.int32, (2 * CMP_NEAR, ncp), 0) & (CMP_NEAR - 1)
    shift_eye = jnp.where(cid - first_near == jrow, 1.0, 0.0).astype(BF16)
    pad_col = jnp.where(lax.broadcasted_iota(jnp.int32, (rows, dh), 1) == 0, NEG_INF, 0.0).astype(BF16)
    q_win = jnp.concatenate([q, pad_col], axis=1)

    s = _dot_nt(q, kc) + jnp.dot(tcmp_ref[...], shift_eye, preferred_element_type=F32)
    s_w = _dot_nt(q_win, kw_ref[pl.ds(win0, nw), :]) + twin_ref[...]

    cvis = lax.broadcasted_iota(jnp.int32, (1, ncp), 1) < first_near + CMP_NEAR
    s = jnp.where(cvis, s, NEG_INF)
    live = s > 0.1 * NEG_INF
    m = jnp.max(s, axis=-1, keepdims=True)
    e = jnp.where(live, jnp.exp(s - m), 0.0)
    p = e * (1.0 / jnp.maximum(jnp.sum(e, axis=-1, keepdims=True), 1e-30))
    oc_ref[...] = to_tokens(_dot(p, vc_ref[...]))

    psum = p[0:tq, :]
    for h in range(1, hpg):
        psum = psum + p[h * tq:(h + 1) * tq, :]
    p_hi = psum.astype(BF16)
    p_r1 = psum - p_hi.astype(F32)
    p_mid = p_r1.astype(BF16)
    p_lo = (p_r1 - p_mid.astype(F32)).astype(BF16)
    p3 = jnp.concatenate([p_hi, p_mid, p_lo], axis=1)
    imp_t = _dot_nt(ov_ref[...], p3)

    m_w = jnp.max(s_w, axis=-1, keepdims=True)
    e_w = jnp.exp(s_w - m_w)
    ow_ref[...] = finish(_dot(e_w, vw_ref[pl.ds(win0, nw), :]))

    nblk = imp_t.shape[0]
    blk = lax.broadcasted_iota(jnp.int32, (nblk, tq), 0)
    cur = blk0 + (lax.broadcasted_iota(jnp.int32, (nblk, tq), 1) >> SLC_SHIFT)
    forced = (blk == 0) | (blk == cur) | (blk == cur - 1)
    val = jnp.where(forced, SEL_BOOST, jnp.where(blk > cur, -SEL_BOOST, imp_t))
    nslab = nblk // SUBLANES
    slabs = [val[SUBLANES * r:SUBLANES * (r + 1), :] for r in range(nslab)]
    sub = lax.broadcasted_iota(jnp.int32, (SUBLANES, tq), 0)
    n_acc = 4
    ranks = [[jnp.zeros((SUBLANES, tq), jnp.int32) for _ in range(n_acc)] for _ in range(nslab)]
    for j in range(nblk):
        vj = jnp.broadcast_to(val[j:j + 1, :], (SUBLANES, tq))
        for r in range(nslab):
            lo = SUBLANES * r
            if lo > j:
                ahead = vj >= slabs[r]
            elif lo + SUBLANES - 1 <= j:
                ahead = vj > slabs[r]
            else:
                ahead = (vj > slabs[r]) | ((vj == slabs[r]) & (sub > j - lo))
            ranks[r][j % n_acc] = ranks[r][j % n_acc] + ahead.astype(jnp.int32)
    rank = jnp.concatenate([(a[0] + a[1]) + (a[2] + a[3]) for a in ranks], axis=0)
    sel_t = (rank < N_SEL) & (blk <= cur)
    far_t = jnp.where(sel_t & (blk <= blk0 - 3), 0.0, NEG_INF)
    near_t = jnp.where(sel_t & (blk >= blk0 - 2), 0.0, NEG_INF)

    def q_with_mask(mask_t):
        mk = mask_t.T.astype(BF16)
        return jnp.concatenate([q, jnp.concatenate([mk] * hpg, axis=0)], axis=1)

    q_far = q_with_mask(far_t)
    q_near = q_with_mask(near_t)

    kt_sz = SEL_KEY_TILE
    n_far_keys = jnp.maximum(blk0 - 2, 0) * L_SLC
    n_pairs = (n_far_keys + 2 * kt_sz - 1) // (2 * kt_sz)

    def far_scores(tile):
        start = pl.multiple_of(KV_PAD + tile * kt_sz, kt_sz)
        return _dot_nt(q_far, ks_ref[pl.ds(start, kt_sz), :])

    def far_values(tile):
        start = pl.multiple_of(KV_PAD + tile * kt_sz, kt_sz)
        return vs_ref[pl.ds(start, kt_sz), :]

    def update(carry, s_t, v_t):
        m_i, acc = carry
        m_n = jnp.maximum(m_i, jnp.max(s_t, axis=-1, keepdims=True))
        e_t = jnp.exp(s_t - m_n)
        return m_n, jnp.exp(m_i - m_n) * acc + _dot(e_t, v_t)

    def pair_step(j, carry):
        sb_ref[...] = far_scores(2 * j + 1)
        carry = update(carry, sa_ref[...], far_values(2 * j))
        sa_ref[...] = far_scores(2 * j + 2)
        return update(carry, sb_ref[...], far_values(2 * j + 1))

    sa_ref[...] = far_scores(0)
    last = jnp.maximum(n_pairs, 1) - 1
    carry = (jnp.full((rows, 1), NEG_INF, F32), jnp.zeros((rows, 2 * dh), F32))
    carry = lax.fori_loop(0, last, pair_step, carry)
    sb_ref[...] = far_scores(2 * last + 1)
    carry = update(carry, sa_ref[...], far_values(2 * last))
    nk = (NSA_SUB + 2) * L_SLC
    near0 = pl.multiple_of(KV_PAD + q0 - 2 * L_SLC, L_SLC)
    s_n = _dot_nt(q_near, ks_ref[pl.ds(near0, nk), :]) + tsel_ref[...]
    carry = update(carry, sb_ref[...], far_values(2 * last + 1))
    _, acc = update(carry, s_n, vs_ref[pl.ds(near0, nk), :])
    os_ref[...] = finish(acc)


def _nsa(q, kcv, ks, vs, kw, vw, tcmp, tsel, twin, ov):
    bn, t, _ = q.shape
    tq = NSA_Q_TILE
    gw = B_HPG * B_HEAD_DIM
    ncp = kcv.shape[2]
    tp = ks.shape[2]
    rows = B_HPG * tq
    per_bg = pl.BlockSpec((None, None, tp, ks.shape[3]), lambda b, g, i: (b, g, 0, 0))
    per_g = lambda b, g, i: (g, 0, 0)
    out_spec = pl.BlockSpec((None, tq, gw), lambda b, g, i: (b, i, g))
    out_sd = jax.ShapeDtypeStruct((bn, t, B_WIDTH), F32)
    return pl.pallas_call(
        _nsa_kernel,
        grid=(bn, B_GROUPS, t // tq),
        in_specs=[pl.BlockSpec((None, tq, gw), lambda b, g, i: (b, i, g)),
                  pl.BlockSpec((None, None, ncp, B_HEAD_DIM), lambda b, g, i: (b, g, 0, 0)),
                  pl.BlockSpec((None, None, ncp, B_HEAD_DIM), lambda b, g, i: (b, B_GROUPS + g, 0, 0)),
                  per_bg, per_bg, per_bg, per_bg,
                  pl.BlockSpec((None, rows, tcmp.shape[2]), per_g),
                  pl.BlockSpec((None, rows, tsel.shape[2]), per_g),
                  pl.BlockSpec((None, rows, twin.shape[2]), per_g),
                  pl.BlockSpec(ov.shape, lambda b, g, i: (0, 0))],
        out_specs=[out_spec, out_spec, out_spec],
        out_shape=[out_sd, out_sd, out_sd],
        scratch_shapes=[pltpu.VMEM((rows, SEL_KEY_TILE), F32), pltpu.VMEM((rows, SEL_KEY_TILE), F32)],
        compiler_params=_cparams("arbitrary", "arbitrary", "arbitrary"),
    )(q, kcv, kcv, ks, vs, kw, vw, tcmp, tsel, twin, ov)


def _final_kernel(oc_ref, os_ref, ow_ref, z_ref, gates_ref, ex_ref, x1_ref, gate_ref, wo_ref, fg_ref, o_ref):
    gt = gates_ref[...]
    g_hi = gt.astype(BF16)
    g_lo = (gt - g_hi.astype(F32)).astype(BF16)
    ghl = jnp.concatenate([g_hi, g_lo], axis=1)
    y = None
    for br, o_ref_br in enumerate((oc_ref, os_ref, ow_ref)):
        gexp = jnp.dot(ghl, ex_ref[br], preferred_element_type=F32)
        term = gexp * o_ref_br[...] * _silu(z_ref[:, br * B_WIDTH:(br + 1) * B_WIDTH])
        y = term if y is None else y + term
    x2 = x1_ref[...] + gate_ref[...] * _dot(y, wo_ref[...])
    o_ref[...] = _rms(x2, fg_ref[...])


def _final(oc, osel, ow, z, gates, x1, gate1, wo, fg):
    bn, t, d = x1.shape
    tm = ROW_TILE
    ng = gates.shape[2]
    row = lambda b, i: (b, i, 0)
    ex = np.zeros((N_BRANCH, 2 * ng, B_WIDTH), np.float32)
    for br in range(N_BRANCH):
        for h in range(B_HEADS):
            ex[br, br * B_HEADS + h, h * B_HEAD_DIM:(h + 1) * B_HEAD_DIM] = 1.0
            ex[br, ng + br * B_HEADS + h, h * B_HEAD_DIM:(h + 1) * B_HEAD_DIM] = 1.0
    ex = jnp.asarray(ex, BF16)
    return pl.pallas_call(
        _final_kernel,
        grid=(bn, t // tm),
        in_specs=[pl.BlockSpec((None, tm, B_WIDTH), row)] * 3
        + [pl.BlockSpec((None, tm, N_BRANCH * B_WIDTH), row),
           pl.BlockSpec((None, tm, ng), row),
           pl.BlockSpec(ex.shape, lambda b, i: (0, 0, 0)),
           pl.BlockSpec((None, tm, d), row),
           pl.BlockSpec((None, 1, d), lambda b, i: (b, 0, 0)),
           pl.BlockSpec(wo.shape, lambda b, i: (0, 0)),
           pl.BlockSpec((1, d), lambda b, i: (0, 0))],
        out_specs=pl.BlockSpec((None, tm, d), row),
        out_shape=jax.ShapeDtypeStruct((bn, t, d), F32),
        compiler_params=_cparams("arbitrary", "arbitrary"),
    )(oc, osel, ow, z, gates, ex, x1, gate1, wo, fg)


def _overlap_matrix(ncp, n_cmp, n_slc, nblk):
    cells = np.arange(n_cmp)[:, None] + np.arange(L_CMP // CMP_STRIDE)[None, :]
    ov = (cells[:, None, :] // (L_SLC // CMP_STRIDE) == np.arange(n_slc)[None, :, None]).sum(-1)
    out = np.zeros((ncp, nblk), np.float32)
    out[:n_cmp, :n_slc] = ov
    return out


def _block_onehot(t, nblk):
    oh = np.zeros((KV_PAD + t, nblk), np.float32)
    oh[KV_PAD + np.arange(t), np.arange(t) // L_SLC] = 1.0
    oh[:KV_PAD, nblk - 1] = 1.0
    return oh


def kernel(x, c, rel_bias, ada_w, ada_b, norm_g, a_in_w, a_conv_w, a_A_log, a_dt_bias, a_onorm_g, a_out_w,
           kv_norm_g, kv_w, cmp_pos_k, cmp_pos_v, cmp_k_w1, cmp_k_w2, cmp_v_w1, cmp_v_w2,
           b_in_w, b_out_w, final_g):
    bn, t, d = x.shape
    assert ada_w.shape[0] == 2 and a_in_w.shape[0] == 1 and b_in_w.shape[0] == 1
    assert t % max(ROW_TILE, 2 * SEL_KEY_TILE, NSA_Q_TILE) == 0
    n_slc = t // L_SLC
    nblk = 64
    assert n_slc <= nblk
    n_cmp = (t - L_CMP) // CMP_STRIDE + 1
    ncp = t // CMP_STRIDE

    mod = _ada_modulation(c, ada_w, ada_b)
    shift = mod[:, :, None, :d]
    scale = mod[:, :, None, d:2 * d]
    gate = mod[:, :, None, 2 * d:]

    hb = GDN_HEADS_PER_STEP
    nhg = A_HEADS // hb
    w_in = a_in_w[0]
    wqkv = w_in[:, :3 * A_WIDTH].astype(BF16)
    wz = w_in[:, 3 * A_WIDTH:4 * A_WIDTH].astype(BF16)
    wb = w_in[:, 4 * A_WIDTH:4 * A_WIDTH + A_HEADS]
    wa = w_in[:, 4 * A_WIDTH + A_HEADS:]
    half = LANES // 2
    wba = jnp.zeros((d, nhg, LANES), F32)
    wba = wba.at[:, :, :hb].set(wb.reshape(d, nhg, hb)).at[:, :, half:half + hb].set(wa.reshape(d, nhg, hb))
    wba = wba.reshape(d, nhg * LANES).astype(BF16)
    lane_rows = lambda v: jnp.zeros((nhg, 1, LANES), F32).at[:, 0, half:half + hb].set(v.reshape(nhg, hb))
    qkv, z_a, ba = _in_proj_a(x, norm_g[0:1], scale[0], shift[0], wqkv, wz, wba)
    og = _gdn(qkv, z_a, ba, a_conv_w[0], lane_rows(a_A_log[0]), lane_rows(a_dt_bias[0]), a_onorm_g[0:1])

    ndh = B_GROUPS * B_HEAD_DIM
    wkvc = kv_w[:, :2 * ndh].astype(BF16)
    wkvr = kv_w[:, 2 * ndh:].astype(BF16)
    w_b = b_in_w[0]
    wq = w_b[:, :B_WIDTH].astype(BF16)
    wzb = w_b[:, B_WIDTH:4 * B_WIDTH].astype(BF16)
    wg = jnp.zeros((d, LANES), F32).at[:, :N_BRANCH * B_HEADS].set(w_b[:, 4 * B_WIDTH:]).astype(BF16)
    x1, kvc, kvr, q, z_b, gates = _post_a(x, og, gate[0], a_out_w[0].astype(BF16), kv_norm_g[None, :], wkvc, wkvr,
                                          norm_g[1:2], scale[1], shift[1], wq, wzb, wg)

    halves = kvc.reshape(bn, ncp, CMP_STRIDE, 2 * B_GROUPS, B_HEAD_DIM).transpose(0, 3, 1, 2, 4)
    halves = halves.reshape(bn, 2 * B_GROUPS, ncp, CMP_STRIDE * B_HEAD_DIM)
    pos = jnp.stack([cmp_pos_k, cmp_pos_v])
    hw = CMP_STRIDE * B_HEAD_DIM
    ptop = pos[:, :CMP_STRIDE].reshape(2, 1, hw)
    pbot = pos[:, CMP_STRIDE:].reshape(2, 1, hw)
    w1 = jnp.stack([cmp_k_w1, cmp_v_w1]).astype(BF16)
    w2 = jnp.stack([cmp_k_w2, cmp_v_w2]).astype(BF16)
    kcv = _compress(halves, ptop, pbot, w1[:, :hw], w1[:, hw:], w2)

    kvr4 = kvr.reshape(bn, t, 4, B_GROUPS, B_HEAD_DIM).transpose(2, 0, 3, 1, 4)
    kvr4 = jnp.pad(kvr4, ((0, 0), (0, 0), (0, 0), (KV_PAD, 0), (0, 0)))
    tp = KV_PAD + t

    def beside(x, extra):
        return jnp.concatenate([x, jnp.broadcast_to(jnp.asarray(extra, BF16), x.shape[:2] + extra.shape)], axis=-1)

    first_col = np.zeros((tp, B_HEAD_DIM), np.float32)
    ones_col = first_col.copy()
    ones_col[:, 0] = 1.0
    pad_col = first_col.copy()
    pad_col[:KV_PAD, 0] = 1.0
    ks = beside(kvr4[0], _block_onehot(t, nblk))
    vs = beside(kvr4[1], ones_col)
    kw = beside(kvr4[2], pad_col)
    vw = beside(kvr4[3], ones_col)

    twin, tsel, tcmp = _bias_tables(rel_bias)
    tc_hi = tcmp.astype(BF16)
    tc_lo = (tcmp - tc_hi.astype(F32)).astype(BF16)
    tcmp2 = jnp.concatenate([tc_hi, tc_lo], axis=-1)
    ov_t = _overlap_matrix(ncp, n_cmp, n_slc, nblk).T
    ov3 = jnp.asarray(np.concatenate([ov_t] * 3, axis=1), BF16)

    oc, osel, ow = _nsa(q, kcv, ks, vs, kw, vw, tcmp2, tsel, twin, ov3)

    return _final(oc, osel, ow, z_b, gates, x1, gate[1], b_out_w[0].astype(BF16), final_g[None, :])
```

```python
import functools
import math

import numpy as np
import jax
import jax.numpy as jnp
from jax import lax
from jax.experimental import pallas as pl
from jax.experimental.pallas import tpu as pltpu

F32 = jnp.float32
BF16 = jnp.bfloat16
HIGHEST = lax.Precision.HIGHEST

A_HEADS = 8
A_HEAD_DIM = 128
A_WIDTH = A_HEADS * A_HEAD_DIM
A_CONV = 4
A_CHUNK = 64
B_HEADS = 16
B_GROUPS = 2
B_HPG = B_HEADS // B_GROUPS
B_HEAD_DIM = 64
B_WIDTH = B_HEADS * B_HEAD_DIM
N_BRANCH = 3
L_CMP = 32
CMP_STRIDE = 16
L_SLC = 64
N_SEL = 16
WINDOW = 512
Q_BLOCK = 64
NUM_BUCKETS = 32
MAX_DISTANCE = 128
EPS = 1e-6
NEG_INF = -1e30
SEL_BOOST = 1e9

LANES = 128
SUBLANES = 8
VMEM_LIMIT_BYTES = 56 * 1024 * 1024

ROW_TILE = 256
GDN_HEADS_PER_STEP = 8
GDN_CHUNKS_PER_STEP = 2
SEL_KEY_TILE = 512
KV_PAD = WINDOW
NSA_Q_TILE = 128
NSA_SUB = NSA_Q_TILE // Q_BLOCK
SLC_SHIFT = L_SLC.bit_length() - 1
assert 1 << SLC_SHIFT == L_SLC and L_SLC == Q_BLOCK
CMP_LEAD = 12
CMP_NEAR = 32
assert CMP_NEAR >= CMP_LEAD + NSA_Q_TILE // CMP_STRIDE and NSA_Q_TILE % Q_BLOCK == 0
BLK16 = 16


def _cparams(*sem):
    return pltpu.CompilerParams(dimension_semantics=sem, vmem_limit_bytes=VMEM_LIMIT_BYTES)


def _sigmoid(x):
    return 1.0 / (1.0 + jnp.exp(-x))


def _silu(x):
    return x * _sigmoid(x)


def _dot(a, b):
    return jnp.dot(a.astype(BF16), b.astype(BF16), preferred_element_type=F32)


def _dot_nt(a, b):
    return lax.dot_general(a.astype(BF16), b.astype(BF16), (((1,), (1,)), ((), ())),
                           preferred_element_type=F32)


def _dot_f32(a, b):
    return jnp.dot(a, b, precision=HIGHEST, preferred_element_type=F32)


def _rms(x, g):
    ms = jnp.mean(x * x, axis=-1, keepdims=True)
    return x * lax.rsqrt(ms + EPS) * g


def _ada_kernel(c_ref, w_ref, b_ref, o_ref):
    o_ref[...] = _dot_f32(_silu(c_ref[...]), w_ref[...]) + b_ref[...]


def _ada_modulation(c, ada_w, ada_b):
    depth, d, d3 = ada_w.shape
    bn = c.shape[0]
    return pl.pallas_call(
        _ada_kernel,
        grid=(depth, d3 // d),
        in_specs=[pl.BlockSpec((bn, d), lambda l, j: (0, 0)),
                  pl.BlockSpec((None, d, d), lambda l, j: (l, 0, j)),
                  pl.BlockSpec((None, 1, d), lambda l, j: (l, 0, j))],
        out_specs=pl.BlockSpec((None, bn, d), lambda l, j: (l, 0, j)),
        out_shape=jax.ShapeDtypeStruct((depth, bn, d3), F32),
        compiler_params=_cparams("arbitrary", "arbitrary"),
    )(c, ada_w, ada_b.reshape(depth, 1, d3))


def _in_proj_a_kernel(x_ref, g_ref, scale_ref, shift_ref, wqkv_ref, wz_ref, wba_ref,
                      qkv_ref, z_ref, ba_ref):
    h = _rms(x_ref[...], g_ref[...]) * (1.0 + scale_ref[...]) + shift_ref[...]
    hb = h.astype(BF16)
    qkv_ref[...] = jnp.dot(hb, wqkv_ref[...], preferred_element_type=F32)
    z_ref[...] = jnp.dot(hb, wz_ref[...], preferred_element_type=F32)
    ba_ref[...] = jnp.dot(hb, wba_ref[...], preferred_element_type=F32)


def _in_proj_a(x, g, scale, shift, wqkv, wz, wba):
    bn, t, d = x.shape
    tm = ROW_TILE
    row = lambda b, i: (b, i, 0)
    per_b = lambda b, i: (b, 0, 0)
    const = lambda b, i: (0, 0)
    nba = wba.shape[1]
    return pl.pallas_call(
        _in_proj_a_kernel,
        grid=(bn, t // tm),
        in_specs=[pl.BlockSpec((None, tm, d), row),
                  pl.BlockSpec((1, d), const),
                  pl.BlockSpec((None, 1, d), per_b),
                  pl.BlockSpec((None, 1, d), per_b),
                  pl.BlockSpec(wqkv.shape, const),
                  pl.BlockSpec(wz.shape, const),
                  pl.BlockSpec(wba.shape, const)],
        out_specs=[pl.BlockSpec((None, tm, 3 * A_WIDTH), row),
                   pl.BlockSpec((None, tm, A_WIDTH), row),
                   pl.BlockSpec((None, tm, nba), row)],
        out_shape=[jax.ShapeDtypeStruct((bn, t, 3 * A_WIDTH), F32),
                   jax.ShapeDtypeStruct((bn, t, A_WIDTH), F32),
                   jax.ShapeDtypeStruct((bn, t, nba), F32)],
        compiler_params=_cparams("arbitrary", "arbitrary"),
    )(x, g, scale, shift, wqkv, wz, wba)


def _cumsum_rows(x):
    n = x.shape[0]
    row = lax.broadcasted_iota(jnp.int32, x.shape, 0)
    s = 1
    while s < n:
        x = x + jnp.where(row >= s, pltpu.roll(x, s, axis=0), 0.0)
        s *= 2
    return x


def _unit_lower_inverse(ms):
    c = ms[0].shape[0]
    row = lax.broadcasted_iota(jnp.int32, (c, c), 0)
    col = lax.broadcasted_iota(jnp.int32, (c, c), 1)
    eye = (row == col).astype(F32)
    same_blk = (row & -BLK16) == (col & -BLK16)
    d = [jnp.where(same_blk, m, 0.0) for m in ms]
    mo = [m - x for m, x in zip(ms, d)]
    d2 = [_dot(x, x) for x in d]
    td = [eye - x for x in d]
    d4 = [_dot(x, x) for x in d2]
    td = [t + _dot(t, x) for t, x in zip(td, d2)]
    d8 = [_dot(x, x) for x in d4]
    td = [t + _dot(t, x) for t, x in zip(td, d4)]
    td = [t + _dot(t, x) for t, x in zip(td, d8)]
    n = [_dot(t, x) for t, x in zip(td, mo)]
    n2 = [_dot(x, x) for x in n]
    r = [eye - x for x in n]
    r = [a + _dot(a, x) for a, x in zip(r, n2)]
    return [_dot(a, t) for a, t in zip(r, td)]


def _gdn_kernel(q_ref, k_ref, v_ref, z_ref, ba_ref, cwq_ref, cwk_ref, cwv_ref, alog_ref, dtb_ref,
                ong_ref, o_ref, buf_ref, s_ref, *, hb, nc):
    c = A_CHUNK
    dh = A_HEAD_DIM
    halo = SUBLANES
    rows = nc * c
    n = pl.program_id(2)

    @pl.when(n == 0)
    def _():
        buf_ref[:, 0:halo, :] = jnp.zeros((3, halo, hb * dh), F32)
        s_ref[...] = jnp.zeros(s_ref.shape, F32)

    def conv_silu(idx, x_ref, cw_ref):
        buf_ref[idx, halo:halo + rows, :] = x_ref[...]
        cw = cw_ref[...]
        off = halo - (A_CONV - 1)
        y = buf_ref[idx, off:off + rows, :] * cw[0:1, :]
        for kk in range(1, A_CONV):
            y = y + buf_ref[idx, off + kk:off + kk + rows, :] * cw[kk:kk + 1, :]
        buf_ref[idx, 0:halo, :] = x_ref[rows - halo:rows, :]
        return _silu(y)

    q_all = conv_silu(0, q_ref, cwq_ref)
    k_all = conv_silu(1, k_ref, cwk_ref)
    v_all = conv_silu(2, v_ref, cwv_ref)

    ba = ba_ref[...]
    beta_t = _sigmoid(ba)
    xa = ba + dtb_ref[...]
    softplus = jnp.maximum(xa, 0.0) + jnp.log(1.0 + jnp.exp(-jnp.abs(xa)))
    g_t = -jnp.exp(alog_ref[...]) * softplus
    gc_t = [_cumsum_rows(g_t[ci * c:(ci + 1) * c, :]) for ci in range(nc)]
    gc_tt = [x.T for x in gc_t]
    egc_t = [jnp.exp(x) for x in gc_t]
    ekd_t = [jnp.exp(x[c - 1:c, :] - x) for x in gc_t]
    egl_t = [jnp.exp(x[c - 1:c, :]) for x in gc_t]

    row = lax.broadcasted_iota(jnp.int32, (c, c), 0)
    col = lax.broadcasted_iota(jnp.int32, (c, c), 1)
    incl = row >= col
    strict = row > col
    heads = range(hb)
    jobs = [(ci, i) for ci in range(nc) for i in heads]
    la = LANES // 2

    def head(x, ci, i):
        return x[ci * c:(ci + 1) * c, i * dh:(i + 1) * dh]

    def lane(xs, ci, i):
        return xs[ci][:, la + i:la + i + 1]

    def l2n(x):
        return x * lax.rsqrt(jnp.sum(x * x, axis=-1, keepdims=True) + EPS)

    qn = [l2n(head(q_all, ci, i)) * (dh ** -0.5) for ci, i in jobs]
    kn = [l2n(head(k_all, ci, i)) for ci, i in jobs]
    knb = [x.astype(BF16) for x in kn]
    beta = [beta_t[ci * c:(ci + 1) * c, i:i + 1] for ci, i in jobs]
    kb = [x * y for x, y in zip(kn, beta)]
    decay = [jnp.where(incl, jnp.exp(jnp.where(incl, lane(gc_t, ci, i) - gc_tt[ci][la + i:la + i + 1, :], 0.0)), 0.0)
             for ci, i in jobs]
    m = [jnp.where(strict, _dot_nt(x, y) * d, 0.0) for x, y, d in zip(kb, knb, decay)]
    attn = [(_dot_nt(x, y) * d).astype(BF16) for x, y, d in zip(qn, knb, decay)]
    rhs = [jnp.concatenate([head(v_all, ci, i) * beta[j], kb[j] * lane(egc_t, ci, i)], axis=1).astype(BF16)
           for j, (ci, i) in enumerate(jobs)]
    qdec = [(qn[j] * lane(egc_t, ci, i)).astype(BF16) for j, (ci, i) in enumerate(jobs)]
    kdec_t = [(kn[j] * lane(ekd_t, ci, i)).T.astype(BF16) for j, (ci, i) in enumerate(jobs)]
    tinv = _unit_lower_inverse(m)
    uw = [_dot(x, y) for x, y in zip(tinv, rhs)]

    s_cur = [s_ref[i] for i in heads]
    for ci in range(nc):
        sb = [x.astype(BF16) for x in s_cur]
        job = [ci * hb + i for i in heads]
        v_new = [uw[j][:, :dh] - _dot(uw[j][:, dh:], sb[i]) for i, j in zip(heads, job)]
        vnb = [x.astype(BF16) for x in v_new]
        s_cur = [s_cur[i] * lane(egl_t, ci, i) + jnp.dot(kdec_t[j], vnb[i], preferred_element_type=F32)
                 for i, j in zip(heads, job)]
        o = [jnp.dot(qdec[j], sb[i], preferred_element_type=F32)
             + jnp.dot(attn[j], vnb[i], preferred_element_type=F32) for i, j in zip(heads, job)]
        for i in heads:
            rs = slice(ci * c, (ci + 1) * c)
            sl = slice(i * dh, (i + 1) * dh)
            o_ref[rs, sl] = _rms(o[i], ong_ref[...]) * _silu(z_ref[rs, sl])
    for i in heads:
        s_ref[i] = s_cur[i]


def _gdn(qkv, z, ba, conv_w, alog_row, dtb_row, onorm_g):
    bn, t, _ = qkv.shape
    hb = GDN_HEADS_PER_STEP
    nc = GDN_CHUNKS_PER_STEP
    nhg = A_HEADS // hb
    c = nc * A_CHUNK
    w = hb * A_HEAD_DIM
    kern = functools.partial(_gdn_kernel, hb=hb, nc=nc)

    def qkv_spec(which):
        return pl.BlockSpec((None, c, w), lambda b, hg, n: (b, n, which * nhg + hg))

    def cw_spec(which):
        return pl.BlockSpec((A_CONV, w), lambda b, hg, n: (0, which * nhg + hg))

    hg_row = pl.BlockSpec((None, 1, LANES), lambda b, hg, n: (hg, 0, 0))
    return pl.pallas_call(
        kern,
        grid=(bn, nhg, t // c),
        in_specs=[qkv_spec(0), qkv_spec(1), qkv_spec(2),
                  pl.BlockSpec((None, c, w), lambda b, hg, n: (b, n, hg)),
                  pl.BlockSpec((None, c, LANES), lambda b, hg, n: (b, n, hg)),
                  cw_spec(0), cw_spec(1), cw_spec(2),
                  hg_row, hg_row,
                  pl.BlockSpec((1, A_HEAD_DIM), lambda b, hg, n: (0, 0))],
        out_specs=pl.BlockSpec((None, c, w), lambda b, hg, n: (b, n, hg)),
        out_shape=jax.ShapeDtypeStruct((bn, t, A_WIDTH), F32),
        scratch_shapes=[pltpu.VMEM((3, SUBLANES + c, w), F32),
                        pltpu.VMEM((hb, A_HEAD_DIM, A_HEAD_DIM), F32)],
        compiler_params=_cparams("arbitrary", "arbitrary", "arbitrary"),
    )(qkv, qkv, qkv, z, ba, conv_w, conv_w, conv_w, alog_row, dtb_row, onorm_g)


def _post_a_kernel(x_ref, og_ref, gate_ref, wo_ref, kvg_ref, wkvc_ref, wkvr_ref, g1_ref, scale_ref,
                   shift_ref, wq_ref, wz_ref, wg_ref,
                   x1_ref, kvc_ref, kvr_ref, q_ref, z_ref, gates_ref):
    out = _dot(og_ref[...], wo_ref[...])
    x1 = x_ref[...] + gate_ref[...] * out
    x1_ref[...] = x1
    sb = _rms(x1, kvg_ref[...]).astype(BF16)
    kvc_ref[...] = jnp.dot(sb, wkvc_ref[...], preferred_element_type=F32)
    kvr_ref[...] = jnp.dot(sb, wkvr_ref[...], preferred_element_type=F32).astype(BF16)
    h = _rms(x1, g1_ref[...]) * (1.0 + scale_ref[...]) + shift_ref[...]
    hb = h.astype(BF16)
    q = jnp.dot(hb, wq_ref[...], preferred_element_type=F32) * (B_HEAD_DIM ** -0.5)
    q_ref[...] = q.astype(BF16)
    z_ref[...] = jnp.dot(hb, wz_ref[...], preferred_element_type=F32)
    gates_ref[...] = _sigmoid(jnp.dot(hb, wg_ref[...], preferred_element_type=F32))


def _post_a(x, og, gate0, wo, kvg, wkvc, wkvr, g1, scale1, shift1, wq, wz, wg):
    bn, t, d = x.shape
    tm = ROW_TILE
    row = lambda b, i: (b, i, 0)
    per_b = lambda b, i: (b, 0, 0)
    const = lambda b, i: (0, 0)
    full = lambda a: pl.BlockSpec(a.shape, const)
    vec = pl.BlockSpec((1, d), const)
    bvec = pl.BlockSpec((None, 1, d), per_b)
    outs = [(d, F32), (wkvc.shape[1], F32), (wkvr.shape[1], BF16), (wq.shape[1], BF16),
            (wz.shape[1], F32), (wg.shape[1], F32)]
    return pl.pallas_call(
        _post_a_kernel,
        grid=(bn, t // tm),
        in_specs=[pl.BlockSpec((None, tm, d), row), pl.BlockSpec((None, tm, A_WIDTH), row), bvec,
                  full(wo), vec, full(wkvc), full(wkvr), vec, bvec, bvec, full(wq), full(wz), full(wg)],
        out_specs=[pl.BlockSpec((None, tm, n), row) for n, _ in outs],
        out_shape=[jax.ShapeDtypeStruct((bn, t, n), dt) for n, dt in outs],
        compiler_params=_cparams("arbitrary", "arbitrary"),
    )(x, og, gate0, wo, kvg, wkvc, wkvr, g1, scale1, shift1, wq, wz, wg)


def _compress_kernel(hv_ref, ptop_ref, pbot_ref, w1t_ref, w1b_ref, w2_ref, o_ref):
    hv = hv_ref[...]
    a = _dot(hv + ptop_ref[...], w1t_ref[...])
    b = _dot(hv + pbot_ref[...], w1b_ref[...])
    nrow = a.shape[0]
    hid = a + pltpu.roll(b, nrow - 1, axis=0)
    o_ref[...] = _dot(_silu(hid), w2_ref[...]).astype(o_ref.dtype)


def _compress(halves, ptop, pbot, w1t, w1b, w2):
    bn, four, nh, wd = halves.shape
    hid = w1t.shape[-1]
    per_kind = lambda b, j: (j // B_GROUPS, 0, 0)
    return pl.pallas_call(
        _compress_kernel,
        grid=(bn, four),
        in_specs=[pl.BlockSpec((None, None, nh, wd), lambda b, j: (b, j, 0, 0)),
                  pl.BlockSpec((None, 1, wd), per_kind),
                  pl.BlockSpec((None, 1, wd), per_kind),
                  pl.BlockSpec((None, wd, hid), per_kind),
                  pl.BlockSpec((None, wd, hid), per_kind),
                  pl.BlockSpec((None, hid, B_HEAD_DIM), per_kind)],
        out_specs=pl.BlockSpec((None, None, nh, B_HEAD_DIM), lambda b, j: (b, j, 0, 0)),
        out_shape=jax.ShapeDtypeStruct((bn, four, nh, B_HEAD_DIM), BF16),
        compiler_params=_cparams("arbitrary", "arbitrary"),
    )(halves, ptop, pbot, w1t, w1b, w2)


def _t5_bucket_np(dist):
    n = np.maximum(dist, 0)
    max_exact = NUM_BUCKETS // 2
    nf = np.maximum(n, 1).astype(np.float64)
    val = np.log(nf / max_exact) / math.log(MAX_DISTANCE / max_exact) * (NUM_BUCKETS - max_exact)
    frac = np.abs(val - np.round(val))
    safe = (frac > 1e-6) | (n <= max_exact) | (n >= MAX_DISTANCE)
    assert bool(np.all(safe)), "bucket boundary too close to an integer distance"
    large = np.minimum(max_exact + np.floor(np.maximum(val, 0.0)).astype(np.int64), NUM_BUCKETS - 1)
    return np.where(n < max_exact, n, large)


def _bias_onehot():
    r = np.arange(NSA_Q_TILE)[:, None]
    tiles = []
    j = np.arange(WINDOW + NSA_Q_TILE)[None, :]
    d = r + WINDOW - j
    tiles.append((d, (d >= 0) & (d < WINDOW)))
    j = np.arange((NSA_SUB + 2) * L_SLC)[None, :]
    d = r + 2 * L_SLC - j
    tiles.append((d, d >= 0))
    j = np.arange(CMP_NEAR)[None, :]
    d = r - CMP_STRIDE * (j - CMP_LEAD) - (L_CMP - 1)
    tiles.append((d, d >= 0))
    cols = [np.where(valid, _t5_bucket_np(d), NUM_BUCKETS).reshape(-1) for d, valid in tiles]
    widths = [c.size for c in cols]
    return np.concatenate(cols).astype(np.int32)[None, :], widths


def _bias_kernel(rb_ref, bk_ref, o_ref):
    rb = rb_ref[...]
    lane = lax.broadcasted_iota(jnp.int32, rb.shape, 1)
    rbs = rb - rb[:, NUM_BUCKETS - 1:NUM_BUCKETS]
    rbs = jnp.where(lane < NUM_BUCKETS, rbs, jnp.where(lane == NUM_BUCKETS, NEG_INF, 0.0))
    bk = bk_ref[...]
    onehot = jnp.where(lax.broadcasted_iota(jnp.int32, (2 * NUM_BUCKETS, bk.shape[1]), 0) == bk, 1.0, 0.0)
    o_ref[...] = _dot_f32(rbs, onehot)


def _bias_tables(rel_bias):
    bk, widths = _bias_onehot()
    ncol = bk.shape[1]
    nt = 8
    assert ncol % (nt * LANES) == 0
    tc = ncol // nt
    rb = jnp.concatenate([rel_bias.T, jnp.zeros((B_HEADS, NUM_BUCKETS), F32)], axis=1)
    flat = pl.pallas_call(
        _bias_kernel,
        grid=(nt,),
        in_specs=[pl.BlockSpec((B_HEADS, 2 * NUM_BUCKETS), lambda i: (0, 0)),
                  pl.BlockSpec((1, tc), lambda i: (0, i))],
        out_specs=pl.BlockSpec((B_HEADS, tc), lambda i: (0, i)),
        out_shape=jax.ShapeDtypeStruct((B_HEADS, ncol), F32),
        compiler_params=_cparams("arbitrary"),
    )(rb, jnp.asarray(bk))
    out, start = [], 0
    for wd in widths:
        tile = flat[:, start:start + wd].reshape(B_GROUPS, B_HPG * NSA_Q_TILE, wd // NSA_Q_TILE)
        out.append(tile)
        start += wd
    return out


def _nsa_kernel(q_ref, kc_ref, vc_ref, ks_ref, vs_ref, kw_ref, vw_ref, tcmp_ref, tsel_ref, twin_ref,
                ov_ref, oc_ref, os_ref, ow_ref, sa_ref, sb_ref):
    tq = NSA_Q_TILE
    dh = B_HEAD_DIM
    hpg = B_HPG
    rows = hpg * tq
    ti = pl.program_id(2)
    q0 = ti * tq
    blk0 = ti * NSA_SUB
    qt = q_ref[...]
    q = jnp.concatenate([qt[:, h * dh:(h + 1) * dh] for h in range(hpg)], axis=0)

    def to_tokens(o):
        return jnp.concatenate([o[h * tq:(h + 1) * tq, :] for h in range(hpg)], axis=1)

    def finish(pv):
        return to_tokens(pv[:, :dh] * (1.0 / pv[:, dh:dh + 1]))

    kc = kc_ref[...]
    ncp = kc.shape[0]
    nw = WINDOW + tq
    win0 = pl.multiple_of(q0, tq)
    first_near = (tq // CMP_STRIDE) * ti - CMP_LEAD
    cid = lax.broadcasted_iota(jnp.int32, (2 * CMP_NEAR, ncp), 1)
    jrow = lax.broadcasted_iota(jnp.int32, (2 * CMP_NEAR, ncp), 0) & (CMP_NEAR - 1)
    shift_eye = jnp.where(cid - first_near == jrow, 1.0, 0.0).astype(BF16)
    pad_col = jnp.where(lax.broadcasted_iota(jnp.int32, (rows, dh), 1) == 0, NEG_INF, 0.0).astype(BF16)
    q_win = jnp.concatenate([q, pad_col], axis=1)

    s = _dot_nt(q, kc) + jnp.dot(tcmp_ref[...], shift_eye, preferred_element_type=F32)
    s_w = _dot_nt(q_win, kw_ref[pl.ds(win0, nw), :]) + twin_ref[...]

    cvis = lax.broadcasted_iota(jnp.int32, (1, ncp), 1) < first_near + CMP_NEAR
    s = jnp.where(cvis, s, NEG_INF)
    live = s > 0.1 * NEG_INF
    m = jnp.max(s, axis=-1, keepdims=True)
    e = jnp.where(live, jnp.exp(s - m), 0.0)
    p = e * (1.0 / jnp.maximum(jnp.sum(e, axis=-1, keepdims=True), 1e-30))
    oc_ref[...] = to_tokens(_dot(p, vc_ref[...]))

    psum = p[0:tq, :]
    for h in range(1, hpg):
        psum = psum + p[h * tq:(h + 1) * tq, :]
    p_hi = psum.astype(BF16)
    p_r1 = psum - p_hi.astype(F32)
    p_mid = p_r1.astype(BF16)
    p_lo = (p_r1 - p_mid.astype(F32)).astype(BF16)
    p3 = jnp.concatenate([p_hi, p_mid, p_lo], axis=1)
    imp_t = _dot_nt(ov_ref[...], p3)

    m_w = jnp.max(s_w, axis=-1, keepdims=True)
    e_w = jnp.exp(s_w - m_w)
    ow_ref[...] = finish(_dot(e_w, vw_ref[pl.ds(win0, nw), :]))

    nblk = imp_t.shape[0]
    blk = lax.broadcasted_iota(jnp.int32, (nblk, tq), 0)
    cur = blk0 + (lax.broadcasted_iota(jnp.int32, (nblk, tq), 1) >> SLC_SHIFT)
    forced = (blk == 0) | (blk == cur) | (blk == cur - 1)
    val = jnp.where(forced, SEL_BOOST, jnp.where(blk > cur, -SEL_BOOST, imp_t))
    nslab = nblk // SUBLANES
    slabs = [val[SUBLANES * r:SUBLANES * (r + 1), :] for r in range(nslab)]
    sub = lax.broadcasted_iota(jnp.int32, (SUBLANES, tq), 0)
    n_acc = 4
    ranks = [[jnp.zeros((SUBLANES, tq), jnp.int32) for _ in range(n_acc)] for _ in range(nslab)]
    for j in range(nblk):
        vj = jnp.broadcast_to(val[j:j + 1, :], (SUBLANES, tq))
        for r in range(nslab):
            lo = SUBLANES * r
            if lo > j:
                ahead = vj >= slabs[r]
            elif lo + SUBLANES - 1 <= j:
                ahead = vj > slabs[r]
            else:
                ahead = (vj > slabs[r]) | ((vj == slabs[r]) & (sub > j - lo))
            ranks[r][j % n_acc] = ranks[r][j % n_acc] + ahead.astype(jnp.int32)
    rank = jnp.concatenate([(a[0] + a[1]) + (a[2] + a[3]) for a in ranks], axis=0)
    sel_t = (rank < N_SEL) & (blk <= cur)
    far_t = jnp.where(sel_t & (blk <= blk0 - 3), 0.0, NEG_INF)
    near_t = jnp.where(sel_t & (blk >= blk0 - 2), 0.0, NEG_INF)

    def q_with_mask(mask_t):
        mk = mask_t.T.astype(BF16)
        return jnp.concatenate([q, jnp.concatenate([mk] * hpg, axis=0)], axis=1)

    q_far = q_with_mask(far_t)
    q_near = q_with_mask(near_t)

    kt_sz = SEL_KEY_TILE
    n_far_keys = jnp.maximum(blk0 - 2, 0) * L_SLC
    n_pairs = (n_far_keys + 2 * kt_sz - 1) // (2 * kt_sz)

    def far_scores(tile):
        start = pl.multiple_of(KV_PAD + tile * kt_sz, kt_sz)
        return _dot_nt(q_far, ks_ref[pl.ds(start, kt_sz), :])

    def far_values(tile):
        start = pl.multiple_of(KV_PAD + tile * kt_sz, kt_sz)
        return vs_ref[pl.ds(start, kt_sz), :]

    def update(carry, s_t, v_t):
        m_i, acc = carry
        m_n = jnp.maximum(m_i, jnp.max(s_t, axis=-1, keepdims=True))
        e_t = jnp.exp(s_t - m_n)
        return m_n, jnp.exp(m_i - m_n) * acc + _dot(e_t, v_t)

    def pair_step(j, carry):
        sb_ref[...] = far_scores(2 * j + 1)
        carry = update(carry, sa_ref[...], far_values(2 * j))
        sa_ref[...] = far_scores(2 * j + 2)
        return update(carry, sb_ref[...], far_values(2 * j + 1))

    sa_ref[...] = far_scores(0)
    last = jnp.maximum(n_pairs, 1) - 1
    carry = (jnp.full((rows, 1), NEG_INF, F32), jnp.zeros((rows, 2 * dh), F32))
    carry = lax.fori_loop(0, last, pair_step, carry)
    sb_ref[...] = far_scores(2 * last + 1)
    carry = update(carry, sa_ref[...], far_values(2 * last))
    nk = (NSA_SUB + 2) * L_SLC
    near0 = pl.multiple_of(KV_PAD + q0 - 2 * L_SLC, L_SLC)
    s_n = _dot_nt(q_near, ks_ref[pl.ds(near0, nk), :]) + tsel_ref[...]
    carry = update(carry, sb_ref[...], far_values(2 * last + 1))
    _, acc = update(carry, s_n, vs_ref[pl.ds(near0, nk), :])
    os_ref[...] = finish(acc)


def _nsa(q, kcv, ks, vs, kw, vw, tcmp, tsel, twin, ov):
    bn, t, _ = q.shape
    tq = NSA_Q_TILE
    gw = B_HPG * B_HEAD_DIM
    ncp = kcv.shape[2]
    tp = ks.shape[2]
    rows = B_HPG * tq
    per_bg = pl.BlockSpec((None, None, tp, ks.shape[3]), lambda b, g, i: (b, g, 0, 0))
    per_g = lambda b, g, i: (g, 0, 0)
    out_spec = pl.BlockSpec((None, tq, gw), lambda b, g, i: (b, i, g))
    out_sd = jax.ShapeDtypeStruct((bn, t, B_WIDTH), F32)
    return pl.pallas_call(
        _nsa_kernel,
        grid=(bn, B_GROUPS, t // tq),
        in_specs=[pl.BlockSpec((None, tq, gw), lambda b, g, i: (b, i, g)),
                  pl.BlockSpec((None, None, ncp, B_HEAD_DIM), lambda b, g, i: (b, g, 0, 0)),
                  pl.BlockSpec((None, None, ncp, B_HEAD_DIM), lambda b, g, i: (b, B_GROUPS + g, 0, 0)),
                  per_bg, per_bg, per_bg, per_bg,
                  pl.BlockSpec((None, rows, tcmp.shape[2]), per_g),
                  pl.BlockSpec((None, rows, tsel.shape[2]), per_g),
                  pl.BlockSpec((None, rows, twin.shape[2]), per_g),
                  pl.BlockSpec(ov.shape, lambda b, g, i: (0, 0))],
        out_specs=[out_spec, out_spec, out_spec],
        out_shape=[out_sd, out_sd, out_sd],
        scratch_shapes=[pltpu.VMEM((rows, SEL_KEY_TILE), F32), pltpu.VMEM((rows, SEL_KEY_TILE), F32)],
        compiler_params=_cparams("arbitrary", "arbitrary", "arbitrary"),
    )(q, kcv, kcv, ks, vs, kw, vw, tcmp, tsel, twin, ov)


def _final_kernel(oc_ref, os_ref, ow_ref, z_ref, gates_ref, ex_ref, x1_ref, gate_ref, wo_ref, fg_ref, o_ref):
    gt = gates_ref[...]
    g_hi = gt.astype(BF16)
    g_lo = (gt - g_hi.astype(F32)).astype(BF16)
    ghl = jnp.concatenate([g_hi, g_lo], axis=1)
    y = None
    for br, o_ref_br in enumerate((oc_ref, os_ref, ow_ref)):
        gexp = jnp.dot(ghl, ex_ref[br], preferred_element_type=F32)
        term = gexp * o_ref_br[...] * _silu(z_ref[:, br * B_WIDTH:(br + 1) * B_WIDTH])
        y = term if y is None else y + term
    x2 = x1_ref[...] + gate_ref[...] * _dot(y, wo_ref[...])
    o_ref[...] = _rms(x2, fg_ref[...])


def _final(oc, osel, ow, z, gates, x1, gate1, wo, fg):
    bn, t, d = x1.shape
    tm = ROW_TILE
    ng = gates.shape[2]
    row = lambda b, i: (b, i, 0)
    ex = np.zeros((N_BRANCH, 2 * ng, B_WIDTH), np.float32)
    for br in range(N_BRANCH):
        for h in range(B_HEADS):
            ex[br, br * B_HEADS + h, h * B_HEAD_DIM:(h + 1) * B_HEAD_DIM] = 1.0
            ex[br, ng + br * B_HEADS + h, h * B_HEAD_DIM:(h + 1) * B_HEAD_DIM] = 1.0
    ex = jnp.asarray(ex, BF16)
    return pl.pallas_call(
        _final_kernel,
        grid=(bn, t // tm),
        in_specs=[pl.BlockSpec((None, tm, B_WIDTH), row)] * 3
        + [pl.BlockSpec((None, tm, N_BRANCH * B_WIDTH), row),
           pl.BlockSpec((None, tm, ng), row),
           pl.BlockSpec(ex.shape, lambda b, i: (0, 0, 0)),
           pl.BlockSpec((None, tm, d), row),
           pl.BlockSpec((None, 1, d), lambda b, i: (b, 0, 0)),
           pl.BlockSpec(wo.shape, lambda b, i: (0, 0)),
           pl.BlockSpec((1, d), lambda b, i: (0, 0))],
        out_specs=pl.BlockSpec((None, tm, d), row),
        out_shape=jax.ShapeDtypeStruct((bn, t, d), F32),
        compiler_params=_cparams("arbitrary", "arbitrary"),
    )(oc, osel, ow, z, gates, ex, x1, gate1, wo, fg)


def _overlap_matrix(ncp, n_cmp, n_slc, nblk):
    cells = np.arange(n_cmp)[:, None] + np.arange(L_CMP // CMP_STRIDE)[None, :]
    ov = (cells[:, None, :] // (L_SLC // CMP_STRIDE) == np.arange(n_slc)[None, :, None]).sum(-1)
    out = np.zeros((ncp, nblk), np.float32)
    out[:n_cmp, :n_slc] = ov
    return out


def _block_onehot(t, nblk):
    oh = np.zeros((KV_PAD + t, nblk), np.float32)
    oh[KV_PAD + np.arange(t), np.arange(t) // L_SLC] = 1.0
    oh[:KV_PAD, nblk - 1] = 1.0
    return oh


def kernel(x, c, rel_bias, ada_w, ada_b, norm_g, a_in_w, a_conv_w, a_A_log, a_dt_bias, a_onorm_g, a_out_w,
           kv_norm_g, kv_w, cmp_pos_k, cmp_pos_v, cmp_k_w1, cmp_k_w2, cmp_v_w1, cmp_v_w2,
           b_in_w, b_out_w, final_g):
    bn, t, d = x.shape
    assert ada_w.shape[0] == 2 and a_in_w.shape[0] == 1 and b_in_w.shape[0] == 1
    assert t % max(ROW_TILE, 2 * SEL_KEY_TILE, NSA_Q_TILE) == 0
    n_slc = t // L_SLC
    nblk = 64
    assert n_slc <= nblk
    n_cmp = (t - L_CMP) // CMP_STRIDE + 1
    ncp = t // CMP_STRIDE

    mod = _ada_modulation(c, ada_w, ada_b)
    shift = mod[:, :, None, :d]
    scale = mod[:, :, None, d:2 * d]
    gate = mod[:, :, None, 2 * d:]

    hb = GDN_HEADS_PER_STEP
    nhg = A_HEADS // hb
    w_in = a_in_w[0]
    wqkv = w_in[:, :3 * A_WIDTH].astype(BF16)
    wz = w_in[:, 3 * A_WIDTH:4 * A_WIDTH].astype(BF16)
    wb = w_in[:, 4 * A_WIDTH:4 * A_WIDTH + A_HEADS]
    wa = w_in[:, 4 * A_WIDTH + A_HEADS:]
    half = LANES // 2
    wba = jnp.zeros((d, nhg, LANES), F32)
    wba = wba.at[:, :, :hb].set(wb.reshape(d, nhg, hb)).at[:, :, half:half + hb].set(wa.reshape(d, nhg, hb))
    wba = wba.reshape(d, nhg * LANES).astype(BF16)
    lane_rows = lambda v: jnp.zeros((nhg, 1, LANES), F32).at[:, 0, half:half + hb].set(v.reshape(nhg, hb))
    qkv, z_a, ba = _in_proj_a(x, norm_g[0:1], scale[0], shift[0], wqkv, wz, wba)
    og = _gdn(qkv, z_a, ba, a_conv_w[0], lane_rows(a_A_log[0]), lane_rows(a_dt_bias[0]), a_onorm_g[0:1])

    ndh = B_GROUPS * B_HEAD_DIM
    wkvc = kv_w[:, :2 * ndh].astype(BF16)
    wkvr = kv_w[:, 2 * ndh:].astype(BF16)
    w_b = b_in_w[0]
    wq = w_b[:, :B_WIDTH].astype(BF16)
    wzb = w_b[:, B_WIDTH:4 * B_WIDTH].astype(BF16)
    wg = jnp.zeros((d, LANES), F32).at[:, :N_BRANCH * B_HEADS].set(w_b[:, 4 * B_WIDTH:]).astype(BF16)
    x1, kvc, kvr, q, z_b, gates = _post_a(x, og, gate[0], a_out_w[0].astype(BF16), kv_norm_g[None, :], wkvc, wkvr,
                                          norm_g[1:2], scale[1], shift[1], wq, wzb, wg)

    halves = kvc.reshape(bn, ncp, CMP_STRIDE, 2 * B_GROUPS, B_HEAD_DIM).transpose(0, 3, 1, 2, 4)
    halves = halves.reshape(bn, 2 * B_GROUPS, ncp, CMP_STRIDE * B_HEAD_DIM)
    pos = jnp.stack([cmp_pos_k, cmp_pos_v])
    hw = CMP_STRIDE * B_HEAD_DIM
    ptop = pos[:, :CMP_STRIDE].reshape(2, 1, hw)
    pbot = pos[:, CMP_STRIDE:].reshape(2, 1, hw)
    w1 = jnp.stack([cmp_k_w1, cmp_v_w1]).astype(BF16)
    w2 = jnp.stack([cmp_k_w2, cmp_v_w2]).astype(BF16)
    kcv = _compress(halves, ptop, pbot, w1[:, :hw], w1[:, hw:], w2)

    kvr4 = kvr.reshape(bn, t, 4, B_GROUPS, B_HEAD_DIM).transpose(2, 0, 3, 1, 4)
    kvr4 = jnp.pad(kvr4, ((0, 0), (0, 0), (0, 0), (KV_PAD, 0), (0, 0)))
    tp = KV_PAD + t

    def beside(x, extra):
        return jnp.concatenate([x, jnp.broadcast_to(jnp.asarray(extra, BF16), x.shape[:2] + extra.shape)], axis=-1)

    first_col = np.zeros((tp, B_HEAD_DIM), np.float32)
    ones_col = first_col.copy()
    ones_col[:, 0] = 1.0
    pad_col = first_col.copy()
    pad_col[:KV_PAD, 0] = 1.0
    ks = beside(kvr4[0], _block_onehot(t, nblk))
    vs = beside(kvr4[1], ones_col)
    kw = beside(kvr4[2], pad_col)
    vw = beside(kvr4[3], ones_col)

    twin, tsel, tcmp = _bias_tables(rel_bias)
    tc_hi = tcmp.astype(BF16)
    tc_lo = (tcmp - tc_hi.astype(F32)).astype(BF16)
    tcmp2 = jnp.concatenate([tc_hi, tc_lo], axis=-1)
    ov_t = _overlap_matrix(ncp, n_cmp, n_slc, nblk).T
    ov3 = jnp.asarray(np.concatenate([ov_t] * 3, axis=1), BF16)

    oc, osel, ow = _nsa(q, kcv, ks, vs, kw, vw, tcmp2, tsel, twin, ov3)

    return _final(oc, osel, ow, z_b, gates, x1, gate[1], b_out_w[0].astype(BF16), final_g[None, :])
```

```python
import functools
import math

import numpy as np
import jax
import jax.numpy as jnp
from jax import lax
from jax.experimental import pallas as pl
from jax.experimental.pallas import tpu as pltpu

F32 = jnp.float32
BF16 = jnp.bfloat16
HIGHEST = lax.Precision.HIGHEST

A_HEADS = 8
A_HEAD_DIM = 128
A_WIDTH = A_HEADS * A_HEAD_DIM
A_CONV = 4
A_CHUNK = 64
B_HEADS = 16
B_GROUPS = 2
B_HPG = B_HEADS // B_GROUPS
B_HEAD_DIM = 64
B_WIDTH = B_HEADS * B_HEAD_DIM
N_BRANCH = 3
L_CMP = 32
CMP_STRIDE = 16
L_SLC = 64
N_SEL = 16
WINDOW = 512
Q_BLOCK = 64
NUM_BUCKETS = 32
MAX_DISTANCE = 128
EPS = 1e-6
NEG_INF = -1e30
SEL_BOOST = 1e9

LANES = 128
SUBLANES = 8
VMEM_LIMIT_BYTES = 56 * 1024 * 1024

ROW_TILE = 256
GDN_HEADS_PER_STEP = 8
GDN_CHUNKS_PER_STEP = 2
SEL_KEY_TILE = 512
KV_PAD = WINDOW
NSA_Q_TILE = 128
NSA_SUB = NSA_Q_TILE // Q_BLOCK
SLC_SHIFT = L_SLC.bit_length() - 1
assert 1 << SLC_SHIFT == L_SLC and L_SLC == Q_BLOCK
CMP_LEAD = 12
CMP_NEAR = 32
assert CMP_NEAR >= CMP_LEAD + NSA_Q_TILE // CMP_STRIDE and NSA_Q_TILE % Q_BLOCK == 0
BLK16 = 16


def _cparams(*sem):
    return pltpu.CompilerParams(dimension_semantics=sem, vmem_limit_bytes=VMEM_LIMIT_BYTES)


def _sigmoid(x):
    return 1.0 / (1.0 + jnp.exp(-x))


def _silu(x):
    return x * _sigmoid(x)


def _dot(a, b):
    return jnp.dot(a.astype(BF16), b.astype(BF16), preferred_element_type=F32)


def _dot_nt(a, b):
    return lax.dot_general(a.astype(BF16), b.astype(BF16), (((1,), (1,)), ((), ())),
                           preferred_element_type=F32)


def _dot_f32(a, b):
    return jnp.dot(a, b, precision=HIGHEST, preferred_element_type=F32)


def _rms(x, g):
    ms = jnp.mean(x * x, axis=-1, keepdims=True)
    return x * lax.rsqrt(ms + EPS) * g


def _ada_kernel(c_ref, w_ref, b_ref, o_ref):
    o_ref[...] = _dot_f32(_silu(c_ref[...]), w_ref[...]) + b_ref[...]


def _ada_modulation(c, ada_w, ada_b):
    depth, d, d3 = ada_w.shape
    bn = c.shape[0]
    return pl.pallas_call(
        _ada_kernel,
        grid=(depth, d3 // d),
        in_specs=[pl.BlockSpec((bn, d), lambda l, j: (0, 0)),
                  pl.BlockSpec((None, d, d), lambda l, j: (l, 0, j)),
                  pl.BlockSpec((None, 1, d), lambda l, j: (l, 0, j))],
        out_specs=pl.BlockSpec((None, bn, d), lambda l, j: (l, 0, j)),
        out_shape=jax.ShapeDtypeStruct((depth, bn, d3), F32),
        compiler_params=_cparams("arbitrary", "arbitrary"),
    )(c, ada_w, ada_b.reshape(depth, 1, d3))


def _in_proj_a_kernel(x_ref, g_ref, scale_ref, shift_ref, wqkv_ref, wz_ref, wba_ref, cw_ref,
                      q_ref, k_ref, v_ref, z_ref, ba_ref, buf_ref):
    tm = x_ref.shape[0]
    halo = SUBLANES
    dh = A_HEAD_DIM

    @pl.when(pl.program_id(1) == 0)
    def _():
        buf_ref[0:halo, :] = jnp.zeros((halo, 3 * A_WIDTH), F32)

    h = _rms(x_ref[...], g_ref[...]) * (1.0 + scale_ref[...]) + shift_ref[...]
    hb = h.astype(BF16)
    buf_ref[halo:halo + tm, :] = jnp.dot(hb, wqkv_ref[...], preferred_element_type=F32)
    cw = cw_ref[...]
    xp = buf_ref[...]
    y = xp[halo:, :] * cw[A_CONV - 1:A_CONV, :]
    for kk in range(A_CONV - 1):
        y = y + pltpu.roll(xp, A_CONV - 1 - kk, axis=0)[halo:, :] * cw[kk:kk + 1, :]
    buf_ref[0:halo, :] = xp[tm:, :]
    y = _silu(y)
    for i in range(A_HEADS):
        for which, o_ref, gain in ((0, q_ref, dh ** -0.5), (1, k_ref, 1.0)):
            xh = y[:, which * A_WIDTH + i * dh:which * A_WIDTH + (i + 1) * dh]
            inv = lax.rsqrt(jnp.sum(xh * xh, axis=-1, keepdims=True) + EPS) * gain
            o_ref[:, i * dh:(i + 1) * dh] = (xh * inv).astype(BF16)
    v_ref[...] = y[:, 2 * A_WIDTH:].astype(BF16)
    z_ref[...] = jnp.dot(hb, wz_ref[...], preferred_element_type=F32).astype(BF16)
    ba_ref[...] = jnp.dot(hb, wba_ref[...], preferred_element_type=F32)


def _in_proj_a(x, g, scale, shift, wqkv, wz, wba, conv_w):
    bn, t, d = x.shape
    tm = ROW_TILE
    row = lambda b, i: (b, i, 0)
    per_b = lambda b, i: (b, 0, 0)
    const = lambda b, i: (0, 0)
    nba = wba.shape[1]
    wide = pl.BlockSpec((None, tm, A_WIDTH), row)
    wide_sd = jax.ShapeDtypeStruct((bn, t, A_WIDTH), BF16)
    return pl.pallas_call(
        _in_proj_a_kernel,
        grid=(bn, t // tm),
        in_specs=[pl.BlockSpec((None, tm, d), row),
                  pl.BlockSpec((1, d), const),
                  pl.BlockSpec((None, 1, d), per_b),
                  pl.BlockSpec((None, 1, d), per_b),
                  pl.BlockSpec(wqkv.shape, const),
                  pl.BlockSpec(wz.shape, const),
                  pl.BlockSpec(wba.shape, const),
                  pl.BlockSpec(conv_w.shape, const)],
        out_specs=[wide, wide, wide, wide, pl.BlockSpec((None, tm, nba), row)],
        out_shape=[wide_sd, wide_sd, wide_sd, wide_sd, jax.ShapeDtypeStruct((bn, t, nba), F32)],
        scratch_shapes=[pltpu.VMEM((SUBLANES + tm, 3 * A_WIDTH), F32)],
        compiler_params=_cparams("arbitrary", "arbitrary"),
    )(x, g, scale, shift, wqkv, wz, wba, conv_w)


def _cumsum_rows(x):
    n = x.shape[0]
    row = lax.broadcasted_iota(jnp.int32, x.shape, 0)
    s = 1
    while s < n:
        x = x + jnp.where(row >= s, pltpu.roll(x, s, axis=0), 0.0)
        s *= 2
    return x


def _unit_lower_inverse(ms):
    c = ms[0].shape[0]
    row = lax.broadcasted_iota(jnp.int32, (c, c), 0)
    col = lax.broadcasted_iota(jnp.int32, (c, c), 1)
    eye = (row == col).astype(F32)
    same_blk = (row & -BLK16) == (col & -BLK16)
    d = [jnp.where(same_blk, m, 0.0) for m in ms]
    mo = [m - x for m, x in zip(ms, d)]
    d2 = [_dot(x, x) for x in d]
    td = [eye - x for x in d]
    d4 = [_dot(x, x) for x in d2]
    td = [t + _dot(t, x) for t, x in zip(td, d2)]
    d8 = [_dot(x, x) for x in d4]
    td = [t + _dot(t, x) for t, x in zip(td, d4)]
    td = [t + _dot(t, x) for t, x in zip(td, d8)]
    n = [_dot(t, x) for t, x in zip(td, mo)]
    n2 = [_dot(x, x) for x in n]
    r = [eye - x for x in n]
    r = [a + _dot(a, x) for a, x in zip(r, n2)]
    return [_dot(a, t) for a, t in zip(r, td)]


def _gdn_kernel(q_ref, k_ref, v_ref, z_ref, ba_ref, alog_ref, dtb_ref, ong_ref, o_ref, s_ref, *, hb, nc):
    c = A_CHUNK
    dh = A_HEAD_DIM

    @pl.when(pl.program_id(2) == 0)
    def _():
        s_ref[...] = jnp.zeros(s_ref.shape, F32)

    q_all = q_ref[...]
    k_all = k_ref[...]
    v_all = v_ref[...]
    ba = ba_ref[...]
    beta_t = _sigmoid(ba)
    xa = ba + dtb_ref[...]
    softplus = jnp.maximum(xa, 0.0) + jnp.log(1.0 + jnp.exp(-jnp.abs(xa)))
    g_t = -jnp.exp(alog_ref[...]) * softplus
    gc_t = [_cumsum_rows(g_t[ci * c:(ci + 1) * c, :]) for ci in range(nc)]
    gc_tt = [x.T for x in gc_t]
    egc_t = [jnp.exp(x) for x in gc_t]
    ekd_t = [jnp.exp(x[c - 1:c, :] - x) for x in gc_t]
    egl_t = [jnp.exp(x[c - 1:c, :]) for x in gc_t]

    row = lax.broadcasted_iota(jnp.int32, (c, c), 0)
    col = lax.broadcasted_iota(jnp.int32, (c, c), 1)
    incl = row >= col
    strict = row > col
    heads = range(hb)
    jobs = [(ci, i) for ci in range(nc) for i in heads]
    la = LANES // 2

    def head(x, ci, i):
        return x[ci * c:(ci + 1) * c, i * dh:(i + 1) * dh]

    def lane(xs, ci, i):
        return xs[ci][:, la + i:la + i + 1]

    qnb = [head(q_all, ci, i) for ci, i in jobs]
    knb = [head(k_all, ci, i) for ci, i in jobs]
    qn = [x.astype(F32) for x in qnb]
    kn = [x.astype(F32) for x in knb]
    beta = [beta_t[ci * c:(ci + 1) * c, i:i + 1] for ci, i in jobs]
    kb = [x * y for x, y in zip(kn, beta)]
    decay = [jnp.where(incl, jnp.exp(jnp.where(incl, lane(gc_t, ci, i) - gc_tt[ci][la + i:la + i + 1, :], 0.0)), 0.0)
             for ci, i in jobs]
    m = [jnp.where(strict, _dot_nt(x, y) * d, 0.0) for x, y, d in zip(kb, knb, decay)]
    attn = [(_dot_nt(x, y) * d).astype(BF16) for x, y, d in zip(qnb, knb, decay)]
    rhs = [jnp.concatenate([head(v_all, ci, i).astype(F32) * beta[j], kb[j] * lane(egc_t, ci, i)],
                           axis=1).astype(BF16) for j, (ci, i) in enumerate(jobs)]
    qdec = [(qn[j] * lane(egc_t, ci, i)).astype(BF16) for j, (ci, i) in enumerate(jobs)]
    kdec_t = [(kn[j] * lane(ekd_t, ci, i)).T.astype(BF16) for j, (ci, i) in enumerate(jobs)]
    tinv = _unit_lower_inverse(m)
    uw = [_dot(x, y) for x, y in zip(tinv, rhs)]

    s_cur = [s_ref[i] for i in heads]
    for ci in range(nc):
        sb = [x.astype(BF16) for x in s_cur]
        job = [ci * hb + i for i in heads]
        v_new = [uw[j][:, :dh] - _dot(uw[j][:, dh:], sb[i]) for i, j in zip(heads, job)]
        vnb = [x.astype(BF16) for x in v_new]
        s_cur = [s_cur[i] * lane(egl_t, ci, i) + jnp.dot(kdec_t[j], vnb[i], preferred_element_type=F32)
                 for i, j in zip(heads, job)]
        o = [jnp.dot(qdec[j], sb[i], preferred_element_type=F32)
             + jnp.dot(attn[j], vnb[i], preferred_element_type=F32) for i, j in zip(heads, job)]
        for i in heads:
            rs = slice(ci * c, (ci + 1) * c)
            sl = slice(i * dh, (i + 1) * dh)
            o_ref[rs, sl] = (_rms(o[i], ong_ref[...]) * _silu(z_ref[rs, sl].astype(F32))).astype(o_ref.dtype)
    for i in heads:
        s_ref[i] = s_cur[i]


def _gdn(q, k, v, z, ba, alog_row, dtb_row, onorm_g):
    bn, t, _ = q.shape
    hb = GDN_HEADS_PER_STEP
    nc = GDN_CHUNKS_PER_STEP
    nhg = A_HEADS // hb
    c = nc * A_CHUNK
    w = hb * A_HEAD_DIM
    kern = functools.partial(_gdn_kernel, hb=hb, nc=nc)
    wide = pl.BlockSpec((None, c, w), lambda b, hg, n: (b, n, hg))
    hg_row = pl.BlockSpec((None, 1, LANES), lambda b, hg, n: (hg, 0, 0))
    return pl.pallas_call(
        kern,
        grid=(bn, nhg, t // c),
        in_specs=[wide, wide, wide, wide,
                  pl.BlockSpec((None, c, LANES), lambda b, hg, n: (b, n, hg)),
                  hg_row, hg_row,
                  pl.BlockSpec((1, A_HEAD_DIM), lambda b, hg, n: (0, 0))],
        out_specs=wide,
        out_shape=jax.ShapeDtypeStruct((bn, t, A_WIDTH), BF16),
        scratch_shapes=[pltpu.VMEM((hb, A_HEAD_DIM, A_HEAD_DIM), F32)],
        compiler_params=_cparams("arbitrary", "arbitrary", "arbitrary"),
    )(q, k, v, z, ba, alog_row, dtb_row, onorm_g)


def _post_a_kernel(x_ref, og_ref, gate_ref, wo_ref, kvg_ref, wkvc_ref, wkvr_ref, g1_ref, scale_ref,
                   shift_ref, wq_ref, wz_ref, wg_ref,
                   x1_ref, kvc_ref, kvr_ref, q_ref, z_ref, gates_ref):
    out = _dot(og_ref[...], wo_ref[...])
    x1 = x_ref[...] + gate_ref[...] * out
    x1_ref[...] = x1
    sb = _rms(x1, kvg_ref[...]).astype(BF16)
    kvc_ref[...] = jnp.dot(sb, wkvc_ref[...], preferred_element_type=F32)
    kvr_ref[...] = jnp.dot(sb, wkvr_ref[...], preferred_element_type=F32).astype(BF16)
    h = _rms(x1, g1_ref[...]) * (1.0 + scale_ref[...]) + shift_ref[...]
    hb = h.astype(BF16)
    q = jnp.dot(hb, wq_ref[...], preferred_element_type=F32) * (B_HEAD_DIM ** -0.5)
    q_ref[...] = q.astype(BF16)
    z_ref[...] = jnp.dot(hb, wz_ref[...], preferred_element_type=F32)
    gates_ref[...] = _sigmoid(jnp.dot(hb, wg_ref[...], preferred_element_type=F32))


def _post_a(x, og, gate0, wo, kvg, wkvc, wkvr, g1, scale1, shift1, wq, wz, wg):
    bn, t, d = x.shape
    tm = ROW_TILE
    row = lambda b, i: (b, i, 0)
    per_b = lambda b, i: (b, 0, 0)
    const = lambda b, i: (0, 0)
    full = lambda a: pl.BlockSpec(a.shape, const)
    vec = pl.BlockSpec((1, d), const)
    bvec = pl.BlockSpec((None, 1, d), per_b)
    outs = [(d, F32), (wkvc.shape[1], F32), (wkvr.shape[1], BF16), (wq.shape[1], BF16),
            (wz.shape[1], F32), (wg.shape[1], F32)]
    return pl.pallas_call(
        _post_a_kernel,
        grid=(bn, t // tm),
        in_specs=[pl.BlockSpec((None, tm, d), row), pl.BlockSpec((None, tm, A_WIDTH), row), bvec,
                  full(wo), vec, full(wkvc), full(wkvr), vec, bvec, bvec, full(wq), full(wz), full(wg)],
        out_specs=[pl.BlockSpec((None, tm, n), row) for n, _ in outs],
        out_shape=[jax.ShapeDtypeStruct((bn, t, n), dt) for n, dt in outs],
        compiler_params=_cparams("arbitrary", "arbitrary"),
    )(x, og, gate0, wo, kvg, wkvc, wkvr, g1, scale1, shift1, wq, wz, wg)


def _compress_kernel(hv_ref, ptop_ref, pbot_ref, w1t_ref, w1b_ref, w2_ref, o_ref):
    hv = hv_ref[...]
    a = _dot(hv + ptop_ref[...], w1t_ref[...])
    b = _dot(hv + pbot_ref[...], w1b_ref[...])
    nrow = a.shape[0]
    hid = a + pltpu.roll(b, nrow - 1, axis=0)
    o_ref[...] = _dot(_silu(hid), w2_ref[...]).astype(o_ref.dtype)


def _compress(halves, ptop, pbot, w1t, w1b, w2):
    bn, four, nh, wd = halves.shape
    hid = w1t.shape[-1]
    per_kind = lambda b, j: (j // B_GROUPS, 0, 0)
    return pl.pallas_call(
        _compress_kernel,
        grid=(bn, four),
        in_specs=[pl.BlockSpec((None, None, nh, wd), lambda b, j: (b, j, 0, 0)),
                  pl.BlockSpec((None, 1, wd), per_kind),
                  pl.BlockSpec((None, 1, wd), per_kind),
                  pl.BlockSpec((None, wd, hid), per_kind),
                  pl.BlockSpec((None, wd, hid), per_kind),
                  pl.BlockSpec((None, hid, B_HEAD_DIM), per_kind)],
        out_specs=pl.BlockSpec((None, None, nh, B_HEAD_DIM), lambda b, j: (b, j, 0, 0)),
        out_shape=jax.ShapeDtypeStruct((bn, four, nh, B_HEAD_DIM), BF16),
        compiler_params=_cparams("arbitrary", "arbitrary"),
    )(halves, ptop, pbot, w1t, w1b, w2)


def _t5_bucket_np(dist):
    n = np.maximum(dist, 0)
    max_exact = NUM_BUCKETS // 2
    nf = np.maximum(n, 1).astype(np.float64)
    val = np.log(nf / max_exact) / math.log(MAX_DISTANCE / max_exact) * (NUM_BUCKETS - max_exact)
    frac = np.abs(val - np.round(val))
    safe = (frac > 1e-6) | (n <= max_exact) | (n >= MAX_DISTANCE)
    assert bool(np.all(safe)), "bucket boundary too close to an integer distance"
    large = np.minimum(max_exact + np.floor(np.maximum(val, 0.0)).astype(np.int64), NUM_BUCKETS - 1)
    return np.where(n < max_exact, n, large)


def _bias_onehot():
    r = np.arange(NSA_Q_TILE)[:, None]
    tiles = []
    j = np.arange(WINDOW + NSA_Q_TILE)[None, :]
    d = r + WINDOW - j
    tiles.append((d, (d >= 0) & (d < WINDOW)))
    j = np.arange((NSA_SUB + 2) * L_SLC)[None, :]
    d = r + 2 * L_SLC - j
    tiles.append((d, d >= 0))
    j = np.arange(CMP_NEAR)[None, :]
    d = r - CMP_STRIDE * (j - CMP_LEAD) - (L_CMP - 1)
    tiles.append((d, d >= 0))
    cols = [np.where(valid, _t5_bucket_np(d), NUM_BUCKETS).reshape(-1) for d, valid in tiles]
    widths = [c.size for c in cols]
    return np.concatenate(cols).astype(np.int32)[None, :], widths


def _bias_kernel(rb_ref, bk_ref, o_ref):
    rb = rb_ref[...]
    lane = lax.broadcasted_iota(jnp.int32, rb.shape, 1)
    rbs = rb - rb[:, NUM_BUCKETS - 1:NUM_BUCKETS]
    rbs = jnp.where(lane < NUM_BUCKETS, rbs, jnp.where(lane == NUM_BUCKETS, NEG_INF, 0.0))
    bk = bk_ref[...]
    onehot = jnp.where(lax.broadcasted_iota(jnp.int32, (2 * NUM_BUCKETS, bk.shape[1]), 0) == bk, 1.0, 0.0)
    o_ref[...] = _dot_f32(rbs, onehot)


def _bias_tables(rel_bias):
    bk, widths = _bias_onehot()
    ncol = bk.shape[1]
    nt = 8
    assert ncol % (nt * LANES) == 0
    tc = ncol // nt
    rb = jnp.concatenate([rel_bias.T, jnp.zeros((B_HEADS, NUM_BUCKETS), F32)], axis=1)
    flat = pl.pallas_call(
        _bias_kernel,
        grid=(nt,),
        in_specs=[pl.BlockSpec((B_HEADS, 2 * NUM_BUCKETS), lambda i: (0, 0)),
                  pl.BlockSpec((1, tc), lambda i: (0, i))],
        out_specs=pl.BlockSpec((B_HEADS, tc), lambda i: (0, i)),
        out_shape=jax.ShapeDtypeStruct((B_HEADS, ncol), F32),
        compiler_params=_cparams("arbitrary"),
    )(rb, jnp.asarray(bk))
    out, start = [], 0
    for wd in widths:
        tile = flat[:, start:start + wd].reshape(B_GROUPS, B_HPG * NSA_Q_TILE, wd // NSA_Q_TILE)
        out.append(tile)
        start += wd
    return out


def _nsa_kernel(q_ref, kc_ref, vc_ref, ks_ref, vs_ref, kw_ref, vw_ref, tcmp_ref, tsel_ref, twin_ref,
                ov_ref, oc_ref, os_ref, ow_ref, sa_ref, sb_ref):
    tq = NSA_Q_TILE
    dh = B_HEAD_DIM
    hpg = B_HPG
    rows = hpg * tq
    ti = pl.program_id(2)
    q0 = ti * tq
    blk0 = ti * NSA_SUB
    qt = q_ref[...]
    q = jnp.concatenate([qt[:, h * dh:(h + 1) * dh] for h in range(hpg)], axis=0)

    def to_tokens(o):
        return jnp.concatenate([o[h * tq:(h + 1) * tq, :] for h in range(hpg)], axis=1)

    def finish(pv):
        return to_tokens(pv[:, :dh] * (1.0 / pv[:, dh:dh + 1]))

    kc = kc_ref[...]
    ncp = kc.shape[0]
    nw = WINDOW + tq
    win0 = pl.multiple_of(q0, tq)
    first_near = (tq // CMP_STRIDE) * ti - CMP_LEAD
    cid = lax.broadcasted_iota(jnp.int32, (2 * CMP_NEAR, ncp), 1)
    jrow = lax.broadcasted_iota(jnp.int32, (2 * CMP_NEAR, ncp), 0) & (CMP_NEAR - 1)
    shift_eye = jnp.where(cid - first_near == jrow, 1.0, 0.0).astype(BF16)
    pad_col = jnp.where(lax.broadcasted_iota(jnp.int32, (rows, dh), 1) == 0, NEG_INF, 0.0).astype(BF16)
    q_win = jnp.concatenate([q, pad_col], axis=1)

    s = _dot_nt(q, kc) + jnp.dot(tcmp_ref[...], shift_eye, preferred_element_type=F32)
    s_w = _dot_nt(q_win, kw_ref[pl.ds(win0, nw), :]) + twin_ref[...]

    cvis = lax.broadcasted_iota(jnp.int32, (1, ncp), 1) < first_near + CMP_NEAR
    s = jnp.where(cvis, s, NEG_INF)
    live = s > 0.1 * NEG_INF
    m = jnp.max(s, axis=-1, keepdims=True)
    e = jnp.where(live, jnp.exp(s - m), 0.0)
    p = e * (1.0 / jnp.maximum(jnp.sum(e, axis=-1, keepdims=True), 1e-30))
    oc_ref[...] = to_tokens(_dot(p, vc_ref[...]))

    psum = p[0:tq, :]
    for h in range(1, hpg):
        psum = psum + p[h * tq:(h + 1) * tq, :]
    p_hi = psum.astype(BF16)
    p_r1 = psum - p_hi.astype(F32)
    p_mid = p_r1.astype(BF16)
    p_lo = (p_r1 - p_mid.astype(F32)).astype(BF16)
    p3 = jnp.concatenate([p_hi, p_mid, p_lo], axis=1)
    imp_t = _dot_nt(ov_ref[...], p3)

    m_w = jnp.max(s_w, axis=-1, keepdims=True)
    e_w = jnp.exp(s_w - m_w)
    ow_ref[...] = finish(_dot(e_w, vw_ref[pl.ds(win0, nw), :]))

    nblk = imp_t.shape[0]
    blk = lax.broadcasted_iota(jnp.int32, (nblk, tq), 0)
    cur = blk0 + (lax.broadcasted_iota(jnp.int32, (nblk, tq), 1) >> SLC_SHIFT)
    forced = (blk == 0) | (blk == cur) | (blk == cur - 1)
    val = jnp.where(forced, SEL_BOOST, jnp.where(blk > cur, -SEL_BOOST, imp_t))
    nslab = nblk // SUBLANES
    slabs = [val[SUBLANES * r:SUBLANES * (r + 1), :] for r in range(nslab)]
    sub = lax.broadcasted_iota(jnp.int32, (SUBLANES, tq), 0)
    n_acc = 4
    ranks = [[jnp.zeros((SUBLANES, tq), jnp.int32) for _ in range(n_acc)] for _ in range(nslab)]
    for j in range(nblk):
        vj = jnp.broadcast_to(val[j:j + 1, :], (SUBLANES, tq))
        for r in range(nslab):
            lo = SUBLANES * r
            if lo > j:
                ahead = vj >= slabs[r]
            elif lo + SUBLANES - 1 <= j:
                ahead = vj > slabs[r]
            else:
                ahead = (vj > slabs[r]) | ((vj == slabs[r]) & (sub > j - lo))
            ranks[r][j % n_acc] = ranks[r][j % n_acc] + ahead.astype(jnp.int32)
    rank = jnp.concatenate([(a[0] + a[1]) + (a[2] + a[3]) for a in ranks], axis=0)
    sel_t = (rank < N_SEL) & (blk <= cur)
    far_t = jnp.where(sel_t & (blk <= blk0 - 3), 0.0, NEG_INF)
    near_t = jnp.where(sel_t & (blk >= blk0 - 2), 0.0, NEG_INF)

    def q_with_mask(mask_t):
        mk = mask_t.T.astype(BF16)
        return jnp.concatenate([q, jnp.concatenate([mk] * hpg, axis=0)], axis=1)

    q_far = q_with_mask(far_t)
    q_near = q_with_mask(near_t)

    kt_sz = SEL_KEY_TILE
    n_far_keys = jnp.maximum(blk0 - 2, 0) * L_SLC
    n_pairs = (n_far_keys + 2 * kt_sz - 1) // (2 * kt_sz)

    def far_scores(tile):
        start = pl.multiple_of(KV_PAD + tile * kt_sz, kt_sz)
        return _dot_nt(q_far, ks_ref[pl.ds(start, kt_sz), :])

    def far_values(tile):
        start = pl.multiple_of(KV_PAD + tile * kt_sz, kt_sz)
        return vs_ref[pl.ds(start, kt_sz), :]

    def update(carry, s_t, v_t):
        m_i, acc = carry
        m_n = jnp.maximum(m_i, jnp.max(s_t, axis=-1, keepdims=True))
        e_t = jnp.exp(s_t - m_n)
        return m_n, jnp.exp(m_i - m_n) * acc + _dot(e_t, v_t)

    def pair_step(j, carry):
        sb_ref[...] = far_scores(2 * j + 1)
        carry = update(carry, sa_ref[...], far_values(2 * j))
        sa_ref[...] = far_scores(2 * j + 2)
        return update(carry, sb_ref[...], far_values(2 * j + 1))

    sa_ref[...] = far_scores(0)
    last = jnp.maximum(n_pairs, 1) - 1
    carry = (jnp.full((rows, 1), NEG_INF, F32), jnp.zeros((rows, 2 * dh), F32))
    carry = lax.fori_loop(0, last, pair_step, carry)
    sb_ref[...] = far_scores(2 * last + 1)
    carry = update(carry, sa_ref[...], far_values(2 * last))
    nk = (NSA_SUB + 2) * L_SLC
    near0 = pl.multiple_of(KV_PAD + q0 - 2 * L_SLC, L_SLC)
    s_n = _dot_nt(q_near, ks_ref[pl.ds(near0, nk), :]) + tsel_ref[...]
    carry = update(carry, sb_ref[...], far_values(2 * last + 1))
    _, acc = update(carry, s_n, vs_ref[pl.ds(near0, nk), :])
    os_ref[...] = finish(acc)


def _nsa(q, kcv, ks, vs, kw, vw, tcmp, tsel, twin, ov):
    bn, t, _ = q.shape
    tq = NSA_Q_TILE
    gw = B_HPG * B_HEAD_DIM
    ncp = kcv.shape[2]
    tp = ks.shape[2]
    rows = B_HPG * tq
    per_bg = pl.BlockSpec((None, None, tp, ks.shape[3]), lambda b, g, i: (b, g, 0, 0))
    per_g = lambda b, g, i: (g, 0, 0)
    out_spec = pl.BlockSpec((None, tq, gw), lambda b, g, i: (b, i, g))
    out_sd = jax.ShapeDtypeStruct((bn, t, B_WIDTH), F32)
    return pl.pallas_call(
        _nsa_kernel,
        grid=(bn, B_GROUPS, t // tq),
        in_specs=[pl.BlockSpec((None, tq, gw), lambda b, g, i: (b, i, g)),
                  pl.BlockSpec((None, None, ncp, B_HEAD_DIM), lambda b, g, i: (b, g, 0, 0)),
                  pl.BlockSpec((None, None, ncp, B_HEAD_DIM), lambda b, g, i: (b, B_GROUPS + g, 0, 0)),
                  per_bg, per_bg, per_bg, per_bg,
                  pl.BlockSpec((None, rows, tcmp.shape[2]), per_g),
                  pl.BlockSpec((None, rows, tsel.shape[2]), per_g),
                  pl.BlockSpec((None, rows, twin.shape[2]), per_g),
                  pl.BlockSpec(ov.shape, lambda b, g, i: (0, 0))],
        out_specs=[out_spec, out_spec, out_spec],
        out_shape=[out_sd, out_sd, out_sd],
        scratch_shapes=[pltpu.VMEM((rows, SEL_KEY_TILE), F32), pltpu.VMEM((rows, SEL_KEY_TILE), F32)],
        compiler_params=_cparams("arbitrary", "arbitrary", "arbitrary"),
    )(q, kcv, kcv, ks, vs, kw, vw, tcmp, tsel, twin, ov)


def _final_kernel(oc_ref, os_ref, ow_ref, z_ref, gates_ref, ex_ref, x1_ref, gate_ref, wo_ref, fg_ref, o_ref):
    gt = gates_ref[...]
    g_hi = gt.astype(BF16)
    g_lo = (gt - g_hi.astype(F32)).astype(BF16)
    ghl = jnp.concatenate([g_hi, g_lo], axis=1)
    y = None
    for br, o_ref_br in enumerate((oc_ref, os_ref, ow_ref)):
        gexp = jnp.dot(ghl, ex_ref[br], preferred_element_type=F32)
        term = gexp * o_ref_br[...] * _silu(z_ref[:, br * B_WIDTH:(br + 1) * B_WIDTH])
        y = term if y is None else y + term
    x2 = x1_ref[...] + gate_ref[...] * _dot(y, wo_ref[...])
    o_ref[...] = _rms(x2, fg_ref[...])


def _final(oc, osel, ow, z, gates, x1, gate1, wo, fg):
    bn, t, d = x1.shape
    tm = ROW_TILE
    ng = gates.shape[2]
    row = lambda b, i: (b, i, 0)
    ex = np.zeros((N_BRANCH, 2 * ng, B_WIDTH), np.float32)
    for br in range(N_BRANCH):
        for h in range(B_HEADS):
            ex[br, br * B_HEADS + h, h * B_HEAD_DIM:(h + 1) * B_HEAD_DIM] = 1.0
            ex[br, ng + br * B_HEADS + h, h * B_HEAD_DIM:(h + 1) * B_HEAD_DIM] = 1.0
    ex = jnp.asarray(ex, BF16)
    return pl.pallas_call(
        _final_kernel,
        grid=(bn, t // tm),
        in_specs=[pl.BlockSpec((None, tm, B_WIDTH), row)] * 3
        + [pl.BlockSpec((None, tm, N_BRANCH * B_WIDTH), row),
           pl.BlockSpec((None, tm, ng), row),
           pl.BlockSpec(ex.shape, lambda b, i: (0, 0, 0)),
           pl.BlockSpec((None, tm, d), row),
           pl.BlockSpec((None, 1, d), lambda b, i: (b, 0, 0)),
           pl.BlockSpec(wo.shape, lambda b, i: (0, 0)),
           pl.BlockSpec((1, d), lambda b, i: (0, 0))],
        out_specs=pl.BlockSpec((None, tm, d), row),
        out_shape=jax.ShapeDtypeStruct((bn, t, d), F32),
        compiler_params=_cparams("arbitrary", "arbitrary"),
    )(oc, osel, ow, z, gates, ex, x1, gate1, wo, fg)


def _overlap_matrix(ncp, n_cmp, n_slc, nblk):
    cells = np.arange(n_cmp)[:, None] + np.arange(L_CMP // CMP_STRIDE)[None, :]
    ov = (cells[:, None, :] // (L_SLC // CMP_STRIDE) == np.arange(n_slc)[None, :, None]).sum(-1)
    out = np.zeros((ncp, nblk), np.float32)
    out[:n_cmp, :n_slc] = ov
    return out


def _block_onehot(t, nblk):
    oh = np.zeros((KV_PAD + t, nblk), np.float32)
    oh[KV_PAD + np.arange(t), np.arange(t) // L_SLC] = 1.0
    oh[:KV_PAD, nblk - 1] = 1.0
    return oh


def kernel(x, c, rel_bias, ada_w, ada_b, norm_g, a_in_w, a_conv_w, a_A_log, a_dt_bias, a_onorm_g, a_out_w,
           kv_norm_g, kv_w, cmp_pos_k, cmp_pos_v, cmp_k_w1, cmp_k_w2, cmp_v_w1, cmp_v_w2,
           b_in_w, b_out_w, final_g):
    bn, t, d = x.shape
    assert ada_w.shape[0] == 2 and a_in_w.shape[0] == 1 and b_in_w.shape[0] == 1
    assert t % max(ROW_TILE, 2 * SEL_KEY_TILE, NSA_Q_TILE) == 0
    n_slc = t // L_SLC
    nblk = 64
    assert n_slc <= nblk
    n_cmp = (t - L_CMP) // CMP_STRIDE + 1
    ncp = t // CMP_STRIDE

    mod = _ada_modulation(c, ada_w, ada_b)
    shift = mod[:, :, None, :d]
    scale = mod[:, :, None, d:2 * d]
    gate = mod[:, :, None, 2 * d:]

    hb = GDN_HEADS_PER_STEP
    nhg = A_HEADS // hb
    w_in = a_in_w[0]
    wqkv = w_in[:, :3 * A_WIDTH].astype(BF16)
    wz = w_in[:, 3 * A_WIDTH:4 * A_WIDTH].astype(BF16)
    wb = w_in[:, 4 * A_WIDTH:4 * A_WIDTH + A_HEADS]
    wa = w_in[:, 4 * A_WIDTH + A_HEADS:]
    half = LANES // 2
    wba = jnp.zeros((d, nhg, LANES), F32)
    wba = wba.at[:, :, :hb].set(wb.reshape(d, nhg, hb)).at[:, :, half:half + hb].set(wa.reshape(d, nhg, hb))
    wba = wba.reshape(d, nhg * LANES).astype(BF16)
    lane_rows = lambda v: jnp.zeros((nhg, 1, LANES), F32).at[:, 0, half:half + hb].set(v.reshape(nhg, hb))
    q_a, k_a, v_a, z_a, ba = _in_proj_a(x, norm_g[0:1], scale[0], shift[0], wqkv, wz, wba, a_conv_w[0])
    og = _gdn(q_a, k_a, v_a, z_a, ba, lane_rows(a_A_log[0]), lane_rows(a_dt_bias[0]), a_onorm_g[0:1])

    ndh = B_GROUPS * B_HEAD_DIM
    wkvc = kv_w[:, :2 * ndh].astype(BF16)
    wkvr = kv_w[:, 2 * ndh:].astype(BF16)
    w_b = b_in_w[0]
    wq = w_b[:, :B_WIDTH].astype(BF16)
    wzb = w_b[:, B_WIDTH:4 * B_WIDTH].astype(BF16)
    wg = jnp.zeros((d, LANES), F32).at[:, :N_BRANCH * B_HEADS].set(w_b[:, 4 * B_WIDTH:]).astype(BF16)
    x1, kvc, kvr, q, z_b, gates = _post_a(x, og, gate[0], a_out_w[0].astype(BF16), kv_norm_g[None, :], wkvc, wkvr,
                                          norm_g[1:2], scale[1], shift[1], wq, wzb, wg)

    halves = kvc.reshape(bn, ncp, CMP_STRIDE, 2 * B_GROUPS, B_HEAD_DIM).transpose(0, 3, 1, 2, 4)
    halves = halves.reshape(bn, 2 * B_GROUPS, ncp, CMP_STRIDE * B_HEAD_DIM)
    pos = jnp.stack([cmp_pos_k, cmp_pos_v])
    hw = CMP_STRIDE * B_HEAD_DIM
    ptop = pos[:, :CMP_STRIDE].reshape(2, 1, hw)
    pbot = pos[:, CMP_STRIDE:].reshape(2, 1, hw)
    w1 = jnp.stack([cmp_k_w1, cmp_v_w1]).astype(BF16)
    w2 = jnp.stack([cmp_k_w2, cmp_v_w2]).astype(BF16)
    kcv = _compress(halves, ptop, pbot, w1[:, :hw], w1[:, hw:], w2)

    kvr4 = kvr.reshape(bn, t, 4, B_GROUPS, B_HEAD_DIM).transpose(2, 0, 3, 1, 4)
    kvr4 = jnp.pad(kvr4, ((0, 0), (0, 0), (0, 0), (KV_PAD, 0), (0, 0)))
    tp = KV_PAD + t

    def beside(x, extra):
        return jnp.concatenate([x, jnp.broadcast_to(jnp.asarray(extra, BF16), x.shape[:2] + extra.shape)], axis=-1)

    first_col = np.zeros((tp, B_HEAD_DIM), np.float32)
    ones_col = first_col.copy()
    ones_col[:, 0] = 1.0
    pad_col = first_col.copy()
    pad_col[:KV_PAD, 0] = 1.0
    ks = beside(kvr4[0], _block_onehot(t, nblk))
    vs = beside(kvr4[1], ones_col)
    kw = beside(kvr4[2], pad_col)
    vw = beside(kvr4[3], ones_col)

    twin, tsel, tcmp = _bias_tables(rel_bias)
    tc_hi = tcmp.astype(BF16)
    tc_lo = (tcmp - tc_hi.astype(F32)).astype(BF16)
    tcmp2 = jnp.concatenate([tc_hi, tc_lo], axis=-1)
    ov_t = _overlap_matrix(ncp, n_cmp, n_slc, nblk).T
    ov3 = jnp.asarray(np.concatenate([ov_t] * 3, axis=1), BF16)

    oc, osel, ow = _nsa(q, kcv, ks, vs, kw, vw, tcmp2, tsel, twin, ov3)

    return _final(oc, osel, ow, z_b, gates, x1, gate[1], b_out_w[0].astype(BF16), final_g[None, :])
```

```python
import functools
import math

import numpy as np
import jax
import jax.numpy as jnp
from jax import lax
from jax.experimental import pallas as pl
from jax.experimental.pallas import tpu as pltpu

F32 = jnp.float32
BF16 = jnp.bfloat16
HIGHEST = lax.Precision.HIGHEST

A_HEADS = 8
A_HEAD_DIM = 128
A_WIDTH = A_HEADS * A_HEAD_DIM
A_CONV = 4
A_CHUNK = 64
B_HEADS = 16
B_GROUPS = 2
B_HPG = B_HEADS // B_GROUPS
B_HEAD_DIM = 64
B_WIDTH = B_HEADS * B_HEAD_DIM
N_BRANCH = 3
L_CMP = 32
CMP_STRIDE = 16
L_SLC = 64
N_SEL = 16
WINDOW = 512
Q_BLOCK = 64
NUM_BUCKETS = 32
MAX_DISTANCE = 128
EPS = 1e-6
NEG_INF = -1e30
SEL_BOOST = 1e9

LANES = 128
SUBLANES = 8
VMEM_LIMIT_BYTES = 56 * 1024 * 1024

ROW_TILE = 256
GDN_HEADS_PER_STEP = 8
GDN_CHUNKS_PER_STEP = 2
SEL_KEY_TILE = 512
KV_PAD = WINDOW
NSA_Q_TILE = 128
NSA_SUB = NSA_Q_TILE // Q_BLOCK
SLC_SHIFT = L_SLC.bit_length() - 1
assert 1 << SLC_SHIFT == L_SLC and L_SLC == Q_BLOCK
CMP_LEAD = 12
CMP_NEAR = 32
assert CMP_NEAR >= CMP_LEAD + NSA_Q_TILE // CMP_STRIDE and NSA_Q_TILE % Q_BLOCK == 0
BLK16 = 16


def _cparams(*sem):
    return pltpu.CompilerParams(dimension_semantics=sem, vmem_limit_bytes=VMEM_LIMIT_BYTES)


def _sigmoid(x):
    return 1.0 / (1.0 + jnp.exp(-x))


def _silu(x):
    return x * _sigmoid(x)


def _dot(a, b):
    return jnp.dot(a.astype(BF16), b.astype(BF16), preferred_element_type=F32)


def _dot_nt(a, b):
    return lax.dot_general(a.astype(BF16), b.astype(BF16), (((1,), (1,)), ((), ())),
                           preferred_element_type=F32)


def _dot_f32(a, b):
    return jnp.dot(a, b, precision=HIGHEST, preferred_element_type=F32)


def _rms(x, g):
    ms = jnp.mean(x * x, axis=-1, keepdims=True)
    return x * lax.rsqrt(ms + EPS) * g


def _ada_kernel(c_ref, w_ref, b_ref, o_ref):
    o_ref[...] = _dot_f32(_silu(c_ref[...]), w_ref[...]) + b_ref[...]


def _ada_modulation(c, ada_w, ada_b):
    depth, d, d3 = ada_w.shape
    bn = c.shape[0]
    return pl.pallas_call(
        _ada_kernel,
        grid=(depth, d3 // d),
        in_specs=[pl.BlockSpec((bn, d), lambda l, j: (0, 0)),
                  pl.BlockSpec((None, d, d), lambda l, j: (l, 0, j)),
                  pl.BlockSpec((None, 1, d), lambda l, j: (l, 0, j))],
        out_specs=pl.BlockSpec((None, bn, d), lambda l, j: (l, 0, j)),
        out_shape=jax.ShapeDtypeStruct((depth, bn, d3), F32),
        compiler_params=_cparams("arbitrary", "arbitrary"),
    )(c, ada_w, ada_b.reshape(depth, 1, d3))


def _in_proj_a_kernel(x_ref, g_ref, scale_ref, shift_ref, wqkv_ref, wz_ref, wba_ref, cw_ref,
                      q_ref, k_ref, v_ref, z_ref, ba_ref, buf_ref):
    tm = x_ref.shape[0]
    halo = SUBLANES
    dh = A_HEAD_DIM

    @pl.when(pl.program_id(1) == 0)
    def _():
        buf_ref[0:halo, :] = jnp.zeros((halo, 3 * A_WIDTH), F32)

    h = _rms(x_ref[...], g_ref[...]) * (1.0 + scale_ref[...]) + shift_ref[...]
    hb = h.astype(BF16)
    buf_ref[halo:halo + tm, :] = jnp.dot(hb, wqkv_ref[...], preferred_element_type=F32)
    cw = cw_ref[...]
    xp = buf_ref[...]
    y = xp[halo:, :] * cw[A_CONV - 1:A_CONV, :]
    for kk in range(A_CONV - 1):
        y = y + pltpu.roll(xp, A_CONV - 1 - kk, axis=0)[halo:, :] * cw[kk:kk + 1, :]
    buf_ref[0:halo, :] = xp[tm:, :]
    y = _silu(y)
    for i in range(A_HEADS):
        for which, o_ref, gain in ((0, q_ref, dh ** -0.5), (1, k_ref, 1.0)):
            xh = y[:, which * A_WIDTH + i * dh:which * A_WIDTH + (i + 1) * dh]
            inv = lax.rsqrt(jnp.sum(xh * xh, axis=-1, keepdims=True) + EPS) * gain
            o_ref[:, i * dh:(i + 1) * dh] = (xh * inv).astype(BF16)
    v_ref[...] = y[:, 2 * A_WIDTH:].astype(BF16)
    z_ref[...] = jnp.dot(hb, wz_ref[...], preferred_element_type=F32).astype(BF16)
    ba_ref[...] = jnp.dot(hb, wba_ref[...], preferred_element_type=F32)


def _in_proj_a(x, g, scale, shift, wqkv, wz, wba, conv_w):
    bn, t, d = x.shape
    tm = ROW_TILE
    row = lambda b, i: (b, i, 0)
    per_b = lambda b, i: (b, 0, 0)
    const = lambda b, i: (0, 0)
    nba = wba.shape[1]
    wide = pl.BlockSpec((None, tm, A_WIDTH), row)
    wide_sd = jax.ShapeDtypeStruct((bn, t, A_WIDTH), BF16)
    return pl.pallas_call(
        _in_proj_a_kernel,
        grid=(bn, t // tm),
        in_specs=[pl.BlockSpec((None, tm, d), row),
                  pl.BlockSpec((1, d), const),
                  pl.BlockSpec((None, 1, d), per_b),
                  pl.BlockSpec((None, 1, d), per_b),
                  pl.BlockSpec(wqkv.shape, const),
                  pl.BlockSpec(wz.shape, const),
                  pl.BlockSpec(wba.shape, const),
                  pl.BlockSpec(conv_w.shape, const)],
        out_specs=[wide, wide, wide, wide, pl.BlockSpec((None, tm, nba), row)],
        out_shape=[wide_sd, wide_sd, wide_sd, wide_sd, jax.ShapeDtypeStruct((bn, t, nba), F32)],
        scratch_shapes=[pltpu.VMEM((SUBLANES + tm, 3 * A_WIDTH), F32)],
        compiler_params=_cparams("arbitrary", "arbitrary"),
    )(x, g, scale, shift, wqkv, wz, wba, conv_w)


def _cumsum_rows(x):
    n = x.shape[0]
    row = lax.broadcasted_iota(jnp.int32, x.shape, 0)
    s = 1
    while s < n:
        x = x + jnp.where(row >= s, pltpu.roll(x, s, axis=0), 0.0)
        s *= 2
    return x


def _unit_lower_inverse(ms):
    c = ms[0].shape[0]
    row = lax.broadcasted_iota(jnp.int32, (c, c), 0)
    col = lax.broadcasted_iota(jnp.int32, (c, c), 1)
    eye = (row == col).astype(F32)
    same_blk = (row & -BLK16) == (col & -BLK16)
    d = [jnp.where(same_blk, m, 0.0) for m in ms]
    mo = [m - x for m, x in zip(ms, d)]
    d2 = [_dot(x, x) for x in d]
    td = [eye - x for x in d]
    d4 = [_dot(x, x) for x in d2]
    td = [t + _dot(t, x) for t, x in zip(td, d2)]
    d8 = [_dot(x, x) for x in d4]
    td = [t + _dot(t, x) for t, x in zip(td, d4)]
    td = [t + _dot(t, x) for t, x in zip(td, d8)]
    n = [_dot(t, x) for t, x in zip(td, mo)]
    n2 = [_dot(x, x) for x in n]
    r = [eye - x for x in n]
    r = [a + _dot(a, x) for a, x in zip(r, n2)]
    return [_dot(a, t) for a, t in zip(r, td)]


def _gdn_kernel(q_ref, k_ref, v_ref, z_ref, ba_ref, alog_ref, dtb_ref, ong_ref, o_ref, s_ref, *, hb, nc):
    c = A_CHUNK
    dh = A_HEAD_DIM

    @pl.when(pl.program_id(2) == 0)
    def _():
        s_ref[...] = jnp.zeros(s_ref.shape, F32)

    q_all = q_ref[...]
    k_all = k_ref[...]
    v_all = v_ref[...]
    ba = ba_ref[...]
    beta_t = _sigmoid(ba)
    xa = ba + dtb_ref[...]
    softplus = jnp.maximum(xa, 0.0) + jnp.log(1.0 + jnp.exp(-jnp.abs(xa)))
    g_t = -jnp.exp(alog_ref[...]) * softplus
    gc_t = [_cumsum_rows(g_t[ci * c:(ci + 1) * c, :]) for ci in range(nc)]
    gc_tt = [x.T for x in gc_t]
    egc_t = [jnp.exp(x) for x in gc_t]
    ekd_t = [jnp.exp(x[c - 1:c, :] - x) for x in gc_t]
    egl_t = [jnp.exp(x[c - 1:c, :]) for x in gc_t]

    row = lax.broadcasted_iota(jnp.int32, (c, c), 0)
    col = lax.broadcasted_iota(jnp.int32, (c, c), 1)
    incl = row >= col
    strict = row > col
    heads = range(hb)
    jobs = [(ci, i) for ci in range(nc) for i in heads]
    la = LANES // 2

    def head(x, ci, i):
        return x[ci * c:(ci + 1) * c, i * dh:(i + 1) * dh]

    def lane(xs, ci, i):
        return xs[ci][:, la + i:la + i + 1]

    qnb = [head(q_all, ci, i) for ci, i in jobs]
    knb = [head(k_all, ci, i) for ci, i in jobs]
    qn = [x.astype(F32) for x in qnb]
    kn = [x.astype(F32) for x in knb]
    beta = [beta_t[ci * c:(ci + 1) * c, i:i + 1] for ci, i in jobs]
    kb = [x * y for x, y in zip(kn, beta)]
    decay = [jnp.where(incl, jnp.exp(jnp.where(incl, lane(gc_t, ci, i) - gc_tt[ci][la + i:la + i + 1, :], 0.0)), 0.0)
             for ci, i in jobs]
    m = [jnp.where(strict, _dot_nt(x, y) * d, 0.0) for x, y, d in zip(kb, knb, decay)]
    attn = [(_dot_nt(x, y) * d).astype(BF16) for x, y, d in zip(qnb, knb, decay)]
    rhs = [jnp.concatenate([head(v_all, ci, i).astype(F32) * beta[j], kb[j] * lane(egc_t, ci, i)],
                           axis=1).astype(BF16) for j, (ci, i) in enumerate(jobs)]
    qdec = [(qn[j] * lane(egc_t, ci, i)).astype(BF16) for j, (ci, i) in enumerate(jobs)]
    kdec_t = [(kn[j] * lane(ekd_t, ci, i)).T.astype(BF16) for j, (ci, i) in enumerate(jobs)]
    tinv = _unit_lower_inverse(m)
    uw = [_dot(x, y) for x, y in zip(tinv, rhs)]

    s_cur = [s_ref[i] for i in heads]
    for ci in range(nc):
        sb = [x.astype(BF16) for x in s_cur]
        job = [ci * hb + i for i in heads]
        v_new = [uw[j][:, :dh] - _dot(uw[j][:, dh:], sb[i]) for i, j in zip(heads, job)]
        vnb = [x.astype(BF16) for x in v_new]
        s_cur = [s_cur[i] * lane(egl_t, ci, i) + jnp.dot(kdec_t[j], vnb[i], preferred_element_type=F32)
                 for i, j in zip(heads, job)]
        o = [jnp.dot(qdec[j], sb[i], preferred_element_type=F32)
             + jnp.dot(attn[j], vnb[i], preferred_element_type=F32) for i, j in zip(heads, job)]
        for i in heads:
            rs = slice(ci * c, (ci + 1) * c)
            sl = slice(i * dh, (i + 1) * dh)
            o_ref[rs, sl] = (_rms(o[i], ong_ref[...]) * _silu(z_ref[rs, sl].astype(F32))).astype(o_ref.dtype)
    for i in heads:
        s_ref[i] = s_cur[i]


def _gdn(q, k, v, z, ba, alog_row, dtb_row, onorm_g):
    bn, t, _ = q.shape
    hb = GDN_HEADS_PER_STEP
    nc = GDN_CHUNKS_PER_STEP
    nhg = A_HEADS // hb
    c = nc * A_CHUNK
    w = hb * A_HEAD_DIM
    kern = functools.partial(_gdn_kernel, hb=hb, nc=nc)
    wide = pl.BlockSpec((None, c, w), lambda b, hg, n: (b, n, hg))
    hg_row = pl.BlockSpec((None, 1, LANES), lambda b, hg, n: (hg, 0, 0))
    return pl.pallas_call(
        kern,
        grid=(bn, nhg, t // c),
        in_specs=[wide, wide, wide, wide,
                  pl.BlockSpec((None, c, LANES), lambda b, hg, n: (b, n, hg)),
                  hg_row, hg_row,
                  pl.BlockSpec((1, A_HEAD_DIM), lambda b, hg, n: (0, 0))],
        out_specs=wide,
        out_shape=jax.ShapeDtypeStruct((bn, t, A_WIDTH), BF16),
        scratch_shapes=[pltpu.VMEM((hb, A_HEAD_DIM, A_HEAD_DIM), F32)],
        compiler_params=_cparams("arbitrary", "arbitrary", "arbitrary"),
    )(q, k, v, z, ba, alog_row, dtb_row, onorm_g)


def _post_a_kernel(x_ref, og_ref, gate_ref, wo_ref, kvg_ref, wkvc_ref, wkvr_ref, g1_ref, scale_ref,
                   shift_ref, wq_ref, wz_ref, wg_ref,
                   x1_ref, kvc_ref, kvr_ref, q_ref, z_ref, gates_ref):
    out = _dot(og_ref[...], wo_ref[...])
    x1 = x_ref[...] + gate_ref[...] * out
    x1_ref[...] = x1
    sb = _rms(x1, kvg_ref[...]).astype(BF16)
    kvc_ref[...] = jnp.dot(sb, wkvc_ref[...], preferred_element_type=F32)
    kvr_ref[...] = jnp.dot(sb, wkvr_ref[...], preferred_element_type=F32).astype(BF16)
    h = _rms(x1, g1_ref[...]) * (1.0 + scale_ref[...]) + shift_ref[...]
    hb = h.astype(BF16)
    q = jnp.dot(hb, wq_ref[...], preferred_element_type=F32) * (B_HEAD_DIM ** -0.5)
    q_ref[...] = q.astype(BF16)
    z_ref[...] = jnp.dot(hb, wz_ref[...], preferred_element_type=F32).astype(BF16)
    gates_ref[...] = _sigmoid(jnp.dot(hb, wg_ref[...], preferred_element_type=F32))


def _post_a(x, og, gate0, wo, kvg, wkvc, wkvr, g1, scale1, shift1, wq, wz, wg):
    bn, t, d = x.shape
    tm = ROW_TILE
    row = lambda b, i: (b, i, 0)
    per_b = lambda b, i: (b, 0, 0)
    const = lambda b, i: (0, 0)
    full = lambda a: pl.BlockSpec(a.shape, const)
    vec = pl.BlockSpec((1, d), const)
    bvec = pl.BlockSpec((None, 1, d), per_b)
    outs = [(d, F32), (wkvc.shape[1], F32), (wkvr.shape[1], BF16), (wq.shape[1], BF16),
            (wz.shape[1], BF16), (wg.shape[1], F32)]
    return pl.pallas_call(
        _post_a_kernel,
        grid=(bn, t // tm),
        in_specs=[pl.BlockSpec((None, tm, d), row), pl.BlockSpec((None, tm, A_WIDTH), row), bvec,
                  full(wo), vec, full(wkvc), full(wkvr), vec, bvec, bvec, full(wq), full(wz), full(wg)],
        out_specs=[pl.BlockSpec((None, tm, n), row) for n, _ in outs],
        out_shape=[jax.ShapeDtypeStruct((bn, t, n), dt) for n, dt in outs],
        compiler_params=_cparams("arbitrary", "arbitrary"),
    )(x, og, gate0, wo, kvg, wkvc, wkvr, g1, scale1, shift1, wq, wz, wg)


def _compress_kernel(hv_ref, ptop_ref, pbot_ref, w1t_ref, w1b_ref, w2_ref, o_ref):
    hv = hv_ref[...]
    a = _dot(hv + ptop_ref[...], w1t_ref[...])
    b = _dot(hv + pbot_ref[...], w1b_ref[...])
    nrow = a.shape[0]
    hid = a + pltpu.roll(b, nrow - 1, axis=0)
    o_ref[...] = _dot(_silu(hid), w2_ref[...]).astype(o_ref.dtype)


def _compress(halves, ptop, pbot, w1t, w1b, w2):
    bn, four, nh, wd = halves.shape
    hid = w1t.shape[-1]
    per_kind = lambda b, j: (j // B_GROUPS, 0, 0)
    return pl.pallas_call(
        _compress_kernel,
        grid=(bn, four),
        in_specs=[pl.BlockSpec((None, None, nh, wd), lambda b, j: (b, j, 0, 0)),
                  pl.BlockSpec((None, 1, wd), per_kind),
                  pl.BlockSpec((None, 1, wd), per_kind),
                  pl.BlockSpec((None, wd, hid), per_kind),
                  pl.BlockSpec((None, wd, hid), per_kind),
                  pl.BlockSpec((None, hid, B_HEAD_DIM), per_kind)],
        out_specs=pl.BlockSpec((None, None, nh, B_HEAD_DIM), lambda b, j: (b, j, 0, 0)),
        out_shape=jax.ShapeDtypeStruct((bn, four, nh, B_HEAD_DIM), BF16),
        compiler_params=_cparams("arbitrary", "arbitrary"),
    )(halves, ptop, pbot, w1t, w1b, w2)


def _t5_bucket_np(dist):
    n = np.maximum(dist, 0)
    max_exact = NUM_BUCKETS // 2
    nf = np.maximum(n, 1).astype(np.float64)
    val = np.log(nf / max_exact) / math.log(MAX_DISTANCE / max_exact) * (NUM_BUCKETS - max_exact)
    frac = np.abs(val - np.round(val))
    safe = (frac > 1e-6) | (n <= max_exact) | (n >= MAX_DISTANCE)
    assert bool(np.all(safe)), "bucket boundary too close to an integer distance"
    large = np.minimum(max_exact + np.floor(np.maximum(val, 0.0)).astype(np.int64), NUM_BUCKETS - 1)
    return np.where(n < max_exact, n, large)


def _bias_onehot():
    r = np.arange(NSA_Q_TILE)[:, None]
    tiles = []
    j = np.arange(WINDOW + NSA_Q_TILE)[None, :]
    d = r + WINDOW - j
    tiles.append((d, (d >= 0) & (d < WINDOW)))
    j = np.arange((NSA_SUB + 2) * L_SLC)[None, :]
    d = r + 2 * L_SLC - j
    tiles.append((d, d >= 0))
    j = np.arange(CMP_NEAR)[None, :]
    d = r - CMP_STRIDE * (j - CMP_LEAD) - (L_CMP - 1)
    tiles.append((d, d >= 0))
    cols = [np.where(valid, _t5_bucket_np(d), NUM_BUCKETS).reshape(-1) for d, valid in tiles]
    widths = [c.size for c in cols]
    return np.concatenate(cols).astype(np.int32)[None, :], widths


def _bias_kernel(rb_ref, bk_ref, o_ref):
    rb = rb_ref[...]
    lane = lax.broadcasted_iota(jnp.int32, rb.shape, 1)
    rbs = rb - rb[:, NUM_BUCKETS - 1:NUM_BUCKETS]
    rbs = jnp.where(lane < NUM_BUCKETS, rbs, jnp.where(lane == NUM_BUCKETS, NEG_INF, 0.0))
    bk = bk_ref[...]
    onehot = jnp.where(lax.broadcasted_iota(jnp.int32, (2 * NUM_BUCKETS, bk.shape[1]), 0) == bk, 1.0, 0.0)
    o_ref[...] = _dot_f32(rbs, onehot)


def _bias_tables(rel_bias):
    bk, widths = _bias_onehot()
    ncol = bk.shape[1]
    nt = 8
    assert ncol % (nt * LANES) == 0
    tc = ncol // nt
    rb = jnp.concatenate([rel_bias.T, jnp.zeros((B_HEADS, NUM_BUCKETS), F32)], axis=1)
    flat = pl.pallas_call(
        _bias_kernel,
        grid=(nt,),
        in_specs=[pl.BlockSpec((B_HEADS, 2 * NUM_BUCKETS), lambda i: (0, 0)),
                  pl.BlockSpec((1, tc), lambda i: (0, i))],
        out_specs=pl.BlockSpec((B_HEADS, tc), lambda i: (0, i)),
        out_shape=jax.ShapeDtypeStruct((B_HEADS, ncol), F32),
        compiler_params=_cparams("arbitrary"),
    )(rb, jnp.asarray(bk))
    out, start = [], 0
    for wd in widths:
        tile = flat[:, start:start + wd].reshape(B_GROUPS, B_HPG * NSA_Q_TILE, wd // NSA_Q_TILE)
        out.append(tile)
        start += wd
    return out


def _nsa_kernel(q_ref, kc_ref, vc_ref, ks_ref, vs_ref, kw_ref, vw_ref, tcmp_ref, tsel_ref, twin_ref,
                ov_ref, zc_ref, zs_ref, zw_ref, gates_ref, ex_ref, y_ref, sa_ref, sb_ref):
    tq = NSA_Q_TILE
    dh = B_HEAD_DIM
    hpg = B_HPG
    rows = hpg * tq
    ti = pl.program_id(2)
    q0 = ti * tq
    blk0 = ti * NSA_SUB
    qt = q_ref[...]
    q = jnp.concatenate([qt[:, h * dh:(h + 1) * dh] for h in range(hpg)], axis=0)

    def to_tokens(o):
        return jnp.concatenate([o[h * tq:(h + 1) * tq, :] for h in range(hpg)], axis=1)

    def finish(pv):
        return to_tokens(pv[:, :dh] * (1.0 / pv[:, dh:dh + 1]))

    gt = gates_ref[...]
    g_hi = gt.astype(BF16)
    ghl = jnp.concatenate([g_hi, (gt - g_hi.astype(F32)).astype(BF16)], axis=1)

    def gated(o_tok, br, z_ref):
        gexp = jnp.dot(ghl, ex_ref[br], preferred_element_type=F32)
        return gexp * o_tok * _silu(z_ref[...].astype(F32))

    kc = kc_ref[...]
    ncp = kc.shape[0]
    nw = WINDOW + tq
    win0 = pl.multiple_of(q0, tq)
    first_near = (tq // CMP_STRIDE) * ti - CMP_LEAD
    cid = lax.broadcasted_iota(jnp.int32, (2 * CMP_NEAR, ncp), 1)
    jrow = lax.broadcasted_iota(jnp.int32, (2 * CMP_NEAR, ncp), 0) & (CMP_NEAR - 1)
    shift_eye = jnp.where(cid - first_near == jrow, 1.0, 0.0).astype(BF16)
    pad_col = jnp.where(lax.broadcasted_iota(jnp.int32, (rows, dh), 1) == 0, NEG_INF, 0.0).astype(BF16)
    q_win = jnp.concatenate([q, pad_col], axis=1)

    s = _dot_nt(q, kc) + jnp.dot(tcmp_ref[...], shift_eye, preferred_element_type=F32)
    s_w = _dot_nt(q_win, kw_ref[pl.ds(win0, nw), :]) + twin_ref[...]

    cvis = lax.broadcasted_iota(jnp.int32, (1, ncp), 1) < first_near + CMP_NEAR
    s = jnp.where(cvis, s, NEG_INF)
    live = s > 0.1 * NEG_INF
    m = jnp.max(s, axis=-1, keepdims=True)
    e = jnp.where(live, jnp.exp(s - m), 0.0)
    p = e * (1.0 / jnp.maximum(jnp.sum(e, axis=-1, keepdims=True), 1e-30))
    y_c = gated(to_tokens(_dot(p, vc_ref[...])), 0, zc_ref)

    psum = p[0:tq, :]
    for h in range(1, hpg):
        psum = psum + p[h * tq:(h + 1) * tq, :]
    p_hi = psum.astype(BF16)
    p_r1 = psum - p_hi.astype(F32)
    p_mid = p_r1.astype(BF16)
    p_lo = (p_r1 - p_mid.astype(F32)).astype(BF16)
    p3 = jnp.concatenate([p_hi, p_mid, p_lo], axis=1)
    imp_t = _dot_nt(ov_ref[...], p3)

    m_w = jnp.max(s_w, axis=-1, keepdims=True)
    e_w = jnp.exp(s_w - m_w)
    y_cw = y_c + gated(finish(_dot(e_w, vw_ref[pl.ds(win0, nw), :])), 2, zw_ref)

    nblk = imp_t.shape[0]
    blk = lax.broadcasted_iota(jnp.int32, (nblk, tq), 0)
    cur = blk0 + (lax.broadcasted_iota(jnp.int32, (nblk, tq), 1) >> SLC_SHIFT)
    forced = (blk == 0) | (blk == cur) | (blk == cur - 1)
    val = jnp.where(forced, SEL_BOOST, jnp.where(blk > cur, -SEL_BOOST, imp_t))
    nslab = nblk // SUBLANES
    slabs = [val[SUBLANES * r:SUBLANES * (r + 1), :] for r in range(nslab)]
    sub = lax.broadcasted_iota(jnp.int32, (SUBLANES, tq), 0)
    n_acc = 4
    ranks = [[jnp.zeros((SUBLANES, tq), jnp.int32) for _ in range(n_acc)] for _ in range(nslab)]
    for j in range(nblk):
        vj = jnp.broadcast_to(val[j:j + 1, :], (SUBLANES, tq))
        for r in range(nslab):
            lo = SUBLANES * r
            if lo > j:
                ahead = vj >= slabs[r]
            elif lo + SUBLANES - 1 <= j:
                ahead = vj > slabs[r]
            else:
                ahead = (vj > slabs[r]) | ((vj == slabs[r]) & (sub > j - lo))
            ranks[r][j % n_acc] = ranks[r][j % n_acc] + ahead.astype(jnp.int32)
    rank = jnp.concatenate([(a[0] + a[1]) + (a[2] + a[3]) for a in ranks], axis=0)
    sel_t = (rank < N_SEL) & (blk <= cur)
    far_t = jnp.where(sel_t & (blk <= blk0 - 3), 0.0, NEG_INF)
    near_t = jnp.where(sel_t & (blk >= blk0 - 2), 0.0, NEG_INF)

    def q_with_mask(mask_t):
        mk = mask_t.T.astype(BF16)
        return jnp.concatenate([q, jnp.concatenate([mk] * hpg, axis=0)], axis=1)

    q_far = q_with_mask(far_t)
    q_near = q_with_mask(near_t)

    kt_sz = SEL_KEY_TILE
    n_far_keys = jnp.maximum(blk0 - 2, 0) * L_SLC
    n_pairs = (n_far_keys + 2 * kt_sz - 1) // (2 * kt_sz)

    def far_scores(tile):
        start = pl.multiple_of(KV_PAD + tile * kt_sz, kt_sz)
        return _dot_nt(q_far, ks_ref[pl.ds(start, kt_sz), :])

    def far_values(tile):
        start = pl.multiple_of(KV_PAD + tile * kt_sz, kt_sz)
        return vs_ref[pl.ds(start, kt_sz), :]

    def update(carry, s_t, v_t):
        m_i, acc = carry
        m_n = jnp.maximum(m_i, jnp.max(s_t, axis=-1, keepdims=True))
        e_t = jnp.exp(s_t - m_n)
        return m_n, jnp.exp(m_i - m_n) * acc + _dot(e_t, v_t)

    def pair_step(j, carry):
        sb_ref[...] = far_scores(2 * j + 1)
        carry = update(carry, sa_ref[...], far_values(2 * j))
        sa_ref[...] = far_scores(2 * j + 2)
        return update(carry, sb_ref[...], far_values(2 * j + 1))

    sa_ref[...] = far_scores(0)
    last = jnp.maximum(n_pairs, 1) - 1
    carry = (jnp.full((rows, 1), NEG_INF, F32), jnp.zeros((rows, 2 * dh), F32))
    carry = lax.fori_loop(0, last, pair_step, carry)
    sb_ref[...] = far_scores(2 * last + 1)
    carry = update(carry, sa_ref[...], far_values(2 * last))
    nk = (NSA_SUB + 2) * L_SLC
    near0 = pl.multiple_of(KV_PAD + q0 - 2 * L_SLC, L_SLC)
    s_n = _dot_nt(q_near, ks_ref[pl.ds(near0, nk), :]) + tsel_ref[...]
    carry = update(carry, sb_ref[...], far_values(2 * last + 1))
    _, acc = update(carry, s_n, vs_ref[pl.ds(near0, nk), :])
    y_ref[...] = (y_cw + gated(finish(acc), 1, zs_ref)).astype(y_ref.dtype)


def _gate_selectors(ng):
    ex = np.zeros((B_GROUPS, N_BRANCH, 2 * ng, B_HPG * B_HEAD_DIM), np.float32)
    for g in range(B_GROUPS):
        for br in range(N_BRANCH):
            for h in range(B_HPG):
                lane = br * B_HEADS + g * B_HPG + h
                ex[g, br, lane, h * B_HEAD_DIM:(h + 1) * B_HEAD_DIM] = 1.0
                ex[g, br, ng + lane, h * B_HEAD_DIM:(h + 1) * B_HEAD_DIM] = 1.0
    return ex


def _nsa(q, kcv, ks, vs, kw, vw, tcmp, tsel, twin, ov, z, gates):
    bn, t, _ = q.shape
    tq = NSA_Q_TILE
    gw = B_HPG * B_HEAD_DIM
    ncp = kcv.shape[2]
    tp = ks.shape[2]
    ng = gates.shape[2]
    rows = B_HPG * tq
    per_bg = pl.BlockSpec((None, None, tp, ks.shape[3]), lambda b, g, i: (b, g, 0, 0))
    per_g = lambda b, g, i: (g, 0, 0)
    ex = jnp.asarray(_gate_selectors(ng), BF16)

    def z_spec(br):
        return pl.BlockSpec((None, tq, gw), lambda b, g, i: (b, i, br * B_GROUPS + g))

    return pl.pallas_call(
        _nsa_kernel,
        grid=(bn, B_GROUPS, t // tq),
        in_specs=[pl.BlockSpec((None, tq, gw), lambda b, g, i: (b, i, g)),
                  pl.BlockSpec((None, None, ncp, B_HEAD_DIM), lambda b, g, i: (b, g, 0, 0)),
                  pl.BlockSpec((None, None, ncp, B_HEAD_DIM), lambda b, g, i: (b, B_GROUPS + g, 0, 0)),
                  per_bg, per_bg, per_bg, per_bg,
                  pl.BlockSpec((None, rows, tcmp.shape[2]), per_g),
                  pl.BlockSpec((None, rows, tsel.shape[2]), per_g),
                  pl.BlockSpec((None, rows, twin.shape[2]), per_g),
                  pl.BlockSpec(ov.shape, lambda b, g, i: (0, 0)),
                  z_spec(0), z_spec(1), z_spec(2),
                  pl.BlockSpec((None, tq, ng), lambda b, g, i: (b, i, 0)),
                  pl.BlockSpec((None,) + ex.shape[1:], lambda b, g, i: (g, 0, 0, 0))],
        out_specs=pl.BlockSpec((None, tq, gw), lambda b, g, i: (b, i, g)),
        out_shape=jax.ShapeDtypeStruct((bn, t, B_WIDTH), BF16),
        scratch_shapes=[pltpu.VMEM((rows, SEL_KEY_TILE), F32), pltpu.VMEM((rows, SEL_KEY_TILE), F32)],
        compiler_params=_cparams("arbitrary", "arbitrary", "arbitrary"),
    )(q, kcv, kcv, ks, vs, kw, vw, tcmp, tsel, twin, ov, z, z, z, gates, ex)


def _final_kernel(y_ref, x1_ref, gate_ref, wo_ref, fg_ref, o_ref):
    x2 = x1_ref[...] + gate_ref[...] * jnp.dot(y_ref[...], wo_ref[...], preferred_element_type=F32)
    o_ref[...] = _rms(x2, fg_ref[...])


def _final(y, x1, gate1, wo, fg):
    bn, t, d = x1.shape
    tm = ROW_TILE
    row = lambda b, i: (b, i, 0)
    return pl.pallas_call(
        _final_kernel,
        grid=(bn, t // tm),
        in_specs=[pl.BlockSpec((None, tm, B_WIDTH), row),
                  pl.BlockSpec((None, tm, d), row),
                  pl.BlockSpec((None, 1, d), lambda b, i: (b, 0, 0)),
                  pl.BlockSpec(wo.shape, lambda b, i: (0, 0)),
                  pl.BlockSpec((1, d), lambda b, i: (0, 0))],
        out_specs=pl.BlockSpec((None, tm, d), row),
        out_shape=jax.ShapeDtypeStruct((bn, t, d), F32),
        compiler_params=_cparams("arbitrary", "arbitrary"),
    )(y, x1, gate1, wo, fg)


def _overlap_matrix(ncp, n_cmp, n_slc, nblk):
    cells = np.arange(n_cmp)[:, None] + np.arange(L_CMP // CMP_STRIDE)[None, :]
    ov = (cells[:, None, :] // (L_SLC // CMP_STRIDE) == np.arange(n_slc)[None, :, None]).sum(-1)
    out = np.zeros((ncp, nblk), np.float32)
    out[:n_cmp, :n_slc] = ov
    return out


def _block_onehot(t, nblk):
    oh = np.zeros((KV_PAD + t, nblk), np.float32)
    oh[KV_PAD + np.arange(t), np.arange(t) // L_SLC] = 1.0
    oh[:KV_PAD, nblk - 1] = 1.0
    return oh


def kernel(x, c, rel_bias, ada_w, ada_b, norm_g, a_in_w, a_conv_w, a_A_log, a_dt_bias, a_onorm_g, a_out_w,
           kv_norm_g, kv_w, cmp_pos_k, cmp_pos_v, cmp_k_w1, cmp_k_w2, cmp_v_w1, cmp_v_w2,
           b_in_w, b_out_w, final_g):
    bn, t, d = x.shape
    assert ada_w.shape[0] == 2 and a_in_w.shape[0] == 1 and b_in_w.shape[0] == 1
    assert t % max(ROW_TILE, 2 * SEL_KEY_TILE, NSA_Q_TILE) == 0
    n_slc = t // L_SLC
    nblk = 64
    assert n_slc <= nblk
    n_cmp = (t - L_CMP) // CMP_STRIDE + 1
    ncp = t // CMP_STRIDE

    mod = _ada_modulation(c, ada_w, ada_b)
    shift = mod[:, :, None, :d]
    scale = mod[:, :, None, d:2 * d]
    gate = mod[:, :, None, 2 * d:]

    hb = GDN_HEADS_PER_STEP
    nhg = A_HEADS // hb
    w_in = a_in_w[0]
    wqkv = w_in[:, :3 * A_WIDTH].astype(BF16)
    wz = w_in[:, 3 * A_WIDTH:4 * A_WIDTH].astype(BF16)
    wb = w_in[:, 4 * A_WIDTH:4 * A_WIDTH + A_HEADS]
    wa = w_in[:, 4 * A_WIDTH + A_HEADS:]
    half = LANES // 2
    wba = jnp.zeros((d, nhg, LANES), F32)
    wba = wba.at[:, :, :hb].set(wb.reshape(d, nhg, hb)).at[:, :, half:half + hb].set(wa.reshape(d, nhg, hb))
    wba = wba.reshape(d, nhg * LANES).astype(BF16)
    lane_rows = lambda v: jnp.zeros((nhg, 1, LANES), F32).at[:, 0, half:half + hb].set(v.reshape(nhg, hb))
    q_a, k_a, v_a, z_a, ba = _in_proj_a(x, norm_g[0:1], scale[0], shift[0], wqkv, wz, wba, a_conv_w[0])
    og = _gdn(q_a, k_a, v_a, z_a, ba, lane_rows(a_A_log[0]), lane_rows(a_dt_bias[0]), a_onorm_g[0:1])

    ndh = B_GROUPS * B_HEAD_DIM
    wkvc = kv_w[:, :2 * ndh].astype(BF16)
    wkvr = kv_w[:, 2 * ndh:].astype(BF16)
    w_b = b_in_w[0]
    wq = w_b[:, :B_WIDTH].astype(BF16)
    wzb = w_b[:, B_WIDTH:4 * B_WIDTH].astype(BF16)
    wg = jnp.zeros((d, LANES), F32).at[:, :N_BRANCH * B_HEADS].set(w_b[:, 4 * B_WIDTH:]).astype(BF16)
    x1, kvc, kvr, q, z_b, gates = _post_a(x, og, gate[0], a_out_w[0].astype(BF16), kv_norm_g[None, :], wkvc, wkvr,
                                          norm_g[1:2], scale[1], shift[1], wq, wzb, wg)

    halves = kvc.reshape(bn, ncp, CMP_STRIDE, 2 * B_GROUPS, B_HEAD_DIM).transpose(0, 3, 1, 2, 4)
    halves = halves.reshape(bn, 2 * B_GROUPS, ncp, CMP_STRIDE * B_HEAD_DIM)
    pos = jnp.stack([cmp_pos_k, cmp_pos_v])
    hw = CMP_STRIDE * B_HEAD_DIM
    ptop = pos[:, :CMP_STRIDE].reshape(2, 1, hw)
    pbot = pos[:, CMP_STRIDE:].reshape(2, 1, hw)
    w1 = jnp.stack([cmp_k_w1, cmp_v_w1]).astype(BF16)
    w2 = jnp.stack([cmp_k_w2, cmp_v_w2]).astype(BF16)
    kcv = _compress(halves, ptop, pbot, w1[:, :hw], w1[:, hw:], w2)

    kvr4 = kvr.reshape(bn, t, 4, B_GROUPS, B_HEAD_DIM).transpose(2, 0, 3, 1, 4)
    kvr4 = jnp.pad(kvr4, ((0, 0), (0, 0), (0, 0), (KV_PAD, 0), (0, 0)))
    tp = KV_PAD + t

    def beside(x, extra):
        return jnp.concatenate([x, jnp.broadcast_to(jnp.asarray(extra, BF16), x.shape[:2] + extra.shape)], axis=-1)

    first_col = np.zeros((tp, B_HEAD_DIM), np.float32)
    ones_col = first_col.copy()
    ones_col[:, 0] = 1.0
    pad_col = first_col.copy()
    pad_col[:KV_PAD, 0] = 1.0
    ks = beside(kvr4[0], _block_onehot(t, nblk))
    vs = beside(kvr4[1], ones_col)
    kw = beside(kvr4[2], pad_col)
    vw = beside(kvr4[3], ones_col)

    twin, tsel, tcmp = _bias_tables(rel_bias)
    tc_hi = tcmp.astype(BF16)
    tc_lo = (tcmp - tc_hi.astype(F32)).astype(BF16)
    tcmp2 = jnp.concatenate([tc_hi, tc_lo], axis=-1)
    ov_t = _overlap_matrix(ncp, n_cmp, n_slc, nblk).T
    ov3 = jnp.asarray(np.concatenate([ov_t] * 3, axis=1), BF16)

    y = _nsa(q, kcv, ks, vs, kw, vw, tcmp2, tsel, twin, ov3, z_b, gates)

    return _final(y, x1, gate[1], b_out_w[0].astype(BF16), final_g[None, :])
```

```python
import functools
import math

import numpy as np
import jax
import jax.numpy as jnp
from jax import lax
from jax.experimental import pallas as pl
from jax.experimental.pallas import tpu as pltpu

F32 = jnp.float32
BF16 = jnp.bfloat16
HIGHEST = lax.Precision.HIGHEST

A_HEADS = 8
A_HEAD_DIM = 128
A_WIDTH = A_HEADS * A_HEAD_DIM
A_CONV = 4
A_CHUNK = 64
B_HEADS = 16
B_GROUPS = 2
B_HPG = B_HEADS // B_GROUPS
B_HEAD_DIM = 64
B_WIDTH = B_HEADS * B_HEAD_DIM
N_BRANCH = 3
L_CMP = 32
CMP_STRIDE = 16
L_SLC = 64
N_SEL = 16
WINDOW = 512
Q_BLOCK = 64
NUM_BUCKETS = 32
MAX_DISTANCE = 128
EPS = 1e-6
NEG_INF = -1e30
SEL_BOOST = 1e9

LANES = 128
SUBLANES = 8
VMEM_LIMIT_BYTES = 56 * 1024 * 1024

ROW_TILE = 256
GDN_HEADS_PER_STEP = 8
GDN_CHUNKS_PER_STEP = 2
SEL_KEY_TILE = 512
KV_PAD = WINDOW
NSA_Q_TILE = 128
NSA_SUB = NSA_Q_TILE // Q_BLOCK
SLC_SHIFT = L_SLC.bit_length() - 1
assert 1 << SLC_SHIFT == L_SLC and L_SLC == Q_BLOCK
CMP_LEAD = 12
CMP_NEAR = 32
assert CMP_NEAR >= CMP_LEAD + NSA_Q_TILE // CMP_STRIDE and NSA_Q_TILE % Q_BLOCK == 0
BLK16 = 16


def _cparams(*sem):
    return pltpu.CompilerParams(dimension_semantics=sem, vmem_limit_bytes=VMEM_LIMIT_BYTES)


def _sigmoid(x):
    return 1.0 / (1.0 + jnp.exp(-x))


def _silu(x):
    return x * _sigmoid(x)


def _dot(a, b):
    return jnp.dot(a.astype(BF16), b.astype(BF16), preferred_element_type=F32)


def _dot_nt(a, b):
    return lax.dot_general(a.astype(BF16), b.astype(BF16), (((1,), (1,)), ((), ())),
                           preferred_element_type=F32)


def _dot_f32(a, b):
    return jnp.dot(a, b, precision=HIGHEST, preferred_element_type=F32)


def _rms(x, g):
    ms = jnp.mean(x * x, axis=-1, keepdims=True)
    return x * lax.rsqrt(ms + EPS) * g


def _ada_kernel(c_ref, w_ref, b_ref, o_ref):
    o_ref[...] = _dot_f32(_silu(c_ref[...]), w_ref[...]) + b_ref[...]


def _ada_modulation(c, ada_w, ada_b):
    depth, d, d3 = ada_w.shape
    bn = c.shape[0]
    return pl.pallas_call(
        _ada_kernel,
        grid=(depth, d3 // d),
        in_specs=[pl.BlockSpec((bn, d), lambda l, j: (0, 0)),
                  pl.BlockSpec((None, d, d), lambda l, j: (l, 0, j)),
                  pl.BlockSpec((None, 1, d), lambda l, j: (l, 0, j))],
        out_specs=pl.BlockSpec((None, bn, d), lambda l, j: (l, 0, j)),
        out_shape=jax.ShapeDtypeStruct((depth, bn, d3), F32),
        compiler_params=_cparams("arbitrary", "arbitrary"),
    )(c, ada_w, ada_b.reshape(depth, 1, d3))


def _in_proj_a_kernel(x_ref, g_ref, scale_ref, shift_ref, wqkv_ref, wz_ref, wba_ref, cw_ref,
                      q_ref, k_ref, v_ref, z_ref, ba_ref, buf_ref):
    tm = x_ref.shape[0]
    halo = SUBLANES
    dh = A_HEAD_DIM

    @pl.when(pl.program_id(1) == 0)
    def _():
        buf_ref[0:halo, :] = jnp.zeros((halo, 3 * A_WIDTH), F32)

    h = _rms(x_ref[...], g_ref[...]) * (1.0 + scale_ref[...]) + shift_ref[...]
    hb = h.astype(BF16)
    buf_ref[halo:halo + tm, :] = jnp.dot(hb, wqkv_ref[...], preferred_element_type=F32)
    cw = cw_ref[...]
    xp = buf_ref[...]
    y = xp[halo:, :] * cw[A_CONV - 1:A_CONV, :]
    for kk in range(A_CONV - 1):
        y = y + pltpu.roll(xp, A_CONV - 1 - kk, axis=0)[halo:, :] * cw[kk:kk + 1, :]
    buf_ref[0:halo, :] = xp[tm:, :]
    y = _silu(y)
    for i in range(A_HEADS):
        for which, o_ref, gain in ((0, q_ref, dh ** -0.5), (1, k_ref, 1.0)):
            xh = y[:, which * A_WIDTH + i * dh:which * A_WIDTH + (i + 1) * dh]
            inv = lax.rsqrt(jnp.sum(xh * xh, axis=-1, keepdims=True) + EPS) * gain
            o_ref[:, i * dh:(i + 1) * dh] = (xh * inv).astype(BF16)
    v_ref[...] = y[:, 2 * A_WIDTH:].astype(BF16)
    z_ref[...] = jnp.dot(hb, wz_ref[...], preferred_element_type=F32).astype(BF16)
    ba_ref[...] = jnp.dot(hb, wba_ref[...], preferred_element_type=F32)


def _in_proj_a(x, g, scale, shift, wqkv, wz, wba, conv_w):
    bn, t, d = x.shape
    tm = ROW_TILE
    row = lambda b, i: (b, i, 0)
    per_b = lambda b, i: (b, 0, 0)
    const = lambda b, i: (0, 0)
    nba = wba.shape[1]
    wide = pl.BlockSpec((None, tm, A_WIDTH), row)
    wide_sd = jax.ShapeDtypeStruct((bn, t, A_WIDTH), BF16)
    return pl.pallas_call(
        _in_proj_a_kernel,
        grid=(bn, t // tm),
        in_specs=[pl.BlockSpec((None, tm, d), row),
                  pl.BlockSpec((1, d), const),
                  pl.BlockSpec((None, 1, d), per_b),
                  pl.BlockSpec((None, 1, d), per_b),
                  pl.BlockSpec(wqkv.shape, const),
                  pl.BlockSpec(wz.shape, const),
                  pl.BlockSpec(wba.shape, const),
                  pl.BlockSpec(conv_w.shape, const)],
        out_specs=[wide, wide, wide, wide, pl.BlockSpec((None, tm, nba), row)],
        out_shape=[wide_sd, wide_sd, wide_sd, wide_sd, jax.ShapeDtypeStruct((bn, t, nba), F32)],
        scratch_shapes=[pltpu.VMEM((SUBLANES + tm, 3 * A_WIDTH), F32)],
        compiler_params=_cparams("arbitrary", "arbitrary"),
    )(x, g, scale, shift, wqkv, wz, wba, conv_w)


def _cumsum_rows(x):
    n = x.shape[0]
    row = lax.broadcasted_iota(jnp.int32, x.shape, 0)
    s = 1
    while s < n:
        x = x + jnp.where(row >= s, pltpu.roll(x, s, axis=0), 0.0)
        s *= 2
    return x


def _unit_lower_inverse(ms):
    c = ms[0].shape[0]
    row = lax.broadcasted_iota(jnp.int32, (c, c), 0)
    col = lax.broadcasted_iota(jnp.int32, (c, c), 1)
    eye = (row == col).astype(F32)
    same_blk = (row & -BLK16) == (col & -BLK16)
    d = [jnp.where(same_blk, m, 0.0) for m in ms]
    mo = [m - x for m, x in zip(ms, d)]
    d2 = [_dot(x, x) for x in d]
    td = [eye - x for x in d]
    d4 = [_dot(x, x) for x in d2]
    td = [t + _dot(t, x) for t, x in zip(td, d2)]
    d8 = [_dot(x, x) for x in d4]
    td = [t + _dot(t, x) for t, x in zip(td, d4)]
    td = [t + _dot(t, x) for t, x in zip(td, d8)]
    n = [_dot(t, x) for t, x in zip(td, mo)]
    n2 = [_dot(x, x) for x in n]
    r = [eye - x for x in n]
    r = [a + _dot(a, x) for a, x in zip(r, n2)]
    return [_dot(a, t) for a, t in zip(r, td)]


def _gdn_kernel(q_ref, k_ref, v_ref, z_ref, ba_ref, alog_ref, dtb_ref, ong_ref, o_ref, s_ref, *, hb, nc):
    c = A_CHUNK
    dh = A_HEAD_DIM

    @pl.when(pl.program_id(2) == 0)
    def _():
        s_ref[...] = jnp.zeros(s_ref.shape, F32)

    q_all = q_ref[...]
    k_all = k_ref[...]
    v_all = v_ref[...]
    ba = ba_ref[...]
    beta_t = _sigmoid(ba)
    xa = ba + dtb_ref[...]
    softplus = jnp.maximum(xa, 0.0) + jnp.log(1.0 + jnp.exp(-jnp.abs(xa)))
    g_t = -jnp.exp(alog_ref[...]) * softplus
    gc_t = [_cumsum_rows(g_t[ci * c:(ci + 1) * c, :]) for ci in range(nc)]
    gc_tt = [x.T for x in gc_t]
    egc_t = [jnp.exp(x) for x in gc_t]
    ekd_t = [jnp.exp(x[c - 1:c, :] - x) for x in gc_t]
    egl_t = [jnp.exp(x[c - 1:c, :]) for x in gc_t]

    row = lax.broadcasted_iota(jnp.int32, (c, c), 0)
    col = lax.broadcasted_iota(jnp.int32, (c, c), 1)
    incl = row >= col
    strict = row > col
    heads = range(hb)
    jobs = [(ci, i) for ci in range(nc) for i in heads]
    la = LANES // 2

    def head(x, ci, i):
        return x[ci * c:(ci + 1) * c, i * dh:(i + 1) * dh]

    def lane(xs, ci, i):
        return xs[ci][:, la + i:la + i + 1]

    qnb = [head(q_all, ci, i) for ci, i in jobs]
    knb = [head(k_all, ci, i) for ci, i in jobs]
    qn = [x.astype(F32) for x in qnb]
    kn = [x.astype(F32) for x in knb]
    beta = [beta_t[ci * c:(ci + 1) * c, i:i + 1] for ci, i in jobs]
    kb = [x * y for x, y in zip(kn, beta)]
    decay = [jnp.where(incl, jnp.exp(jnp.where(incl, lane(gc_t, ci, i) - gc_tt[ci][la + i:la + i + 1, :], 0.0)), 0.0)
             for ci, i in jobs]
    m = [jnp.where(strict, _dot_nt(x, y) * d, 0.0) for x, y, d in zip(kb, knb, decay)]
    attn = [(_dot_nt(x, y) * d).astype(BF16) for x, y, d in zip(qnb, knb, decay)]
    rhs = [jnp.concatenate([head(v_all, ci, i).astype(F32) * beta[j], kb[j] * lane(egc_t, ci, i)],
                           axis=1).astype(BF16) for j, (ci, i) in enumerate(jobs)]
    qdec = [(qn[j] * lane(egc_t, ci, i)).astype(BF16) for j, (ci, i) in enumerate(jobs)]
    kdec_t = [(kn[j] * lane(ekd_t, ci, i)).T.astype(BF16) for j, (ci, i) in enumerate(jobs)]
    tinv = _unit_lower_inverse(m)
    uw = [_dot(x, y) for x, y in zip(tinv, rhs)]

    s_cur = [s_ref[i] for i in heads]
    for ci in range(nc):
        sb = [x.astype(BF16) for x in s_cur]
        job = [ci * hb + i for i in heads]
        v_new = [uw[j][:, :dh] - _dot(uw[j][:, dh:], sb[i]) for i, j in zip(heads, job)]
        vnb = [x.astype(BF16) for x in v_new]
        s_cur = [s_cur[i] * lane(egl_t, ci, i) + jnp.dot(kdec_t[j], vnb[i], preferred_element_type=F32)
                 for i, j in zip(heads, job)]
        o = [jnp.dot(qdec[j], sb[i], preferred_element_type=F32)
             + jnp.dot(attn[j], vnb[i], preferred_element_type=F32) for i, j in zip(heads, job)]
        for i in heads:
            rs = slice(ci * c, (ci + 1) * c)
            sl = slice(i * dh, (i + 1) * dh)
            o_ref[rs, sl] = (_rms(o[i], ong_ref[...]) * _silu(z_ref[rs, sl].astype(F32))).astype(o_ref.dtype)
    for i in heads:
        s_ref[i] = s_cur[i]


def _gdn(q, k, v, z, ba, alog_row, dtb_row, onorm_g):
    bn, t, _ = q.shape
    hb = GDN_HEADS_PER_STEP
    nc = GDN_CHUNKS_PER_STEP
    nhg = A_HEADS // hb
    c = nc * A_CHUNK
    w = hb * A_HEAD_DIM
    kern = functools.partial(_gdn_kernel, hb=hb, nc=nc)
    wide = pl.BlockSpec((None, c, w), lambda b, hg, n: (b, n, hg))
    hg_row = pl.BlockSpec((None, 1, LANES), lambda b, hg, n: (hg, 0, 0))
    return pl.pallas_call(
        kern,
        grid=(bn, nhg, t // c),
        in_specs=[wide, wide, wide, wide,
                  pl.BlockSpec((None, c, LANES), lambda b, hg, n: (b, n, hg)),
                  hg_row, hg_row,
                  pl.BlockSpec((1, A_HEAD_DIM), lambda b, hg, n: (0, 0))],
        out_specs=wide,
        out_shape=jax.ShapeDtypeStruct((bn, t, A_WIDTH), BF16),
        scratch_shapes=[pltpu.VMEM((hb, A_HEAD_DIM, A_HEAD_DIM), F32)],
        compiler_params=_cparams("arbitrary", "arbitrary", "arbitrary"),
    )(q, k, v, z, ba, alog_row, dtb_row, onorm_g)


def _post_a_kernel(x_ref, og_ref, gate_ref, wo_ref, kvg_ref, wkvc_ref, wkvr_ref, g1_ref, scale_ref,
                   shift_ref, wq_ref, wz_ref, wg_ref,
                   x1_ref, kvc_ref, kvr_ref, q_ref, z_ref, gates_ref):
    out = _dot(og_ref[...], wo_ref[...])
    x1 = x_ref[...] + gate_ref[...] * out
    x1_ref[...] = x1
    sb = _rms(x1, kvg_ref[...]).astype(BF16)
    kvc_ref[...] = jnp.dot(sb, wkvc_ref[...], preferred_element_type=F32)
    kvr_ref[...] = jnp.dot(sb, wkvr_ref[...], preferred_element_type=F32).astype(BF16)
    h = _rms(x1, g1_ref[...]) * (1.0 + scale_ref[...]) + shift_ref[...]
    hb = h.astype(BF16)
    q = jnp.dot(hb, wq_ref[...], preferred_element_type=F32) * (B_HEAD_DIM ** -0.5)
    q_ref[...] = q.astype(BF16)
    z_ref[...] = jnp.dot(hb, wz_ref[...], preferred_element_type=F32).astype(BF16)
    gates_ref[...] = _sigmoid(jnp.dot(hb, wg_ref[...], preferred_element_type=F32))


def _post_a(x, og, gate0, wo, kvg, wkvc, wkvr, g1, scale1, shift1, wq, wz, wg):
    bn, t, d = x.shape
    tm = ROW_TILE
    row = lambda b, i: (b, i, 0)
    per_b = lambda b, i: (b, 0, 0)
    const = lambda b, i: (0, 0)
    full = lambda a: pl.BlockSpec(a.shape, const)
    vec = pl.BlockSpec((1, d), const)
    bvec = pl.BlockSpec((None, 1, d), per_b)
    outs = [(d, F32), (wkvc.shape[1], F32), (wkvr.shape[1], BF16), (wq.shape[1], BF16),
            (wz.shape[1], BF16), (wg.shape[1], F32)]
    return pl.pallas_call(
        _post_a_kernel,
        grid=(bn, t // tm),
        in_specs=[pl.BlockSpec((None, tm, d), row), pl.BlockSpec((None, tm, A_WIDTH), row), bvec,
                  full(wo), vec, full(wkvc), full(wkvr), vec, bvec, bvec, full(wq), full(wz), full(wg)],
        out_specs=[pl.BlockSpec((None, tm, n), row) for n, _ in outs],
        out_shape=[jax.ShapeDtypeStruct((bn, t, n), dt) for n, dt in outs],
        compiler_params=_cparams("arbitrary", "arbitrary"),
    )(x, og, gate0, wo, kvg, wkvc, wkvr, g1, scale1, shift1, wq, wz, wg)


def _compress_kernel(hv_ref, ptop_ref, pbot_ref, w1t_ref, w1b_ref, w2_ref, o_ref):
    hv = hv_ref[...]
    a = _dot(hv + ptop_ref[...], w1t_ref[...])
    b = _dot(hv + pbot_ref[...], w1b_ref[...])
    nrow = a.shape[0]
    hid = a + pltpu.roll(b, nrow - 1, axis=0)
    o_ref[...] = _dot(_silu(hid), w2_ref[...]).astype(o_ref.dtype)


def _compress(halves, ptop, pbot, w1t, w1b, w2):
    bn, four, nh, wd = halves.shape
    hid = w1t.shape[-1]
    per_kind = lambda b, j: (j // B_GROUPS, 0, 0)
    return pl.pallas_call(
        _compress_kernel,
        grid=(bn, four),
        in_specs=[pl.BlockSpec((None, None, nh, wd), lambda b, j: (b, j, 0, 0)),
                  pl.BlockSpec((None, 1, wd), per_kind),
                  pl.BlockSpec((None, 1, wd), per_kind),
                  pl.BlockSpec((None, wd, hid), per_kind),
                  pl.BlockSpec((None, wd, hid), per_kind),
                  pl.BlockSpec((None, hid, B_HEAD_DIM), per_kind)],
        out_specs=pl.BlockSpec((None, None, nh, B_HEAD_DIM), lambda b, j: (b, j, 0, 0)),
        out_shape=jax.ShapeDtypeStruct((bn, four, nh, B_HEAD_DIM), BF16),
        compiler_params=_cparams("arbitrary", "arbitrary"),
    )(halves, ptop, pbot, w1t, w1b, w2)


def _t5_bucket_np(dist):
    n = np.maximum(dist, 0)
    max_exact = NUM_BUCKETS // 2
    nf = np.maximum(n, 1).astype(np.float64)
    val = np.log(nf / max_exact) / math.log(MAX_DISTANCE / max_exact) * (NUM_BUCKETS - max_exact)
    frac = np.abs(val - np.round(val))
    safe = (frac > 1e-6) | (n <= max_exact) | (n >= MAX_DISTANCE)
    assert bool(np.all(safe)), "bucket boundary too close to an integer distance"
    large = np.minimum(max_exact + np.floor(np.maximum(val, 0.0)).astype(np.int64), NUM_BUCKETS - 1)
    return np.where(n < max_exact, n, large)


def _bias_onehot():
    r = np.arange(NSA_Q_TILE)[:, None]
    tiles = []
    j = np.arange(WINDOW + NSA_Q_TILE)[None, :]
    d = r + WINDOW - j
    tiles.append((d, (d >= 0) & (d < WINDOW)))
    j = np.arange((NSA_SUB + 2) * L_SLC)[None, :]
    d = r + 2 * L_SLC - j
    tiles.append((d, d >= 0))
    j = np.arange(CMP_NEAR)[None, :]
    d = r - CMP_STRIDE * (j - CMP_LEAD) - (L_CMP - 1)
    tiles.append((d, d >= 0))
    cols = [np.where(valid, _t5_bucket_np(d), NUM_BUCKETS).reshape(-1) for d, valid in tiles]
    widths = [c.size for c in cols]
    return np.concatenate(cols).astype(np.int32)[None, :], widths


def _bias_kernel(rb_ref, bk_ref, o_ref):
    rb = rb_ref[...]
    lane = lax.broadcasted_iota(jnp.int32, rb.shape, 1)
    rbs = rb - rb[:, NUM_BUCKETS - 1:NUM_BUCKETS]
    rbs = jnp.where(lane < NUM_BUCKETS, rbs, jnp.where(lane == NUM_BUCKETS, NEG_INF, 0.0))
    bk = bk_ref[...]
    onehot = jnp.where(lax.broadcasted_iota(jnp.int32, (2 * NUM_BUCKETS, bk.shape[1]), 0) == bk, 1.0, 0.0)
    o_ref[...] = _dot_f32(rbs, onehot)


def _bias_tables(rel_bias):
    bk, widths = _bias_onehot()
    ncol = bk.shape[1]
    nt = 8
    assert ncol % (nt * LANES) == 0
    tc = ncol // nt
    rb = jnp.concatenate([rel_bias.T, jnp.zeros((B_HEADS, NUM_BUCKETS), F32)], axis=1)
    flat = pl.pallas_call(
        _bias_kernel,
        grid=(nt,),
        in_specs=[pl.BlockSpec((B_HEADS, 2 * NUM_BUCKETS), lambda i: (0, 0)),
                  pl.BlockSpec((1, tc), lambda i: (0, i))],
        out_specs=pl.BlockSpec((B_HEADS, tc), lambda i: (0, i)),
        out_shape=jax.ShapeDtypeStruct((B_HEADS, ncol), F32),
        compiler_params=_cparams("arbitrary"),
    )(rb, jnp.asarray(bk))
    out, start = [], 0
    for wd in widths:
        tile = flat[:, start:start + wd].reshape(B_GROUPS, B_HPG * NSA_Q_TILE, wd // NSA_Q_TILE)
        out.append(tile)
        start += wd
    return out


def _nsa_kernel(q_ref, kc_ref, vc_ref, slc_ref, win_ref, kst_ref, kwt_ref, vt_ref, tcmp_ref, tsel_ref, twin_ref,
                ov_ref, zc_ref, zs_ref, zw_ref, gates_ref, ex_ref, y_ref,
                sa_ref, sb_ref, ks_ref, vs_ref, kw_ref, vw_ref):
    tq = NSA_Q_TILE
    dh = B_HEAD_DIM
    hpg = B_HPG
    rows = hpg * tq
    ti = pl.program_id(2)

    @pl.when(ti == 0)
    def _():
        ks_ref[...] = kst_ref[...]
        kw_ref[...] = kwt_ref[...]
        vs_ref[...] = vt_ref[...]
        vw_ref[...] = vt_ref[...]
        slc = slc_ref[...]
        win = win_ref[...]
        ks_ref[KV_PAD:, 0:dh] = slc[:, 0:dh]
        vs_ref[KV_PAD:, 0:dh] = slc[:, dh:]
        kw_ref[KV_PAD:, 0:dh] = win[:, 0:dh]
        vw_ref[KV_PAD:, 0:dh] = win[:, dh:]

    q0 = ti * tq
    blk0 = ti * NSA_SUB
    qt = q_ref[...]
    q = jnp.concatenate([qt[:, h * dh:(h + 1) * dh] for h in range(hpg)], axis=0)

    def to_tokens(o):
        return jnp.concatenate([o[h * tq:(h + 1) * tq, :] for h in range(hpg)], axis=1)

    def finish(pv):
        return to_tokens(pv[:, :dh] * (1.0 / pv[:, dh:dh + 1]))

    gt = gates_ref[...]
    g_hi = gt.astype(BF16)
    ghl = jnp.concatenate([g_hi, (gt - g_hi.astype(F32)).astype(BF16)], axis=1)

    def gated(o_tok, br, z_ref):
        gexp = jnp.dot(ghl, ex_ref[br], preferred_element_type=F32)
        return gexp * o_tok * _silu(z_ref[...].astype(F32))

    kc = kc_ref[...]
    ncp = kc.shape[0]
    nw = WINDOW + tq
    win0 = pl.multiple_of(q0, tq)
    first_near = (tq // CMP_STRIDE) * ti - CMP_LEAD
    cid = lax.broadcasted_iota(jnp.int32, (2 * CMP_NEAR, ncp), 1)
    jrow = lax.broadcasted_iota(jnp.int32, (2 * CMP_NEAR, ncp), 0) & (CMP_NEAR - 1)
    shift_eye = jnp.where(cid - first_near == jrow, 1.0, 0.0).astype(BF16)
    pad_col = jnp.where(lax.broadcasted_iota(jnp.int32, (rows, dh), 1) == 0, NEG_INF, 0.0).astype(BF16)
    q_win = jnp.concatenate([q, pad_col], axis=1)

    s = _dot_nt(q, kc) + jnp.dot(tcmp_ref[...], shift_eye, preferred_element_type=F32)
    s_w = _dot_nt(q_win, kw_ref[pl.ds(win0, nw), :]) + twin_ref[...]

    cvis = lax.broadcasted_iota(jnp.int32, (1, ncp), 1) < first_near + CMP_NEAR
    s = jnp.where(cvis, s, NEG_INF)
    live = s > 0.1 * NEG_INF
    m = jnp.max(s, axis=-1, keepdims=True)
    e = jnp.where(live, jnp.exp(s - m), 0.0)
    p = e * (1.0 / jnp.maximum(jnp.sum(e, axis=-1, keepdims=True), 1e-30))
    y_c = gated(to_tokens(_dot(p, vc_ref[...])), 0, zc_ref)

    psum = p[0:tq, :]
    for h in range(1, hpg):
        psum = psum + p[h * tq:(h + 1) * tq, :]
    p_hi = psum.astype(BF16)
    p_r1 = psum - p_hi.astype(F32)
    p_mid = p_r1.astype(BF16)
    p_lo = (p_r1 - p_mid.astype(F32)).astype(BF16)
    p3 = jnp.concatenate([p_hi, p_mid, p_lo], axis=1)
    imp_t = _dot_nt(ov_ref[...], p3)

    m_w = jnp.max(s_w, axis=-1, keepdims=True)
    e_w = jnp.exp(s_w - m_w)
    y_cw = y_c + gated(finish(_dot(e_w, vw_ref[pl.ds(win0, nw), :])), 2, zw_ref)

    nblk = imp_t.shape[0]
    blk = lax.broadcasted_iota(jnp.int32, (nblk, tq), 0)
    cur = blk0 + (lax.broadcasted_iota(jnp.int32, (nblk, tq), 1) >> SLC_SHIFT)
    forced = (blk == 0) | (blk == cur) | (blk == cur - 1)
    val = jnp.where(forced, SEL_BOOST, jnp.where(blk > cur, -SEL_BOOST, imp_t))
    nslab = nblk // SUBLANES
    slabs = [val[SUBLANES * r:SUBLANES * (r + 1), :] for r in range(nslab)]
    sub = lax.broadcasted_iota(jnp.int32, (SUBLANES, tq), 0)
    n_acc = 4
    ranks = [[jnp.zeros((SUBLANES, tq), jnp.int32) for _ in range(n_acc)] for _ in range(nslab)]
    for j in range(nblk):
        vj = jnp.broadcast_to(val[j:j + 1, :], (SUBLANES, tq))
        for r in range(nslab):
            lo = SUBLANES * r
            if lo > j:
                ahead = vj >= slabs[r]
            elif lo + SUBLANES - 1 <= j:
                ahead = vj > slabs[r]
            else:
                ahead = (vj > slabs[r]) | ((vj == slabs[r]) & (sub > j - lo))
            ranks[r][j % n_acc] = ranks[r][j % n_acc] + ahead.astype(jnp.int32)
    rank = jnp.concatenate([(a[0] + a[1]) + (a[2] + a[3]) for a in ranks], axis=0)
    sel_t = (rank < N_SEL) & (blk <= cur)
    far_t = jnp.where(sel_t & (blk <= blk0 - 3), 0.0, NEG_INF)
    near_t = jnp.where(sel_t & (blk >= blk0 - 2), 0.0, NEG_INF)

    def q_with_mask(mask_t):
        mk = mask_t.T.astype(BF16)
        return jnp.concatenate([q, jnp.concatenate([mk] * hpg, axis=0)], axis=1)

    q_far = q_with_mask(far_t)
    q_near = q_with_mask(near_t)

    kt_sz = SEL_KEY_TILE
    n_far_keys = jnp.maximum(blk0 - 2, 0) * L_SLC
    n_pairs = (n_far_keys + 2 * kt_sz - 1) // (2 * kt_sz)

    def far_scores(tile):
        start = pl.multiple_of(KV_PAD + tile * kt_sz, kt_sz)
        return _dot_nt(q_far, ks_ref[pl.ds(start, kt_sz), :])

    def far_values(tile):
        start = pl.multiple_of(KV_PAD + tile * kt_sz, kt_sz)
        return vs_ref[pl.ds(start, kt_sz), :]

    def update(carry, s_t, v_t):
        m_i, acc = carry
        m_n = jnp.maximum(m_i, jnp.max(s_t, axis=-1, keepdims=True))
        e_t = jnp.exp(s_t - m_n)
        return m_n, jnp.exp(m_i - m_n) * acc + _dot(e_t, v_t)

    def pair_step(j, carry):
        sb_ref[...] = far_scores(2 * j + 1)
        carry = update(carry, sa_ref[...], far_values(2 * j))
        sa_ref[...] = far_scores(2 * j + 2)
        return update(carry, sb_ref[...], far_values(2 * j + 1))

    sa_ref[...] = far_scores(0)
    last = jnp.maximum(n_pairs, 1) - 1
    carry = (jnp.full((rows, 1), NEG_INF, F32), jnp.zeros((rows, 2 * dh), F32))
    carry = lax.fori_loop(0, last, pair_step, carry)
    sb_ref[...] = far_scores(2 * last + 1)
    carry = update(carry, sa_ref[...], far_values(2 * last))
    nk = (NSA_SUB + 2) * L_SLC
    near0 = pl.multiple_of(KV_PAD + q0 - 2 * L_SLC, L_SLC)
    s_n = _dot_nt(q_near, ks_ref[pl.ds(near0, nk), :]) + tsel_ref[...]
    carry = update(carry, sb_ref[...], far_values(2 * last + 1))
    _, acc = update(carry, s_n, vs_ref[pl.ds(near0, nk), :])
    y_ref[...] = (y_cw + gated(finish(acc), 1, zs_ref)).astype(y_ref.dtype)


def _gate_selectors(ng):
    ex = np.zeros((B_GROUPS, N_BRANCH, 2 * ng, B_HPG * B_HEAD_DIM), np.float32)
    for g in range(B_GROUPS):
        for br in range(N_BRANCH):
            for h in range(B_HPG):
                lane = br * B_HEADS + g * B_HPG + h
                ex[g, br, lane, h * B_HEAD_DIM:(h + 1) * B_HEAD_DIM] = 1.0
                ex[g, br, ng + lane, h * B_HEAD_DIM:(h + 1) * B_HEAD_DIM] = 1.0
    return ex


def _kv_templates(t, nblk):
    tp = KV_PAD + t
    ks_t = np.zeros((tp, 2 * B_HEAD_DIM), np.float32)
    ks_t[:, B_HEAD_DIM:] = _block_onehot(t, nblk)
    kw_t = np.zeros((tp, 2 * B_HEAD_DIM), np.float32)
    kw_t[:KV_PAD, B_HEAD_DIM] = 1.0
    v_t = np.zeros((tp, 2 * B_HEAD_DIM), np.float32)
    v_t[:, B_HEAD_DIM] = 1.0
    return [jnp.asarray(a, BF16) for a in (ks_t, kw_t, v_t)]


def _nsa(q, kcv, kvr, tcmp, tsel, twin, ov, z, gates):
    bn, t, _ = q.shape
    tq = NSA_Q_TILE
    gw = B_HPG * B_HEAD_DIM
    ncp = kcv.shape[2]
    tp = KV_PAD + t
    kvw = 2 * B_HEAD_DIM
    nblk = ov.shape[0]
    ng = gates.shape[2]
    rows = B_HPG * tq
    per_g = lambda b, g, i: (g, 0, 0)
    tmpl = pl.BlockSpec((tp, kvw), lambda b, g, i: (0, 0))
    ex = jnp.asarray(_gate_selectors(ng), BF16)
    kst, kwt, vt = _kv_templates(t, nblk)

    def z_spec(br):
        return pl.BlockSpec((None, tq, gw), lambda b, g, i: (b, i, br * B_GROUPS + g))

    return pl.pallas_call(
        _nsa_kernel,
        grid=(bn, B_GROUPS, t // tq),
        in_specs=[pl.BlockSpec((None, tq, gw), lambda b, g, i: (b, i, g)),
                  pl.BlockSpec((None, None, ncp, B_HEAD_DIM), lambda b, g, i: (b, g, 0, 0)),
                  pl.BlockSpec((None, None, ncp, B_HEAD_DIM), lambda b, g, i: (b, B_GROUPS + g, 0, 0)),
                  pl.BlockSpec((None, t, kvw), lambda b, g, i: (b, 0, g)),
                  pl.BlockSpec((None, t, kvw), lambda b, g, i: (b, 0, B_GROUPS + g)),
                  tmpl, tmpl, tmpl,
                  pl.BlockSpec((None, rows, tcmp.shape[2]), per_g),
                  pl.BlockSpec((None, rows, tsel.shape[2]), per_g),
                  pl.BlockSpec((None, rows, twin.shape[2]), per_g),
                  pl.BlockSpec(ov.shape, lambda b, g, i: (0, 0)),
                  z_spec(0), z_spec(1), z_spec(2),
                  pl.BlockSpec((None, tq, ng), lambda b, g, i: (b, i, 0)),
                  pl.BlockSpec((None,) + ex.shape[1:], lambda b, g, i: (g, 0, 0, 0))],
        out_specs=pl.BlockSpec((None, tq, gw), lambda b, g, i: (b, i, g)),
        out_shape=jax.ShapeDtypeStruct((bn, t, B_WIDTH), BF16),
        scratch_shapes=[pltpu.VMEM((rows, SEL_KEY_TILE), F32), pltpu.VMEM((rows, SEL_KEY_TILE), F32)]
        + [pltpu.VMEM((tp, kvw), BF16)] * 4,
        compiler_params=_cparams("arbitrary", "arbitrary", "arbitrary"),
    )(q, kcv, kcv, kvr, kvr, kst, kwt, vt, tcmp, tsel, twin, ov, z, z, z, gates, ex)


def _final_kernel(y_ref, x1_ref, gate_ref, wo_ref, fg_ref, o_ref):
    x2 = x1_ref[...] + gate_ref[...] * jnp.dot(y_ref[...], wo_ref[...], preferred_element_type=F32)
    o_ref[...] = _rms(x2, fg_ref[...])


def _final(y, x1, gate1, wo, fg):
    bn, t, d = x1.shape
    tm = ROW_TILE
    row = lambda b, i: (b, i, 0)
    return pl.pallas_call(
        _final_kernel,
        grid=(bn, t // tm),
        in_specs=[pl.BlockSpec((None, tm, B_WIDTH), row),
                  pl.BlockSpec((None, tm, d), row),
                  pl.BlockSpec((None, 1, d), lambda b, i: (b, 0, 0)),
                  pl.BlockSpec(wo.shape, lambda b, i: (0, 0)),
                  pl.BlockSpec((1, d), lambda b, i: (0, 0))],
        out_specs=pl.BlockSpec((None, tm, d), row),
        out_shape=jax.ShapeDtypeStruct((bn, t, d), F32),
        compiler_params=_cparams("arbitrary", "arbitrary"),
    )(y, x1, gate1, wo, fg)


def _overlap_matrix(ncp, n_cmp, n_slc, nblk):
    cells = np.arange(n_cmp)[:, None] + np.arange(L_CMP // CMP_STRIDE)[None, :]
    ov = (cells[:, None, :] // (L_SLC // CMP_STRIDE) == np.arange(n_slc)[None, :, None]).sum(-1)
    out = np.zeros((ncp, nblk), np.float32)
    out[:n_cmp, :n_slc] = ov
    return out


def _block_onehot(t, nblk):
    oh = np.zeros((KV_PAD + t, nblk), np.float32)
    oh[KV_PAD + np.arange(t), np.arange(t) // L_SLC] = 1.0
    oh[:KV_PAD, nblk - 1] = 1.0
    return oh


def kernel(x, c, rel_bias, ada_w, ada_b, norm_g, a_in_w, a_conv_w, a_A_log, a_dt_bias, a_onorm_g, a_out_w,
           kv_norm_g, kv_w, cmp_pos_k, cmp_pos_v, cmp_k_w1, cmp_k_w2, cmp_v_w1, cmp_v_w2,
           b_in_w, b_out_w, final_g):
    bn, t, d = x.shape
    assert ada_w.shape[0] == 2 and a_in_w.shape[0] == 1 and b_in_w.shape[0] == 1
    assert t % max(ROW_TILE, 2 * SEL_KEY_TILE, NSA_Q_TILE) == 0
    n_slc = t // L_SLC
    nblk = 64
    assert n_slc <= nblk
    n_cmp = (t - L_CMP) // CMP_STRIDE + 1
    ncp = t // CMP_STRIDE

    mod = _ada_modulation(c, ada_w, ada_b)
    shift = mod[:, :, None, :d]
    scale = mod[:, :, None, d:2 * d]
    gate = mod[:, :, None, 2 * d:]

    hb = GDN_HEADS_PER_STEP
    nhg = A_HEADS // hb
    w_in = a_in_w[0]
    wqkv = w_in[:, :3 * A_WIDTH].astype(BF16)
    wz = w_in[:, 3 * A_WIDTH:4 * A_WIDTH].astype(BF16)
    wb = w_in[:, 4 * A_WIDTH:4 * A_WIDTH + A_HEADS]
    wa = w_in[:, 4 * A_WIDTH + A_HEADS:]
    half = LANES // 2
    wba = jnp.zeros((d, nhg, LANES), F32)
    wba = wba.at[:, :, :hb].set(wb.reshape(d, nhg, hb)).at[:, :, half:half + hb].set(wa.reshape(d, nhg, hb))
    wba = wba.reshape(d, nhg * LANES).astype(BF16)
    lane_rows = lambda v: jnp.zeros((nhg, 1, LANES), F32).at[:, 0, half:half + hb].set(v.reshape(nhg, hb))
    q_a, k_a, v_a, z_a, ba = _in_proj_a(x, norm_g[0:1], scale[0], shift[0], wqkv, wz, wba, a_conv_w[0])
    og = _gdn(q_a, k_a, v_a, z_a, ba, lane_rows(a_A_log[0]), lane_rows(a_dt_bias[0]), a_onorm_g[0:1])

    ndh = B_GROUPS * B_HEAD_DIM
    wkvc = kv_w[:, :2 * ndh].astype(BF16)
    kv_rest = kv_w[:, 2 * ndh:].reshape(d, 4, B_GROUPS, B_HEAD_DIM)
    wkvr = jnp.concatenate([jnp.stack([kv_rest[:, 0], kv_rest[:, 1]], axis=2),
                            jnp.stack([kv_rest[:, 2], kv_rest[:, 3]], axis=2)], axis=1)
    wkvr = wkvr.reshape(d, 4 * ndh).astype(BF16)
    w_b = b_in_w[0]
    wq = w_b[:, :B_WIDTH].astype(BF16)
    wzb = w_b[:, B_WIDTH:4 * B_WIDTH].astype(BF16)
    wg = jnp.zeros((d, LANES), F32).at[:, :N_BRANCH * B_HEADS].set(w_b[:, 4 * B_WIDTH:]).astype(BF16)
    x1, kvc, kvr, q, z_b, gates = _post_a(x, og, gate[0], a_out_w[0].astype(BF16), kv_norm_g[None, :], wkvc, wkvr,
                                          norm_g[1:2], scale[1], shift[1], wq, wzb, wg)

    halves = kvc.reshape(bn, ncp, CMP_STRIDE, 2 * B_GROUPS, B_HEAD_DIM).transpose(0, 3, 1, 2, 4)
    halves = halves.reshape(bn, 2 * B_GROUPS, ncp, CMP_STRIDE * B_HEAD_DIM)
    pos = jnp.stack([cmp_pos_k, cmp_pos_v])
    hw = CMP_STRIDE * B_HEAD_DIM
    ptop = pos[:, :CMP_STRIDE].reshape(2, 1, hw)
    pbot = pos[:, CMP_STRIDE:].reshape(2, 1, hw)
    w1 = jnp.stack([cmp_k_w1, cmp_v_w1]).astype(BF16)
    w2 = jnp.stack([cmp_k_w2, cmp_v_w2]).astype(BF16)
    kcv = _compress(halves, ptop, pbot, w1[:, :hw], w1[:, hw:], w2)

    twin, tsel, tcmp = _bias_tables(rel_bias)
    tc_hi = tcmp.astype(BF16)
    tc_lo = (tcmp - tc_hi.astype(F32)).astype(BF16)
    tcmp2 = jnp.concatenate([tc_hi, tc_lo], axis=-1)
    ov_t = _overlap_matrix(ncp, n_cmp, n_slc, nblk).T
    ov3 = jnp.asarray(np.concatenate([ov_t] * 3, axis=1), BF16)

    y = _nsa(q, kcv, kvr, tcmp2, tsel, twin, ov3, z_b, gates)

    return _final(y, x1, gate[1], b_out_w[0].astype(BF16), final_g[None, :])
```

```python
import functools
import math

import numpy as np
import jax
import jax.numpy as jnp
from jax import lax
from jax.experimental import pallas as pl
from jax.experimental.pallas import tpu as pltpu

F32 = jnp.float32
BF16 = jnp.bfloat16
HIGHEST = lax.Precision.HIGHEST

A_HEADS = 8
A_HEAD_DIM = 128
A_WIDTH = A_HEADS * A_HEAD_DIM
A_CONV = 4
A_CHUNK = 64
B_HEADS = 16
B_GROUPS = 2
B_HPG = B_HEADS // B_GROUPS
B_HEAD_DIM = 64
B_WIDTH = B_HEADS * B_HEAD_DIM
N_BRANCH = 3
L_CMP = 32
CMP_STRIDE = 16
L_SLC = 64
N_SEL = 16
WINDOW = 512
Q_BLOCK = 64
NUM_BUCKETS = 32
MAX_DISTANCE = 128
EPS = 1e-6
NEG_INF = -1e30
SEL_BOOST = 1e9

LANES = 128
SUBLANES = 8
VMEM_LIMIT_BYTES = 56 * 1024 * 1024

ROW_TILE = 256
GDN_HEADS_PER_STEP = 8
GDN_CHUNKS_PER_STEP = 4
SEL_KEY_TILE = 512
KV_PAD = WINDOW
NSA_Q_TILE = 128
NSA_SUB = NSA_Q_TILE // Q_BLOCK
SLC_SHIFT = L_SLC.bit_length() - 1
assert 1 << SLC_SHIFT == L_SLC and L_SLC == Q_BLOCK
CMP_LEAD = 12
CMP_NEAR = 32
assert CMP_NEAR >= CMP_LEAD + NSA_Q_TILE // CMP_STRIDE and NSA_Q_TILE % Q_BLOCK == 0
BLK16 = 16


def _cparams(*sem):
    return pltpu.CompilerParams(dimension_semantics=sem, vmem_limit_bytes=VMEM_LIMIT_BYTES)


def _sigmoid(x):
    return 1.0 / (1.0 + jnp.exp(-x))


def _silu(x):
    return x * _sigmoid(x)


def _dot(a, b):
    return jnp.dot(a.astype(BF16), b.astype(BF16), preferred_element_type=F32)


def _dot_nt(a, b):
    return lax.dot_general(a.astype(BF16), b.astype(BF16), (((1,), (1,)), ((), ())),
                           preferred_element_type=F32)


def _dot_f32(a, b):
    return jnp.dot(a, b, precision=HIGHEST, preferred_element_type=F32)


def _rms(x, g):
    ms = jnp.mean(x * x, axis=-1, keepdims=True)
    return x * lax.rsqrt(ms + EPS) * g


def _ada_kernel(c_ref, w_ref, b_ref, o_ref):
    o_ref[...] = _dot_f32(_silu(c_ref[...]), w_ref[...]) + b_ref[...]


def _ada_modulation(c, ada_w, ada_b):
    depth, d, d3 = ada_w.shape
    bn = c.shape[0]
    return pl.pallas_call(
        _ada_kernel,
        grid=(depth, d3 // d),
        in_specs=[pl.BlockSpec((bn, d), lambda l, j: (0, 0)),
                  pl.BlockSpec((None, d, d), lambda l, j: (l, 0, j)),
                  pl.BlockSpec((None, 1, d), lambda l, j: (l, 0, j))],
        out_specs=pl.BlockSpec((None, bn, d), lambda l, j: (l, 0, j)),
        out_shape=jax.ShapeDtypeStruct((depth, bn, d3), F32),
        compiler_params=_cparams("arbitrary", "arbitrary"),
    )(c, ada_w, ada_b.reshape(depth, 1, d3))


def _in_proj_a_kernel(x_ref, g_ref, scale_ref, shift_ref, wqkv_ref, wz_ref, wba_ref, cw_ref,
                      q_ref, k_ref, v_ref, z_ref, ba_ref, buf_ref):
    tm = x_ref.shape[0]
    halo = SUBLANES
    dh = A_HEAD_DIM

    @pl.when(pl.program_id(1) == 0)
    def _():
        buf_ref[0:halo, :] = jnp.zeros((halo, 3 * A_WIDTH), F32)

    h = _rms(x_ref[...], g_ref[...]) * (1.0 + scale_ref[...]) + shift_ref[...]
    hb = h.astype(BF16)
    buf_ref[halo:halo + tm, :] = jnp.dot(hb, wqkv_ref[...], preferred_element_type=F32)
    cw = cw_ref[...]
    xp = buf_ref[...]
    y = xp[halo:, :] * cw[A_CONV - 1:A_CONV, :]
    for kk in range(A_CONV - 1):
        y = y + pltpu.roll(xp, A_CONV - 1 - kk, axis=0)[halo:, :] * cw[kk:kk + 1, :]
    buf_ref[0:halo, :] = xp[tm:, :]
    y = _silu(y)
    for i in range(A_HEADS):
        for which, o_ref, gain in ((0, q_ref, dh ** -0.5), (1, k_ref, 1.0)):
            xh = y[:, which * A_WIDTH + i * dh:which * A_WIDTH + (i + 1) * dh]
            inv = lax.rsqrt(jnp.sum(xh * xh, axis=-1, keepdims=True) + EPS) * gain
            o_ref[:, i * dh:(i + 1) * dh] = (xh * inv).astype(BF16)
    v_ref[...] = y[:, 2 * A_WIDTH:].astype(BF16)
    z_ref[...] = jnp.dot(hb, wz_ref[...], preferred_element_type=F32).astype(BF16)
    ba_ref[...] = jnp.dot(hb, wba_ref[...], preferred_element_type=F32)


def _in_proj_a(x, g, scale, shift, wqkv, wz, wba, conv_w):
    bn, t, d = x.shape
    tm = ROW_TILE
    row = lambda b, i: (b, i, 0)
    per_b = lambda b, i: (b, 0, 0)
    const = lambda b, i: (0, 0)
    nba = wba.shape[1]
    wide = pl.BlockSpec((None, tm, A_WIDTH), row)
    wide_sd = jax.ShapeDtypeStruct((bn, t, A_WIDTH), BF16)
    return pl.pallas_call(
        _in_proj_a_kernel,
        grid=(bn, t // tm),
        in_specs=[pl.BlockSpec((None, tm, d), row),
                  pl.BlockSpec((1, d), const),
                  pl.BlockSpec((None, 1, d), per_b),
                  pl.BlockSpec((None, 1, d), per_b),
                  pl.BlockSpec(wqkv.shape, const),
                  pl.BlockSpec(wz.shape, const),
                  pl.BlockSpec(wba.shape, const),
                  pl.BlockSpec(conv_w.shape, const)],
        out_specs=[wide, wide, wide, wide, pl.BlockSpec((None, tm, nba), row)],
        out_shape=[wide_sd, wide_sd, wide_sd, wide_sd, jax.ShapeDtypeStruct((bn, t, nba), F32)],
        scratch_shapes=[pltpu.VMEM((SUBLANES + tm, 3 * A_WIDTH), F32)],
        compiler_params=_cparams("arbitrary", "arbitrary"),
    )(x, g, scale, shift, wqkv, wz, wba, conv_w)


def _cumsum_rows(x):
    n = x.shape[0]
    row = lax.broadcasted_iota(jnp.int32, x.shape, 0)
    s = 1
    while s < n:
        x = x + jnp.where(row >= s, pltpu.roll(x, s, axis=0), 0.0)
        s *= 2
    return x


def _unit_lower_inverse(ms):
    c = ms[0].shape[0]
    row = lax.broadcasted_iota(jnp.int32, (c, c), 0)
    col = lax.broadcasted_iota(jnp.int32, (c, c), 1)
    eye = (row == col).astype(F32)
    same_blk = (row & -BLK16) == (col & -BLK16)
    d = [jnp.where(same_blk, m, 0.0) for m in ms]
    mo = [m - x for m, x in zip(ms, d)]
    d2 = [_dot(x, x) for x in d]
    td = [eye - x for x in d]
    d4 = [_dot(x, x) for x in d2]
    td = [t + _dot(t, x) for t, x in zip(td, d2)]
    d8 = [_dot(x, x) for x in d4]
    td = [t + _dot(t, x) for t, x in zip(td, d4)]
    td = [t + _dot(t, x) for t, x in zip(td, d8)]
    n = [_dot(t, x) for t, x in zip(td, mo)]
    n2 = [_dot(x, x) for x in n]
    r = [eye - x for x in n]
    r = [a + _dot(a, x) for a, x in zip(r, n2)]
    return [_dot(a, t) for a, t in zip(r, td)]


def _gdn_kernel(q_ref, k_ref, v_ref, z_ref, ba_ref, alog_ref, dtb_ref, ong_ref, o_ref, s_ref, *, hb, nc):
    c = A_CHUNK
    dh = A_HEAD_DIM

    @pl.when(pl.program_id(2) == 0)
    def _():
        s_ref[...] = jnp.zeros(s_ref.shape, F32)

    q_all = q_ref[...]
    k_all = k_ref[...]
    v_all = v_ref[...]
    ba = ba_ref[...]
    beta_t = _sigmoid(ba)
    xa = ba + dtb_ref[...]
    softplus = jnp.maximum(xa, 0.0) + jnp.log(1.0 + jnp.exp(-jnp.abs(xa)))
    g_t = -jnp.exp(alog_ref[...]) * softplus
    gc_t = [_cumsum_rows(g_t[ci * c:(ci + 1) * c, :]) for ci in range(nc)]
    gc_tt = [x.T for x in gc_t]
    egc_t = [jnp.exp(x) for x in gc_t]
    ekd_t = [jnp.exp(x[c - 1:c, :] - x) for x in gc_t]
    egl_t = [jnp.exp(x[c - 1:c, :]) for x in gc_t]

    row = lax.broadcasted_iota(jnp.int32, (c, c), 0)
    col = lax.broadcasted_iota(jnp.int32, (c, c), 1)
    incl = row >= col
    strict = row > col
    heads = range(hb)
    jobs = [(ci, i) for ci in range(nc) for i in heads]
    la = LANES // 2

    def head(x, ci, i):
        return x[ci * c:(ci + 1) * c, i * dh:(i + 1) * dh]

    def lane(xs, ci, i):
        return xs[ci][:, la + i:la + i + 1]

    qnb = [head(q_all, ci, i) for ci, i in jobs]
    knb = [head(k_all, ci, i) for ci, i in jobs]
    qn = [x.astype(F32) for x in qnb]
    kn = [x.astype(F32) for x in knb]
    beta = [beta_t[ci * c:(ci + 1) * c, i:i + 1] for ci, i in jobs]
    kb = [x * y for x, y in zip(kn, beta)]
    decay = [jnp.where(incl, jnp.exp(jnp.where(incl, lane(gc_t, ci, i) - gc_tt[ci][la + i:la + i + 1, :], 0.0)), 0.0)
             for ci, i in jobs]
    m = [jnp.where(strict, _dot_nt(x, y) * d, 0.0) for x, y, d in zip(kb, knb, decay)]
    attn = [(_dot_nt(x, y) * d).astype(BF16) for x, y, d in zip(qnb, knb, decay)]
    rhs = [jnp.concatenate([head(v_all, ci, i).astype(F32) * beta[j], kb[j] * lane(egc_t, ci, i)],
                           axis=1).astype(BF16) for j, (ci, i) in enumerate(jobs)]
    qdec = [(qn[j] * lane(egc_t, ci, i)).astype(BF16) for j, (ci, i) in enumerate(jobs)]
    kdec_t = [(kn[j] * lane(ekd_t, ci, i)).T.astype(BF16) for j, (ci, i) in enumerate(jobs)]
    tinv = _unit_lower_inverse(m)
    uw = [_dot(x, y) for x, y in zip(tinv, rhs)]

    s_cur = [s_ref[i] for i in heads]
    for ci in range(nc):
        sb = [x.astype(BF16) for x in s_cur]
        job = [ci * hb + i for i in heads]
        v_new = [uw[j][:, :dh] - _dot(uw[j][:, dh:], sb[i]) for i, j in zip(heads, job)]
        vnb = [x.astype(BF16) for x in v_new]
        s_cur = [s_cur[i] * lane(egl_t, ci, i) + jnp.dot(kdec_t[j], vnb[i], preferred_element_type=F32)
                 for i, j in zip(heads, job)]
        o = [jnp.dot(qdec[j], sb[i], preferred_element_type=F32)
             + jnp.dot(attn[j], vnb[i], preferred_element_type=F32) for i, j in zip(heads, job)]
        for i in heads:
            rs = slice(ci * c, (ci + 1) * c)
            sl = slice(i * dh, (i + 1) * dh)
            o_ref[rs, sl] = (_rms(o[i], ong_ref[...]) * _silu(z_ref[rs, sl].astype(F32))).astype(o_ref.dtype)
    for i in heads:
        s_ref[i] = s_cur[i]


def _gdn(q, k, v, z, ba, alog_row, dtb_row, onorm_g):
    bn, t, _ = q.shape
    hb = GDN_HEADS_PER_STEP
    nc = GDN_CHUNKS_PER_STEP
    nhg = A_HEADS // hb
    c = nc * A_CHUNK
    w = hb * A_HEAD_DIM
    kern = functools.partial(_gdn_kernel, hb=hb, nc=nc)
    wide = pl.BlockSpec((None, c, w), lambda b, hg, n: (b, n, hg))
    hg_row = pl.BlockSpec((None, 1, LANES), lambda b, hg, n: (hg, 0, 0))
    return pl.pallas_call(
        kern,
        grid=(bn, nhg, t // c),
        in_specs=[wide, wide, wide, wide,
                  pl.BlockSpec((None, c, LANES), lambda b, hg, n: (b, n, hg)),
                  hg_row, hg_row,
                  pl.BlockSpec((1, A_HEAD_DIM), lambda b, hg, n: (0, 0))],
        out_specs=wide,
        out_shape=jax.ShapeDtypeStruct((bn, t, A_WIDTH), BF16),
        scratch_shapes=[pltpu.VMEM((hb, A_HEAD_DIM, A_HEAD_DIM), F32)],
        compiler_params=_cparams("arbitrary", "arbitrary", "arbitrary"),
    )(q, k, v, z, ba, alog_row, dtb_row, onorm_g)


def _post_a_kernel(x_ref, og_ref, gate_ref, wo_ref, kvg_ref, wkvc_ref, wkvr_ref, g1_ref, scale_ref,
                   shift_ref, wq_ref, wz_ref, wg_ref,
                   x1_ref, kvc_ref, kvr_ref, q_ref, z_ref, gates_ref):
    out = _dot(og_ref[...], wo_ref[...])
    x1 = x_ref[...] + gate_ref[...] * out
    x1_ref[...] = x1
    sb = _rms(x1, kvg_ref[...]).astype(BF16)
    kvc_ref[...] = jnp.dot(sb, wkvc_ref[...], preferred_element_type=F32)
    kvr_ref[...] = jnp.dot(sb, wkvr_ref[...], preferred_element_type=F32).astype(BF16)
    h = _rms(x1, g1_ref[...]) * (1.0 + scale_ref[...]) + shift_ref[...]
    hb = h.astype(BF16)
    q = jnp.dot(hb, wq_ref[...], preferred_element_type=F32) * (B_HEAD_DIM ** -0.5)
    q_ref[...] = q.astype(BF16)
    z_ref[...] = jnp.dot(hb, wz_ref[...], preferred_element_type=F32).astype(BF16)
    gates_ref[...] = _sigmoid(jnp.dot(hb, wg_ref[...], preferred_element_type=F32))


def _post_a(x, og, gate0, wo, kvg, wkvc, wkvr, g1, scale1, shift1, wq, wz, wg):
    bn, t, d = x.shape
    tm = ROW_TILE
    row = lambda b, i: (b, i, 0)
    per_b = lambda b, i: (b, 0, 0)
    const = lambda b, i: (0, 0)
    full = lambda a: pl.BlockSpec(a.shape, const)
    vec = pl.BlockSpec((1, d), const)
    bvec = pl.BlockSpec((None, 1, d), per_b)
    outs = [(d, F32), (wkvc.shape[1], F32), (wkvr.shape[1], BF16), (wq.shape[1], BF16),
            (wz.shape[1], BF16), (wg.shape[1], F32)]
    return pl.pallas_call(
        _post_a_kernel,
        grid=(bn, t // tm),
        in_specs=[pl.BlockSpec((None, tm, d), row), pl.BlockSpec((None, tm, A_WIDTH), row), bvec,
                  full(wo), vec, full(wkvc), full(wkvr), vec, bvec, bvec, full(wq), full(wz), full(wg)],
        out_specs=[pl.BlockSpec((None, tm, n), row) for n, _ in outs],
        out_shape=[jax.ShapeDtypeStruct((bn, t, n), dt) for n, dt in outs],
        compiler_params=_cparams("arbitrary", "arbitrary"),
    )(x, og, gate0, wo, kvg, wkvc, wkvr, g1, scale1, shift1, wq, wz, wg)


def _compress_kernel(kc_ref, vc_ref, pos_ref, w1_ref, w2_ref, o_ref):
    dh = B_HEAD_DIM
    ncp = kc_ref.shape[0] // CMP_STRIDE
    njob = 2 * B_GROUPS
    first = [None] * njob
    second = [None] * njob
    for l in range(CMP_STRIDE):
        x = [r[pl.ds(l, ncp, stride=CMP_STRIDE), :] for r in (kc_ref, vc_ref)]
        for j in range(njob):
            kind, g = divmod(j, B_GROUPS)
            xj = x[kind][:, g * dh:(g + 1) * dh]
            for half, acc in ((0, first), (1, second)):
                row = half * CMP_STRIDE + l
                term = _dot(xj + pos_ref[kind, row:row + 1, :], w1_ref[kind, row * dh:(row + 1) * dh, :])
                acc[j] = term if acc[j] is None else acc[j] + term
    for j in range(njob):
        hid = first[j] + pltpu.roll(second[j], ncp - 1, axis=0)
        o_ref[j] = _dot(_silu(hid), w2_ref[j // B_GROUPS]).astype(o_ref.dtype)


def _compress(kvc, pos, w1, w2):
    bn, t, wd = kvc.shape
    njob = wd // B_HEAD_DIM
    ncp = t // CMP_STRIDE
    full = lambda a: pl.BlockSpec(a.shape, lambda b: (0,) * a.ndim)
    return pl.pallas_call(
        _compress_kernel,
        grid=(bn,),
        in_specs=[pl.BlockSpec((None, t, wd // 2), lambda b: (b, 0, 0)),
                  pl.BlockSpec((None, t, wd // 2), lambda b: (b, 0, 1)), full(pos), full(w1), full(w2)],
        out_specs=pl.BlockSpec((None, njob, ncp, B_HEAD_DIM), lambda b: (b, 0, 0, 0)),
        out_shape=jax.ShapeDtypeStruct((bn, njob, ncp, B_HEAD_DIM), BF16),
        compiler_params=_cparams("arbitrary"),
    )(kvc, kvc, pos, w1, w2)


def _t5_bucket_np(dist):
    n = np.maximum(dist, 0)
    max_exact = NUM_BUCKETS // 2
    nf = np.maximum(n, 1).astype(np.float64)
    val = np.log(nf / max_exact) / math.log(MAX_DISTANCE / max_exact) * (NUM_BUCKETS - max_exact)
    frac = np.abs(val - np.round(val))
    safe = (frac > 1e-6) | (n <= max_exact) | (n >= MAX_DISTANCE)
    assert bool(np.all(safe)), "bucket boundary too close to an integer distance"
    large = np.minimum(max_exact + np.floor(np.maximum(val, 0.0)).astype(np.int64), NUM_BUCKETS - 1)
    return np.where(n < max_exact, n, large)


def _bias_onehot():
    r = np.arange(NSA_Q_TILE)[:, None]
    tiles = []
    j = np.arange(WINDOW + NSA_Q_TILE)[None, :]
    d = r + WINDOW - j
    tiles.append((d, (d >= 0) & (d < WINDOW)))
    j = np.arange((NSA_SUB + 2) * L_SLC)[None, :]
    d = r + 2 * L_SLC - j
    tiles.append((d, d >= 0))
    j = np.arange(CMP_NEAR)[None, :]
    d = r - CMP_STRIDE * (j - CMP_LEAD) - (L_CMP - 1)
    tiles.append((d, d >= 0))
    cols = [np.where(valid, _t5_bucket_np(d), NUM_BUCKETS).reshape(-1) for d, valid in tiles]
    widths = [c.size for c in cols]
    return np.concatenate(cols).astype(np.int32)[None, :], widths


def _bias_kernel(rb_ref, bk_ref, o_ref):
    rb = rb_ref[...]
    lane = lax.broadcasted_iota(jnp.int32, rb.shape, 1)
    rbs = rb - rb[:, NUM_BUCKETS - 1:NUM_BUCKETS]
    rbs = jnp.where(lane < NUM_BUCKETS, rbs, jnp.where(lane == NUM_BUCKETS, NEG_INF, 0.0))
    bk = bk_ref[...]
    onehot = jnp.where(lax.broadcasted_iota(jnp.int32, (2 * NUM_BUCKETS, bk.shape[1]), 0) == bk, 1.0, 0.0)
    o_ref[...] = _dot_f32(rbs, onehot)


def _bias_tables(rel_bias):
    bk, widths = _bias_onehot()
    ncol = bk.shape[1]
    nt = 8
    assert ncol % (nt * LANES) == 0
    tc = ncol // nt
    rb = jnp.concatenate([rel_bias.T, jnp.zeros((B_HEADS, NUM_BUCKETS), F32)], axis=1)
    flat = pl.pallas_call(
        _bias_kernel,
        grid=(nt,),
        in_specs=[pl.BlockSpec((B_HEADS, 2 * NUM_BUCKETS), lambda i: (0, 0)),
                  pl.BlockSpec((1, tc), lambda i: (0, i))],
        out_specs=pl.BlockSpec((B_HEADS, tc), lambda i: (0, i)),
        out_shape=jax.ShapeDtypeStruct((B_HEADS, ncol), F32),
        compiler_params=_cparams("arbitrary"),
    )(rb, jnp.asarray(bk))
    out, start = [], 0
    for wd in widths:
        tile = flat[:, start:start + wd].reshape(B_GROUPS, B_HPG * NSA_Q_TILE, wd // NSA_Q_TILE)
        out.append(tile)
        start += wd
    return out


def _nsa_kernel(q_ref, kc_ref, vc_ref, slc_ref, win_ref, kst_ref, kwt_ref, vt_ref, tcmp_ref, tsel_ref, twin_ref,
                ov_ref, zc_ref, zs_ref, zw_ref, gates_ref, ex_ref, y_ref,
                sa_ref, sb_ref, ks_ref, vs_ref, kw_ref, vw_ref):
    tq = NSA_Q_TILE
    dh = B_HEAD_DIM
    hpg = B_HPG
    rows = hpg * tq
    ti = pl.program_id(2)

    @pl.when(ti == 0)
    def _():
        ks_ref[...] = kst_ref[...]
        kw_ref[...] = kwt_ref[...]
        vs_ref[...] = vt_ref[...]
        vw_ref[...] = vt_ref[...]
        slc = slc_ref[...]
        win = win_ref[...]
        ks_ref[KV_PAD:, 0:dh] = slc[:, 0:dh]
        vs_ref[KV_PAD:, 0:dh] = slc[:, dh:]
        kw_ref[KV_PAD:, 0:dh] = win[:, 0:dh]
        vw_ref[KV_PAD:, 0:dh] = win[:, dh:]

    q0 = ti * tq
    blk0 = ti * NSA_SUB
    qt = q_ref[...]
    q = jnp.concatenate([qt[:, h * dh:(h + 1) * dh] for h in range(hpg)], axis=0)

    def to_tokens(o):
        return jnp.concatenate([o[h * tq:(h + 1) * tq, :] for h in range(hpg)], axis=1)

    def finish(pv):
        return to_tokens(pv[:, :dh] * (1.0 / pv[:, dh:dh + 1]))

    gt = gates_ref[...]
    g_hi = gt.astype(BF16)
    ghl = jnp.concatenate([g_hi, (gt - g_hi.astype(F32)).astype(BF16)], axis=1)

    def gated(o_tok, br, z_ref):
        gexp = jnp.dot(ghl, ex_ref[br], preferred_element_type=F32)
        return gexp * o_tok * _silu(z_ref[...].astype(F32))

    kc = kc_ref[...]
    ncp = kc.shape[0]
    nw = WINDOW + tq
    win0 = pl.multiple_of(q0, tq)
    first_near = (tq // CMP_STRIDE) * ti - CMP_LEAD
    cid = lax.broadcasted_iota(jnp.int32, (2 * CMP_NEAR, ncp), 1)
    jrow = lax.broadcasted_iota(jnp.int32, (2 * CMP_NEAR, ncp), 0) & (CMP_NEAR - 1)
    shift_eye = jnp.where(cid - first_near == jrow, 1.0, 0.0).astype(BF16)
    pad_col = jnp.where(lax.broadcasted_iota(jnp.int32, (rows, dh), 1) == 0, NEG_INF, 0.0).astype(BF16)
    q_win = jnp.concatenate([q, pad_col], axis=1)

    s = _dot_nt(q, kc) + jnp.dot(tcmp_ref[...], shift_eye, preferred_element_type=F32)
    s_w = _dot_nt(q_win, kw_ref[pl.ds(win0, nw), :]) + twin_ref[...]

    cvis = lax.broadcasted_iota(jnp.int32, (1, ncp), 1) < first_near + CMP_NEAR
    s = jnp.where(cvis, s, NEG_INF)
    live = s > 0.1 * NEG_INF
    m = jnp.max(s, axis=-1, keepdims=True)
    e = jnp.where(live, jnp.exp(s - m), 0.0)
    p = e * (1.0 / jnp.maximum(jnp.sum(e, axis=-1, keepdims=True), 1e-30))
    y_c = gated(to_tokens(_dot(p, vc_ref[...])), 0, zc_ref)

    psum = p[0:tq, :]
    for h in range(1, hpg):
        psum = psum + p[h * tq:(h + 1) * tq, :]
    p_hi = psum.astype(BF16)
    p_r1 = psum - p_hi.astype(F32)
    p_mid = p_r1.astype(BF16)
    p_lo = (p_r1 - p_mid.astype(F32)).astype(BF16)
    p3 = jnp.concatenate([p_hi, p_mid, p_lo], axis=1)
    imp_t = _dot_nt(ov_ref[...], p3)

    m_w = jnp.max(s_w, axis=-1, keepdims=True)
    e_w = jnp.exp(s_w - m_w)
    y_cw = y_c + gated(finish(_dot(e_w, vw_ref[pl.ds(win0, nw), :])), 2, zw_ref)

    nblk = imp_t.shape[0]
    blk = lax.broadcasted_iota(jnp.int32, (nblk, tq), 0)
    cur = blk0 + (lax.broadcasted_iota(jnp.int32, (nblk, tq), 1) >> SLC_SHIFT)
    forced = (blk == 0) | (blk == cur) | (blk == cur - 1)
    val = jnp.where(forced, SEL_BOOST, jnp.where(blk > cur, -SEL_BOOST, imp_t))
    nslab = nblk // SUBLANES
    slabs = [val[SUBLANES * r:SUBLANES * (r + 1), :] for r in range(nslab)]
    sub = lax.broadcasted_iota(jnp.int32, (SUBLANES, tq), 0)
    n_acc = 4
    ranks = [[jnp.zeros((SUBLANES, tq), jnp.int32) for _ in range(n_acc)] for _ in range(nslab)]
    for j in range(nblk):
        vj = jnp.broadcast_to(val[j:j + 1, :], (SUBLANES, tq))
        for r in range(nslab):
            lo = SUBLANES * r
            if lo > j:
                ahead = vj >= slabs[r]
            elif lo + SUBLANES - 1 <= j:
                ahead = vj > slabs[r]
            else:
                ahead = (vj > slabs[r]) | ((vj == slabs[r]) & (sub > j - lo))
            ranks[r][j % n_acc] = ranks[r][j % n_acc] + ahead.astype(jnp.int32)
    rank = jnp.concatenate([(a[0] + a[1]) + (a[2] + a[3]) for a in ranks], axis=0)
    sel_t = (rank < N_SEL) & (blk <= cur)
    far_t = jnp.where(sel_t & (blk <= blk0 - 3), 0.0, NEG_INF)
    near_t = jnp.where(sel_t & (blk >= blk0 - 2), 0.0, NEG_INF)

    def q_with_mask(mask_t):
        mk = mask_t.T.astype(BF16)
        return jnp.concatenate([q, jnp.concatenate([mk] * hpg, axis=0)], axis=1)

    q_far = q_with_mask(far_t)
    q_near = q_with_mask(near_t)

    kt_sz = SEL_KEY_TILE
    n_far_keys = jnp.maximum(blk0 - 2, 0) * L_SLC
    n_pairs = (n_far_keys + 2 * kt_sz - 1) // (2 * kt_sz)

    def far_scores(tile):
        start = pl.multiple_of(KV_PAD + tile * kt_sz, kt_sz)
        return _dot_nt(q_far, ks_ref[pl.ds(start, kt_sz), :])

    def far_values(tile):
        start = pl.multiple_of(KV_PAD + tile * kt_sz, kt_sz)
        return vs_ref[pl.ds(start, kt_sz), :]

    def update(carry, s_t, v_t):
        m_i, acc = carry
        m_n = jnp.maximum(m_i, jnp.max(s_t, axis=-1, keepdims=True))
        e_t = jnp.exp(s_t - m_n)
        return m_n, jnp.exp(m_i - m_n) * acc + _dot(e_t, v_t)

    def pair_step(j, carry):
        sb_ref[...] = far_scores(2 * j + 1)
        carry = update(carry, sa_ref[...], far_values(2 * j))
        sa_ref[...] = far_scores(2 * j + 2)
        return update(carry, sb_ref[...], far_values(2 * j + 1))

    sa_ref[...] = far_scores(0)
    last = jnp.maximum(n_pairs, 1) - 1
    carry = (jnp.full((rows, 1), NEG_INF, F32), jnp.zeros((rows, 2 * dh), F32))
    carry = lax.fori_loop(0, last, pair_step, carry)
    sb_ref[...] = far_scores(2 * last + 1)
    carry = update(carry, sa_ref[...], far_values(2 * last))
    nk = (NSA_SUB + 2) * L_SLC
    near0 = pl.multiple_of(KV_PAD + q0 - 2 * L_SLC, L_SLC)
    s_n = _dot_nt(q_near, ks_ref[pl.ds(near0, nk), :]) + tsel_ref[...]
    carry = update(carry, sb_ref[...], far_values(2 * last + 1))
    _, acc = update(carry, s_n, vs_ref[pl.ds(near0, nk), :])
    y_ref[...] = (y_cw + gated(finish(acc), 1, zs_ref)).astype(y_ref.dtype)


def _gate_selectors(ng):
    ex = np.zeros((B_GROUPS, N_BRANCH, 2 * ng, B_HPG * B_HEAD_DIM), np.float32)
    for g in range(B_GROUPS):
        for br in range(N_BRANCH):
            for h in range(B_HPG):
                lane = br * B_HEADS + g * B_HPG + h
                ex[g, br, lane, h * B_HEAD_DIM:(h + 1) * B_HEAD_DIM] = 1.0
                ex[g, br, ng + lane, h * B_HEAD_DIM:(h + 1) * B_HEAD_DIM] = 1.0
    return ex


def _kv_templates(t, nblk):
    tp = KV_PAD + t
    ks_t = np.zeros((tp, 2 * B_HEAD_DIM), np.float32)
    ks_t[:, B_HEAD_DIM:] = _block_onehot(t, nblk)
    kw_t = np.zeros((tp, 2 * B_HEAD_DIM), np.float32)
    kw_t[:KV_PAD, B_HEAD_DIM] = 1.0
    v_t = np.zeros((tp, 2 * B_HEAD_DIM), np.float32)
    v_t[:, B_HEAD_DIM] = 1.0
    return [jnp.asarray(a, BF16) for a in (ks_t, kw_t, v_t)]


def _nsa(q, kcv, kvr, tcmp, tsel, twin, ov, z, gates):
    bn, t, _ = q.shape
    tq = NSA_Q_TILE
    gw = B_HPG * B_HEAD_DIM
    ncp = kcv.shape[2]
    tp = KV_PAD + t
    kvw = 2 * B_HEAD_DIM
    nblk = ov.shape[0]
    ng = gates.shape[2]
    rows = B_HPG * tq
    per_g = lambda b, g, i: (g, 0, 0)
    tmpl = pl.BlockSpec((tp, kvw), lambda b, g, i: (0, 0))
    ex = jnp.asarray(_gate_selectors(ng), BF16)
    kst, kwt, vt = _kv_templates(t, nblk)

    def z_spec(br):
        return pl.BlockSpec((None, tq, gw), lambda b, g, i: (b, i, br * B_GROUPS + g))

    return pl.pallas_call(
        _nsa_kernel,
        grid=(bn, B_GROUPS, t // tq),
        in_specs=[pl.BlockSpec((None, tq, gw), lambda b, g, i: (b, i, g)),
                  pl.BlockSpec((None, None, ncp, B_HEAD_DIM), lambda b, g, i: (b, g, 0, 0)),
                  pl.BlockSpec((None, None, ncp, B_HEAD_DIM), lambda b, g, i: (b, B_GROUPS + g, 0, 0)),
                  pl.BlockSpec((None, t, kvw), lambda b, g, i: (b, 0, g)),
                  pl.BlockSpec((None, t, kvw), lambda b, g, i: (b, 0, B_GROUPS + g)),
                  tmpl, tmpl, tmpl,
                  pl.BlockSpec((None, rows, tcmp.shape[2]), per_g),
                  pl.BlockSpec((None, rows, tsel.shape[2]), per_g),
                  pl.BlockSpec((None, rows, twin.shape[2]), per_g),
                  pl.BlockSpec(ov.shape, lambda b, g, i: (0, 0)),
                  z_spec(0), z_spec(1), z_spec(2),
                  pl.BlockSpec((None, tq, ng), lambda b, g, i: (b, i, 0)),
                  pl.BlockSpec((None,) + ex.shape[1:], lambda b, g, i: (g, 0, 0, 0))],
        out_specs=pl.BlockSpec((None, tq, gw), lambda b, g, i: (b, i, g)),
        out_shape=jax.ShapeDtypeStruct((bn, t, B_WIDTH), BF16),
        scratch_shapes=[pltpu.VMEM((rows, SEL_KEY_TILE), F32), pltpu.VMEM((rows, SEL_KEY_TILE), F32)]
        + [pltpu.VMEM((tp, kvw), BF16)] * 4,
        compiler_params=_cparams("arbitrary", "arbitrary", "arbitrary"),
    )(q, kcv, kcv, kvr, kvr, kst, kwt, vt, tcmp, tsel, twin, ov, z, z, z, gates, ex)


def _final_kernel(y_ref, x1_ref, gate_ref, wo_ref, fg_ref, o_ref):
    x2 = x1_ref[...] + gate_ref[...] * jnp.dot(y_ref[...], wo_ref[...], preferred_element_type=F32)
    o_ref[...] = _rms(x2, fg_ref[...])


def _final(y, x1, gate1, wo, fg):
    bn, t, d = x1.shape
    tm = ROW_TILE
    row = lambda b, i: (b, i, 0)
    return pl.pallas_call(
        _final_kernel,
        grid=(bn, t // tm),
        in_specs=[pl.BlockSpec((None, tm, B_WIDTH), row),
                  pl.BlockSpec((None, tm, d), row),
                  pl.BlockSpec((None, 1, d), lambda b, i: (b, 0, 0)),
                  pl.BlockSpec(wo.shape, lambda b, i: (0, 0)),
                  pl.BlockSpec((1, d), lambda b, i: (0, 0))],
        out_specs=pl.BlockSpec((None, tm, d), row),
        out_shape=jax.ShapeDtypeStruct((bn, t, d), F32),
        compiler_params=_cparams("arbitrary", "arbitrary"),
    )(y, x1, gate1, wo, fg)


def _overlap_matrix(ncp, n_cmp, n_slc, nblk):
    cells = np.arange(n_cmp)[:, None] + np.arange(L_CMP // CMP_STRIDE)[None, :]
    ov = (cells[:, None, :] // (L_SLC // CMP_STRIDE) == np.arange(n_slc)[None, :, None]).sum(-1)
    out = np.zeros((ncp, nblk), np.float32)
    out[:n_cmp, :n_slc] = ov
    return out


def _block_onehot(t, nblk):
    oh = np.zeros((KV_PAD + t, nblk), np.float32)
    oh[KV_PAD + np.arange(t), np.arange(t) // L_SLC] = 1.0
    oh[:KV_PAD, nblk - 1] = 1.0
    return oh


def kernel(x, c, rel_bias, ada_w, ada_b, norm_g, a_in_w, a_conv_w, a_A_log, a_dt_bias, a_onorm_g, a_out_w,
           kv_norm_g, kv_w, cmp_pos_k, cmp_pos_v, cmp_k_w1, cmp_k_w2, cmp_v_w1, cmp_v_w2,
           b_in_w, b_out_w, final_g):
    bn, t, d = x.shape
    assert ada_w.shape[0] == 2 and a_in_w.shape[0] == 1 and b_in_w.shape[0] == 1
    assert t % max(ROW_TILE, 2 * SEL_KEY_TILE, NSA_Q_TILE) == 0
    n_slc = t // L_SLC
    nblk = 64
    assert n_slc <= nblk
    n_cmp = (t - L_CMP) // CMP_STRIDE + 1
    ncp = t // CMP_STRIDE

    mod = _ada_modulation(c, ada_w, ada_b)
    shift = mod[:, :, None, :d]
    scale = mod[:, :, None, d:2 * d]
    gate = mod[:, :, None, 2 * d:]

    hb = GDN_HEADS_PER_STEP
    nhg = A_HEADS // hb
    w_in = a_in_w[0]
    wqkv = w_in[:, :3 * A_WIDTH].astype(BF16)
    wz = w_in[:, 3 * A_WIDTH:4 * A_WIDTH].astype(BF16)
    wb = w_in[:, 4 * A_WIDTH:4 * A_WIDTH + A_HEADS]
    wa = w_in[:, 4 * A_WIDTH + A_HEADS:]
    half = LANES // 2
    wba = jnp.zeros((d, nhg, LANES), F32)
    wba = wba.at[:, :, :hb].set(wb.reshape(d, nhg, hb)).at[:, :, half:half + hb].set(wa.reshape(d, nhg, hb))
    wba = wba.reshape(d, nhg * LANES).astype(BF16)
    lane_rows = lambda v: jnp.zeros((nhg, 1, LANES), F32).at[:, 0, half:half + hb].set(v.reshape(nhg, hb))
    q_a, k_a, v_a, z_a, ba = _in_proj_a(x, norm_g[0:1], scale[0], shift[0], wqkv, wz, wba, a_conv_w[0])
    og = _gdn(q_a, k_a, v_a, z_a, ba, lane_rows(a_A_log[0]), lane_rows(a_dt_bias[0]), a_onorm_g[0:1])

    ndh = B_GROUPS * B_HEAD_DIM
    wkvc = kv_w[:, :2 * ndh].astype(BF16)
    kv_rest = kv_w[:, 2 * ndh:].reshape(d, 4, B_GROUPS, B_HEAD_DIM)
    wkvr = jnp.concatenate([jnp.stack([kv_rest[:, 0], kv_rest[:, 1]], axis=2),
                            jnp.stack([kv_rest[:, 2], kv_rest[:, 3]], axis=2)], axis=1)
    wkvr = wkvr.reshape(d, 4 * ndh).astype(BF16)
    w_b = b_in_w[0]
    wq = w_b[:, :B_WIDTH].astype(BF16)
    wzb = w_b[:, B_WIDTH:4 * B_WIDTH].astype(BF16)
    wg = jnp.zeros((d, LANES), F32).at[:, :N_BRANCH * B_HEADS].set(w_b[:, 4 * B_WIDTH:]).astype(BF16)
    x1, kvc, kvr, q, z_b, gates = _post_a(x, og, gate[0], a_out_w[0].astype(BF16), kv_norm_g[None, :], wkvc, wkvr,
                                          norm_g[1:2], scale[1], shift[1], wq, wzb, wg)

    pos = jnp.stack([cmp_pos_k, cmp_pos_v])
    w1 = jnp.stack([cmp_k_w1, cmp_v_w1]).astype(BF16)
    w2 = jnp.stack([cmp_k_w2, cmp_v_w2]).astype(BF16)
    kcv = _compress(kvc, pos, w1, w2)

    twin, tsel, tcmp = _bias_tables(rel_bias)
    tc_hi = tcmp.astype(BF16)
    tc_lo = (tcmp - tc_hi.astype(F32)).astype(BF16)
    tcmp2 = jnp.concatenate([tc_hi, tc_lo], axis=-1)
    ov_t = _overlap_matrix(ncp, n_cmp, n_slc, nblk).T
    ov3 = jnp.asarray(np.concatenate([ov_t] * 3, axis=1), BF16)

    y = _nsa(q, kcv, kvr, tcmp2, tsel, twin, ov3, z_b, gates)

    return _final(y, x1, gate[1], b_out_w[0].astype(BF16), final_g[None, :])
```

```python
import functools
import math

import numpy as np
import jax
import jax.numpy as jnp
from jax import lax
from jax.experimental import pallas as pl
from jax.experimental.pallas import tpu as pltpu

F32 = jnp.float32
BF16 = jnp.bfloat16
HIGHEST = lax.Precision.HIGHEST

A_HEADS = 8
A_HEAD_DIM = 128
A_WIDTH = A_HEADS * A_HEAD_DIM
A_CONV = 4
A_CHUNK = 64
B_HEADS = 16
B_GROUPS = 2
B_HPG = B_HEADS // B_GROUPS
B_HEAD_DIM = 64
B_WIDTH = B_HEADS * B_HEAD_DIM
N_BRANCH = 3
L_CMP = 32
CMP_STRIDE = 16
L_SLC = 64
N_SEL = 16
WINDOW = 512
Q_BLOCK = 64
NUM_BUCKETS = 32
MAX_DISTANCE = 128
EPS = 1e-6
NEG_INF = -1e30
SEL_BOOST = 1e9
LOG2E = math.log2(math.e)
NSA_Q_SCALE = B_HEAD_DIM ** -0.5 * LOG2E

LANES = 128
SUBLANES = 8
VMEM_LIMIT_BYTES = 56 * 1024 * 1024

ROW_TILE = 256
GDN_HEADS_PER_STEP = 8
GDN_CHUNKS_PER_STEP = 4
SEL_KEY_TILE = 512
KV_PAD = WINDOW
NSA_Q_TILE = 128
NSA_SUB = NSA_Q_TILE // Q_BLOCK
SLC_SHIFT = L_SLC.bit_length() - 1
assert 1 << SLC_SHIFT == L_SLC and L_SLC == Q_BLOCK
CMP_LEAD = 12
CMP_NEAR = 32
assert CMP_NEAR >= CMP_LEAD + NSA_Q_TILE // CMP_STRIDE and NSA_Q_TILE % Q_BLOCK == 0
BLK16 = 16


def _cparams(*sem):
    return pltpu.CompilerParams(dimension_semantics=sem, vmem_limit_bytes=VMEM_LIMIT_BYTES)


def _sigmoid(x):
    return 1.0 / (1.0 + jnp.exp(-x))


def _silu(x):
    return x * _sigmoid(x)


def _dot(a, b):
    return jnp.dot(a.astype(BF16), b.astype(BF16), preferred_element_type=F32)


def _dot_nt(a, b):
    return lax.dot_general(a.astype(BF16), b.astype(BF16), (((1,), (1,)), ((), ())),
                           preferred_element_type=F32)


def _dot_f32(a, b):
    return jnp.dot(a, b, precision=HIGHEST, preferred_element_type=F32)


def _rms(x, g):
    ms = jnp.mean(x * x, axis=-1, keepdims=True)
    return x * lax.rsqrt(ms + EPS) * g


def _ada_kernel(c_ref, w_ref, b_ref, o_ref):
    o_ref[...] = _dot_f32(_silu(c_ref[...]), w_ref[...]) + b_ref[...]


def _ada_modulation(c, ada_w, ada_b):
    depth, d, d3 = ada_w.shape
    bn = c.shape[0]
    return pl.pallas_call(
        _ada_kernel,
        grid=(depth, d3 // d),
        in_specs=[pl.BlockSpec((bn, d), lambda l, j: (0, 0)),
                  pl.BlockSpec((None, d, d), lambda l, j: (l, 0, j)),
                  pl.BlockSpec((None, 1, d), lambda l, j: (l, 0, j))],
        out_specs=pl.BlockSpec((None, bn, d), lambda l, j: (l, 0, j)),
        out_shape=jax.ShapeDtypeStruct((depth, bn, d3), F32),
        compiler_params=_cparams("arbitrary", "arbitrary"),
    )(c, ada_w, ada_b.reshape(depth, 1, d3))


def _in_proj_a_kernel(x_ref, g_ref, scale_ref, shift_ref, wqkv_ref, wz_ref, wba_ref, cw_ref,
                      q_ref, k_ref, v_ref, z_ref, ba_ref, buf_ref):
    tm = x_ref.shape[0]
    halo = SUBLANES
    dh = A_HEAD_DIM

    @pl.when(pl.program_id(1) == 0)
    def _():
        buf_ref[0:halo, :] = jnp.zeros((halo, 3 * A_WIDTH), F32)

    h = _rms(x_ref[...], g_ref[...]) * (1.0 + scale_ref[...]) + shift_ref[...]
    hb = h.astype(BF16)
    buf_ref[halo:halo + tm, :] = jnp.dot(hb, wqkv_ref[...], preferred_element_type=F32)
    cw = cw_ref[...]
    xp = buf_ref[...]
    y = xp[halo:, :] * cw[A_CONV - 1:A_CONV, :]
    for kk in range(A_CONV - 1):
        y = y + pltpu.roll(xp, A_CONV - 1 - kk, axis=0)[halo:, :] * cw[kk:kk + 1, :]
    buf_ref[0:halo, :] = xp[tm:, :]
    y = _silu(y)
    for i in range(A_HEADS):
        for which, o_ref, gain in ((0, q_ref, dh ** -0.5), (1, k_ref, 1.0)):
            xh = y[:, which * A_WIDTH + i * dh:which * A_WIDTH + (i + 1) * dh]
            inv = lax.rsqrt(jnp.sum(xh * xh, axis=-1, keepdims=True) + EPS) * gain
            o_ref[:, i * dh:(i + 1) * dh] = (xh * inv).astype(BF16)
    v_ref[...] = y[:, 2 * A_WIDTH:].astype(BF16)
    z_ref[...] = jnp.dot(hb, wz_ref[...], preferred_element_type=F32).astype(BF16)
    ba_ref[...] = jnp.dot(hb, wba_ref[...], preferred_element_type=F32)


def _in_proj_a(x, g, scale, shift, wqkv, wz, wba, conv_w):
    bn, t, d = x.shape
    tm = ROW_TILE
    row = lambda b, i: (b, i, 0)
    per_b = lambda b, i: (b, 0, 0)
    const = lambda b, i: (0, 0)
    nba = wba.shape[1]
    wide = pl.BlockSpec((None, tm, A_WIDTH), row)
    wide_sd = jax.ShapeDtypeStruct((bn, t, A_WIDTH), BF16)
    return pl.pallas_call(
        _in_proj_a_kernel,
        grid=(bn, t // tm),
        in_specs=[pl.BlockSpec((None, tm, d), row),
                  pl.BlockSpec((1, d), const),
                  pl.BlockSpec((None, 1, d), per_b),
                  pl.BlockSpec((None, 1, d), per_b),
                  pl.BlockSpec(wqkv.shape, const),
                  pl.BlockSpec(wz.shape, const),
                  pl.BlockSpec(wba.shape, const),
                  pl.BlockSpec(conv_w.shape, const)],
        out_specs=[wide, wide, wide, wide, pl.BlockSpec((None, tm, nba), row)],
        out_shape=[wide_sd, wide_sd, wide_sd, wide_sd, jax.ShapeDtypeStruct((bn, t, nba), F32)],
        scratch_shapes=[pltpu.VMEM((SUBLANES + tm, 3 * A_WIDTH), F32)],
        compiler_params=_cparams("arbitrary", "arbitrary"),
    )(x, g, scale, shift, wqkv, wz, wba, conv_w)


def _cumsum_rows(x):
    n = x.shape[0]
    row = lax.broadcasted_iota(jnp.int32, x.shape, 0)
    s = 1
    while s < n:
        x = x + jnp.where(row >= s, pltpu.roll(x, s, axis=0), 0.0)
        s *= 2
    return x


def _unit_lower_inverse(ms):
    c = ms[0].shape[0]
    row = lax.broadcasted_iota(jnp.int32, (c, c), 0)
    col = lax.broadcasted_iota(jnp.int32, (c, c), 1)
    eye = (row == col).astype(F32)
    same_blk = (row & -BLK16) == (col & -BLK16)
    d = [jnp.where(same_blk, m, 0.0) for m in ms]
    mo = [m - x for m, x in zip(ms, d)]
    d2 = [_dot(x, x) for x in d]
    td = [eye - x for x in d]
    d4 = [_dot(x, x) for x in d2]
    td = [t + _dot(t, x) for t, x in zip(td, d2)]
    d8 = [_dot(x, x) for x in d4]
    td = [t + _dot(t, x) for t, x in zip(td, d4)]
    td = [t + _dot(t, x) for t, x in zip(td, d8)]
    n = [_dot(t, x) for t, x in zip(td, mo)]
    n2 = [_dot(x, x) for x in n]
    r = [eye - x for x in n]
    r = [a + _dot(a, x) for a, x in zip(r, n2)]
    return [_dot(a, t) for a, t in zip(r, td)]


def _gdn_kernel(q_ref, k_ref, v_ref, z_ref, ba_ref, alog_ref, dtb_ref, ong_ref, o_ref, s_ref, *, hb, nc):
    c = A_CHUNK
    dh = A_HEAD_DIM

    @pl.when(pl.program_id(2) == 0)
    def _():
        s_ref[...] = jnp.zeros(s_ref.shape, F32)

    q_all = q_ref[...]
    k_all = k_ref[...]
    v_all = v_ref[...]
    ba = ba_ref[...]
    beta_t = _sigmoid(ba)
    xa = ba + dtb_ref[...]
    softplus = jnp.maximum(xa, 0.0) + jnp.log(1.0 + jnp.exp(-jnp.abs(xa)))
    g_t = -jnp.exp(alog_ref[...]) * softplus
    gc_t = [_cumsum_rows(g_t[ci * c:(ci + 1) * c, :]) for ci in range(nc)]
    gc_tt = [x.T for x in gc_t]
    egc_t = [jnp.exp(x) for x in gc_t]
    ekd_t = [jnp.exp(x[c - 1:c, :] - x) for x in gc_t]
    egl_t = [jnp.exp(x[c - 1:c, :]) for x in gc_t]

    row = lax.broadcasted_iota(jnp.int32, (c, c), 0)
    col = lax.broadcasted_iota(jnp.int32, (c, c), 1)
    incl = row >= col
    strict = row > col
    heads = range(hb)
    jobs = [(ci, i) for ci in range(nc) for i in heads]
    la = LANES // 2

    def head(x, ci, i):
        return x[ci * c:(ci + 1) * c, i * dh:(i + 1) * dh]

    def lane(xs, ci, i):
        return xs[ci][:, la + i:la + i + 1]

    qnb = [head(q_all, ci, i) for ci, i in jobs]
    knb = [head(k_all, ci, i) for ci, i in jobs]
    qn = [x.astype(F32) for x in qnb]
    kn = [x.astype(F32) for x in knb]
    beta = [beta_t[ci * c:(ci + 1) * c, i:i + 1] for ci, i in jobs]
    kb = [x * y for x, y in zip(kn, beta)]
    decay = [jnp.where(incl, jnp.exp(jnp.where(incl, lane(gc_t, ci, i) - gc_tt[ci][la + i:la + i + 1, :], 0.0)), 0.0)
             for ci, i in jobs]
    m = [jnp.where(strict, _dot_nt(x, y) * d, 0.0) for x, y, d in zip(kb, knb, decay)]
    attn = [(_dot_nt(x, y) * d).astype(BF16) for x, y, d in zip(qnb, knb, decay)]
    rhs = [jnp.concatenate([head(v_all, ci, i).astype(F32) * beta[j], kb[j] * lane(egc_t, ci, i)],
                           axis=1).astype(BF16) for j, (ci, i) in enumerate(jobs)]
    qdec = [(qn[j] * lane(egc_t, ci, i)).astype(BF16) for j, (ci, i) in enumerate(jobs)]
    kdec_t = [(kn[j] * lane(ekd_t, ci, i)).T.astype(BF16) for j, (ci, i) in enumerate(jobs)]
    tinv = _unit_lower_inverse(m)
    uw = [_dot(x, y) for x, y in zip(tinv, rhs)]

    s_cur = [s_ref[i] for i in heads]
    for ci in range(nc):
        sb = [x.astype(BF16) for x in s_cur]
        job = [ci * hb + i for i in heads]
        v_new = [uw[j][:, :dh] - _dot(uw[j][:, dh:], sb[i]) for i, j in zip(heads, job)]
        vnb = [x.astype(BF16) for x in v_new]
        s_cur = [s_cur[i] * lane(egl_t, ci, i) + jnp.dot(kdec_t[j], vnb[i], preferred_element_type=F32)
                 for i, j in zip(heads, job)]
        o = [jnp.dot(qdec[j], sb[i], preferred_element_type=F32)
             + jnp.dot(attn[j], vnb[i], preferred_element_type=F32) for i, j in zip(heads, job)]
        for i in heads:
            rs = slice(ci * c, (ci + 1) * c)
            sl = slice(i * dh, (i + 1) * dh)
            o_ref[rs, sl] = (_rms(o[i], ong_ref[...]) * _silu(z_ref[rs, sl].astype(F32))).astype(o_ref.dtype)
    for i in heads:
        s_ref[i] = s_cur[i]


def _gdn(q, k, v, z, ba, alog_row, dtb_row, onorm_g):
    bn, t, _ = q.shape
    hb = GDN_HEADS_PER_STEP
    nc = GDN_CHUNKS_PER_STEP
    nhg = A_HEADS // hb
    c = nc * A_CHUNK
    w = hb * A_HEAD_DIM
    kern = functools.partial(_gdn_kernel, hb=hb, nc=nc)
    wide = pl.BlockSpec((None, c, w), lambda b, hg, n: (b, n, hg))
    hg_row = pl.BlockSpec((None, 1, LANES), lambda b, hg, n: (hg, 0, 0))
    return pl.pallas_call(
        kern,
        grid=(bn, nhg, t // c),
        in_specs=[wide, wide, wide, wide,
                  pl.BlockSpec((None, c, LANES), lambda b, hg, n: (b, n, hg)),
                  hg_row, hg_row,
                  pl.BlockSpec((1, A_HEAD_DIM), lambda b, hg, n: (0, 0))],
        out_specs=wide,
        out_shape=jax.ShapeDtypeStruct((bn, t, A_WIDTH), BF16),
        scratch_shapes=[pltpu.VMEM((hb, A_HEAD_DIM, A_HEAD_DIM), F32)],
        compiler_params=_cparams("arbitrary", "arbitrary", "arbitrary"),
    )(q, k, v, z, ba, alog_row, dtb_row, onorm_g)


def _post_a_kernel(x_ref, og_ref, gate_ref, wo_ref, kvg_ref, wkvc_ref, wkvr_ref, g1_ref, scale_ref,
                   shift_ref, wq_ref, wz_ref, wg_ref,
                   x1_ref, kvc_ref, kvr_ref, q_ref, z_ref, gates_ref):
    out = _dot(og_ref[...], wo_ref[...])
    x1 = x_ref[...] + gate_ref[...] * out
    x1_ref[...] = x1
    sb = _rms(x1, kvg_ref[...]).astype(BF16)
    kvc_ref[...] = jnp.dot(sb, wkvc_ref[...], preferred_element_type=F32)
    kvr_ref[...] = jnp.dot(sb, wkvr_ref[...], preferred_element_type=F32).astype(BF16)
    h = _rms(x1, g1_ref[...]) * (1.0 + scale_ref[...]) + shift_ref[...]
    hb = h.astype(BF16)
    q = jnp.dot(hb, wq_ref[...], preferred_element_type=F32) * NSA_Q_SCALE
    q_ref[...] = q.astype(BF16)
    z_ref[...] = jnp.dot(hb, wz_ref[...], preferred_element_type=F32).astype(BF16)
    gates_ref[...] = _sigmoid(jnp.dot(hb, wg_ref[...], preferred_element_type=F32))


def _post_a(x, og, gate0, wo, kvg, wkvc, wkvr, g1, scale1, shift1, wq, wz, wg):
    bn, t, d = x.shape
    tm = ROW_TILE
    row = lambda b, i: (b, i, 0)
    per_b = lambda b, i: (b, 0, 0)
    const = lambda b, i: (0, 0)
    full = lambda a: pl.BlockSpec(a.shape, const)
    vec = pl.BlockSpec((1, d), const)
    bvec = pl.BlockSpec((None, 1, d), per_b)
    outs = [(d, F32), (wkvc.shape[1], F32), (wkvr.shape[1], BF16), (wq.shape[1], BF16),
            (wz.shape[1], BF16), (wg.shape[1], F32)]
    return pl.pallas_call(
        _post_a_kernel,
        grid=(bn, t // tm),
        in_specs=[pl.BlockSpec((None, tm, d), row), pl.BlockSpec((None, tm, A_WIDTH), row), bvec,
                  full(wo), vec, full(wkvc), full(wkvr), vec, bvec, bvec, full(wq), full(wz), full(wg)],
        out_specs=[pl.BlockSpec((None, tm, n), row) for n, _ in outs],
        out_shape=[jax.ShapeDtypeStruct((bn, t, n), dt) for n, dt in outs],
        compiler_params=_cparams("arbitrary", "arbitrary"),
    )(x, og, gate0, wo, kvg, wkvc, wkvr, g1, scale1, shift1, wq, wz, wg)


def _compress_kernel(kc_ref, vc_ref, pos_ref, w1_ref, w2_ref, o_ref):
    dh = B_HEAD_DIM
    ncp = kc_ref.shape[0] // CMP_STRIDE
    njob = 2 * B_GROUPS
    first = [None] * njob
    second = [None] * njob
    for l in range(CMP_STRIDE):
        x = [r[pl.ds(l, ncp, stride=CMP_STRIDE), :] for r in (kc_ref, vc_ref)]
        for j in range(njob):
            kind, g = divmod(j, B_GROUPS)
            xj = x[kind][:, g * dh:(g + 1) * dh]
            for half, acc in ((0, first), (1, second)):
                row = half * CMP_STRIDE + l
                term = _dot(xj + pos_ref[kind, row:row + 1, :], w1_ref[kind, row * dh:(row + 1) * dh, :])
                acc[j] = term if acc[j] is None else acc[j] + term
    for j in range(njob):
        hid = first[j] + pltpu.roll(second[j], ncp - 1, axis=0)
        o_ref[j] = _dot(_silu(hid), w2_ref[j // B_GROUPS]).astype(o_ref.dtype)


def _compress(kvc, pos, w1, w2):
    bn, t, wd = kvc.shape
    njob = wd // B_HEAD_DIM
    ncp = t // CMP_STRIDE
    full = lambda a: pl.BlockSpec(a.shape, lambda b: (0,) * a.ndim)
    return pl.pallas_call(
        _compress_kernel,
        grid=(bn,),
        in_specs=[pl.BlockSpec((None, t, wd // 2), lambda b: (b, 0, 0)),
                  pl.BlockSpec((None, t, wd // 2), lambda b: (b, 0, 1)), full(pos), full(w1), full(w2)],
        out_specs=pl.BlockSpec((None, njob, ncp, B_HEAD_DIM), lambda b: (b, 0, 0, 0)),
        out_shape=jax.ShapeDtypeStruct((bn, njob, ncp, B_HEAD_DIM), BF16),
        compiler_params=_cparams("arbitrary"),
    )(kvc, kvc, pos, w1, w2)


def _t5_bucket_np(dist):
    n = np.maximum(dist, 0)
    max_exact = NUM_BUCKETS // 2
    nf = np.maximum(n, 1).astype(np.float64)
    val = np.log(nf / max_exact) / math.log(MAX_DISTANCE / max_exact) * (NUM_BUCKETS - max_exact)
    frac = np.abs(val - np.round(val))
    safe = (frac > 1e-6) | (n <= max_exact) | (n >= MAX_DISTANCE)
    assert bool(np.all(safe)), "bucket boundary too close to an integer distance"
    large = np.minimum(max_exact + np.floor(np.maximum(val, 0.0)).astype(np.int64), NUM_BUCKETS - 1)
    return np.where(n < max_exact, n, large)


def _bias_onehot():
    r = np.arange(NSA_Q_TILE)[:, None]
    tiles = []
    j = np.arange(WINDOW + NSA_Q_TILE)[None, :]
    d = r + WINDOW - j
    tiles.append((d, (d >= 0) & (d < WINDOW)))
    j = np.arange((NSA_SUB + 2) * L_SLC)[None, :]
    d = r + 2 * L_SLC - j
    tiles.append((d, d >= 0))
    j = np.arange(CMP_NEAR)[None, :]
    d = r - CMP_STRIDE * (j - CMP_LEAD) - (L_CMP - 1)
    tiles.append((d, d >= 0))
    cols = [np.where(valid, _t5_bucket_np(d), NUM_BUCKETS).reshape(-1) for d, valid in tiles]
    widths = [c.size for c in cols]
    return np.concatenate(cols).astype(np.int32)[None, :], widths


def _bias_kernel(rb_ref, bk_ref, o_ref):
    rb = rb_ref[...]
    lane = lax.broadcasted_iota(jnp.int32, rb.shape, 1)
    rbs = rb - rb[:, NUM_BUCKETS - 1:NUM_BUCKETS]
    rbs = jnp.where(lane < NUM_BUCKETS, rbs, jnp.where(lane == NUM_BUCKETS, NEG_INF, 0.0))
    bk = bk_ref[...]
    onehot = jnp.where(lax.broadcasted_iota(jnp.int32, (2 * NUM_BUCKETS, bk.shape[1]), 0) == bk, 1.0, 0.0)
    o_ref[...] = _dot_f32(rbs * LOG2E, onehot)


def _bias_tables(rel_bias):
    bk, widths = _bias_onehot()
    ncol = bk.shape[1]
    nt = 8
    assert ncol % (nt * LANES) == 0
    tc = ncol // nt
    rb = jnp.concatenate([rel_bias.T, jnp.zeros((B_HEADS, NUM_BUCKETS), F32)], axis=1)
    flat = pl.pallas_call(
        _bias_kernel,
        grid=(nt,),
        in_specs=[pl.BlockSpec((B_HEADS, 2 * NUM_BUCKETS), lambda i: (0, 0)),
                  pl.BlockSpec((1, tc), lambda i: (0, i))],
        out_specs=pl.BlockSpec((B_HEADS, tc), lambda i: (0, i)),
        out_shape=jax.ShapeDtypeStruct((B_HEADS, ncol), F32),
        compiler_params=_cparams("arbitrary"),
    )(rb, jnp.asarray(bk))
    out, start = [], 0
    for wd in widths:
        tile = flat[:, start:start + wd].reshape(B_GROUPS, B_HPG * NSA_Q_TILE, wd // NSA_Q_TILE)
        out.append(tile)
        start += wd
    return out


def _nsa_kernel(q_ref, kc_ref, vc_ref, slc_ref, win_ref, kst_ref, kwt_ref, vt_ref, tcmp_ref, tsel_ref, twin_ref,
                ov_ref, zc_ref, zs_ref, zw_ref, gates_ref, ex_ref, y_ref,
                sa_ref, sb_ref, ks_ref, vs_ref, kw_ref, vw_ref):
    tq = NSA_Q_TILE
    dh = B_HEAD_DIM
    hpg = B_HPG
    rows = hpg * tq
    ti = pl.program_id(2)

    @pl.when(ti == 0)
    def _():
        ks_ref[...] = kst_ref[...]
        kw_ref[...] = kwt_ref[...]
        vs_ref[...] = vt_ref[...]
        vw_ref[...] = vt_ref[...]
        slc = slc_ref[...]
        win = win_ref[...]
        ks_ref[KV_PAD:, 0:dh] = slc[:, 0:dh]
        vs_ref[KV_PAD:, 0:dh] = slc[:, dh:]
        kw_ref[KV_PAD:, 0:dh] = win[:, 0:dh]
        vw_ref[KV_PAD:, 0:dh] = win[:, dh:]

    q0 = ti * tq
    blk0 = ti * NSA_SUB
    qt = q_ref[...]
    q = jnp.concatenate([qt[:, h * dh:(h + 1) * dh] for h in range(hpg)], axis=0)

    def to_tokens(o):
        return jnp.concatenate([o[h * tq:(h + 1) * tq, :] for h in range(hpg)], axis=1)

    def finish(pv):
        return to_tokens(pv[:, :dh] * (1.0 / pv[:, dh:dh + 1]))

    gt = gates_ref[...]
    g_hi = gt.astype(BF16)
    ghl = jnp.concatenate([g_hi, (gt - g_hi.astype(F32)).astype(BF16)], axis=1)

    def gated(o_tok, br, z_ref):
        gexp = jnp.dot(ghl, ex_ref[br], preferred_element_type=F32)
        return gexp * o_tok * _silu(z_ref[...].astype(F32))

    kc = kc_ref[...]
    ncp = kc.shape[0]
    nw = WINDOW + tq
    win0 = pl.multiple_of(q0, tq)
    first_near = (tq // CMP_STRIDE) * ti - CMP_LEAD
    cid = lax.broadcasted_iota(jnp.int32, (2 * CMP_NEAR, ncp), 1)
    jrow = lax.broadcasted_iota(jnp.int32, (2 * CMP_NEAR, ncp), 0) & (CMP_NEAR - 1)
    shift_eye = jnp.where(cid - first_near == jrow, 1.0, 0.0).astype(BF16)
    pad_col = jnp.where(lax.broadcasted_iota(jnp.int32, (rows, dh), 1) == 0, NEG_INF, 0.0).astype(BF16)
    q_win = jnp.concatenate([q, pad_col], axis=1)

    s = _dot_nt(q, kc) + jnp.dot(tcmp_ref[...], shift_eye, preferred_element_type=F32)
    s_w = _dot_nt(q_win, kw_ref[pl.ds(win0, nw), :]) + twin_ref[...]

    cvis = lax.broadcasted_iota(jnp.int32, (1, ncp), 1) < first_near + CMP_NEAR
    s = jnp.where(cvis, s, NEG_INF)
    live = s > 0.1 * NEG_INF
    m = jnp.max(s, axis=-1, keepdims=True)
    e = jnp.where(live, jnp.exp2(s - m), 0.0)
    p = e * (1.0 / jnp.maximum(jnp.sum(e, axis=-1, keepdims=True), 1e-30))
    y_c = gated(to_tokens(_dot(p, vc_ref[...])), 0, zc_ref)

    psum = p[0:tq, :]
    for h in range(1, hpg):
        psum = psum + p[h * tq:(h + 1) * tq, :]
    p_hi = psum.astype(BF16)
    p_r1 = psum - p_hi.astype(F32)
    p_mid = p_r1.astype(BF16)
    p_lo = (p_r1 - p_mid.astype(F32)).astype(BF16)
    p3 = jnp.concatenate([p_hi, p_mid, p_lo], axis=1)
    imp_t = _dot_nt(ov_ref[...], p3)

    m_w = jnp.max(s_w, axis=-1, keepdims=True)
    e_w = jnp.exp2(s_w - m_w)
    y_cw = y_c + gated(finish(_dot(e_w, vw_ref[pl.ds(win0, nw), :])), 2, zw_ref)

    nblk = imp_t.shape[0]
    blk = lax.broadcasted_iota(jnp.int32, (nblk, tq), 0)
    cur = blk0 + (lax.broadcasted_iota(jnp.int32, (nblk, tq), 1) >> SLC_SHIFT)
    forced = (blk == 0) | (blk == cur) | (blk == cur - 1)
    val = jnp.where(forced, SEL_BOOST, jnp.where(blk > cur, -SEL_BOOST, imp_t))
    nslab = nblk // SUBLANES
    slabs = [val[SUBLANES * r:SUBLANES * (r + 1), :] for r in range(nslab)]
    sub = lax.broadcasted_iota(jnp.int32, (SUBLANES, tq), 0)
    n_acc = 4
    ranks = [[jnp.zeros((SUBLANES, tq), jnp.int32) for _ in range(n_acc)] for _ in range(nslab)]
    for j in range(nblk):
        vj = jnp.broadcast_to(val[j:j + 1, :], (SUBLANES, tq))
        for r in range(nslab):
            lo = SUBLANES * r
            if lo > j:
                ahead = vj >= slabs[r]
            elif lo + SUBLANES - 1 <= j:
                ahead = vj > slabs[r]
            else:
                ahead = (vj > slabs[r]) | ((vj == slabs[r]) & (sub > j - lo))
            ranks[r][j % n_acc] = ranks[r][j % n_acc] + ahead.astype(jnp.int32)
    rank = jnp.concatenate([(a[0] + a[1]) + (a[2] + a[3]) for a in ranks], axis=0)
    sel_t = (rank < N_SEL) & (blk <= cur)
    far_t = jnp.where(sel_t & (blk <= blk0 - 3), 0.0, NEG_INF)
    near_t = jnp.where(sel_t & (blk >= blk0 - 2), 0.0, NEG_INF)

    def q_with_mask(mask_t):
        mk = mask_t.T.astype(BF16)
        return jnp.concatenate([q, jnp.concatenate([mk] * hpg, axis=0)], axis=1)

    q_far = q_with_mask(far_t)
    q_near = q_with_mask(near_t)

    kt_sz = SEL_KEY_TILE
    n_far_keys = jnp.maximum(blk0 - 2, 0) * L_SLC
    n_pairs = (n_far_keys + 2 * kt_sz - 1) // (2 * kt_sz)

    def far_scores(tile):
        start = pl.multiple_of(KV_PAD + tile * kt_sz, kt_sz)
        return _dot_nt(q_far, ks_ref[pl.ds(start, kt_sz), :])

    def far_values(tile):
        start = pl.multiple_of(KV_PAD + tile * kt_sz, kt_sz)
        return vs_ref[pl.ds(start, kt_sz), :]

    def update(carry, s_t, v_t):
        m_i, acc = carry
        m_n = jnp.maximum(m_i, jnp.max(s_t, axis=-1, keepdims=True))
        e_t = jnp.exp2(s_t - m_n)
        return m_n, jnp.exp2(m_i - m_n) * acc + _dot(e_t, v_t)

    def pair_step(j, carry):
        sb_ref[...] = far_scores(2 * j + 1)
        carry = update(carry, sa_ref[...], far_values(2 * j))
        sa_ref[...] = far_scores(2 * j + 2)
        return update(carry, sb_ref[...], far_values(2 * j + 1))

    sa_ref[...] = far_scores(0)
    last = jnp.maximum(n_pairs, 1) - 1
    carry = (jnp.full((rows, 1), NEG_INF, F32), jnp.zeros((rows, 2 * dh), F32))
    carry = lax.fori_loop(0, last, pair_step, carry)
    sb_ref[...] = far_scores(2 * last + 1)
    carry = update(carry, sa_ref[...], far_values(2 * last))
    nk = (NSA_SUB + 2) * L_SLC
    near0 = pl.multiple_of(KV_PAD + q0 - 2 * L_SLC, L_SLC)
    s_n = _dot_nt(q_near, ks_ref[pl.ds(near0, nk), :]) + tsel_ref[...]
    carry = update(carry, sb_ref[...], far_values(2 * last + 1))
    _, acc = update(carry, s_n, vs_ref[pl.ds(near0, nk), :])
    y_ref[...] = (y_cw + gated(finish(acc), 1, zs_ref)).astype(y_ref.dtype)


def _gate_selectors(ng):
    ex = np.zeros((B_GROUPS, N_BRANCH, 2 * ng, B_HPG * B_HEAD_DIM), np.float32)
    for g in range(B_GROUPS):
        for br in range(N_BRANCH):
            for h in range(B_HPG):
                lane = br * B_HEADS + g * B_HPG + h
                ex[g, br, lane, h * B_HEAD_DIM:(h + 1) * B_HEAD_DIM] = 1.0
                ex[g, br, ng + lane, h * B_HEAD_DIM:(h + 1) * B_HEAD_DIM] = 1.0
    return ex


def _kv_templates(t, nblk):
    tp = KV_PAD + t
    ks_t = np.zeros((tp, 2 * B_HEAD_DIM), np.float32)
    ks_t[:, B_HEAD_DIM:] = _block_onehot(t, nblk)
    kw_t = np.zeros((tp, 2 * B_HEAD_DIM), np.float32)
    kw_t[:KV_PAD, B_HEAD_DIM] = 1.0
    v_t = np.zeros((tp, 2 * B_HEAD_DIM), np.float32)
    v_t[:, B_HEAD_DIM] = 1.0
    return [jnp.asarray(a, BF16) for a in (ks_t, kw_t, v_t)]


def _nsa(q, kcv, kvr, tcmp, tsel, twin, ov, z, gates):
    bn, t, _ = q.shape
    tq = NSA_Q_TILE
    gw = B_HPG * B_HEAD_DIM
    ncp = kcv.shape[2]
    tp = KV_PAD + t
    kvw = 2 * B_HEAD_DIM
    nblk = ov.shape[0]
    ng = gates.shape[2]
    rows = B_HPG * tq
    per_g = lambda b, g, i: (g, 0, 0)
    tmpl = pl.BlockSpec((tp, kvw), lambda b, g, i: (0, 0))
    ex = jnp.asarray(_gate_selectors(ng), BF16)
    kst, kwt, vt = _kv_templates(t, nblk)

    def z_spec(br):
        return pl.BlockSpec((None, tq, gw), lambda b, g, i: (b, i, br * B_GROUPS + g))

    return pl.pallas_call(
        _nsa_kernel,
        grid=(bn, B_GROUPS, t // tq),
        in_specs=[pl.BlockSpec((None, tq, gw), lambda b, g, i: (b, i, g)),
                  pl.BlockSpec((None, None, ncp, B_HEAD_DIM), lambda b, g, i: (b, g, 0, 0)),
                  pl.BlockSpec((None, None, ncp, B_HEAD_DIM), lambda b, g, i: (b, B_GROUPS + g, 0, 0)),
                  pl.BlockSpec((None, t, kvw), lambda b, g, i: (b, 0, g)),
                  pl.BlockSpec((None, t, kvw), lambda b, g, i: (b, 0, B_GROUPS + g)),
                  tmpl, tmpl, tmpl,
                  pl.BlockSpec((None, rows, tcmp.shape[2]), per_g),
                  pl.BlockSpec((None, rows, tsel.shape[2]), per_g),
                  pl.BlockSpec((None, rows, twin.shape[2]), per_g),
                  pl.BlockSpec(ov.shape, lambda b, g, i: (0, 0)),
                  z_spec(0), z_spec(1), z_spec(2),
                  pl.BlockSpec((None, tq, ng), lambda b, g, i: (b, i, 0)),
                  pl.BlockSpec((None,) + ex.shape[1:], lambda b, g, i: (g, 0, 0, 0))],
        out_specs=pl.BlockSpec((None, tq, gw), lambda b, g, i: (b, i, g)),
        out_shape=jax.ShapeDtypeStruct((bn, t, B_WIDTH), BF16),
        scratch_shapes=[pltpu.VMEM((rows, SEL_KEY_TILE), F32), pltpu.VMEM((rows, SEL_KEY_TILE), F32)]
        + [pltpu.VMEM((tp, kvw), BF16)] * 4,
        compiler_params=_cparams("arbitrary", "arbitrary", "arbitrary"),
    )(q, kcv, kcv, kvr, kvr, kst, kwt, vt, tcmp, tsel, twin, ov, z, z, z, gates, ex)


def _final_kernel(y_ref, x1_ref, gate_ref, wo_ref, fg_ref, o_ref):
    x2 = x1_ref[...] + gate_ref[...] * jnp.dot(y_ref[...], wo_ref[...], preferred_element_type=F32)
    o_ref[...] = _rms(x2, fg_ref[...])


def _final(y, x1, gate1, wo, fg):
    bn, t, d = x1.shape
    tm = ROW_TILE
    row = lambda b, i: (b, i, 0)
    return pl.pallas_call(
        _final_kernel,
        grid=(bn, t // tm),
        in_specs=[pl.BlockSpec((None, tm, B_WIDTH), row),
                  pl.BlockSpec((None, tm, d), row),
                  pl.BlockSpec((None, 1, d), lambda b, i: (b, 0, 0)),
                  pl.BlockSpec(wo.shape, lambda b, i: (0, 0)),
                  pl.BlockSpec((1, d), lambda b, i: (0, 0))],
        out_specs=pl.BlockSpec((None, tm, d), row),
        out_shape=jax.ShapeDtypeStruct((bn, t, d), F32),
        compiler_params=_cparams("arbitrary", "arbitrary"),
    )(y, x1, gate1, wo, fg)


def _overlap_matrix(ncp, n_cmp, n_slc, nblk):
    cells = np.arange(n_cmp)[:, None] + np.arange(L_CMP // CMP_STRIDE)[None, :]
    ov = (cells[:, None, :] // (L_SLC // CMP_STRIDE) == np.arange(n_slc)[None, :, None]).sum(-1)
    out = np.zeros((ncp, nblk), np.float32)
    out[:n_cmp, :n_slc] = ov
    return out


def _block_onehot(t, nblk):
    oh = np.zeros((KV_PAD + t, nblk), np.float32)
    oh[KV_PAD + np.arange(t), np.arange(t) // L_SLC] = 1.0
    oh[:KV_PAD, nblk - 1] = 1.0
    return oh


def kernel(x, c, rel_bias, ada_w, ada_b, norm_g, a_in_w, a_conv_w, a_A_log, a_dt_bias, a_onorm_g, a_out_w,
           kv_norm_g, kv_w, cmp_pos_k, cmp_pos_v, cmp_k_w1, cmp_k_w2, cmp_v_w1, cmp_v_w2,
           b_in_w, b_out_w, final_g):
    bn, t, d = x.shape
    assert ada_w.shape[0] == 2 and a_in_w.shape[0] == 1 and b_in_w.shape[0] == 1
    assert t % max(ROW_TILE, 2 * SEL_KEY_TILE, NSA_Q_TILE) == 0
    n_slc = t // L_SLC
    nblk = 64
    assert n_slc <= nblk
    n_cmp = (t - L_CMP) // CMP_STRIDE + 1
    ncp = t // CMP_STRIDE

    mod = _ada_modulation(c, ada_w, ada_b)
    shift = mod[:, :, None, :d]
    scale = mod[:, :, None, d:2 * d]
    gate = mod[:, :, None, 2 * d:]

    hb = GDN_HEADS_PER_STEP
    nhg = A_HEADS // hb
    w_in = a_in_w[0]
    wqkv = w_in[:, :3 * A_WIDTH].astype(BF16)
    wz = w_in[:, 3 * A_WIDTH:4 * A_WIDTH].astype(BF16)
    wb = w_in[:, 4 * A_WIDTH:4 * A_WIDTH + A_HEADS]
    wa = w_in[:, 4 * A_WIDTH + A_HEADS:]
    half = LANES // 2
    wba = jnp.zeros((d, nhg, LANES), F32)
    wba = wba.at[:, :, :hb].set(wb.reshape(d, nhg, hb)).at[:, :, half:half + hb].set(wa.reshape(d, nhg, hb))
    wba = wba.reshape(d, nhg * LANES).astype(BF16)
    lane_rows = lambda v: jnp.zeros((nhg, 1, LANES), F32).at[:, 0, half:half + hb].set(v.reshape(nhg, hb))
    q_a, k_a, v_a, z_a, ba = _in_proj_a(x, norm_g[0:1], scale[0], shift[0], wqkv, wz, wba, a_conv_w[0])
    og = _gdn(q_a, k_a, v_a, z_a, ba, lane_rows(a_A_log[0]), lane_rows(a_dt_bias[0]), a_onorm_g[0:1])

    ndh = B_GROUPS * B_HEAD_DIM
    wkvc = kv_w[:, :2 * ndh].astype(BF16)
    kv_rest = kv_w[:, 2 * ndh:].reshape(d, 4, B_GROUPS, B_HEAD_DIM)
    wkvr = jnp.concatenate([jnp.stack([kv_rest[:, 0], kv_rest[:, 1]], axis=2),
                            jnp.stack([kv_rest[:, 2], kv_rest[:, 3]], axis=2)], axis=1)
    wkvr = wkvr.reshape(d, 4 * ndh).astype(BF16)
    w_b = b_in_w[0]
    wq = w_b[:, :B_WIDTH].astype(BF16)
    wzb = w_b[:, B_WIDTH:4 * B_WIDTH].astype(BF16)
    wg = jnp.zeros((d, LANES), F32).at[:, :N_BRANCH * B_HEADS].set(w_b[:, 4 * B_WIDTH:]).astype(BF16)
    x1, kvc, kvr, q, z_b, gates = _post_a(x, og, gate[0], a_out_w[0].astype(BF16), kv_norm_g[None, :], wkvc, wkvr,
                                          norm_g[1:2], scale[1], shift[1], wq, wzb, wg)

    pos = jnp.stack([cmp_pos_k, cmp_pos_v])
    w1 = jnp.stack([cmp_k_w1, cmp_v_w1]).astype(BF16)
    w2 = jnp.stack([cmp_k_w2, cmp_v_w2]).astype(BF16)
    kcv = _compress(kvc, pos, w1, w2)

    twin, tsel, tcmp = _bias_tables(rel_bias)
    tc_hi = tcmp.astype(BF16)
    tc_lo = (tcmp - tc_hi.astype(F32)).astype(BF16)
    tcmp2 = jnp.concatenate([tc_hi, tc_lo], axis=-1)
    ov_t = _overlap_matrix(ncp, n_cmp, n_slc, nblk).T
    ov3 = jnp.asarray(np.concatenate([ov_t] * 3, axis=1), BF16)

    y = _nsa(q, kcv, kvr, tcmp2, tsel, twin, ov3, z_b, gates)

    return _final(y, x1, gate[1], b_out_w[0].astype(BF16), final_g[None, :])
```

```python
import functools
import math

import numpy as np
import jax
import jax.numpy as jnp
from jax import lax
from jax.experimental import pallas as pl
from jax.experimental.pallas import tpu as pltpu

F32 = jnp.float32
BF16 = jnp.bfloat16
HIGHEST = lax.Precision.HIGHEST

A_HEADS = 8
A_HEAD_DIM = 128
A_WIDTH = A_HEADS * A_HEAD_DIM
A_CONV = 4
A_CHUNK = 64
B_HEADS = 16
B_GROUPS = 2
B_HPG = B_HEADS // B_GROUPS
B_HEAD_DIM = 64
B_WIDTH = B_HEADS * B_HEAD_DIM
N_BRANCH = 3
L_CMP = 32
CMP_STRIDE = 16
L_SLC = 64
N_SEL = 16
WINDOW = 512
Q_BLOCK = 64
NUM_BUCKETS = 32
MAX_DISTANCE = 128
EPS = 1e-6
NEG_INF = -1e30
SEL_BOOST = 1e9
LOG2E = math.log2(math.e)
NSA_Q_SCALE = B_HEAD_DIM ** -0.5 * LOG2E

LANES = 128
SUBLANES = 8
VMEM_LIMIT_BYTES = 56 * 1024 * 1024

ROW_TILE = 256
POST_ROW_TILE = 512
GDN_HEADS_PER_STEP = 8
GDN_CHUNKS_PER_STEP = 4
SEL_KEY_TILE = 512
KV_PAD = WINDOW
NSA_Q_TILE = 128
NSA_SUB = NSA_Q_TILE // Q_BLOCK
SLC_SHIFT = L_SLC.bit_length() - 1
assert 1 << SLC_SHIFT == L_SLC and L_SLC == Q_BLOCK
CMP_LEAD = 12
CMP_NEAR = 32
assert CMP_NEAR >= CMP_LEAD + NSA_Q_TILE // CMP_STRIDE and NSA_Q_TILE % Q_BLOCK == 0
BLK16 = 16
SEL_MASK_LANES = B_HEAD_DIM
BIAS_COL_TILES = 8
RANK_PARTIALS = 4
LIVE_THRESHOLD = 0.1 * NEG_INF


def _cparams(*sem):
    return pltpu.CompilerParams(dimension_semantics=sem, vmem_limit_bytes=VMEM_LIMIT_BYTES)


def _sigmoid(x):
    return 1.0 / (1.0 + jnp.exp(-x))


def _silu(x):
    return x * _sigmoid(x)


def _dot(a, b):
    return jnp.dot(a.astype(BF16), b.astype(BF16), preferred_element_type=F32)


def _dot_nt(a, b):
    return lax.dot_general(a.astype(BF16), b.astype(BF16), (((1,), (1,)), ((), ())),
                           preferred_element_type=F32)


def _dot_f32(a, b):
    return jnp.dot(a, b, precision=HIGHEST, preferred_element_type=F32)


def _rms(x, g):
    ms = jnp.mean(x * x, axis=-1, keepdims=True)
    return x * lax.rsqrt(ms + EPS) * g


def _ada_kernel(c_ref, w_ref, b_ref, o_ref):
    o_ref[...] = _dot_f32(_silu(c_ref[...]), w_ref[...]) + b_ref[...]


def _ada_modulation(c, ada_w, ada_b):
    depth, d, d3 = ada_w.shape
    bn = c.shape[0]
    return pl.pallas_call(
        _ada_kernel,
        grid=(depth, d3 // d),
        in_specs=[pl.BlockSpec((bn, d), lambda l, j: (0, 0)),
                  pl.BlockSpec((None, d, d), lambda l, j: (l, 0, j)),
                  pl.BlockSpec((None, 1, d), lambda l, j: (l, 0, j))],
        out_specs=pl.BlockSpec((None, bn, d), lambda l, j: (l, 0, j)),
        out_shape=jax.ShapeDtypeStruct((depth, bn, d3), F32),
        compiler_params=_cparams("arbitrary", "arbitrary"),
    )(c, ada_w, ada_b.reshape(depth, 1, d3))


def _in_proj_a_kernel(x_ref, g_ref, scale_ref, shift_ref, wqkv_ref, wz_ref, wba_ref, cw_ref,
                      q_ref, k_ref, v_ref, z_ref, ba_ref, buf_ref):
    tm = x_ref.shape[0]
    halo = SUBLANES
    dh = A_HEAD_DIM

    @pl.when(pl.program_id(1) == 0)
    def _():
        buf_ref[0:halo, :] = jnp.zeros((halo, 3 * A_WIDTH), F32)

    h = _rms(x_ref[...], g_ref[...]) * (1.0 + scale_ref[...]) + shift_ref[...]
    hb = h.astype(BF16)
    buf_ref[halo:halo + tm, :] = jnp.dot(hb, wqkv_ref[...], preferred_element_type=F32)
    cw = cw_ref[...]
    xp = buf_ref[...]
    y = xp[halo:, :] * cw[A_CONV - 1:A_CONV, :]
    for kk in range(A_CONV - 1):
        y = y + pltpu.roll(xp, A_CONV - 1 - kk, axis=0)[halo:, :] * cw[kk:kk + 1, :]
    buf_ref[0:halo, :] = xp[tm:, :]
    y = _silu(y)
    for i in range(A_HEADS):
        for which, o_ref, gain in ((0, q_ref, dh ** -0.5), (1, k_ref, 1.0)):
            xh = y[:, which * A_WIDTH + i * dh:which * A_WIDTH + (i + 1) * dh]
            inv = lax.rsqrt(jnp.sum(xh * xh, axis=-1, keepdims=True) + EPS) * gain
            o_ref[:, i * dh:(i + 1) * dh] = (xh * inv).astype(BF16)
    v_ref[...] = y[:, 2 * A_WIDTH:].astype(BF16)
    z_ref[...] = jnp.dot(hb, wz_ref[...], preferred_element_type=F32).astype(BF16)
    ba_ref[...] = jnp.dot(hb, wba_ref[...], preferred_element_type=F32)


def _in_proj_a(x, g, scale, shift, wqkv, wz, wba, conv_w):
    bn, t, d = x.shape
    tm = ROW_TILE
    row = lambda b, i: (b, i, 0)
    per_b = lambda b, i: (b, 0, 0)
    const = lambda b, i: (0, 0)
    nba = wba.shape[1]
    wide = pl.BlockSpec((None, tm, A_WIDTH), row)
    wide_sd = jax.ShapeDtypeStruct((bn, t, A_WIDTH), BF16)
    return pl.pallas_call(
        _in_proj_a_kernel,
        grid=(bn, t // tm),
        in_specs=[pl.BlockSpec((None, tm, d), row),
                  pl.BlockSpec((1, d), const),
                  pl.BlockSpec((None, 1, d), per_b),
                  pl.BlockSpec((None, 1, d), per_b),
                  pl.BlockSpec(wqkv.shape, const),
                  pl.BlockSpec(wz.shape, const),
                  pl.BlockSpec(wba.shape, const),
                  pl.BlockSpec(conv_w.shape, const)],
        out_specs=[wide, wide, wide, wide, pl.BlockSpec((None, tm, nba), row)],
        out_shape=[wide_sd, wide_sd, wide_sd, wide_sd, jax.ShapeDtypeStruct((bn, t, nba), F32)],
        scratch_shapes=[pltpu.VMEM((SUBLANES + tm, 3 * A_WIDTH), F32)],
        compiler_params=_cparams("arbitrary", "arbitrary"),
    )(x, g, scale, shift, wqkv, wz, wba, conv_w)


def _cumsum_rows(x):
    n = x.shape[0]
    row = lax.broadcasted_iota(jnp.int32, x.shape, 0)
    s = 1
    while s < n:
        x = x + jnp.where(row >= s, pltpu.roll(x, s, axis=0), 0.0)
        s *= 2
    return x


def _unit_lower_inverse(ms):
    c = ms[0].shape[0]
    row = lax.broadcasted_iota(jnp.int32, (c, c), 0)
    col = lax.broadcasted_iota(jnp.int32, (c, c), 1)
    eye = (row == col).astype(F32)
    same_blk = (row & -BLK16) == (col & -BLK16)
    d = [jnp.where(same_blk, m, 0.0) for m in ms]
    mo = [m - x for m, x in zip(ms, d)]
    d2 = [_dot(x, x) for x in d]
    td = [eye - x for x in d]
    d4 = [_dot(x, x) for x in d2]
    td = [t + _dot(t, x) for t, x in zip(td, d2)]
    d8 = [_dot(x, x) for x in d4]
    td = [t + _dot(t, x) for t, x in zip(td, d4)]
    td = [t + _dot(t, x) for t, x in zip(td, d8)]
    n = [_dot(t, x) for t, x in zip(td, mo)]
    n2 = [_dot(x, x) for x in n]
    r = [eye - x for x in n]
    r = [a + _dot(a, x) for a, x in zip(r, n2)]
    return [_dot(a, t) for a, t in zip(r, td)]


def _gdn_kernel(q_ref, k_ref, v_ref, z_ref, ba_ref, alog_ref, dtb_ref, ong_ref, o_ref, s_ref, *, hb, nc):
    c = A_CHUNK
    dh = A_HEAD_DIM

    @pl.when(pl.program_id(2) == 0)
    def _():
        s_ref[...] = jnp.zeros(s_ref.shape, F32)

    q_all = q_ref[...]
    k_all = k_ref[...]
    v_all = v_ref[...]
    ba = ba_ref[...]
    beta_t = _sigmoid(ba)
    xa = ba + dtb_ref[...]
    softplus = jnp.maximum(xa, 0.0) + jnp.log(1.0 + jnp.exp(-jnp.abs(xa)))
    g_t = -jnp.exp(alog_ref[...]) * softplus
    gc_t = [_cumsum_rows(g_t[ci * c:(ci + 1) * c, :]) for ci in range(nc)]
    gc_tt = [x.T for x in gc_t]
    egc_t = [jnp.exp(x) for x in gc_t]
    ekd_t = [jnp.exp(x[c - 1:c, :] - x) for x in gc_t]
    egl_t = [jnp.exp(x[c - 1:c, :]) for x in gc_t]

    row = lax.broadcasted_iota(jnp.int32, (c, c), 0)
    col = lax.broadcasted_iota(jnp.int32, (c, c), 1)
    incl = row >= col
    strict = row > col
    heads = range(hb)
    jobs = [(ci, i) for ci in range(nc) for i in heads]
    la = LANES // 2

    def head(x, ci, i):
        return x[ci * c:(ci + 1) * c, i * dh:(i + 1) * dh]

    def lane(xs, ci, i):
        return xs[ci][:, la + i:la + i + 1]

    qnb = [head(q_all, ci, i) for ci, i in jobs]
    knb = [head(k_all, ci, i) for ci, i in jobs]
    qn = [x.astype(F32) for x in qnb]
    kn = [x.astype(F32) for x in knb]
    beta = [beta_t[ci * c:(ci + 1) * c, i:i + 1] for ci, i in jobs]
    kb = [x * y for x, y in zip(kn, beta)]
    decay = [jnp.where(incl, jnp.exp(jnp.where(incl, lane(gc_t, ci, i) - gc_tt[ci][la + i:la + i + 1, :], 0.0)), 0.0)
             for ci, i in jobs]
    m = [jnp.where(strict, _dot_nt(x, y) * d, 0.0) for x, y, d in zip(kb, knb, decay)]
    attn = [(_dot_nt(x, y) * d).astype(BF16) for x, y, d in zip(qnb, knb, decay)]
    rhs = [jnp.concatenate([head(v_all, ci, i).astype(F32) * beta[j], kb[j] * lane(egc_t, ci, i)],
                           axis=1).astype(BF16) for j, (ci, i) in enumerate(jobs)]
    qdec = [(qn[j] * lane(egc_t, ci, i)).astype(BF16) for j, (ci, i) in enumerate(jobs)]
    kdec_t = [(kn[j] * lane(ekd_t, ci, i)).T.astype(BF16) for j, (ci, i) in enumerate(jobs)]
    tinv = _unit_lower_inverse(m)
    uw = [_dot(x, y) for x, y in zip(tinv, rhs)]

    s_cur = [s_ref[i] for i in heads]
    for ci in range(nc):
        sb = [x.astype(BF16) for x in s_cur]
        job = [ci * hb + i for i in heads]
        v_new = [uw[j][:, :dh] - _dot(uw[j][:, dh:], sb[i]) for i, j in zip(heads, job)]
        vnb = [x.astype(BF16) for x in v_new]
        s_cur = [s_cur[i] * lane(egl_t, ci, i) + jnp.dot(kdec_t[j], vnb[i], preferred_element_type=F32)
                 for i, j in zip(heads, job)]
        o = [jnp.dot(qdec[j], sb[i], preferred_element_type=F32)
             + jnp.dot(attn[j], vnb[i], preferred_element_type=F32) for i, j in zip(heads, job)]
        for i in heads:
            rs = slice(ci * c, (ci + 1) * c)
            sl = slice(i * dh, (i + 1) * dh)
            o_ref[rs, sl] = (_rms(o[i], ong_ref[...]) * _silu(z_ref[rs, sl].astype(F32))).astype(o_ref.dtype)
    for i in heads:
        s_ref[i] = s_cur[i]


def _gdn(q, k, v, z, ba, alog_row, dtb_row, onorm_g):
    bn, t, _ = q.shape
    hb = GDN_HEADS_PER_STEP
    nc = GDN_CHUNKS_PER_STEP
    nhg = A_HEADS // hb
    c = nc * A_CHUNK
    w = hb * A_HEAD_DIM
    kern = functools.partial(_gdn_kernel, hb=hb, nc=nc)
    wide = pl.BlockSpec((None, c, w), lambda b, hg, n: (b, n, hg))
    hg_row = pl.BlockSpec((None, 1, LANES), lambda b, hg, n: (hg, 0, 0))
    return pl.pallas_call(
        kern,
        grid=(bn, nhg, t // c),
        in_specs=[wide, wide, wide, wide,
                  pl.BlockSpec((None, c, LANES), lambda b, hg, n: (b, n, hg)),
                  hg_row, hg_row,
                  pl.BlockSpec((1, A_HEAD_DIM), lambda b, hg, n: (0, 0))],
        out_specs=wide,
        out_shape=jax.ShapeDtypeStruct((bn, t, A_WIDTH), BF16),
        scratch_shapes=[pltpu.VMEM((hb, A_HEAD_DIM, A_HEAD_DIM), F32)],
        compiler_params=_cparams("arbitrary", "arbitrary", "arbitrary"),
    )(q, k, v, z, ba, alog_row, dtb_row, onorm_g)


def _post_a_kernel(x_ref, og_ref, gate_ref, wo_ref, kvg_ref, wkvc_ref, wkvr_ref, g1_ref, scale_ref,
                   shift_ref, wq_ref, wz_ref, wg_ref,
                   x1_ref, kvc_ref, kvr_ref, q_ref, z_ref, gates_ref):
    out = _dot(og_ref[...], wo_ref[...])
    x1 = x_ref[...] + gate_ref[...] * out
    x1_ref[...] = x1
    sb = _rms(x1, kvg_ref[...]).astype(BF16)
    kvc_ref[...] = jnp.dot(sb, wkvc_ref[...], preferred_element_type=F32)
    kvr_ref[...] = jnp.dot(sb, wkvr_ref[...], preferred_element_type=F32).astype(BF16)
    h = _rms(x1, g1_ref[...]) * (1.0 + scale_ref[...]) + shift_ref[...]
    hb = h.astype(BF16)
    q = jnp.dot(hb, wq_ref[...], preferred_element_type=F32) * NSA_Q_SCALE
    q_ref[...] = q.astype(BF16)
    z_ref[...] = jnp.dot(hb, wz_ref[...], preferred_element_type=F32).astype(BF16)
    gates_ref[...] = _sigmoid(jnp.dot(hb, wg_ref[...], preferred_element_type=F32))


def _post_a(x, og, gate0, wo, kvg, wkvc, wkvr, g1, scale1, shift1, wq, wz, wg):
    bn, t, d = x.shape
    tm = POST_ROW_TILE
    row = lambda b, i: (b, i, 0)
    per_b = lambda b, i: (b, 0, 0)
    const = lambda b, i: (0, 0)
    full = lambda a: pl.BlockSpec(a.shape, const)
    vec = pl.BlockSpec((1, d), const)
    bvec = pl.BlockSpec((None, 1, d), per_b)
    outs = [(d, F32), (wkvc.shape[1], F32), (wkvr.shape[1], BF16), (wq.shape[1], BF16),
            (wz.shape[1], BF16), (wg.shape[1], F32)]
    return pl.pallas_call(
        _post_a_kernel,
        grid=(bn, t // tm),
        in_specs=[pl.BlockSpec((None, tm, d), row), pl.BlockSpec((None, tm, A_WIDTH), row), bvec,
                  full(wo), vec, full(wkvc), full(wkvr), vec, bvec, bvec, full(wq), full(wz), full(wg)],
        out_specs=[pl.BlockSpec((None, tm, n), row) for n, _ in outs],
        out_shape=[jax.ShapeDtypeStruct((bn, t, n), dt) for n, dt in outs],
        compiler_params=_cparams("arbitrary", "arbitrary"),
    )(x, og, gate0, wo, kvg, wkvc, wkvr, g1, scale1, shift1, wq, wz, wg)


def _compress_kernel(kc_ref, vc_ref, pos_ref, w1_ref, w2_ref, o_ref):
    dh = B_HEAD_DIM
    ncp = kc_ref.shape[0] // CMP_STRIDE
    njob = 2 * B_GROUPS
    first = [None] * njob
    second = [None] * njob
    for l in range(CMP_STRIDE):
        x = [r[pl.ds(l, ncp, stride=CMP_STRIDE), :] for r in (kc_ref, vc_ref)]
        for j in range(njob):
            kind, g = divmod(j, B_GROUPS)
            xj = x[kind][:, g * dh:(g + 1) * dh]
            for half, acc in ((0, first), (1, second)):
                row = half * CMP_STRIDE + l
                term = _dot(xj + pos_ref[kind, row:row + 1, :], w1_ref[kind, row * dh:(row + 1) * dh, :])
                acc[j] = term if acc[j] is None else acc[j] + term
    for j in range(njob):
        hid = first[j] + pltpu.roll(second[j], ncp - 1, axis=0)
        o_ref[j] = _dot(_silu(hid), w2_ref[j // B_GROUPS]).astype(o_ref.dtype)


def _compress(kvc, pos, w1, w2):
    bn, t, wd = kvc.shape
    njob = wd // B_HEAD_DIM
    ncp = t // CMP_STRIDE
    full = lambda a: pl.BlockSpec(a.shape, lambda b: (0,) * a.ndim)
    return pl.pallas_call(
        _compress_kernel,
        grid=(bn,),
        in_specs=[pl.BlockSpec((None, t, wd // 2), lambda b: (b, 0, 0)),
                  pl.BlockSpec((None, t, wd // 2), lambda b: (b, 0, 1)), full(pos), full(w1), full(w2)],
        out_specs=pl.BlockSpec((None, njob, ncp, B_HEAD_DIM), lambda b: (b, 0, 0, 0)),
        out_shape=jax.ShapeDtypeStruct((bn, njob, ncp, B_HEAD_DIM), BF16),
        compiler_params=_cparams("arbitrary"),
    )(kvc, kvc, pos, w1, w2)


def _t5_bucket_np(dist):
    n = np.maximum(dist, 0)
    max_exact = NUM_BUCKETS // 2
    nf = np.maximum(n, 1).astype(np.float64)
    val = np.log(nf / max_exact) / math.log(MAX_DISTANCE / max_exact) * (NUM_BUCKETS - max_exact)
    frac = np.abs(val - np.round(val))
    safe = (frac > 1e-6) | (n <= max_exact) | (n >= MAX_DISTANCE)
    assert bool(np.all(safe)), "bucket boundary too close to an integer distance"
    large = np.minimum(max_exact + np.floor(np.maximum(val, 0.0)).astype(np.int64), NUM_BUCKETS - 1)
    return np.where(n < max_exact, n, large)


def _bias_onehot():
    far_sel = 2 * L_SLC + 1
    far_cmp = CMP_STRIDE * (CMP_LEAD + 1) - (L_CMP - 1)
    assert _t5_bucket_np(np.array([far_sel, far_cmp])).min() == NUM_BUCKETS - 1
    r = np.arange(NSA_Q_TILE)[:, None]
    tiles = []
    j = np.arange(WINDOW + NSA_Q_TILE)[None, :]
    d = r + WINDOW - j
    tiles.append((d, (d >= 0) & (d < WINDOW)))
    j = np.arange((NSA_SUB + 2) * L_SLC)[None, :]
    d = r + 2 * L_SLC - j
    tiles.append((d, d >= 0))
    j = np.arange(CMP_NEAR)[None, :]
    d = r - CMP_STRIDE * (j - CMP_LEAD) - (L_CMP - 1)
    tiles.append((d, d >= 0))
    cols = [np.where(valid, _t5_bucket_np(d), NUM_BUCKETS).reshape(-1) for d, valid in tiles]
    widths = [c.size for c in cols]
    return np.concatenate(cols).astype(np.int32)[None, :], widths


def _bias_kernel(rb_ref, bk_ref, o_ref):
    rb = rb_ref[...]
    lane = lax.broadcasted_iota(jnp.int32, rb.shape, 1)
    rbs = rb - rb[:, NUM_BUCKETS - 1:NUM_BUCKETS]
    rbs = jnp.where(lane < NUM_BUCKETS, rbs, jnp.where(lane == NUM_BUCKETS, NEG_INF, 0.0))
    bk = bk_ref[...]
    onehot = jnp.where(lax.broadcasted_iota(jnp.int32, (2 * NUM_BUCKETS, bk.shape[1]), 0) == bk, 1.0, 0.0)
    o_ref[...] = _dot_f32(rbs * LOG2E, onehot)


def _bias_tables(rel_bias):
    bk, widths = _bias_onehot()
    ncol = bk.shape[1]
    nt = BIAS_COL_TILES
    assert ncol % (nt * LANES) == 0
    tc = ncol // nt
    rb = jnp.concatenate([rel_bias.T, jnp.zeros((B_HEADS, NUM_BUCKETS), F32)], axis=1)
    flat = pl.pallas_call(
        _bias_kernel,
        grid=(nt,),
        in_specs=[pl.BlockSpec((B_HEADS, 2 * NUM_BUCKETS), lambda i: (0, 0)),
                  pl.BlockSpec((1, tc), lambda i: (0, i))],
        out_specs=pl.BlockSpec((B_HEADS, tc), lambda i: (0, i)),
        out_shape=jax.ShapeDtypeStruct((B_HEADS, ncol), F32),
        compiler_params=_cparams("arbitrary"),
    )(rb, jnp.asarray(bk))
    out, start = [], 0
    for wd in widths:
        tile = flat[:, start:start + wd].reshape(B_GROUPS, B_HPG * NSA_Q_TILE, wd // NSA_Q_TILE)
        out.append(tile)
        start += wd
    return out


def _nsa_kernel(q_ref, kc_ref, vc_ref, slc_ref, win_ref, kst_ref, kwt_ref, vt_ref, tcmp_ref, tsel_ref, twin_ref,
                ov_ref, zc_ref, zs_ref, zw_ref, gates_ref, ex_ref, y_ref,
                sa_ref, sb_ref, ks_ref, vs_ref, kw_ref, vw_ref):
    tq = NSA_Q_TILE
    dh = B_HEAD_DIM
    hpg = B_HPG
    rows = hpg * tq
    ti = pl.program_id(2)

    @pl.when(ti == 0)
    def _():
        ks_ref[...] = kst_ref[...]
        kw_ref[...] = kwt_ref[...]
        vs_ref[...] = vt_ref[...]
        vw_ref[...] = vt_ref[...]
        slc = slc_ref[...]
        win = win_ref[...]
        ks_ref[KV_PAD:, 0:dh] = slc[:, 0:dh]
        vs_ref[KV_PAD:, 0:dh] = slc[:, dh:]
        kw_ref[KV_PAD:, 0:dh] = win[:, 0:dh]
        vw_ref[KV_PAD:, 0:dh] = win[:, dh:]

    q0 = ti * tq
    blk0 = ti * NSA_SUB
    qt = q_ref[...]
    q = jnp.concatenate([qt[:, h * dh:(h + 1) * dh] for h in range(hpg)], axis=0)

    def to_tokens(o):
        return jnp.concatenate([o[h * tq:(h + 1) * tq, :] for h in range(hpg)], axis=1)

    def finish(pv):
        return to_tokens(pv[:, :dh] * (1.0 / pv[:, dh:dh + 1]))

    gt = gates_ref[...]
    g_hi = gt.astype(BF16)
    ghl = jnp.concatenate([g_hi, (gt - g_hi.astype(F32)).astype(BF16)], axis=1)

    def gated(o_tok, br, z_ref):
        gexp = jnp.dot(ghl, ex_ref[br], preferred_element_type=F32)
        return gexp * o_tok * _silu(z_ref[...].astype(F32))

    kc = kc_ref[...]
    ncp = kc.shape[0]
    nw = WINDOW + tq
    win0 = pl.multiple_of(q0, tq)
    first_near = (tq // CMP_STRIDE) * ti - CMP_LEAD
    cid = lax.broadcasted_iota(jnp.int32, (2 * CMP_NEAR, ncp), 1)
    jrow = lax.broadcasted_iota(jnp.int32, (2 * CMP_NEAR, ncp), 0) & (CMP_NEAR - 1)
    shift_eye = jnp.where(cid - first_near == jrow, 1.0, 0.0).astype(BF16)
    pad_col = jnp.where(lax.broadcasted_iota(jnp.int32, (rows, dh), 1) == 0, NEG_INF, 0.0).astype(BF16)
    q_win = jnp.concatenate([q, pad_col], axis=1)

    s = _dot_nt(q, kc) + jnp.dot(tcmp_ref[...], shift_eye, preferred_element_type=F32)
    s_w = _dot_nt(q_win, kw_ref[pl.ds(win0, nw), :]) + twin_ref[...]

    cvis = lax.broadcasted_iota(jnp.int32, (1, ncp), 1) < first_near + CMP_NEAR
    s = jnp.where(cvis, s, NEG_INF)
    live = s > LIVE_THRESHOLD
    m = jnp.max(s, axis=-1, keepdims=True)
    e = jnp.where(live, jnp.exp2(s - m), 0.0)
    p = e * (1.0 / jnp.maximum(jnp.sum(e, axis=-1, keepdims=True), 1e-30))
    y_c = gated(to_tokens(_dot(p, vc_ref[...])), 0, zc_ref)

    psum = p[0:tq, :]
    for h in range(1, hpg):
        psum = psum + p[h * tq:(h + 1) * tq, :]
    p_hi = psum.astype(BF16)
    p_r1 = psum - p_hi.astype(F32)
    p_mid = p_r1.astype(BF16)
    p_lo = (p_r1 - p_mid.astype(F32)).astype(BF16)
    p3 = jnp.concatenate([p_hi, p_mid, p_lo], axis=1)
    imp_t = _dot_nt(ov_ref[...], p3)

    m_w = jnp.max(s_w, axis=-1, keepdims=True)
    e_w = jnp.exp2(s_w - m_w)
    y_cw = y_c + gated(finish(_dot(e_w, vw_ref[pl.ds(win0, nw), :])), 2, zw_ref)

    nblk = imp_t.shape[0]
    blk = lax.broadcasted_iota(jnp.int32, (nblk, tq), 0)
    cur = blk0 + (lax.broadcasted_iota(jnp.int32, (nblk, tq), 1) >> SLC_SHIFT)
    forced = (blk == 0) | (blk == cur) | (blk == cur - 1)
    val = jnp.where(forced, SEL_BOOST, jnp.where(blk > cur, -SEL_BOOST, imp_t))
    nslab = nblk // SUBLANES
    slabs = [val[SUBLANES * r:SUBLANES * (r + 1), :] for r in range(nslab)]
    sub = lax.broadcasted_iota(jnp.int32, (SUBLANES, tq), 0)
    n_acc = RANK_PARTIALS
    ranks = [[jnp.zeros((SUBLANES, tq), jnp.int32) for _ in range(n_acc)] for _ in range(nslab)]
    for j in range(nblk):
        vj = jnp.broadcast_to(val[j:j + 1, :], (SUBLANES, tq))
        for r in range(nslab):
            lo = SUBLANES * r
            if lo > j:
                ahead = vj >= slabs[r]
            elif lo + SUBLANES - 1 <= j:
                ahead = vj > slabs[r]
            else:
                ahead = (vj > slabs[r]) | ((vj == slabs[r]) & (sub > j - lo))
            ranks[r][j % n_acc] = ranks[r][j % n_acc] + ahead.astype(jnp.int32)
    rank = jnp.concatenate([functools.reduce(lambda u, v: u + v, a) for a in ranks], axis=0)
    sel_t = (rank < N_SEL) & (blk <= cur)
    far_t = jnp.where(sel_t & (blk <= blk0 - 3), 0.0, NEG_INF)
    near_t = jnp.where(sel_t & (blk >= blk0 - 2), 0.0, NEG_INF)

    def q_with_mask(mask_t):
        mk = mask_t.T.astype(BF16)
        return jnp.concatenate([q, jnp.concatenate([mk] * hpg, axis=0)], axis=1)

    q_far = q_with_mask(far_t)
    q_near = q_with_mask(near_t)

    kt_sz = SEL_KEY_TILE
    n_far_keys = jnp.maximum(blk0 - 2, 0) * L_SLC
    n_pairs = (n_far_keys + 2 * kt_sz - 1) // (2 * kt_sz)

    def far_scores(tile):
        start = pl.multiple_of(KV_PAD + tile * kt_sz, kt_sz)
        return _dot_nt(q_far, ks_ref[pl.ds(start, kt_sz), :])

    def far_values(tile):
        start = pl.multiple_of(KV_PAD + tile * kt_sz, kt_sz)
        return vs_ref[pl.ds(start, kt_sz), :]

    def update(carry, s_t, v_t):
        m_i, acc = carry
        m_n = jnp.maximum(m_i, jnp.max(s_t, axis=-1, keepdims=True))
        e_t = jnp.exp2(s_t - m_n)
        return m_n, jnp.exp2(m_i - m_n) * acc + _dot(e_t, v_t)

    def pair_step(j, carry):
        sb_ref[...] = far_scores(2 * j + 1)
        carry = update(carry, sa_ref[...], far_values(2 * j))
        sa_ref[...] = far_scores(2 * j + 2)
        return update(carry, sb_ref[...], far_values(2 * j + 1))

    sa_ref[...] = far_scores(0)
    last = jnp.maximum(n_pairs, 1) - 1
    carry = (jnp.full((rows, 1), NEG_INF, F32), jnp.zeros((rows, 2 * dh), F32))
    carry = lax.fori_loop(0, last, pair_step, carry)
    sb_ref[...] = far_scores(2 * last + 1)
    carry = update(carry, sa_ref[...], far_values(2 * last))
    nk = (NSA_SUB + 2) * L_SLC
    near0 = pl.multiple_of(KV_PAD + q0 - 2 * L_SLC, L_SLC)
    s_n = _dot_nt(q_near, ks_ref[pl.ds(near0, nk), :]) + tsel_ref[...]
    carry = update(carry, sb_ref[...], far_values(2 * last + 1))
    _, acc = update(carry, s_n, vs_ref[pl.ds(near0, nk), :])
    y_ref[...] = (y_cw + gated(finish(acc), 1, zs_ref)).astype(y_ref.dtype)


def _gate_selectors(ng):
    ex = np.zeros((B_GROUPS, N_BRANCH, 2 * ng, B_HPG * B_HEAD_DIM), np.float32)
    for g in range(B_GROUPS):
        for br in range(N_BRANCH):
            for h in range(B_HPG):
                lane = br * B_HEADS + g * B_HPG + h
                ex[g, br, lane, h * B_HEAD_DIM:(h + 1) * B_HEAD_DIM] = 1.0
                ex[g, br, ng + lane, h * B_HEAD_DIM:(h + 1) * B_HEAD_DIM] = 1.0
    return ex


def _kv_templates(t, nblk):
    tp = KV_PAD + t
    ks_t = np.zeros((tp, 2 * B_HEAD_DIM), np.float32)
    ks_t[:, B_HEAD_DIM:] = _block_onehot(t, nblk)
    kw_t = np.zeros((tp, 2 * B_HEAD_DIM), np.float32)
    kw_t[:KV_PAD, B_HEAD_DIM] = 1.0
    v_t = np.zeros((tp, 2 * B_HEAD_DIM), np.float32)
    v_t[:, B_HEAD_DIM] = 1.0
    return [jnp.asarray(a, BF16) for a in (ks_t, kw_t, v_t)]


def _nsa(q, kcv, kvr, tcmp, tsel, twin, ov, z, gates):
    bn, t, _ = q.shape
    tq = NSA_Q_TILE
    gw = B_HPG * B_HEAD_DIM
    ncp = kcv.shape[2]
    tp = KV_PAD + t
    kvw = 2 * B_HEAD_DIM
    nblk = ov.shape[0]
    ng = gates.shape[2]
    rows = B_HPG * tq
    per_g = lambda b, g, i: (g, 0, 0)
    tmpl = pl.BlockSpec((tp, kvw), lambda b, g, i: (0, 0))
    ex = jnp.asarray(_gate_selectors(ng), BF16)
    kst, kwt, vt = _kv_templates(t, nblk)

    def z_spec(br):
        return pl.BlockSpec((None, tq, gw), lambda b, g, i: (b, i, br * B_GROUPS + g))

    return pl.pallas_call(
        _nsa_kernel,
        grid=(bn, B_GROUPS, t // tq),
        in_specs=[pl.BlockSpec((None, tq, gw), lambda b, g, i: (b, i, g)),
                  pl.BlockSpec((None, None, ncp, B_HEAD_DIM), lambda b, g, i: (b, g, 0, 0)),
                  pl.BlockSpec((None, None, ncp, B_HEAD_DIM), lambda b, g, i: (b, B_GROUPS + g, 0, 0)),
                  pl.BlockSpec((None, t, kvw), lambda b, g, i: (b, 0, g)),
                  pl.BlockSpec((None, t, kvw), lambda b, g, i: (b, 0, B_GROUPS + g)),
                  tmpl, tmpl, tmpl,
                  pl.BlockSpec((None, rows, tcmp.shape[2]), per_g),
                  pl.BlockSpec((None, rows, tsel.shape[2]), per_g),
                  pl.BlockSpec((None, rows, twin.shape[2]), per_g),
                  pl.BlockSpec(ov.shape, lambda b, g, i: (0, 0)),
                  z_spec(0), z_spec(1), z_spec(2),
                  pl.BlockSpec((None, tq, ng), lambda b, g, i: (b, i, 0)),
                  pl.BlockSpec((None,) + ex.shape[1:], lambda b, g, i: (g, 0, 0, 0))],
        out_specs=pl.BlockSpec((None, tq, gw), lambda b, g, i: (b, i, g)),
        out_shape=jax.ShapeDtypeStruct((bn, t, B_WIDTH), BF16),
        scratch_shapes=[pltpu.VMEM((rows, SEL_KEY_TILE), F32), pltpu.VMEM((rows, SEL_KEY_TILE), F32)]
        + [pltpu.VMEM((tp, kvw), BF16)] * 4,
        compiler_params=_cparams("arbitrary", "arbitrary", "arbitrary"),
    )(q, kcv, kcv, kvr, kvr, kst, kwt, vt, tcmp, tsel, twin, ov, z, z, z, gates, ex)


def _final_kernel(y_ref, x1_ref, gate_ref, wo_ref, fg_ref, o_ref):
    x2 = x1_ref[...] + gate_ref[...] * jnp.dot(y_ref[...], wo_ref[...], preferred_element_type=F32)
    o_ref[...] = _rms(x2, fg_ref[...])


def _final(y, x1, gate1, wo, fg):
    bn, t, d = x1.shape
    tm = POST_ROW_TILE
    row = lambda b, i: (b, i, 0)
    return pl.pallas_call(
        _final_kernel,
        grid=(bn, t // tm),
        in_specs=[pl.BlockSpec((None, tm, B_WIDTH), row),
                  pl.BlockSpec((None, tm, d), row),
                  pl.BlockSpec((None, 1, d), lambda b, i: (b, 0, 0)),
                  pl.BlockSpec(wo.shape, lambda b, i: (0, 0)),
                  pl.BlockSpec((1, d), lambda b, i: (0, 0))],
        out_specs=pl.BlockSpec((None, tm, d), row),
        out_shape=jax.ShapeDtypeStruct((bn, t, d), F32),
        compiler_params=_cparams("arbitrary", "arbitrary"),
    )(y, x1, gate1, wo, fg)


def _overlap_matrix(ncp, n_cmp, n_slc, nblk):
    cells = np.arange(n_cmp)[:, None] + np.arange(L_CMP // CMP_STRIDE)[None, :]
    ov = (cells[:, None, :] // (L_SLC // CMP_STRIDE) == np.arange(n_slc)[None, :, None]).sum(-1)
    out = np.zeros((ncp, nblk), np.float32)
    out[:n_cmp, :n_slc] = ov
    return out


def _block_onehot(t, nblk):
    oh = np.zeros((KV_PAD + t, nblk), np.float32)
    oh[KV_PAD + np.arange(t), np.arange(t) // L_SLC] = 1.0
    oh[:KV_PAD, nblk - 1] = 1.0
    return oh


def kernel(x, c, rel_bias, ada_w, ada_b, norm_g, a_in_w, a_conv_w, a_A_log, a_dt_bias, a_onorm_g, a_out_w,
           kv_norm_g, kv_w, cmp_pos_k, cmp_pos_v, cmp_k_w1, cmp_k_w2, cmp_v_w1, cmp_v_w2,
           b_in_w, b_out_w, final_g):
    bn, t, d = x.shape
    assert ada_w.shape[0] == 2 and a_in_w.shape[0] == 1 and b_in_w.shape[0] == 1
    assert t % max(ROW_TILE, POST_ROW_TILE, 2 * SEL_KEY_TILE, NSA_Q_TILE) == 0
    n_slc = t // L_SLC
    nblk = SEL_MASK_LANES
    assert n_slc <= nblk
    n_cmp = (t - L_CMP) // CMP_STRIDE + 1
    ncp = t // CMP_STRIDE

    mod = _ada_modulation(c, ada_w, ada_b)
    shift = mod[:, :, None, :d]
    scale = mod[:, :, None, d:2 * d]
    gate = mod[:, :, None, 2 * d:]

    hb = GDN_HEADS_PER_STEP
    nhg = A_HEADS // hb
    w_in = a_in_w[0]
    wqkv = w_in[:, :3 * A_WIDTH].astype(BF16)
    wz = w_in[:, 3 * A_WIDTH:4 * A_WIDTH].astype(BF16)
    wb = w_in[:, 4 * A_WIDTH:4 * A_WIDTH + A_HEADS]
    wa = w_in[:, 4 * A_WIDTH + A_HEADS:]
    half = LANES // 2
    wba = jnp.zeros((d, nhg, LANES), F32)
    wba = wba.at[:, :, :hb].set(wb.reshape(d, nhg, hb)).at[:, :, half:half + hb].set(wa.reshape(d, nhg, hb))
    wba = wba.reshape(d, nhg * LANES).astype(BF16)
    lane_rows = lambda v: jnp.zeros((nhg, 1, LANES), F32).at[:, 0, half:half + hb].set(v.reshape(nhg, hb))
    q_a, k_a, v_a, z_a, ba = _in_proj_a(x, norm_g[0:1], scale[0], shift[0], wqkv, wz, wba, a_conv_w[0])
    og = _gdn(q_a, k_a, v_a, z_a, ba, lane_rows(a_A_log[0]), lane_rows(a_dt_bias[0]), a_onorm_g[0:1])

    ndh = B_GROUPS * B_HEAD_DIM
    wkvc = kv_w[:, :2 * ndh].astype(BF16)
    kv_rest = kv_w[:, 2 * ndh:].reshape(d, 4, B_GROUPS, B_HEAD_DIM)
    wkvr = jnp.concatenate([jnp.stack([kv_rest[:, 0], kv_rest[:, 1]], axis=2),
                            jnp.stack([kv_rest[:, 2], kv_rest[:, 3]], axis=2)], axis=1)
    wkvr = wkvr.reshape(d, 4 * ndh).astype(BF16)
    w_b = b_in_w[0]
    wq = w_b[:, :B_WIDTH].astype(BF16)
    wzb = w_b[:, B_WIDTH:4 * B_WIDTH].astype(BF16)
    wg = jnp.zeros((d, LANES), F32).at[:, :N_BRANCH * B_HEADS].set(w_b[:, 4 * B_WIDTH:]).astype(BF16)
    x1, kvc, kvr, q, z_b, gates = _post_a(x, og, gate[0], a_out_w[0].astype(BF16), kv_norm_g[None, :], wkvc, wkvr,
                                          norm_g[1:2], scale[1], shift[1], wq, wzb, wg)

    pos = jnp.stack([cmp_pos_k, cmp_pos_v])
    w1 = jnp.stack([cmp_k_w1, cmp_v_w1]).astype(BF16)
    w2 = jnp.stack([cmp_k_w2, cmp_v_w2]).astype(BF16)
    kcv = _compress(kvc, pos, w1, w2)

    twin, tsel, tcmp = _bias_tables(rel_bias)
    tc_hi = tcmp.astype(BF16)
    tc_lo = (tcmp - tc_hi.astype(F32)).astype(BF16)
    tcmp2 = jnp.concatenate([tc_hi, tc_lo], axis=-1)
    ov_t = _overlap_matrix(ncp, n_cmp, n_slc, nblk).T
    ov3 = jnp.asarray(np.concatenate([ov_t] * 3, axis=1), BF16)

    y = _nsa(q, kcv, kvr, tcmp2, tsel, twin, ov3, z_b, gates)

    return _final(y, x1, gate[1], b_out_w[0].astype(BF16), final_g[None, :])
```

```python
import functools
import math

import numpy as np
import jax
import jax.numpy as jnp
from jax import lax
from jax.experimental import pallas as pl
from jax.experimental.pallas import tpu as pltpu

F32 = jnp.float32
BF16 = jnp.bfloat16
HIGHEST = lax.Precision.HIGHEST

A_HEADS = 8
A_HEAD_DIM = 128
A_WIDTH = A_HEADS * A_HEAD_DIM
A_CONV = 4
A_CHUNK = 64
B_HEADS = 16
B_GROUPS = 2
B_HPG = B_HEADS // B_GROUPS
B_HEAD_DIM = 64
B_WIDTH = B_HEADS * B_HEAD_DIM
N_BRANCH = 3
L_CMP = 32
CMP_STRIDE = 16
L_SLC = 64
N_SEL = 16
WINDOW = 512
Q_BLOCK = 64
NUM_BUCKETS = 32
MAX_DISTANCE = 128
EPS = 1e-6
NEG_INF = -1e30
SEL_BOOST = 1e9
LOG2E = math.log2(math.e)
NSA_Q_SCALE = B_HEAD_DIM ** -0.5 * LOG2E

LANES = 128
SUBLANES = 8
VMEM_LIMIT_BYTES = 56 * 1024 * 1024

ROW_TILE = 256
POST_ROW_TILE = 512
GDN_HEADS_PER_STEP = 8
GDN_CHUNKS_PER_STEP = 4
SEL_KEY_TILE = 512
KV_PAD = WINDOW
NSA_Q_TILE = 128
NSA_SUB = NSA_Q_TILE // Q_BLOCK
SLC_SHIFT = L_SLC.bit_length() - 1
assert 1 << SLC_SHIFT == L_SLC and L_SLC == Q_BLOCK
CMP_LEAD = 12
CMP_NEAR = 32
assert CMP_NEAR >= CMP_LEAD + NSA_Q_TILE // CMP_STRIDE and NSA_Q_TILE % Q_BLOCK == 0
BLK16 = 16
SEL_MASK_LANES = B_HEAD_DIM
BIAS_COL_TILES = 8
RANK_PARTIALS = 4
LIVE_THRESHOLD = 0.1 * NEG_INF


def _cparams(*sem):
    return pltpu.CompilerParams(dimension_semantics=sem, vmem_limit_bytes=VMEM_LIMIT_BYTES)


def _sigmoid(x):
    return 1.0 / (1.0 + jnp.exp(-x))


def _silu(x):
    return x * _sigmoid(x)


def _dot(a, b):
    return jnp.dot(a.astype(BF16), b.astype(BF16), preferred_element_type=F32)


def _dot_nt(a, b):
    return lax.dot_general(a.astype(BF16), b.astype(BF16), (((1,), (1,)), ((), ())),
                           preferred_element_type=F32)


def _dot_f32(a, b):
    return jnp.dot(a, b, precision=HIGHEST, preferred_element_type=F32)


def _rms(x, g):
    ms = jnp.mean(x * x, axis=-1, keepdims=True)
    return x * lax.rsqrt(ms + EPS) * g


def _ada_kernel(c_ref, w_ref, b_ref, o_ref):
    o_ref[...] = _dot_f32(_silu(c_ref[...]), w_ref[...]) + b_ref[...]


def _ada_modulation(c, ada_w, ada_b):
    depth, d, d3 = ada_w.shape
    bn = c.shape[0]
    return pl.pallas_call(
        _ada_kernel,
        grid=(depth, d3 // d),
        in_specs=[pl.BlockSpec((bn, d), lambda l, j: (0, 0)),
                  pl.BlockSpec((None, d, d), lambda l, j: (l, 0, j)),
                  pl.BlockSpec((None, 1, d), lambda l, j: (l, 0, j))],
        out_specs=pl.BlockSpec((None, bn, d), lambda l, j: (l, 0, j)),
        out_shape=jax.ShapeDtypeStruct((depth, bn, d3), F32),
        compiler_params=_cparams("arbitrary", "arbitrary"),
    )(c, ada_w, ada_b.reshape(depth, 1, d3))


def _in_proj_a_kernel(x_ref, g_ref, scale_ref, shift_ref, wqkv_ref, wz_ref, wba_ref, cw_ref,
                      q_ref, k_ref, v_ref, z_ref, ba_ref, buf_ref):
    tm = x_ref.shape[0]
    halo = SUBLANES
    dh = A_HEAD_DIM

    @pl.when(pl.program_id(1) == 0)
    def _():
        buf_ref[0:halo, :] = jnp.zeros((halo, 3 * A_WIDTH), F32)

    h = _rms(x_ref[...], g_ref[...]) * (1.0 + scale_ref[...]) + shift_ref[...]
    hb = h.astype(BF16)
    buf_ref[halo:halo + tm, :] = jnp.dot(hb, wqkv_ref[...], preferred_element_type=F32)
    cw = cw_ref[...]
    xp = buf_ref[...]
    y = xp[halo:, :] * cw[A_CONV - 1:A_CONV, :]
    for kk in range(A_CONV - 1):
        y = y + pltpu.roll(xp, A_CONV - 1 - kk, axis=0)[halo:, :] * cw[kk:kk + 1, :]
    buf_ref[0:halo, :] = xp[tm:, :]
    y = _silu(y)
    for i in range(A_HEADS):
        for which, o_ref, gain in ((0, q_ref, dh ** -0.5), (1, k_ref, 1.0)):
            xh = y[:, which * A_WIDTH + i * dh:which * A_WIDTH + (i + 1) * dh]
            inv = lax.rsqrt(jnp.sum(xh * xh, axis=-1, keepdims=True) + EPS) * gain
            o_ref[:, i * dh:(i + 1) * dh] = (xh * inv).astype(BF16)
    v_ref[...] = y[:, 2 * A_WIDTH:].astype(BF16)
    z_ref[...] = jnp.dot(hb, wz_ref[...], preferred_element_type=F32).astype(BF16)
    ba_ref[...] = jnp.dot(hb, wba_ref[...], preferred_element_type=F32)


def _in_proj_a(x, g, scale, shift, wqkv, wz, wba, conv_w):
    bn, t, d = x.shape
    tm = ROW_TILE
    row = lambda b, i: (b, i, 0)
    per_b = lambda b, i: (b, 0, 0)
    const = lambda b, i: (0, 0)
    nba = wba.shape[1]
    wide = pl.BlockSpec((None, tm, A_WIDTH), row)
    wide_sd = jax.ShapeDtypeStruct((bn, t, A_WIDTH), BF16)
    return pl.pallas_call(
        _in_proj_a_kernel,
        grid=(bn, t // tm),
        in_specs=[pl.BlockSpec((None, tm, d), row),
                  pl.BlockSpec((1, d), const),
                  pl.BlockSpec((None, 1, d), per_b),
                  pl.BlockSpec((None, 1, d), per_b),
                  pl.BlockSpec(wqkv.shape, const),
                  pl.BlockSpec(wz.shape, const),
                  pl.BlockSpec(wba.shape, const),
                  pl.BlockSpec(conv_w.shape, const)],
        out_specs=[wide, wide, wide, wide, pl.BlockSpec((None, tm, nba), row)],
        out_shape=[wide_sd, wide_sd, wide_sd, wide_sd, jax.ShapeDtypeStruct((bn, t, nba), F32)],
        scratch_shapes=[pltpu.VMEM((SUBLANES + tm, 3 * A_WIDTH), F32)],
        compiler_params=_cparams("arbitrary", "arbitrary"),
    )(x, g, scale, shift, wqkv, wz, wba, conv_w)


def _cumsum_rows(x):
    n = x.shape[0]
    row = lax.broadcasted_iota(jnp.int32, x.shape, 0)
    s = 1
    while s < n:
        x = x + jnp.where(row >= s, pltpu.roll(x, s, axis=0), 0.0)
        s *= 2
    return x


def _unit_lower_inverse(ms):
    c = ms[0].shape[0]
    row = lax.broadcasted_iota(jnp.int32, (c, c), 0)
    col = lax.broadcasted_iota(jnp.int32, (c, c), 1)
    eye = (row == col).astype(F32)
    same_blk = (row & -BLK16) == (col & -BLK16)
    d = [jnp.where(same_blk, m, 0.0) for m in ms]
    mo = [m - x for m, x in zip(ms, d)]
    d2 = [_dot(x, x) for x in d]
    td = [eye - x for x in d]
    d4 = [_dot(x, x) for x in d2]
    td = [t + _dot(t, x) for t, x in zip(td, d2)]
    d8 = [_dot(x, x) for x in d4]
    td = [t + _dot(t, x) for t, x in zip(td, d4)]
    td = [t + _dot(t, x) for t, x in zip(td, d8)]
    n = [_dot(t, x) for t, x in zip(td, mo)]
    n2 = [_dot(x, x) for x in n]
    r = [eye - x for x in n]
    r = [a + _dot(a, x) for a, x in zip(r, n2)]
    return [_dot(a, t) for a, t in zip(r, td)]


def _gdn_kernel(q_ref, k_ref, v_ref, z_ref, ba_ref, alog_ref, dtb_ref, ong_ref, o_ref, s_ref, *, hb, nc):
    c = A_CHUNK
    dh = A_HEAD_DIM

    @pl.when(pl.program_id(2) == 0)
    def _():
        s_ref[...] = jnp.zeros(s_ref.shape, F32)

    q_all = q_ref[...]
    k_all = k_ref[...]
    v_all = v_ref[...]
    ba = ba_ref[...]
    beta_t = _sigmoid(ba)
    xa = ba + dtb_ref[...]
    softplus = jnp.maximum(xa, 0.0) + jnp.log(1.0 + jnp.exp(-jnp.abs(xa)))
    g_t = -jnp.exp(alog_ref[...]) * softplus
    gc_t = [_cumsum_rows(g_t[ci * c:(ci + 1) * c, :]) for ci in range(nc)]
    gc_tt = [x.T for x in gc_t]
    egc_t = [jnp.exp(x) for x in gc_t]
    ekd_t = [jnp.exp(x[c - 1:c, :] - x) for x in gc_t]
    egl_t = [jnp.exp(x[c - 1:c, :]) for x in gc_t]

    row = lax.broadcasted_iota(jnp.int32, (c, c), 0)
    col = lax.broadcasted_iota(jnp.int32, (c, c), 1)
    incl = row >= col
    strict = row > col
    heads = range(hb)
    jobs = [(ci, i) for ci in range(nc) for i in heads]
    la = LANES // 2

    def head(x, ci, i):
        return x[ci * c:(ci + 1) * c, i * dh:(i + 1) * dh]

    def lane(xs, ci, i):
        return xs[ci][:, la + i:la + i + 1]

    qnb = [head(q_all, ci, i) for ci, i in jobs]
    knb = [head(k_all, ci, i) for ci, i in jobs]
    qn = [x.astype(F32) for x in qnb]
    kn = [x.astype(F32) for x in knb]
    beta = [beta_t[ci * c:(ci + 1) * c, i:i + 1] for ci, i in jobs]
    kb = [x * y for x, y in zip(kn, beta)]
    decay = [jnp.where(incl, jnp.exp(jnp.where(incl, lane(gc_t, ci, i) - gc_tt[ci][la + i:la + i + 1, :], 0.0)), 0.0)
             for ci, i in jobs]
    kq = [_dot_nt(jnp.concatenate([x.astype(BF16), y], axis=0), z) for x, y, z in zip(kb, qnb, knb)]
    m = [jnp.where(strict, x[:c] * d, 0.0) for x, d in zip(kq, decay)]
    attn = [(x[c:] * d).astype(BF16) for x, d in zip(kq, decay)]
    rhs = [jnp.concatenate([head(v_all, ci, i).astype(F32) * beta[j], kb[j] * lane(egc_t, ci, i)],
                           axis=1).astype(BF16) for j, (ci, i) in enumerate(jobs)]
    qdec = [(qn[j] * lane(egc_t, ci, i)).astype(BF16) for j, (ci, i) in enumerate(jobs)]
    kdec_t = [(kn[j] * lane(ekd_t, ci, i)).T.astype(BF16) for j, (ci, i) in enumerate(jobs)]
    tinv = _unit_lower_inverse(m)
    uw = [_dot(x, y) for x, y in zip(tinv, rhs)]

    s_cur = [s_ref[i] for i in heads]
    for ci in range(nc):
        sb = [x.astype(BF16) for x in s_cur]
        job = [ci * hb + i for i in heads]
        ws = [jnp.dot(jnp.concatenate([uw[j][:, dh:].astype(BF16), qdec[j]], axis=0), sb[i],
                      preferred_element_type=F32) for i, j in zip(heads, job)]
        v_new = [uw[j][:, :dh] - ws[i][:c] for i, j in zip(heads, job)]
        vnb = [x.astype(BF16) for x in v_new]
        av = [jnp.dot(jnp.concatenate([attn[j], kdec_t[j]], axis=0), vnb[i], preferred_element_type=F32)
              for i, j in zip(heads, job)]
        s_cur = [s_cur[i] * lane(egl_t, ci, i) + av[i][c:] for i in heads]
        o = [ws[i][c:] + av[i][:c] for i in heads]
        for i in heads:
            rs = slice(ci * c, (ci + 1) * c)
            sl = slice(i * dh, (i + 1) * dh)
            o_ref[rs, sl] = (_rms(o[i], ong_ref[...]) * _silu(z_ref[rs, sl].astype(F32))).astype(o_ref.dtype)
    for i in heads:
        s_ref[i] = s_cur[i]


def _gdn(q, k, v, z, ba, alog_row, dtb_row, onorm_g):
    bn, t, _ = q.shape
    hb = GDN_HEADS_PER_STEP
    nc = GDN_CHUNKS_PER_STEP
    nhg = A_HEADS // hb
    c = nc * A_CHUNK
    w = hb * A_HEAD_DIM
    kern = functools.partial(_gdn_kernel, hb=hb, nc=nc)
    wide = pl.BlockSpec((None, c, w), lambda b, hg, n: (b, n, hg))
    hg_row = pl.BlockSpec((None, 1, LANES), lambda b, hg, n: (hg, 0, 0))
    return pl.pallas_call(
        kern,
        grid=(bn, nhg, t // c),
        in_specs=[wide, wide, wide, wide,
                  pl.BlockSpec((None, c, LANES), lambda b, hg, n: (b, n, hg)),
                  hg_row, hg_row,
                  pl.BlockSpec((1, A_HEAD_DIM), lambda b, hg, n: (0, 0))],
        out_specs=wide,
        out_shape=jax.ShapeDtypeStruct((bn, t, A_WIDTH), BF16),
        scratch_shapes=[pltpu.VMEM((hb, A_HEAD_DIM, A_HEAD_DIM), F32)],
        compiler_params=_cparams("arbitrary", "arbitrary", "arbitrary"),
    )(q, k, v, z, ba, alog_row, dtb_row, onorm_g)


def _post_a_kernel(x_ref, og_ref, gate_ref, wo_ref, kvg_ref, wkvc_ref, wkvr_ref, g1_ref, scale_ref,
                   shift_ref, wq_ref, wz_ref, wg_ref,
                   x1_ref, kvc_ref, kvr_ref, q_ref, z_ref, gates_ref):
    out = _dot(og_ref[...], wo_ref[...])
    x1 = x_ref[...] + gate_ref[...] * out
    x1_ref[...] = x1
    sb = _rms(x1, kvg_ref[...]).astype(BF16)
    kvc_ref[...] = jnp.dot(sb, wkvc_ref[...], preferred_element_type=F32)
    kvr_ref[...] = jnp.dot(sb, wkvr_ref[...], preferred_element_type=F32).astype(BF16)
    h = _rms(x1, g1_ref[...]) * (1.0 + scale_ref[...]) + shift_ref[...]
    hb = h.astype(BF16)
    q = jnp.dot(hb, wq_ref[...], preferred_element_type=F32) * NSA_Q_SCALE
    q_ref[...] = q.astype(BF16)
    z_ref[...] = jnp.dot(hb, wz_ref[...], preferred_element_type=F32).astype(BF16)
    gates_ref[...] = _sigmoid(jnp.dot(hb, wg_ref[...], preferred_element_type=F32))


def _post_a(x, og, gate0, wo, kvg, wkvc, wkvr, g1, scale1, shift1, wq, wz, wg):
    bn, t, d = x.shape
    tm = POST_ROW_TILE
    row = lambda b, i: (b, i, 0)
    per_b = lambda b, i: (b, 0, 0)
    const = lambda b, i: (0, 0)
    full = lambda a: pl.BlockSpec(a.shape, const)
    vec = pl.BlockSpec((1, d), const)
    bvec = pl.BlockSpec((None, 1, d), per_b)
    outs = [(d, F32), (wkvc.shape[1], F32), (wkvr.shape[1], BF16), (wq.shape[1], BF16),
            (wz.shape[1], BF16), (wg.shape[1], F32)]
    return pl.pallas_call(
        _post_a_kernel,
        grid=(bn, t // tm),
        in_specs=[pl.BlockSpec((None, tm, d), row), pl.BlockSpec((None, tm, A_WIDTH), row), bvec,
                  full(wo), vec, full(wkvc), full(wkvr), vec, bvec, bvec, full(wq), full(wz), full(wg)],
        out_specs=[pl.BlockSpec((None, tm, n), row) for n, _ in outs],
        out_shape=[jax.ShapeDtypeStruct((bn, t, n), dt) for n, dt in outs],
        compiler_params=_cparams("arbitrary", "arbitrary"),
    )(x, og, gate0, wo, kvg, wkvc, wkvr, g1, scale1, shift1, wq, wz, wg)


def _compress_kernel(kc_ref, vc_ref, pos_ref, w1_ref, w2_ref, o_ref):
    dh = B_HEAD_DIM
    ncp = kc_ref.shape[0] // CMP_STRIDE
    njob = 2 * B_GROUPS
    first = [None] * njob
    second = [None] * njob
    for l in range(CMP_STRIDE):
        x = [r[pl.ds(l, ncp, stride=CMP_STRIDE), :] for r in (kc_ref, vc_ref)]
        for j in range(njob):
            kind, g = divmod(j, B_GROUPS)
            xj = x[kind][:, g * dh:(g + 1) * dh]
            for half, acc in ((0, first), (1, second)):
                row = half * CMP_STRIDE + l
                term = _dot(xj + pos_ref[kind, row:row + 1, :], w1_ref[kind, row * dh:(row + 1) * dh, :])
                acc[j] = term if acc[j] is None else acc[j] + term
    for j in range(njob):
        hid = first[j] + pltpu.roll(second[j], ncp - 1, axis=0)
        o_ref[j] = _dot(_silu(hid), w2_ref[j // B_GROUPS]).astype(o_ref.dtype)


def _compress(kvc, pos, w1, w2):
    bn, t, wd = kvc.shape
    njob = wd // B_HEAD_DIM
    ncp = t // CMP_STRIDE
    full = lambda a: pl.BlockSpec(a.shape, lambda b: (0,) * a.ndim)
    return pl.pallas_call(
        _compress_kernel,
        grid=(bn,),
        in_specs=[pl.BlockSpec((None, t, wd // 2), lambda b: (b, 0, 0)),
                  pl.BlockSpec((None, t, wd // 2), lambda b: (b, 0, 1)), full(pos), full(w1), full(w2)],
        out_specs=pl.BlockSpec((None, njob, ncp, B_HEAD_DIM), lambda b: (b, 0, 0, 0)),
        out_shape=jax.ShapeDtypeStruct((bn, njob, ncp, B_HEAD_DIM), BF16),
        compiler_params=_cparams("arbitrary"),
    )(kvc, kvc, pos, w1, w2)


def _t5_bucket_np(dist):
    n = np.maximum(dist, 0)
    max_exact = NUM_BUCKETS // 2
    nf = np.maximum(n, 1).astype(np.float64)
    val = np.log(nf / max_exact) / math.log(MAX_DISTANCE / max_exact) * (NUM_BUCKETS - max_exact)
    frac = np.abs(val - np.round(val))
    safe = (frac > 1e-6) | (n <= max_exact) | (n >= MAX_DISTANCE)
    assert bool(np.all(safe)), "bucket boundary too close to an integer distance"
    large = np.minimum(max_exact + np.floor(np.maximum(val, 0.0)).astype(np.int64), NUM_BUCKETS - 1)
    return np.where(n < max_exact, n, large)


def _bias_onehot():
    far_sel = 2 * L_SLC + 1
    far_cmp = CMP_STRIDE * (CMP_LEAD + 1) - (L_CMP - 1)
    assert _t5_bucket_np(np.array([far_sel, far_cmp])).min() == NUM_BUCKETS - 1
    r = np.arange(NSA_Q_TILE)[:, None]
    tiles = []
    j = np.arange(WINDOW + NSA_Q_TILE)[None, :]
    d = r + WINDOW - j
    tiles.append((d, (d >= 0) & (d < WINDOW)))
    j = np.arange((NSA_SUB + 2) * L_SLC)[None, :]
    d = r + 2 * L_SLC - j
    tiles.append((d, d >= 0))
    j = np.arange(CMP_NEAR)[None, :]
    d = r - CMP_STRIDE * (j - CMP_LEAD) - (L_CMP - 1)
    tiles.append((d, d >= 0))
    cols = [np.where(valid, _t5_bucket_np(d), NUM_BUCKETS).reshape(-1) for d, valid in tiles]
    widths = [c.size for c in cols]
    return np.concatenate(cols).astype(np.int32)[None, :], widths


def _bias_kernel(rb_ref, bk_ref, o_ref):
    rb = rb_ref[...]
    lane = lax.broadcasted_iota(jnp.int32, rb.shape, 1)
    rbs = rb - rb[:, NUM_BUCKETS - 1:NUM_BUCKETS]
    rbs = jnp.where(lane < NUM_BUCKETS, rbs, jnp.where(lane == NUM_BUCKETS, NEG_INF, 0.0))
    bk = bk_ref[...]
    onehot = jnp.where(lax.broadcasted_iota(jnp.int32, (2 * NUM_BUCKETS, bk.shape[1]), 0) == bk, 1.0, 0.0)
    o_ref[...] = _dot_f32(rbs * LOG2E, onehot)


def _bias_tables(rel_bias):
    bk, widths = _bias_onehot()
    ncol = bk.shape[1]
    nt = BIAS_COL_TILES
    assert ncol % (nt * LANES) == 0
    tc = ncol // nt
    rb = jnp.concatenate([rel_bias.T, jnp.zeros((B_HEADS, NUM_BUCKETS), F32)], axis=1)
    flat = pl.pallas_call(
        _bias_kernel,
        grid=(nt,),
        in_specs=[pl.BlockSpec((B_HEADS, 2 * NUM_BUCKETS), lambda i: (0, 0)),
                  pl.BlockSpec((1, tc), lambda i: (0, i))],
        out_specs=pl.BlockSpec((B_HEADS, tc), lambda i: (0, i)),
        out_shape=jax.ShapeDtypeStruct((B_HEADS, ncol), F32),
        compiler_params=_cparams("arbitrary"),
    )(rb, jnp.asarray(bk))
    out, start = [], 0
    for wd in widths:
        tile = flat[:, start:start + wd].reshape(B_GROUPS, B_HPG * NSA_Q_TILE, wd // NSA_Q_TILE)
        out.append(tile)
        start += wd
    return out


def _nsa_kernel(q_ref, kc_ref, vc_ref, slc_ref, win_ref, kst_ref, kwt_ref, vt_ref, tcmp_ref, tsel_ref, twin_ref,
                ov_ref, zc_ref, zs_ref, zw_ref, gates_ref, ex_ref, y_ref,
                sa_ref, sb_ref, ks_ref, vs_ref, kw_ref, vw_ref):
    tq = NSA_Q_TILE
    dh = B_HEAD_DIM
    hpg = B_HPG
    rows = hpg * tq
    ti = pl.program_id(2)

    @pl.when(ti == 0)
    def _():
        ks_ref[...] = kst_ref[...]
        kw_ref[...] = kwt_ref[...]
        vs_ref[...] = vt_ref[...]
        vw_ref[...] = vt_ref[...]
        slc = slc_ref[...]
        win = win_ref[...]
        ks_ref[KV_PAD:, 0:dh] = slc[:, 0:dh]
        vs_ref[KV_PAD:, 0:dh] = slc[:, dh:]
        kw_ref[KV_PAD:, 0:dh] = win[:, 0:dh]
        vw_ref[KV_PAD:, 0:dh] = win[:, dh:]

    q0 = ti * tq
    blk0 = ti * NSA_SUB
    qt = q_ref[...]
    q = jnp.concatenate([qt[:, h * dh:(h + 1) * dh] for h in range(hpg)], axis=0)

    def to_tokens(o):
        return jnp.concatenate([o[h * tq:(h + 1) * tq, :] for h in range(hpg)], axis=1)

    def finish(pv):
        return to_tokens(pv[:, :dh] * (1.0 / pv[:, dh:dh + 1]))

    gt = gates_ref[...]
    g_hi = gt.astype(BF16)
    ghl = jnp.concatenate([g_hi, (gt - g_hi.astype(F32)).astype(BF16)], axis=1)

    def gated(o_tok, br, z_ref):
        gexp = jnp.dot(ghl, ex_ref[br], preferred_element_type=F32)
        return gexp * o_tok * _silu(z_ref[...].astype(F32))

    kc = kc_ref[...]
    ncp = kc.shape[0]
    nw = WINDOW + tq
    win0 = pl.multiple_of(q0, tq)
    first_near = (tq // CMP_STRIDE) * ti - CMP_LEAD
    cid = lax.broadcasted_iota(jnp.int32, (2 * CMP_NEAR, ncp), 1)
    jrow = lax.broadcasted_iota(jnp.int32, (2 * CMP_NEAR, ncp), 0) & (CMP_NEAR - 1)
    shift_eye = jnp.where(cid - first_near == jrow, 1.0, 0.0).astype(BF16)
    pad_col = jnp.where(lax.broadcasted_iota(jnp.int32, (rows, dh), 1) == 0, NEG_INF, 0.0).astype(BF16)
    q_win = jnp.concatenate([q, pad_col], axis=1)

    s = _dot_nt(q, kc) + jnp.dot(tcmp_ref[...], shift_eye, preferred_element_type=F32)
    s_w = _dot_nt(q_win, kw_ref[pl.ds(win0, nw), :]) + twin_ref[...]

    cvis = lax.broadcasted_iota(jnp.int32, (1, ncp), 1) < first_near + CMP_NEAR
    s = jnp.where(cvis, s, NEG_INF)
    live = s > LIVE_THRESHOLD
    m = jnp.max(s, axis=-1, keepdims=True)
    e = jnp.where(live, jnp.exp2(s - m), 0.0)
    p = e * (1.0 / jnp.maximum(jnp.sum(e, axis=-1, keepdims=True), 1e-30))
    y_c = gated(to_tokens(_dot(p, vc_ref[...])), 0, zc_ref)

    psum = p[0:tq, :]
    for h in range(1, hpg):
        psum = psum + p[h * tq:(h + 1) * tq, :]
    p_hi = psum.astype(BF16)
    p_r1 = psum - p_hi.astype(F32)
    p_mid = p_r1.astype(BF16)
    p_lo = (p_r1 - p_mid.astype(F32)).astype(BF16)
    p3 = jnp.concatenate([p_hi, p_mid, p_lo], axis=1)
    imp_t = _dot_nt(ov_ref[...], p3)

    nblk = imp_t.shape[0]
    blk = lax.broadcasted_iota(jnp.int32, (nblk, tq), 0)
    cur = blk0 + (lax.broadcasted_iota(jnp.int32, (nblk, tq), 1) >> SLC_SHIFT)
    forced = (blk == 0) | (blk == cur) | (blk == cur - 1)
    val = jnp.where(forced, SEL_BOOST, jnp.where(blk > cur, -SEL_BOOST, imp_t))
    nslab = nblk // SUBLANES
    slabs = [val[SUBLANES * r:SUBLANES * (r + 1), :] for r in range(nslab)]
    sub = lax.broadcasted_iota(jnp.int32, (SUBLANES, tq), 0)
    n_acc = RANK_PARTIALS
    ranks = [[jnp.zeros((SUBLANES, tq), jnp.int32) for _ in range(n_acc)] for _ in range(nslab)]
    for j in range(nblk):
        vj = jnp.broadcast_to(val[j:j + 1, :], (SUBLANES, tq))
        for r in range(nslab):
            lo = SUBLANES * r
            if lo > j:
                ahead = vj >= slabs[r]
            elif lo + SUBLANES - 1 <= j:
                ahead = vj > slabs[r]
            else:
                ahead = (vj > slabs[r]) | ((vj == slabs[r]) & (sub > j - lo))
            ranks[r][j % n_acc] = ranks[r][j % n_acc] + ahead.astype(jnp.int32)
    rank = jnp.concatenate([functools.reduce(lambda u, v: u + v, a) for a in ranks], axis=0)
    sel_t = (rank < N_SEL) & (blk <= cur)
    far_t = jnp.where(sel_t & (blk <= blk0 - 3), 0.0, NEG_INF)
    near_t = jnp.where(sel_t & (blk >= blk0 - 2), 0.0, NEG_INF)

    def q_with_mask(mask_t):
        mk = mask_t.T.astype(BF16)
        return jnp.concatenate([q, jnp.concatenate([mk] * hpg, axis=0)], axis=1)

    q_far = q_with_mask(far_t)
    q_near = q_with_mask(near_t)

    m_w = jnp.max(s_w, axis=-1, keepdims=True)
    e_w = jnp.exp2(s_w - m_w)
    y_cw = y_c + gated(finish(_dot(e_w, vw_ref[pl.ds(win0, nw), :])), 2, zw_ref)

    kt_sz = SEL_KEY_TILE
    n_far_keys = jnp.maximum(blk0 - 2, 0) * L_SLC
    n_pairs = (n_far_keys + 2 * kt_sz - 1) // (2 * kt_sz)

    def far_scores(tile):
        start = pl.multiple_of(KV_PAD + tile * kt_sz, kt_sz)
        return _dot_nt(q_far, ks_ref[pl.ds(start, kt_sz), :])

    def far_values(tile):
        start = pl.multiple_of(KV_PAD + tile * kt_sz, kt_sz)
        return vs_ref[pl.ds(start, kt_sz), :]

    def update(carry, s_t, v_t):
        m_i, acc = carry
        m_n = jnp.maximum(m_i, jnp.max(s_t, axis=-1, keepdims=True))
        e_t = jnp.exp2(s_t - m_n)
        return m_n, jnp.exp2(m_i - m_n) * acc + _dot(e_t, v_t)

    def pair_step(j, carry):
        sb_ref[...] = far_scores(2 * j + 1)
        carry = update(carry, sa_ref[...], far_values(2 * j))
        sa_ref[...] = far_scores(2 * j + 2)
        return update(carry, sb_ref[...], far_values(2 * j + 1))

    sa_ref[...] = far_scores(0)
    last = jnp.maximum(n_pairs, 1) - 1
    carry = (jnp.full((rows, 1), NEG_INF, F32), jnp.zeros((rows, 2 * dh), F32))
    carry = lax.fori_loop(0, last, pair_step, carry)
    sb_ref[...] = far_scores(2 * last + 1)
    carry = update(carry, sa_ref[...], far_values(2 * last))
    nk = (NSA_SUB + 2) * L_SLC
    near0 = pl.multiple_of(KV_PAD + q0 - 2 * L_SLC, L_SLC)
    s_n = _dot_nt(q_near, ks_ref[pl.ds(near0, nk), :]) + tsel_ref[...]
    carry = update(carry, sb_ref[...], far_values(2 * last + 1))
    _, acc = update(carry, s_n, vs_ref[pl.ds(near0, nk), :])
    y_ref[...] = (y_cw + gated(finish(acc), 1, zs_ref)).astype(y_ref.dtype)


def _gate_selectors(ng):
    ex = np.zeros((B_GROUPS, N_BRANCH, 2 * ng, B_HPG * B_HEAD_DIM), np.float32)
    for g in range(B_GROUPS):
        for br in range(N_BRANCH):
            for h in range(B_HPG):
                lane = br * B_HEADS + g * B_HPG + h
                ex[g, br, lane, h * B_HEAD_DIM:(h + 1) * B_HEAD_DIM] = 1.0
                ex[g, br, ng + lane, h * B_HEAD_DIM:(h + 1) * B_HEAD_DIM] = 1.0
    return ex


def _kv_templates(t, nblk):
    tp = KV_PAD + t
    ks_t = np.zeros((tp, 2 * B_HEAD_DIM), np.float32)
    ks_t[:, B_HEAD_DIM:] = _block_onehot(t, nblk)
    kw_t = np.zeros((tp, 2 * B_HEAD_DIM), np.float32)
    kw_t[:KV_PAD, B_HEAD_DIM] = 1.0
    v_t = np.zeros((tp, 2 * B_HEAD_DIM), np.float32)
    v_t[:, B_HEAD_DIM] = 1.0
    return [jnp.asarray(a, BF16) for a in (ks_t, kw_t, v_t)]


def _nsa(q, kcv, kvr, tcmp, tsel, twin, ov, z, gates):
    bn, t, _ = q.shape
    tq = NSA_Q_TILE
    gw = B_HPG * B_HEAD_DIM
    ncp = kcv.shape[2]
    tp = KV_PAD + t
    kvw = 2 * B_HEAD_DIM
    nblk = ov.shape[0]
    ng = gates.shape[2]
    rows = B_HPG * tq
    per_g = lambda b, g, i: (g, 0, 0)
    tmpl = pl.BlockSpec((tp, kvw), lambda b, g, i: (0, 0))
    ex = jnp.asarray(_gate_selectors(ng), BF16)
    kst, kwt, vt = _kv_templates(t, nblk)

    def z_spec(br):
        return pl.BlockSpec((None, tq, gw), lambda b, g, i: (b, i, br * B_GROUPS + g))

    return pl.pallas_call(
        _nsa_kernel,
        grid=(bn, B_GROUPS, t // tq),
        in_specs=[pl.BlockSpec((None, tq, gw), lambda b, g, i: (b, i, g)),
                  pl.BlockSpec((None, None, ncp, B_HEAD_DIM), lambda b, g, i: (b, g, 0, 0)),
                  pl.BlockSpec((None, None, ncp, B_HEAD_DIM), lambda b, g, i: (b, B_GROUPS + g, 0, 0)),
                  pl.BlockSpec((None, t, kvw), lambda b, g, i: (b, 0, g)),
                  pl.BlockSpec((None, t, kvw), lambda b, g, i: (b, 0, B_GROUPS + g)),
                  tmpl, tmpl, tmpl,
                  pl.BlockSpec((None, rows, tcmp.shape[2]), per_g),
                  pl.BlockSpec((None, rows, tsel.shape[2]), per_g),
                  pl.BlockSpec((None, rows, twin.shape[2]), per_g),
                  pl.BlockSpec(ov.shape, lambda b, g, i: (0, 0)),
                  z_spec(0), z_spec(1), z_spec(2),
                  pl.BlockSpec((None, tq, ng), lambda b, g, i: (b, i, 0)),
                  pl.BlockSpec((None,) + ex.shape[1:], lambda b, g, i: (g, 0, 0, 0))],
        out_specs=pl.BlockSpec((None, tq, gw), lambda b, g, i: (b, i, g)),
        out_shape=jax.ShapeDtypeStruct((bn, t, B_WIDTH), BF16),
        scratch_shapes=[pltpu.VMEM((rows, SEL_KEY_TILE), F32), pltpu.VMEM((rows, SEL_KEY_TILE), F32)]
        + [pltpu.VMEM((tp, kvw), BF16)] * 4,
        compiler_params=_cparams("arbitrary", "arbitrary", "arbitrary"),
    )(q, kcv, kcv, kvr, kvr, kst, kwt, vt, tcmp, tsel, twin, ov, z, z, z, gates, ex)


def _final_kernel(y_ref, x1_ref, gate_ref, wo_ref, fg_ref, o_ref):
    x2 = x1_ref[...] + gate_ref[...] * jnp.dot(y_ref[...], wo_ref[...], preferred_element_type=F32)
    o_ref[...] = _rms(x2, fg_ref[...])


def _final(y, x1, gate1, wo, fg):
    bn, t, d = x1.shape
    tm = POST_ROW_TILE
    row = lambda b, i: (b, i, 0)
    return pl.pallas_call(
        _final_kernel,
        grid=(bn, t // tm),
        in_specs=[pl.BlockSpec((None, tm, B_WIDTH), row),
                  pl.BlockSpec((None, tm, d), row),
                  pl.BlockSpec((None, 1, d), lambda b, i: (b, 0, 0)),
                  pl.BlockSpec(wo.shape, lambda b, i: (0, 0)),
                  pl.BlockSpec((1, d), lambda b, i: (0, 0))],
        out_specs=pl.BlockSpec((None, tm, d), row),
        out_shape=jax.ShapeDtypeStruct((bn, t, d), F32),
        compiler_params=_cparams("arbitrary", "arbitrary"),
    )(y, x1, gate1, wo, fg)


def _overlap_matrix(ncp, n_cmp, n_slc, nblk):
    cells = np.arange(n_cmp)[:, None] + np.arange(L_CMP // CMP_STRIDE)[None, :]
    ov = (cells[:, None, :] // (L_SLC // CMP_STRIDE) == np.arange(n_slc)[None, :, None]).sum(-1)
    out = np.zeros((ncp, nblk), np.float32)
    out[:n_cmp, :n_slc] = ov
    return out


def _block_onehot(t, nblk):
    oh = np.zeros((KV_PAD + t, nblk), np.float32)
    oh[KV_PAD + np.arange(t), np.arange(t) // L_SLC] = 1.0
    oh[:KV_PAD, nblk - 1] = 1.0
    return oh


def kernel(x, c, rel_bias, ada_w, ada_b, norm_g, a_in_w, a_conv_w, a_A_log, a_dt_bias, a_onorm_g, a_out_w,
           kv_norm_g, kv_w, cmp_pos_k, cmp_pos_v, cmp_k_w1, cmp_k_w2, cmp_v_w1, cmp_v_w2,
           b_in_w, b_out_w, final_g):
    bn, t, d = x.shape
    assert ada_w.shape[0] == 2 and a_in_w.shape[0] == 1 and b_in_w.shape[0] == 1
    assert t % max(ROW_TILE, POST_ROW_TILE, 2 * SEL_KEY_TILE, NSA_Q_TILE) == 0
    n_slc = t // L_SLC
    nblk = SEL_MASK_LANES
    assert n_slc <= nblk
    n_cmp = (t - L_CMP) // CMP_STRIDE + 1
    ncp = t // CMP_STRIDE

    mod = _ada_modulation(c, ada_w, ada_b)
    shift = mod[:, :, None, :d]
    scale = mod[:, :, None, d:2 * d]
    gate = mod[:, :, None, 2 * d:]

    hb = GDN_HEADS_PER_STEP
    nhg = A_HEADS // hb
    w_in = a_in_w[0]
    wqkv = w_in[:, :3 * A_WIDTH].astype(BF16)
    wz = w_in[:, 3 * A_WIDTH:4 * A_WIDTH].astype(BF16)
    wb = w_in[:, 4 * A_WIDTH:4 * A_WIDTH + A_HEADS]
    wa = w_in[:, 4 * A_WIDTH + A_HEADS:]
    half = LANES // 2
    wba = jnp.zeros((d, nhg, LANES), F32)
    wba = wba.at[:, :, :hb].set(wb.reshape(d, nhg, hb)).at[:, :, half:half + hb].set(wa.reshape(d, nhg, hb))
    wba = wba.reshape(d, nhg * LANES).astype(BF16)
    lane_rows = lambda v: jnp.zeros((nhg, 1, LANES), F32).at[:, 0, half:half + hb].set(v.reshape(nhg, hb))
    q_a, k_a, v_a, z_a, ba = _in_proj_a(x, norm_g[0:1], scale[0], shift[0], wqkv, wz, wba, a_conv_w[0])
    og = _gdn(q_a, k_a, v_a, z_a, ba, lane_rows(a_A_log[0]), lane_rows(a_dt_bias[0]), a_onorm_g[0:1])

    ndh = B_GROUPS * B_HEAD_DIM
    wkvc = kv_w[:, :2 * ndh].astype(BF16)
    kv_rest = kv_w[:, 2 * ndh:].reshape(d, 4, B_GROUPS, B_HEAD_DIM)
    wkvr = jnp.concatenate([jnp.stack([kv_rest[:, 0], kv_rest[:, 1]], axis=2),
                            jnp.stack([kv_rest[:, 2], kv_rest[:, 3]], axis=2)], axis=1)
    wkvr = wkvr.reshape(d, 4 * ndh).astype(BF16)
    w_b = b_in_w[0]
    wq = w_b[:, :B_WIDTH].astype(BF16)
    wzb = w_b[:, B_WIDTH:4 * B_WIDTH].astype(BF16)
    wg = jnp.zeros((d, LANES), F32).at[:, :N_BRANCH * B_HEADS].set(w_b[:, 4 * B_WIDTH:]).astype(BF16)
    x1, kvc, kvr, q, z_b, gates = _post_a(x, og, gate[0], a_out_w[0].astype(BF16), kv_norm_g[None, :], wkvc, wkvr,
                                          norm_g[1:2], scale[1], shift[1], wq, wzb, wg)

    pos = jnp.stack([cmp_pos_k, cmp_pos_v])
    w1 = jnp.stack([cmp_k_w1, cmp_v_w1]).astype(BF16)
    w2 = jnp.stack([cmp_k_w2, cmp_v_w2]).astype(BF16)
    kcv = _compress(kvc, pos, w1, w2)

    twin, tsel, tcmp = _bias_tables(rel_bias)
    tc_hi = tcmp.astype(BF16)
    tc_lo = (tcmp - tc_hi.astype(F32)).astype(BF16)
    tcmp2 = jnp.concatenate([tc_hi, tc_lo], axis=-1)
    ov_t = _overlap_matrix(ncp, n_cmp, n_slc, nblk).T
    ov3 = jnp.asarray(np.concatenate([ov_t] * 3, axis=1), BF16)

    y = _nsa(q, kcv, kvr, tcmp2, tsel, twin, ov3, z_b, gates)

    return _final(y, x1, gate[1], b_out_w[0].astype(BF16), final_g[None, :])
```

```python
import functools
import math

import numpy as np
import jax
import jax.numpy as jnp
from jax import lax
from jax.experimental import pallas as pl
from jax.experimental.pallas import tpu as pltpu

F32 = jnp.float32
BF16 = jnp.bfloat16
HIGHEST = lax.Precision.HIGHEST

A_HEADS = 8
A_HEAD_DIM = 128
A_WIDTH = A_HEADS * A_HEAD_DIM
A_CONV = 4
A_CHUNK = 64
B_HEADS = 16
B_GROUPS = 2
B_HPG = B_HEADS // B_GROUPS
B_HEAD_DIM = 64
B_WIDTH = B_HEADS * B_HEAD_DIM
N_BRANCH = 3
L_CMP = 32
CMP_STRIDE = 16
L_SLC = 64
N_SEL = 16
WINDOW = 512
Q_BLOCK = 64
NUM_BUCKETS = 32
MAX_DISTANCE = 128
EPS = 1e-6
NEG_INF = -1e30
SEL_BOOST = 1e9
LOG2E = math.log2(math.e)
NSA_Q_SCALE = B_HEAD_DIM ** -0.5 * LOG2E

LANES = 128
SUBLANES = 8
VMEM_LIMIT_BYTES = 56 * 1024 * 1024

ROW_TILE = 256
POST_ROW_TILE = 512
GDN_HEADS_PER_STEP = 8
GDN_CHUNKS_PER_STEP = 4
SEL_KEY_TILE = 512
KV_PAD = WINDOW
NSA_Q_TILE = 128
NSA_SUB = NSA_Q_TILE // Q_BLOCK
SLC_SHIFT = L_SLC.bit_length() - 1
assert 1 << SLC_SHIFT == L_SLC and L_SLC == Q_BLOCK
CMP_LEAD = 12
CMP_NEAR = 32
assert CMP_NEAR >= CMP_LEAD + NSA_Q_TILE // CMP_STRIDE and NSA_Q_TILE % Q_BLOCK == 0
BLK16 = 16
SEL_MASK_LANES = B_HEAD_DIM
BIAS_COL_TILES = 8
RANK_PARTIALS = 4
LIVE_THRESHOLD = 0.1 * NEG_INF


def _cparams(*sem):
    return pltpu.CompilerParams(dimension_semantics=sem, vmem_limit_bytes=VMEM_LIMIT_BYTES)


def _sigmoid(x):
    return 1.0 / (1.0 + jnp.exp(-x))


def _silu(x):
    return x * _sigmoid(x)


def _dot(a, b):
    return jnp.dot(a.astype(BF16), b.astype(BF16), preferred_element_type=F32)


def _dot_nt(a, b):
    return lax.dot_general(a.astype(BF16), b.astype(BF16), (((1,), (1,)), ((), ())),
                           preferred_element_type=F32)


def _dot_f32(a, b):
    return jnp.dot(a, b, precision=HIGHEST, preferred_element_type=F32)


def _rms(x, g):
    ms = jnp.mean(x * x, axis=-1, keepdims=True)
    return x * lax.rsqrt(ms + EPS) * g


def _ada_kernel(c_ref, w_ref, b_ref, o_ref):
    o_ref[...] = _dot_f32(_silu(c_ref[...]), w_ref[...]) + b_ref[...]


def _ada_modulation(c, ada_w, ada_b):
    depth, d, d3 = ada_w.shape
    bn = c.shape[0]
    return pl.pallas_call(
        _ada_kernel,
        grid=(depth, d3 // d),
        in_specs=[pl.BlockSpec((bn, d), lambda l, j: (0, 0)),
                  pl.BlockSpec((None, d, d), lambda l, j: (l, 0, j)),
                  pl.BlockSpec((None, 1, d), lambda l, j: (l, 0, j))],
        out_specs=pl.BlockSpec((None, bn, d), lambda l, j: (l, 0, j)),
        out_shape=jax.ShapeDtypeStruct((depth, bn, d3), F32),
        compiler_params=_cparams("arbitrary", "arbitrary"),
    )(c, ada_w, ada_b.reshape(depth, 1, d3))


def _in_proj_a_kernel(x_ref, g_ref, scale_ref, shift_ref, wqkv_ref, wz_ref, wba_ref, cw_ref,
                      q_ref, k_ref, v_ref, z_ref, ba_ref, buf_ref):
    tm = x_ref.shape[0]
    halo = SUBLANES
    dh = A_HEAD_DIM

    @pl.when(pl.program_id(1) == 0)
    def _():
        buf_ref[...] = jnp.zeros(buf_ref.shape, F32)

    h = _rms(x_ref[...], g_ref[...]) * (1.0 + scale_ref[...]) + shift_ref[...]
    hb = h.astype(BF16)
    xp = jnp.concatenate([buf_ref[...], jnp.dot(hb, wqkv_ref[...], preferred_element_type=F32)], axis=0)
    cw = cw_ref[...]
    y = xp[halo:, :] * cw[A_CONV - 1:A_CONV, :]
    for kk in range(A_CONV - 1):
        y = y + pltpu.roll(xp, A_CONV - 1 - kk, axis=0)[halo:, :] * cw[kk:kk + 1, :]
    buf_ref[...] = xp[tm:, :]
    y = _silu(y)
    for i in range(A_HEADS):
        for which, o_ref, gain in ((0, q_ref, dh ** -0.5), (1, k_ref, 1.0)):
            xh = y[:, which * A_WIDTH + i * dh:which * A_WIDTH + (i + 1) * dh]
            inv = lax.rsqrt(jnp.sum(xh * xh, axis=-1, keepdims=True) + EPS) * gain
            o_ref[:, i * dh:(i + 1) * dh] = (xh * inv).astype(BF16)
    v_ref[...] = y[:, 2 * A_WIDTH:].astype(BF16)
    z_ref[...] = jnp.dot(hb, wz_ref[...], preferred_element_type=F32).astype(BF16)
    ba_ref[...] = jnp.dot(hb, wba_ref[...], preferred_element_type=F32)


def _in_proj_a(x, g, scale, shift, wqkv, wz, wba, conv_w):
    bn, t, d = x.shape
    tm = ROW_TILE
    row = lambda b, i: (b, i, 0)
    per_b = lambda b, i: (b, 0, 0)
    const = lambda b, i: (0, 0)
    nba = wba.shape[1]
    wide = pl.BlockSpec((None, tm, A_WIDTH), row)
    wide_sd = jax.ShapeDtypeStruct((bn, t, A_WIDTH), BF16)
    return pl.pallas_call(
        _in_proj_a_kernel,
        grid=(bn, t // tm),
        in_specs=[pl.BlockSpec((None, tm, d), row),
                  pl.BlockSpec((1, d), const),
                  pl.BlockSpec((None, 1, d), per_b),
                  pl.BlockSpec((None, 1, d), per_b),
                  pl.BlockSpec(wqkv.shape, const),
                  pl.BlockSpec(wz.shape, const),
                  pl.BlockSpec(wba.shape, const),
                  pl.BlockSpec(conv_w.shape, const)],
        out_specs=[wide, wide, wide, wide, pl.BlockSpec((None, tm, nba), row)],
        out_shape=[wide_sd, wide_sd, wide_sd, wide_sd, jax.ShapeDtypeStruct((bn, t, nba), F32)],
        scratch_shapes=[pltpu.VMEM((SUBLANES, 3 * A_WIDTH), F32)],
        compiler_params=_cparams("arbitrary", "arbitrary"),
    )(x, g, scale, shift, wqkv, wz, wba, conv_w)


def _cumsum_rows(x):
    n = x.shape[0]
    row = lax.broadcasted_iota(jnp.int32, x.shape, 0)
    s = 1
    while s < n:
        x = x + jnp.where(row >= s, pltpu.roll(x, s, axis=0), 0.0)
        s *= 2
    return x


def _unit_lower_inverse(ms):
    c = ms[0].shape[0]
    row = lax.broadcasted_iota(jnp.int32, (c, c), 0)
    col = lax.broadcasted_iota(jnp.int32, (c, c), 1)
    eye = (row == col).astype(F32)
    same_blk = (row & -BLK16) == (col & -BLK16)
    d = [jnp.where(same_blk, m, 0.0) for m in ms]
    mo = [m - x for m, x in zip(ms, d)]
    d2 = [_dot(x, x) for x in d]
    td = [eye - x for x in d]
    both = [_dot(jnp.concatenate([t, x], axis=0), x) for t, x in zip(td, d2)]
    td = [t + b[:c] for t, b in zip(td, both)]
    d4 = [b[c:] for b in both]
    both = [_dot(jnp.concatenate([t, x], axis=0), x) for t, x in zip(td, d4)]
    td = [t + b[:c] for t, b in zip(td, both)]
    d8 = [b[c:] for b in both]
    td = [t + _dot(t, x) for t, x in zip(td, d8)]
    n = [_dot(t, x) for t, x in zip(td, mo)]
    n2 = [_dot(x, x) for x in n]
    r = [eye - x for x in n]
    r = [a + _dot(a, x) for a, x in zip(r, n2)]
    return [_dot(a, t) for a, t in zip(r, td)]


def _gdn_kernel(q_ref, k_ref, v_ref, z_ref, ba_ref, alog_ref, dtb_ref, ong_ref, o_ref, s_ref, *, hb, nc):
    c = A_CHUNK
    dh = A_HEAD_DIM

    @pl.when(pl.program_id(2) == 0)
    def _():
        s_ref[...] = jnp.zeros(s_ref.shape, F32)

    q_all = q_ref[...]
    k_all = k_ref[...]
    v_all = v_ref[...]
    ba = ba_ref[...]
    beta_t = _sigmoid(ba)
    xa = ba + dtb_ref[...]
    softplus = jnp.maximum(xa, 0.0) + jnp.log(1.0 + jnp.exp(-jnp.abs(xa)))
    g_t = -jnp.exp(alog_ref[...]) * softplus
    gc_t = [_cumsum_rows(g_t[ci * c:(ci + 1) * c, :]) for ci in range(nc)]
    gc_tt = [x.T for x in gc_t]
    egc_t = [jnp.exp(x) for x in gc_t]
    ekd_t = [jnp.exp(x[c - 1:c, :] - x) for x in gc_t]
    egl_t = [jnp.exp(x[c - 1:c, :]) for x in gc_t]

    row = lax.broadcasted_iota(jnp.int32, (c, c), 0)
    col = lax.broadcasted_iota(jnp.int32, (c, c), 1)
    incl = row >= col
    strict = row > col
    heads = range(hb)
    jobs = [(ci, i) for ci in range(nc) for i in heads]
    la = LANES // 2

    def head(x, ci, i):
        return x[ci * c:(ci + 1) * c, i * dh:(i + 1) * dh]

    def lane(xs, ci, i):
        return xs[ci][:, la + i:la + i + 1]

    qnb = [head(q_all, ci, i) for ci, i in jobs]
    knb = [head(k_all, ci, i) for ci, i in jobs]
    qn = [x.astype(F32) for x in qnb]
    kn = [x.astype(F32) for x in knb]
    beta = [beta_t[ci * c:(ci + 1) * c, i:i + 1] for ci, i in jobs]
    kb = [x * y for x, y in zip(kn, beta)]
    decay = [jnp.where(incl, jnp.exp(jnp.where(incl, lane(gc_t, ci, i) - gc_tt[ci][la + i:la + i + 1, :], 0.0)), 0.0)
             for ci, i in jobs]
    kq = [_dot_nt(jnp.concatenate([x.astype(BF16), y], axis=0), z) for x, y, z in zip(kb, qnb, knb)]
    m = [jnp.where(strict, x[:c] * d, 0.0) for x, d in zip(kq, decay)]
    attn = [(x[c:] * d).astype(BF16) for x, d in zip(kq, decay)]
    rhs = [jnp.concatenate([head(v_all, ci, i).astype(F32) * beta[j], kb[j] * lane(egc_t, ci, i)],
                           axis=1).astype(BF16) for j, (ci, i) in enumerate(jobs)]
    qdec = [(qn[j] * lane(egc_t, ci, i)).astype(BF16) for j, (ci, i) in enumerate(jobs)]
    kdec_t = [(kn[j] * lane(ekd_t, ci, i)).T.astype(BF16) for j, (ci, i) in enumerate(jobs)]
    tinv = _unit_lower_inverse(m)
    uw = [_dot(x, y) for x, y in zip(tinv, rhs)]

    s_cur = [s_ref[i] for i in heads]
    for ci in range(nc):
        sb = [x.astype(BF16) for x in s_cur]
        job = [ci * hb + i for i in heads]
        ws = [jnp.dot(jnp.concatenate([uw[j][:, dh:].astype(BF16), qdec[j]], axis=0), sb[i],
                      preferred_element_type=F32) for i, j in zip(heads, job)]
        v_new = [uw[j][:, :dh] - ws[i][:c] for i, j in zip(heads, job)]
        vnb = [x.astype(BF16) for x in v_new]
        av = [jnp.dot(jnp.concatenate([attn[j], kdec_t[j]], axis=0), vnb[i], preferred_element_type=F32)
              for i, j in zip(heads, job)]
        s_cur = [s_cur[i] * lane(egl_t, ci, i) + av[i][c:] for i in heads]
        o = [ws[i][c:] + av[i][:c] for i in heads]
        for i in heads:
            rs = slice(ci * c, (ci + 1) * c)
            sl = slice(i * dh, (i + 1) * dh)
            o_ref[rs, sl] = (_rms(o[i], ong_ref[...]) * _silu(z_ref[rs, sl].astype(F32))).astype(o_ref.dtype)
    for i in heads:
        s_ref[i] = s_cur[i]


def _gdn(q, k, v, z, ba, alog_row, dtb_row, onorm_g):
    bn, t, _ = q.shape
    hb = GDN_HEADS_PER_STEP
    nc = GDN_CHUNKS_PER_STEP
    nhg = A_HEADS // hb
    c = nc * A_CHUNK
    w = hb * A_HEAD_DIM
    kern = functools.partial(_gdn_kernel, hb=hb, nc=nc)
    wide = pl.BlockSpec((None, c, w), lambda b, hg, n: (b, n, hg))
    hg_row = pl.BlockSpec((None, 1, LANES), lambda b, hg, n: (hg, 0, 0))
    return pl.pallas_call(
        kern,
        grid=(bn, nhg, t // c),
        in_specs=[wide, wide, wide, wide,
                  pl.BlockSpec((None, c, LANES), lambda b, hg, n: (b, n, hg)),
                  hg_row, hg_row,
                  pl.BlockSpec((1, A_HEAD_DIM), lambda b, hg, n: (0, 0))],
        out_specs=wide,
        out_shape=jax.ShapeDtypeStruct((bn, t, A_WIDTH), BF16),
        scratch_shapes=[pltpu.VMEM((hb, A_HEAD_DIM, A_HEAD_DIM), F32)],
        compiler_params=_cparams("arbitrary", "arbitrary", "arbitrary"),
    )(q, k, v, z, ba, alog_row, dtb_row, onorm_g)


def _post_a_kernel(x_ref, og_ref, gate_ref, wo_ref, kvg_ref, wkvc_ref, wkvr_ref, g1_ref, scale_ref,
                   shift_ref, wq_ref, wz_ref, wg_ref,
                   x1_ref, kvc_ref, kvr_ref, q_ref, z_ref, gates_ref):
    out = _dot(og_ref[...], wo_ref[...])
    x1 = x_ref[...] + gate_ref[...] * out
    x1_ref[...] = x1
    sb = _rms(x1, kvg_ref[...]).astype(BF16)
    kvc_ref[...] = jnp.dot(sb, wkvc_ref[...], preferred_element_type=F32)
    kvr_ref[...] = jnp.dot(sb, wkvr_ref[...], preferred_element_type=F32).astype(BF16)
    h = _rms(x1, g1_ref[...]) * (1.0 + scale_ref[...]) + shift_ref[...]
    hb = h.astype(BF16)
    q = jnp.dot(hb, wq_ref[...], preferred_element_type=F32) * NSA_Q_SCALE
    q_ref[...] = q.astype(BF16)
    z_ref[...] = jnp.dot(hb, wz_ref[...], preferred_element_type=F32).astype(BF16)
    gates_ref[...] = _sigmoid(jnp.dot(hb, wg_ref[...], preferred_element_type=F32))


def _post_a(x, og, gate0, wo, kvg, wkvc, wkvr, g1, scale1, shift1, wq, wz, wg):
    bn, t, d = x.shape
    tm = POST_ROW_TILE
    row = lambda b, i: (b, i, 0)
    per_b = lambda b, i: (b, 0, 0)
    const = lambda b, i: (0, 0)
    full = lambda a: pl.BlockSpec(a.shape, const)
    vec = pl.BlockSpec((1, d), const)
    bvec = pl.BlockSpec((None, 1, d), per_b)
    outs = [(d, F32), (wkvc.shape[1], F32), (wkvr.shape[1], BF16), (wq.shape[1], BF16),
            (wz.shape[1], BF16), (wg.shape[1], F32)]
    return pl.pallas_call(
        _post_a_kernel,
        grid=(bn, t // tm),
        in_specs=[pl.BlockSpec((None, tm, d), row), pl.BlockSpec((None, tm, A_WIDTH), row), bvec,
                  full(wo), vec, full(wkvc), full(wkvr), vec, bvec, bvec, full(wq), full(wz), full(wg)],
        out_specs=[pl.BlockSpec((None, tm, n), row) for n, _ in outs],
        out_shape=[jax.ShapeDtypeStruct((bn, t, n), dt) for n, dt in outs],
        compiler_params=_cparams("arbitrary", "arbitrary"),
    )(x, og, gate0, wo, kvg, wkvc, wkvr, g1, scale1, shift1, wq, wz, wg)


def _compress_kernel(kc_ref, vc_ref, pos_ref, w1_ref, w2_ref, o_ref):
    dh = B_HEAD_DIM
    ncp = kc_ref.shape[0] // CMP_STRIDE
    njob = 2 * B_GROUPS
    first = [None] * njob
    second = [None] * njob
    for l in range(CMP_STRIDE):
        x = [r[pl.ds(l, ncp, stride=CMP_STRIDE), :] for r in (kc_ref, vc_ref)]
        for j in range(njob):
            kind, g = divmod(j, B_GROUPS)
            xj = x[kind][:, g * dh:(g + 1) * dh]
            for half, acc in ((0, first), (1, second)):
                row = half * CMP_STRIDE + l
                term = _dot(xj + pos_ref[kind, row:row + 1, :], w1_ref[kind, row * dh:(row + 1) * dh, :])
                acc[j] = term if acc[j] is None else acc[j] + term
    for j in range(njob):
        hid = first[j] + pltpu.roll(second[j], ncp - 1, axis=0)
        o_ref[j] = _dot(_silu(hid), w2_ref[j // B_GROUPS]).astype(o_ref.dtype)


def _compress(kvc, pos, w1, w2):
    bn, t, wd = kvc.shape
    njob = wd // B_HEAD_DIM
    ncp = t // CMP_STRIDE
    full = lambda a: pl.BlockSpec(a.shape, lambda b: (0,) * a.ndim)
    return pl.pallas_call(
        _compress_kernel,
        grid=(bn,),
        in_specs=[pl.BlockSpec((None, t, wd // 2), lambda b: (b, 0, 0)),
                  pl.BlockSpec((None, t, wd // 2), lambda b: (b, 0, 1)), full(pos), full(w1), full(w2)],
        out_specs=pl.BlockSpec((None, njob, ncp, B_HEAD_DIM), lambda b: (b, 0, 0, 0)),
        out_shape=jax.ShapeDtypeStruct((bn, njob, ncp, B_HEAD_DIM), BF16),
        compiler_params=_cparams("arbitrary"),
    )(kvc, kvc, pos, w1, w2)


def _t5_bucket_np(dist):
    n = np.maximum(dist, 0)
    max_exact = NUM_BUCKETS // 2
    nf = np.maximum(n, 1).astype(np.float64)
    val = np.log(nf / max_exact) / math.log(MAX_DISTANCE / max_exact) * (NUM_BUCKETS - max_exact)
    frac = np.abs(val - np.round(val))
    safe = (frac > 1e-6) | (n <= max_exact) | (n >= MAX_DISTANCE)
    assert bool(np.all(safe)), "bucket boundary too close to an integer distance"
    large = np.minimum(max_exact + np.floor(np.maximum(val, 0.0)).astype(np.int64), NUM_BUCKETS - 1)
    return np.where(n < max_exact, n, large)


def _bias_onehot():
    far_sel = 2 * L_SLC + 1
    far_cmp = CMP_STRIDE * (CMP_LEAD + 1) - (L_CMP - 1)
    assert _t5_bucket_np(np.array([far_sel, far_cmp])).min() == NUM_BUCKETS - 1
    r = np.arange(NSA_Q_TILE)[:, None]
    tiles = []
    j = np.arange(WINDOW + NSA_Q_TILE)[None, :]
    d = r + WINDOW - j
    tiles.append((d, (d >= 0) & (d < WINDOW)))
    j = np.arange((NSA_SUB + 2) * L_SLC)[None, :]
    d = r + 2 * L_SLC - j
    tiles.append((d, d >= 0))
    j = np.arange(CMP_NEAR)[None, :]
    d = r - CMP_STRIDE * (j - CMP_LEAD) - (L_CMP - 1)
    tiles.append((d, d >= 0))
    cols = [np.where(valid, _t5_bucket_np(d), NUM_BUCKETS).reshape(-1) for d, valid in tiles]
    widths = [c.size for c in cols]
    return np.concatenate(cols).astype(np.int32)[None, :], widths


def _bias_kernel(rb_ref, bk_ref, o_ref):
    rb = rb_ref[...]
    lane = lax.broadcasted_iota(jnp.int32, rb.shape, 1)
    rbs = rb - rb[:, NUM_BUCKETS - 1:NUM_BUCKETS]
    rbs = jnp.where(lane < NUM_BUCKETS, rbs, jnp.where(lane == NUM_BUCKETS, NEG_INF, 0.0))
    bk = bk_ref[...]
    onehot = jnp.where(lax.broadcasted_iota(jnp.int32, (2 * NUM_BUCKETS, bk.shape[1]), 0) == bk, 1.0, 0.0)
    o_ref[...] = _dot_f32(rbs * LOG2E, onehot)


def _bias_tables(rel_bias):
    bk, widths = _bias_onehot()
    ncol = bk.shape[1]
    nt = BIAS_COL_TILES
    assert ncol % (nt * LANES) == 0
    tc = ncol // nt
    rb = jnp.concatenate([rel_bias.T, jnp.zeros((B_HEADS, NUM_BUCKETS), F32)], axis=1)
    flat = pl.pallas_call(
        _bias_kernel,
        grid=(nt,),
        in_specs=[pl.BlockSpec((B_HEADS, 2 * NUM_BUCKETS), lambda i: (0, 0)),
                  pl.BlockSpec((1, tc), lambda i: (0, i))],
        out_specs=pl.BlockSpec((B_HEADS, tc), lambda i: (0, i)),
        out_shape=jax.ShapeDtypeStruct((B_HEADS, ncol), F32),
        compiler_params=_cparams("arbitrary"),
    )(rb, jnp.asarray(bk))
    out, start = [], 0
    for wd in widths:
        tile = flat[:, start:start + wd].reshape(B_GROUPS, B_HPG * NSA_Q_TILE, wd // NSA_Q_TILE)
        out.append(tile)
        start += wd
    return out


def _nsa_kernel(q_ref, kc_ref, vc_ref, slc_ref, win_ref, kst_ref, kwt_ref, vt_ref, tcmp_ref, tsel_ref, twin_ref,
                ov_ref, zc_ref, zs_ref, zw_ref, gates_ref, ex_ref, y_ref,
                sa_ref, sb_ref, ks_ref, vs_ref, kw_ref, vw_ref):
    tq = NSA_Q_TILE
    dh = B_HEAD_DIM
    hpg = B_HPG
    rows = hpg * tq
    ti = pl.program_id(2)

    @pl.when(ti == 0)
    def _():
        ks_ref[...] = kst_ref[...]
        kw_ref[...] = kwt_ref[...]
        vs_ref[...] = vt_ref[...]
        vw_ref[...] = vt_ref[...]
        slc = slc_ref[...]
        win = win_ref[...]
        ks_ref[KV_PAD:, 0:dh] = slc[:, 0:dh]
        vs_ref[KV_PAD:, 0:dh] = slc[:, dh:]
        kw_ref[KV_PAD:, 0:dh] = win[:, 0:dh]
        vw_ref[KV_PAD:, 0:dh] = win[:, dh:]

    q0 = ti * tq
    blk0 = ti * NSA_SUB
    qt = q_ref[...]
    q = jnp.concatenate([qt[:, h * dh:(h + 1) * dh] for h in range(hpg)], axis=0)

    def to_tokens(o):
        return jnp.concatenate([o[h * tq:(h + 1) * tq, :] for h in range(hpg)], axis=1)

    def finish(pv):
        return to_tokens(pv[:, :dh] * (1.0 / pv[:, dh:dh + 1]))

    gt = gates_ref[...]
    g_hi = gt.astype(BF16)
    ghl = jnp.concatenate([g_hi, (gt - g_hi.astype(F32)).astype(BF16)], axis=1)

    def gated(o_tok, br, z_ref):
        gexp = jnp.dot(ghl, ex_ref[br], preferred_element_type=F32)
        return gexp * o_tok * _silu(z_ref[...].astype(F32))

    kc = kc_ref[...]
    ncp = kc.shape[0]
    nw = WINDOW + tq
    win0 = pl.multiple_of(q0, tq)
    first_near = (tq // CMP_STRIDE) * ti - CMP_LEAD
    cid = lax.broadcasted_iota(jnp.int32, (2 * CMP_NEAR, ncp), 1)
    jrow = lax.broadcasted_iota(jnp.int32, (2 * CMP_NEAR, ncp), 0) & (CMP_NEAR - 1)
    shift_eye = jnp.where(cid - first_near == jrow, 1.0, 0.0).astype(BF16)
    pad_col = jnp.where(lax.broadcasted_iota(jnp.int32, (rows, dh), 1) == 0, NEG_INF, 0.0).astype(BF16)
    q_win = jnp.concatenate([q, pad_col], axis=1)

    s = _dot_nt(q, kc) + jnp.dot(tcmp_ref[...], shift_eye, preferred_element_type=F32)
    s_w = _dot_nt(q_win, kw_ref[pl.ds(win0, nw), :]) + twin_ref[...]

    cvis = lax.broadcasted_iota(jnp.int32, (1, ncp), 1) < first_near + CMP_NEAR
    s = jnp.where(cvis, s, NEG_INF)
    live = s > LIVE_THRESHOLD
    m = jnp.max(s, axis=-1, keepdims=True)
    e = jnp.where(live, jnp.exp2(s - m), 0.0)
    p = e * (1.0 / jnp.maximum(jnp.sum(e, axis=-1, keepdims=True), 1e-30))
    y_c = gated(to_tokens(_dot(p, vc_ref[...])), 0, zc_ref)

    psum = p[0:tq, :]
    for h in range(1, hpg):
        psum = psum + p[h * tq:(h + 1) * tq, :]
    p_hi = psum.astype(BF16)
    p_r1 = psum - p_hi.astype(F32)
    p_mid = p_r1.astype(BF16)
    p_lo = (p_r1 - p_mid.astype(F32)).astype(BF16)
    p3 = jnp.concatenate([p_hi, p_mid, p_lo], axis=1)
    imp_t = _dot_nt(ov_ref[...], p3)

    nblk = imp_t.shape[0]
    blk = lax.broadcasted_iota(jnp.int32, (nblk, tq), 0)
    cur = blk0 + (lax.broadcasted_iota(jnp.int32, (nblk, tq), 1) >> SLC_SHIFT)
    forced = (blk == 0) | (blk == cur) | (blk == cur - 1)
    val = jnp.where(forced, SEL_BOOST, jnp.where(blk > cur, -SEL_BOOST, imp_t))
    nslab = nblk // SUBLANES
    slabs = [val[SUBLANES * r:SUBLANES * (r + 1), :] for r in range(nslab)]
    sub = lax.broadcasted_iota(jnp.int32, (SUBLANES, tq), 0)
    n_acc = RANK_PARTIALS
    ranks = [[jnp.zeros((SUBLANES, tq), jnp.int32) for _ in range(n_acc)] for _ in range(nslab)]
    for j in range(nblk):
        vj = jnp.broadcast_to(val[j:j + 1, :], (SUBLANES, tq))
        for r in range(nslab):
            lo = SUBLANES * r
            if lo > j:
                ahead = vj >= slabs[r]
            elif lo + SUBLANES - 1 <= j:
                ahead = vj > slabs[r]
            else:
                ahead = (vj > slabs[r]) | ((vj == slabs[r]) & (sub > j - lo))
            ranks[r][j % n_acc] = ranks[r][j % n_acc] + ahead.astype(jnp.int32)
    rank = jnp.concatenate([functools.reduce(lambda u, v: u + v, a) for a in ranks], axis=0)
    sel_t = (rank < N_SEL) & (blk <= cur)
    far_t = jnp.where(sel_t & (blk <= blk0 - 3), 0.0, NEG_INF)
    near_t = jnp.where(sel_t & (blk >= blk0 - 2), 0.0, NEG_INF)

    def q_with_mask(mask_t):
        mk = mask_t.T.astype(BF16)
        return jnp.concatenate([q, jnp.concatenate([mk] * hpg, axis=0)], axis=1)

    q_far = q_with_mask(far_t)
    q_near = q_with_mask(near_t)

    m_w = jnp.max(s_w, axis=-1, keepdims=True)
    e_w = jnp.exp2(s_w - m_w)
    y_cw = y_c + gated(finish(_dot(e_w, vw_ref[pl.ds(win0, nw), :])), 2, zw_ref)

    kt_sz = SEL_KEY_TILE
    n_far_keys = jnp.maximum(blk0 - 2, 0) * L_SLC
    n_pairs = (n_far_keys + 2 * kt_sz - 1) // (2 * kt_sz)

    def far_scores(tile):
        start = pl.multiple_of(KV_PAD + tile * kt_sz, kt_sz)
        return _dot_nt(q_far, ks_ref[pl.ds(start, kt_sz), :])

    def far_values(tile):
        start = pl.multiple_of(KV_PAD + tile * kt_sz, kt_sz)
        return vs_ref[pl.ds(start, kt_sz), :]

    def update(carry, s_t, v_t):
        m_i, acc = carry
        m_n = jnp.maximum(m_i, jnp.max(s_t, axis=-1, keepdims=True))
        e_t = jnp.exp2(s_t - m_n)
        return m_n, jnp.exp2(m_i - m_n) * acc + _dot(e_t, v_t)

    def pair_step(j, carry):
        sb_ref[...] = far_scores(2 * j + 1)
        carry = update(carry, sa_ref[...], far_values(2 * j))
        sa_ref[...] = far_scores(2 * j + 2)
        return update(carry, sb_ref[...], far_values(2 * j + 1))

    sa_ref[...] = far_scores(0)
    last = jnp.maximum(n_pairs, 1) - 1
    carry = (jnp.full((rows, 1), NEG_INF, F32), jnp.zeros((rows, 2 * dh), F32))
    carry = lax.fori_loop(0, last, pair_step, carry)
    sb_ref[...] = far_scores(2 * last + 1)
    carry = update(carry, sa_ref[...], far_values(2 * last))
    nk = (NSA_SUB + 2) * L_SLC
    near0 = pl.multiple_of(KV_PAD + q0 - 2 * L_SLC, L_SLC)
    s_n = _dot_nt(q_near, ks_ref[pl.ds(near0, nk), :]) + tsel_ref[...]
    carry = update(carry, sb_ref[...], far_values(2 * last + 1))
    _, acc = update(carry, s_n, vs_ref[pl.ds(near0, nk), :])
    y_ref[...] = (y_cw + gated(finish(acc), 1, zs_ref)).astype(y_ref.dtype)


def _gate_selectors(ng):
    ex = np.zeros((B_GROUPS, N_BRANCH, 2 * ng, B_HPG * B_HEAD_DIM), np.float32)
    for g in range(B_GROUPS):
        for br in range(N_BRANCH):
            for h in range(B_HPG):
                lane = br * B_HEADS + g * B_HPG + h
                ex[g, br, lane, h * B_HEAD_DIM:(h + 1) * B_HEAD_DIM] = 1.0
                ex[g, br, ng + lane, h * B_HEAD_DIM:(h + 1) * B_HEAD_DIM] = 1.0
    return ex


def _kv_templates(t, nblk):
    tp = KV_PAD + t
    ks_t = np.zeros((tp, 2 * B_HEAD_DIM), np.float32)
    ks_t[:, B_HEAD_DIM:] = _block_onehot(t, nblk)
    kw_t = np.zeros((tp, 2 * B_HEAD_DIM), np.float32)
    kw_t[:KV_PAD, B_HEAD_DIM] = 1.0
    v_t = np.zeros((tp, 2 * B_HEAD_DIM), np.float32)
    v_t[:, B_HEAD_DIM] = 1.0
    return [jnp.asarray(a, BF16) for a in (ks_t, kw_t, v_t)]


def _nsa(q, kcv, kvr, tcmp, tsel, twin, ov, z, gates):
    bn, t, _ = q.shape
    tq = NSA_Q_TILE
    gw = B_HPG * B_HEAD_DIM
    ncp = kcv.shape[2]
    tp = KV_PAD + t
    kvw = 2 * B_HEAD_DIM
    nblk = ov.shape[0]
    ng = gates.shape[2]
    rows = B_HPG * tq
    per_g = lambda b, g, i: (g, 0, 0)
    tmpl = pl.BlockSpec((tp, kvw), lambda b, g, i: (0, 0))
    ex = jnp.asarray(_gate_selectors(ng), BF16)
    kst, kwt, vt = _kv_templates(t, nblk)

    def z_spec(br):
        return pl.BlockSpec((None, tq, gw), lambda b, g, i: (b, i, br * B_GROUPS + g))

    return pl.pallas_call(
        _nsa_kernel,
        grid=(bn, B_GROUPS, t // tq),
        in_specs=[pl.BlockSpec((None, tq, gw), lambda b, g, i: (b, i, g)),
                  pl.BlockSpec((None, None, ncp, B_HEAD_DIM), lambda b, g, i: (b, g, 0, 0)),
                  pl.BlockSpec((None, None, ncp, B_HEAD_DIM), lambda b, g, i: (b, B_GROUPS + g, 0, 0)),
                  pl.BlockSpec((None, t, kvw), lambda b, g, i: (b, 0, g)),
                  pl.BlockSpec((None, t, kvw), lambda b, g, i: (b, 0, B_GROUPS + g)),
                  tmpl, tmpl, tmpl,
                  pl.BlockSpec((None, rows, tcmp.shape[2]), per_g),
                  pl.BlockSpec((None, rows, tsel.shape[2]), per_g),
                  pl.BlockSpec((None, rows, twin.shape[2]), per_g),
                  pl.BlockSpec(ov.shape, lambda b, g, i: (0, 0)),
                  z_spec(0), z_spec(1), z_spec(2),
                  pl.BlockSpec((None, tq, ng), lambda b, g, i: (b, i, 0)),
                  pl.BlockSpec((None,) + ex.shape[1:], lambda b, g, i: (g, 0, 0, 0))],
        out_specs=pl.BlockSpec((None, tq, gw), lambda b, g, i: (b, i, g)),
        out_shape=jax.ShapeDtypeStruct((bn, t, B_WIDTH), BF16),
        scratch_shapes=[pltpu.VMEM((rows, SEL_KEY_TILE), F32), pltpu.VMEM((rows, SEL_KEY_TILE), F32)]
        + [pltpu.VMEM((tp, kvw), BF16)] * 4,
        compiler_params=_cparams("arbitrary", "arbitrary", "arbitrary"),
    )(q, kcv, kcv, kvr, kvr, kst, kwt, vt, tcmp, tsel, twin, ov, z, z, z, gates, ex)


def _final_kernel(y_ref, x1_ref, gate_ref, wo_ref, fg_ref, o_ref):
    x2 = x1_ref[...] + gate_ref[...] * jnp.dot(y_ref[...], wo_ref[...], preferred_element_type=F32)
    o_ref[...] = _rms(x2, fg_ref[...])


def _final(y, x1, gate1, wo, fg):
    bn, t, d = x1.shape
    tm = POST_ROW_TILE
    row = lambda b, i: (b, i, 0)
    return pl.pallas_call(
        _final_kernel,
        grid=(bn, t // tm),
        in_specs=[pl.BlockSpec((None, tm, B_WIDTH), row),
                  pl.BlockSpec((None, tm, d), row),
                  pl.BlockSpec((None, 1, d), lambda b, i: (b, 0, 0)),
                  pl.BlockSpec(wo.shape, lambda b, i: (0, 0)),
                  pl.BlockSpec((1, d), lambda b, i: (0, 0))],
        out_specs=pl.BlockSpec((None, tm, d), row),
        out_shape=jax.ShapeDtypeStruct((bn, t, d), F32),
        compiler_params=_cparams("arbitrary", "arbitrary"),
    )(y, x1, gate1, wo, fg)


def _overlap_matrix(ncp, n_cmp, n_slc, nblk):
    cells = np.arange(n_cmp)[:, None] + np.arange(L_CMP // CMP_STRIDE)[None, :]
    ov = (cells[:, None, :] // (L_SLC // CMP_STRIDE) == np.arange(n_slc)[None, :, None]).sum(-1)
    out = np.zeros((ncp, nblk), np.float32)
    out[:n_cmp, :n_slc] = ov
    return out


def _block_onehot(t, nblk):
    oh = np.zeros((KV_PAD + t, nblk), np.float32)
    oh[KV_PAD + np.arange(t), np.arange(t) // L_SLC] = 1.0
    oh[:KV_PAD, nblk - 1] = 1.0
    return oh


def kernel(x, c, rel_bias, ada_w, ada_b, norm_g, a_in_w, a_conv_w, a_A_log, a_dt_bias, a_onorm_g, a_out_w,
           kv_norm_g, kv_w, cmp_pos_k, cmp_pos_v, cmp_k_w1, cmp_k_w2, cmp_v_w1, cmp_v_w2,
           b_in_w, b_out_w, final_g):
    bn, t, d = x.shape
    assert ada_w.shape[0] == 2 and a_in_w.shape[0] == 1 and b_in_w.shape[0] == 1
    assert t % max(ROW_TILE, POST_ROW_TILE, 2 * SEL_KEY_TILE, NSA_Q_TILE) == 0
    n_slc = t // L_SLC
    nblk = SEL_MASK_LANES
    assert n_slc <= nblk
    n_cmp = (t - L_CMP) // CMP_STRIDE + 1
    ncp = t // CMP_STRIDE

    mod = _ada_modulation(c, ada_w, ada_b)
    shift = mod[:, :, None, :d]
    scale = mod[:, :, None, d:2 * d]
    gate = mod[:, :, None, 2 * d:]

    hb = GDN_HEADS_PER_STEP
    nhg = A_HEADS // hb
    w_in = a_in_w[0]
    wqkv = w_in[:, :3 * A_WIDTH].astype(BF16)
    wz = w_in[:, 3 * A_WIDTH:4 * A_WIDTH].astype(BF16)
    wb = w_in[:, 4 * A_WIDTH:4 * A_WIDTH + A_HEADS]
    wa = w_in[:, 4 * A_WIDTH + A_HEADS:]
    half = LANES // 2
    wba = jnp.zeros((d, nhg, LANES), F32)
    wba = wba.at[:, :, :hb].set(wb.reshape(d, nhg, hb)).at[:, :, half:half + hb].set(wa.reshape(d, nhg, hb))
    wba = wba.reshape(d, nhg * LANES).astype(BF16)
    lane_rows = lambda v: jnp.zeros((nhg, 1, LANES), F32).at[:, 0, half:half + hb].set(v.reshape(nhg, hb))
    q_a, k_a, v_a, z_a, ba = _in_proj_a(x, norm_g[0:1], scale[0], shift[0], wqkv, wz, wba, a_conv_w[0])
    og = _gdn(q_a, k_a, v_a, z_a, ba, lane_rows(a_A_log[0]), lane_rows(a_dt_bias[0]), a_onorm_g[0:1])

    ndh = B_GROUPS * B_HEAD_DIM
    wkvc = kv_w[:, :2 * ndh].astype(BF16)
    kv_rest = kv_w[:, 2 * ndh:].reshape(d, 4, B_GROUPS, B_HEAD_DIM)
    wkvr = jnp.concatenate([jnp.stack([kv_rest[:, 0], kv_rest[:, 1]], axis=2),
                            jnp.stack([kv_rest[:, 2], kv_rest[:, 3]], axis=2)], axis=1)
    wkvr = wkvr.reshape(d, 4 * ndh).astype(BF16)
    w_b = b_in_w[0]
    wq = w_b[:, :B_WIDTH].astype(BF16)
    wzb = w_b[:, B_WIDTH:4 * B_WIDTH].astype(BF16)
    wg = jnp.zeros((d, LANES), F32).at[:, :N_BRANCH * B_HEADS].set(w_b[:, 4 * B_WIDTH:]).astype(BF16)
    x1, kvc, kvr, q, z_b, gates = _post_a(x, og, gate[0], a_out_w[0].astype(BF16), kv_norm_g[None, :], wkvc, wkvr,
                                          norm_g[1:2], scale[1], shift[1], wq, wzb, wg)

    pos = jnp.stack([cmp_pos_k, cmp_pos_v])
    w1 = jnp.stack([cmp_k_w1, cmp_v_w1]).astype(BF16)
    w2 = jnp.stack([cmp_k_w2, cmp_v_w2]).astype(BF16)
    kcv = _compress(kvc, pos, w1, w2)

    twin, tsel, tcmp = _bias_tables(rel_bias)
    tc_hi = tcmp.astype(BF16)
    tc_lo = (tcmp - tc_hi.astype(F32)).astype(BF16)
    tcmp2 = jnp.concatenate([tc_hi, tc_lo], axis=-1)
    ov_t = _overlap_matrix(ncp, n_cmp, n_slc, nblk).T
    ov3 = jnp.asarray(np.concatenate([ov_t] * 3, axis=1), BF16)

    y = _nsa(q, kcv, kvr, tcmp2, tsel, twin, ov3, z_b, gates)

    return _final(y, x1, gate[1], b_out_w[0].astype(BF16), final_g[None, :])
```

```python
import functools
import math

import numpy as np
import jax
import jax.numpy as jnp
from jax import lax
from jax.experimental import pallas as pl
from jax.experimental.pallas import tpu as pltpu

F32 = jnp.float32
BF16 = jnp.bfloat16
HIGHEST = lax.Precision.HIGHEST

A_HEADS = 8
A_HEAD_DIM = 128
A_WIDTH = A_HEADS * A_HEAD_DIM
A_CONV = 4
A_CHUNK = 64
B_HEADS = 16
B_GROUPS = 2
B_HPG = B_HEADS // B_GROUPS
B_HEAD_DIM = 64
B_WIDTH = B_HEADS * B_HEAD_DIM
N_BRANCH = 3
L_CMP = 32
CMP_STRIDE = 16
L_SLC = 64
N_SEL = 16
WINDOW = 512
Q_BLOCK = 64
NUM_BUCKETS = 32
MAX_DISTANCE = 128
EPS = 1e-6
NEG_INF = -1e30
SEL_BOOST = 1e9
LOG2E = math.log2(math.e)
NSA_Q_SCALE = B_HEAD_DIM ** -0.5 * LOG2E

LANES = 128
SUBLANES = 8
VMEM_LIMIT_BYTES = 56 * 1024 * 1024

ROW_TILE = 256
POST_ROW_TILE = 512
GDN_HEADS_PER_STEP = 8
GDN_CHUNKS_PER_STEP = 4
SEL_KEY_TILE = 512
KV_PAD = WINDOW
NSA_Q_TILE = 128
NSA_SUB = NSA_Q_TILE // Q_BLOCK
SLC_SHIFT = L_SLC.bit_length() - 1
assert 1 << SLC_SHIFT == L_SLC and L_SLC == Q_BLOCK
CMP_LEAD = 12
CMP_NEAR = 32
assert CMP_NEAR >= CMP_LEAD + NSA_Q_TILE // CMP_STRIDE and NSA_Q_TILE % Q_BLOCK == 0
BLK16 = 16
SEL_MASK_LANES = B_HEAD_DIM
BIAS_COL_TILES = 8
RANK_PARTIALS = 4
LIVE_THRESHOLD = 0.1 * NEG_INF


def _cparams(*sem):
    return pltpu.CompilerParams(dimension_semantics=sem, vmem_limit_bytes=VMEM_LIMIT_BYTES)


def _sigmoid(x):
    return 1.0 / (1.0 + jnp.exp(-x))


def _silu(x):
    return x * _sigmoid(x)


def _dot(a, b):
    return jnp.dot(a.astype(BF16), b.astype(BF16), preferred_element_type=F32)


def _dot_nt(a, b):
    return lax.dot_general(a.astype(BF16), b.astype(BF16), (((1,), (1,)), ((), ())),
                           preferred_element_type=F32)


def _dot_f32(a, b):
    return jnp.dot(a, b, precision=HIGHEST, preferred_element_type=F32)


def _rms(x, g):
    ms = jnp.mean(x * x, axis=-1, keepdims=True)
    return x * lax.rsqrt(ms + EPS) * g


def _ada_kernel(c_ref, w_ref, b_ref, o_ref):
    o_ref[...] = _dot_f32(_silu(c_ref[...]), w_ref[...]) + b_ref[...]


def _ada_modulation(c, ada_w, ada_b):
    depth, d, d3 = ada_w.shape
    bn = c.shape[0]
    return pl.pallas_call(
        _ada_kernel,
        grid=(depth, d3 // d),
        in_specs=[pl.BlockSpec((bn, d), lambda l, j: (0, 0)),
                  pl.BlockSpec((None, d, d), lambda l, j: (l, 0, j)),
                  pl.BlockSpec((None, 1, d), lambda l, j: (l, 0, j))],
        out_specs=pl.BlockSpec((None, bn, d), lambda l, j: (l, 0, j)),
        out_shape=jax.ShapeDtypeStruct((depth, bn, d3), F32),
        compiler_params=_cparams("arbitrary", "arbitrary"),
    )(c, ada_w, ada_b.reshape(depth, 1, d3))


def _in_proj_a_kernel(x_ref, g_ref, scale_ref, shift_ref, wqkv_ref, wz_ref, wba_ref, cw_ref,
                      q_ref, k_ref, v_ref, z_ref, ba_ref, buf_ref):
    tm = x_ref.shape[0]
    halo = SUBLANES
    dh = A_HEAD_DIM

    @pl.when(pl.program_id(1) == 0)
    def _():
        buf_ref[...] = jnp.zeros(buf_ref.shape, F32)

    h = _rms(x_ref[...], g_ref[...]) * (1.0 + scale_ref[...]) + shift_ref[...]
    hb = h.astype(BF16)
    xp = jnp.concatenate([buf_ref[...], jnp.dot(hb, wqkv_ref[...], preferred_element_type=F32)], axis=0)
    cw = cw_ref[...]
    y = xp[halo:, :] * cw[A_CONV - 1:A_CONV, :]
    for kk in range(A_CONV - 1):
        y = y + pltpu.roll(xp, A_CONV - 1 - kk, axis=0)[halo:, :] * cw[kk:kk + 1, :]
    buf_ref[...] = xp[tm:, :]
    y = _silu(y)
    for i in range(A_HEADS):
        for which, o_ref, gain in ((0, q_ref, dh ** -0.5), (1, k_ref, 1.0)):
            xh = y[:, which * A_WIDTH + i * dh:which * A_WIDTH + (i + 1) * dh]
            inv = lax.rsqrt(jnp.sum(xh * xh, axis=-1, keepdims=True) + EPS) * gain
            o_ref[:, i * dh:(i + 1) * dh] = (xh * inv).astype(BF16)
    v_ref[...] = y[:, 2 * A_WIDTH:].astype(BF16)
    z_ref[...] = jnp.dot(hb, wz_ref[...], preferred_element_type=F32).astype(BF16)
    ba_ref[...] = jnp.dot(hb, wba_ref[...], preferred_element_type=F32)


def _in_proj_a(x, g, scale, shift, wqkv, wz, wba, conv_w):
    bn, t, d = x.shape
    tm = ROW_TILE
    row = lambda b, i: (b, i, 0)
    per_b = lambda b, i: (b, 0, 0)
    const = lambda b, i: (0, 0)
    nba = wba.shape[1]
    wide = pl.BlockSpec((None, tm, A_WIDTH), row)
    wide_sd = jax.ShapeDtypeStruct((bn, t, A_WIDTH), BF16)
    return pl.pallas_call(
        _in_proj_a_kernel,
        grid=(bn, t // tm),
        in_specs=[pl.BlockSpec((None, tm, d), row),
                  pl.BlockSpec((1, d), const),
                  pl.BlockSpec((None, 1, d), per_b),
                  pl.BlockSpec((None, 1, d), per_b),
                  pl.BlockSpec(wqkv.shape, const),
                  pl.BlockSpec(wz.shape, const),
                  pl.BlockSpec(wba.shape, const),
                  pl.BlockSpec(conv_w.shape, const)],
        out_specs=[wide, wide, wide, wide, pl.BlockSpec((None, tm, nba), row)],
        out_shape=[wide_sd, wide_sd, wide_sd, wide_sd, jax.ShapeDtypeStruct((bn, t, nba), F32)],
        scratch_shapes=[pltpu.VMEM((SUBLANES, 3 * A_WIDTH), F32)],
        compiler_params=_cparams("arbitrary", "arbitrary"),
    )(x, g, scale, shift, wqkv, wz, wba, conv_w)


def _cumsum_rows(x):
    n = x.shape[0]
    row = lax.broadcasted_iota(jnp.int32, x.shape, 0)
    s = 1
    while s < n:
        x = x + jnp.where(row >= s, pltpu.roll(x, s, axis=0), 0.0)
        s *= 2
    return x


def _unit_lower_inverse(ms):
    c = ms[0].shape[0]
    row = lax.broadcasted_iota(jnp.int32, (c, c), 0)
    col = lax.broadcasted_iota(jnp.int32, (c, c), 1)
    eye = (row == col).astype(F32)
    same_blk = (row & -BLK16) == (col & -BLK16)
    d = [jnp.where(same_blk, m, 0.0) for m in ms]
    mo = [m - x for m, x in zip(ms, d)]
    d2 = [_dot(x, x) for x in d]
    td = [eye - x for x in d]
    both = [_dot(jnp.concatenate([t, x], axis=0), x) for t, x in zip(td, d2)]
    td = [t + b[:c] for t, b in zip(td, both)]
    d4 = [b[c:] for b in both]
    both = [_dot(jnp.concatenate([t, x], axis=0), x) for t, x in zip(td, d4)]
    td = [t + b[:c] for t, b in zip(td, both)]
    d8 = [b[c:] for b in both]
    td = [t + _dot(t, x) for t, x in zip(td, d8)]
    n = [_dot(t, x) for t, x in zip(td, mo)]
    n2 = [_dot(x, x) for x in n]
    r = [eye - x for x in n]
    r = [a + _dot(a, x) for a, x in zip(r, n2)]
    return [_dot(a, t) for a, t in zip(r, td)]


def _gdn_kernel(q_ref, k_ref, v_ref, z_ref, ba_ref, alog_ref, dtb_ref, ong_ref, o_ref, s_ref, *, hb, nc):
    c = A_CHUNK
    dh = A_HEAD_DIM

    @pl.when(pl.program_id(2) == 0)
    def _():
        s_ref[...] = jnp.zeros(s_ref.shape, F32)

    q_all = q_ref[...]
    k_all = k_ref[...]
    v_all = v_ref[...]
    ba = ba_ref[...]
    beta_t = _sigmoid(ba)
    xa = ba + dtb_ref[...]
    softplus = jnp.maximum(xa, 0.0) + jnp.log(1.0 + jnp.exp(-jnp.abs(xa)))
    g_t = -jnp.exp(alog_ref[...]) * softplus
    gc_t = [_cumsum_rows(g_t[ci * c:(ci + 1) * c, :]) for ci in range(nc)]
    gc_tt = [x.T for x in gc_t]
    egc_t = [jnp.exp(x) for x in gc_t]
    ekd_t = [jnp.exp(x[c - 1:c, :] - x) for x in gc_t]
    egl_t = [jnp.exp(x[c - 1:c, :]) for x in gc_t]

    row = lax.broadcasted_iota(jnp.int32, (c, c), 0)
    col = lax.broadcasted_iota(jnp.int32, (c, c), 1)
    incl = row >= col
    strict = row > col
    heads = range(hb)
    jobs = [(ci, i) for ci in range(nc) for i in heads]
    la = LANES // 2

    def head(x, ci, i):
        return x[ci * c:(ci + 1) * c, i * dh:(i + 1) * dh]

    def lane(xs, ci, i):
        return xs[ci][:, la + i:la + i + 1]

    qnb = [head(q_all, ci, i) for ci, i in jobs]
    knb = [head(k_all, ci, i) for ci, i in jobs]
    qn = [x.astype(F32) for x in qnb]
    kn = [x.astype(F32) for x in knb]
    beta = [beta_t[ci * c:(ci + 1) * c, i:i + 1] for ci, i in jobs]
    kb = [x * y for x, y in zip(kn, beta)]
    decay = [jnp.where(incl, jnp.exp(jnp.where(incl, lane(gc_t, ci, i) - gc_tt[ci][la + i:la + i + 1, :], 0.0)), 0.0)
             for ci, i in jobs]
    kq = [_dot_nt(jnp.concatenate([x.astype(BF16), y], axis=0), z) for x, y, z in zip(kb, qnb, knb)]
    m = [jnp.where(strict, x[:c] * d, 0.0) for x, d in zip(kq, decay)]
    attn = [(x[c:] * d).astype(BF16) for x, d in zip(kq, decay)]
    rhs = [jnp.concatenate([head(v_all, ci, i).astype(F32) * beta[j], kb[j] * lane(egc_t, ci, i)],
                           axis=1).astype(BF16) for j, (ci, i) in enumerate(jobs)]
    qdec = [(qn[j] * lane(egc_t, ci, i)).astype(BF16) for j, (ci, i) in enumerate(jobs)]
    kdec_t = [(kn[j] * lane(ekd_t, ci, i)).T.astype(BF16) for j, (ci, i) in enumerate(jobs)]
    tinv = _unit_lower_inverse(m)
    uw = [_dot(x, y) for x, y in zip(tinv, rhs)]

    s_cur = [s_ref[i] for i in heads]
    for ci in range(nc):
        sb = [x.astype(BF16) for x in s_cur]
        job = [ci * hb + i for i in heads]
        ws = [jnp.dot(jnp.concatenate([uw[j][:, dh:].astype(BF16), qdec[j]], axis=0), sb[i],
                      preferred_element_type=F32) for i, j in zip(heads, job)]
        v_new = [uw[j][:, :dh] - ws[i][:c] for i, j in zip(heads, job)]
        vnb = [x.astype(BF16) for x in v_new]
        av = [jnp.dot(jnp.concatenate([attn[j], kdec_t[j]], axis=0), vnb[i], preferred_element_type=F32)
              for i, j in zip(heads, job)]
        s_cur = [s_cur[i] * lane(egl_t, ci, i) + av[i][c:] for i in heads]
        o = [ws[i][c:] + av[i][:c] for i in heads]
        for i in heads:
            rs = slice(ci * c, (ci + 1) * c)
            sl = slice(i * dh, (i + 1) * dh)
            o_ref[rs, sl] = (_rms(o[i], ong_ref[...]) * _silu(z_ref[rs, sl].astype(F32))).astype(o_ref.dtype)
    for i in heads:
        s_ref[i] = s_cur[i]


def _gdn(q, k, v, z, ba, alog_row, dtb_row, onorm_g):
    bn, t, _ = q.shape
    hb = GDN_HEADS_PER_STEP
    nc = GDN_CHUNKS_PER_STEP
    nhg = A_HEADS // hb
    c = nc * A_CHUNK
    w = hb * A_HEAD_DIM
    kern = functools.partial(_gdn_kernel, hb=hb, nc=nc)
    wide = pl.BlockSpec((None, c, w), lambda b, hg, n: (b, n, hg))
    hg_row = pl.BlockSpec((None, 1, LANES), lambda b, hg, n: (hg, 0, 0))
    return pl.pallas_call(
        kern,
        grid=(bn, nhg, t // c),
        in_specs=[wide, wide, wide, wide,
                  pl.BlockSpec((None, c, LANES), lambda b, hg, n: (b, n, hg)),
                  hg_row, hg_row,
                  pl.BlockSpec((1, A_HEAD_DIM), lambda b, hg, n: (0, 0))],
        out_specs=wide,
        out_shape=jax.ShapeDtypeStruct((bn, t, A_WIDTH), BF16),
        scratch_shapes=[pltpu.VMEM((hb, A_HEAD_DIM, A_HEAD_DIM), F32)],
        compiler_params=_cparams("arbitrary", "arbitrary", "arbitrary"),
    )(q, k, v, z, ba, alog_row, dtb_row, onorm_g)


def _post_a_kernel(x_ref, og_ref, gate_ref, wo_ref, kvg_ref, wkvc_ref, wkvr_ref, g1_ref, scale_ref,
                   shift_ref, wq_ref, wz_ref, wg_ref,
                   x1_ref, kvc_ref, kvr_ref, q_ref, z_ref, gates_ref):
    out = _dot(og_ref[...], wo_ref[...])
    x1 = x_ref[...] + gate_ref[...] * out
    x1_ref[...] = x1
    sb = _rms(x1, kvg_ref[...]).astype(BF16)
    kvc_ref[...] = jnp.dot(sb, wkvc_ref[...], preferred_element_type=F32)
    kvr_ref[...] = jnp.dot(sb, wkvr_ref[...], preferred_element_type=F32).astype(BF16)
    h = _rms(x1, g1_ref[...]) * (1.0 + scale_ref[...]) + shift_ref[...]
    hb = h.astype(BF16)
    q = jnp.dot(hb, wq_ref[...], preferred_element_type=F32) * NSA_Q_SCALE
    q_ref[...] = q.astype(BF16)
    z_ref[...] = jnp.dot(hb, wz_ref[...], preferred_element_type=F32).astype(BF16)
    gates_ref[...] = _sigmoid(jnp.dot(hb, wg_ref[...], preferred_element_type=F32))


def _post_a(x, og, gate0, wo, kvg, wkvc, wkvr, g1, scale1, shift1, wq, wz, wg):
    bn, t, d = x.shape
    tm = POST_ROW_TILE
    row = lambda b, i: (b, i, 0)
    per_b = lambda b, i: (b, 0, 0)
    const = lambda b, i: (0, 0)
    full = lambda a: pl.BlockSpec(a.shape, const)
    vec = pl.BlockSpec((1, d), const)
    bvec = pl.BlockSpec((None, 1, d), per_b)
    outs = [(d, F32), (wkvc.shape[1], F32), (wkvr.shape[1], BF16), (wq.shape[1], BF16),
            (wz.shape[1], BF16), (wg.shape[1], F32)]
    return pl.pallas_call(
        _post_a_kernel,
        grid=(bn, t // tm),
        in_specs=[pl.BlockSpec((None, tm, d), row), pl.BlockSpec((None, tm, A_WIDTH), row), bvec,
                  full(wo), vec, full(wkvc), full(wkvr), vec, bvec, bvec, full(wq), full(wz), full(wg)],
        out_specs=[pl.BlockSpec((None, tm, n), row) for n, _ in outs],
        out_shape=[jax.ShapeDtypeStruct((bn, t, n), dt) for n, dt in outs],
        compiler_params=_cparams("arbitrary", "arbitrary"),
    )(x, og, gate0, wo, kvg, wkvc, wkvr, g1, scale1, shift1, wq, wz, wg)


def _compress_kernel(kc_ref, vc_ref, pos_ref, w1_ref, w2_ref, o_ref):
    dh = B_HEAD_DIM
    ncp = kc_ref.shape[0] // CMP_STRIDE
    njob = 2 * B_GROUPS
    first = [None] * njob
    second = [None] * njob
    for l in range(CMP_STRIDE):
        x = [r[pl.ds(l, ncp, stride=CMP_STRIDE), :] for r in (kc_ref, vc_ref)]
        for j in range(njob):
            kind, g = divmod(j, B_GROUPS)
            xj = x[kind][:, g * dh:(g + 1) * dh]
            for half, acc in ((0, first), (1, second)):
                row = half * CMP_STRIDE + l
                term = _dot(xj + pos_ref[kind, row:row + 1, :], w1_ref[kind, row * dh:(row + 1) * dh, :])
                acc[j] = term if acc[j] is None else acc[j] + term
    for j in range(njob):
        hid = first[j] + pltpu.roll(second[j], ncp - 1, axis=0)
        o_ref[j] = _dot(_silu(hid), w2_ref[j // B_GROUPS]).astype(o_ref.dtype)


def _compress(kvc, pos, w1, w2):
    bn, t, wd = kvc.shape
    njob = wd // B_HEAD_DIM
    ncp = t // CMP_STRIDE
    full = lambda a: pl.BlockSpec(a.shape, lambda b: (0,) * a.ndim)
    return pl.pallas_call(
        _compress_kernel,
        grid=(bn,),
        in_specs=[pl.BlockSpec((None, t, wd // 2), lambda b: (b, 0, 0)),
                  pl.BlockSpec((None, t, wd // 2), lambda b: (b, 0, 1)), full(pos), full(w1), full(w2)],
        out_specs=pl.BlockSpec((None, njob, ncp, B_HEAD_DIM), lambda b: (b, 0, 0, 0)),
        out_shape=jax.ShapeDtypeStruct((bn, njob, ncp, B_HEAD_DIM), BF16),
        compiler_params=_cparams("arbitrary"),
    )(kvc, kvc, pos, w1, w2)


def _t5_bucket_np(dist):
    n = np.maximum(dist, 0)
    max_exact = NUM_BUCKETS // 2
    nf = np.maximum(n, 1).astype(np.float64)
    val = np.log(nf / max_exact) / math.log(MAX_DISTANCE / max_exact) * (NUM_BUCKETS - max_exact)
    frac = np.abs(val - np.round(val))
    safe = (frac > 1e-6) | (n <= max_exact) | (n >= MAX_DISTANCE)
    assert bool(np.all(safe)), "bucket boundary too close to an integer distance"
    large = np.minimum(max_exact + np.floor(np.maximum(val, 0.0)).astype(np.int64), NUM_BUCKETS - 1)
    return np.where(n < max_exact, n, large)


def _bias_onehot():
    far_sel = 2 * L_SLC + 1
    far_cmp = CMP_STRIDE * (CMP_LEAD + 1) - (L_CMP - 1)
    assert _t5_bucket_np(np.array([far_sel, far_cmp])).min() == NUM_BUCKETS - 1
    r = np.arange(NSA_Q_TILE)[:, None]
    tiles = []
    j = np.arange(WINDOW + NSA_Q_TILE)[None, :]
    d = r + WINDOW - j
    tiles.append((d, (d >= 0) & (d < WINDOW)))
    j = np.arange((NSA_SUB + 2) * L_SLC)[None, :]
    d = r + 2 * L_SLC - j
    tiles.append((d, d >= 0))
    j = np.arange(CMP_NEAR)[None, :]
    d = r - CMP_STRIDE * (j - CMP_LEAD) - (L_CMP - 1)
    tiles.append((d, d >= 0))
    cols = [np.where(valid, _t5_bucket_np(d), NUM_BUCKETS).reshape(-1) for d, valid in tiles]
    widths = [c.size for c in cols]
    return np.concatenate(cols).astype(np.int32)[None, :], widths


def _bias_kernel(rb_ref, bk_ref, o_ref):
    rb = rb_ref[...]
    lane = lax.broadcasted_iota(jnp.int32, rb.shape, 1)
    rbs = rb - rb[:, NUM_BUCKETS - 1:NUM_BUCKETS]
    rbs = jnp.where(lane < NUM_BUCKETS, rbs, jnp.where(lane == NUM_BUCKETS, NEG_INF, 0.0))
    bk = bk_ref[...]
    onehot = jnp.where(lax.broadcasted_iota(jnp.int32, (2 * NUM_BUCKETS, bk.shape[1]), 0) == bk, 1.0, 0.0)
    o_ref[...] = _dot_f32(rbs * LOG2E, onehot)


def _bias_tables(rel_bias):
    bk, widths = _bias_onehot()
    ncol = bk.shape[1]
    nt = BIAS_COL_TILES
    assert ncol % (nt * LANES) == 0
    tc = ncol // nt
    rb = jnp.concatenate([rel_bias.T, jnp.zeros((B_HEADS, NUM_BUCKETS), F32)], axis=1)
    flat = pl.pallas_call(
        _bias_kernel,
        grid=(nt,),
        in_specs=[pl.BlockSpec((B_HEADS, 2 * NUM_BUCKETS), lambda i: (0, 0)),
                  pl.BlockSpec((1, tc), lambda i: (0, i))],
        out_specs=pl.BlockSpec((B_HEADS, tc), lambda i: (0, i)),
        out_shape=jax.ShapeDtypeStruct((B_HEADS, ncol), F32),
        compiler_params=_cparams("arbitrary"),
    )(rb, jnp.asarray(bk))
    out, start = [], 0
    for wd in widths:
        tile = flat[:, start:start + wd].reshape(B_GROUPS, B_HPG * NSA_Q_TILE, wd // NSA_Q_TILE)
        out.append(tile)
        start += wd
    return out


def _nsa_kernel(q_ref, kc_ref, vc_ref, slc_ref, win_ref, kst_ref, kwt_ref, vt_ref, tcmp_ref, tsel_ref, twin_ref,
                ov_ref, zc_ref, zs_ref, zw_ref, gates_ref, ex_ref, y_ref,
                ks_ref, vs_ref, kw_ref, vw_ref):
    tq = NSA_Q_TILE
    dh = B_HEAD_DIM
    hpg = B_HPG
    rows = hpg * tq
    ti = pl.program_id(2)

    @pl.when(ti == 0)
    def _():
        ks_ref[...] = kst_ref[...]
        kw_ref[...] = kwt_ref[...]
        vs_ref[...] = vt_ref[...]
        vw_ref[...] = vt_ref[...]
        slc = slc_ref[...]
        win = win_ref[...]
        ks_ref[KV_PAD:, 0:dh] = slc[:, 0:dh]
        vs_ref[KV_PAD:, 0:dh] = slc[:, dh:]
        kw_ref[KV_PAD:, 0:dh] = win[:, 0:dh]
        vw_ref[KV_PAD:, 0:dh] = win[:, dh:]

    q0 = ti * tq
    blk0 = ti * NSA_SUB
    qt = q_ref[...]
    q = jnp.concatenate([qt[:, h * dh:(h + 1) * dh] for h in range(hpg)], axis=0)

    def to_tokens(o):
        return jnp.concatenate([o[h * tq:(h + 1) * tq, :] for h in range(hpg)], axis=1)

    def finish(pv):
        return to_tokens(pv[:, :dh] * (1.0 / pv[:, dh:dh + 1]))

    gt = gates_ref[...]
    g_hi = gt.astype(BF16)
    ghl = jnp.concatenate([g_hi, (gt - g_hi.astype(F32)).astype(BF16)], axis=1)

    def gated(o_tok, br, z_ref):
        gexp = jnp.dot(ghl, ex_ref[br], preferred_element_type=F32)
        return gexp * o_tok * _silu(z_ref[...].astype(F32))

    kc = kc_ref[...]
    ncp = kc.shape[0]
    nw = WINDOW + tq
    win0 = pl.multiple_of(q0, tq)
    first_near = (tq // CMP_STRIDE) * ti - CMP_LEAD
    cid = lax.broadcasted_iota(jnp.int32, (2 * CMP_NEAR, ncp), 1)
    jrow = lax.broadcasted_iota(jnp.int32, (2 * CMP_NEAR, ncp), 0) & (CMP_NEAR - 1)
    shift_eye = jnp.where(cid - first_near == jrow, 1.0, 0.0).astype(BF16)
    pad_col = jnp.where(lax.broadcasted_iota(jnp.int32, (rows, dh), 1) == 0, NEG_INF, 0.0).astype(BF16)
    q_win = jnp.concatenate([q, pad_col], axis=1)

    s = _dot_nt(q, kc) + jnp.dot(tcmp_ref[...], shift_eye, preferred_element_type=F32)
    s_w = _dot_nt(q_win, kw_ref[pl.ds(win0, nw), :]) + twin_ref[...]

    cvis = lax.broadcasted_iota(jnp.int32, (1, ncp), 1) < first_near + CMP_NEAR
    s = jnp.where(cvis, s, NEG_INF)
    live = s > LIVE_THRESHOLD
    m = jnp.max(s, axis=-1, keepdims=True)
    e = jnp.where(live, jnp.exp2(s - m), 0.0)
    p = e * (1.0 / jnp.maximum(jnp.sum(e, axis=-1, keepdims=True), 1e-30))
    y_c = gated(to_tokens(_dot(p, vc_ref[...])), 0, zc_ref)

    psum = p[0:tq, :]
    for h in range(1, hpg):
        psum = psum + p[h * tq:(h + 1) * tq, :]
    p_hi = psum.astype(BF16)
    p_r1 = psum - p_hi.astype(F32)
    p_mid = p_r1.astype(BF16)
    p_lo = (p_r1 - p_mid.astype(F32)).astype(BF16)
    p3 = jnp.concatenate([p_hi, p_mid, p_lo], axis=1)
    imp_t = _dot_nt(ov_ref[...], p3)

    nblk = imp_t.shape[0]
    blk = lax.broadcasted_iota(jnp.int32, (nblk, tq), 0)
    cur = blk0 + (lax.broadcasted_iota(jnp.int32, (nblk, tq), 1) >> SLC_SHIFT)
    forced = (blk == 0) | (blk == cur) | (blk == cur - 1)
    val = jnp.where(forced, SEL_BOOST, jnp.where(blk > cur, -SEL_BOOST, imp_t))
    nslab = nblk // SUBLANES
    slabs = [val[SUBLANES * r:SUBLANES * (r + 1), :] for r in range(nslab)]
    sub = lax.broadcasted_iota(jnp.int32, (SUBLANES, tq), 0)
    n_acc = RANK_PARTIALS
    ranks = [[jnp.zeros((SUBLANES, tq), jnp.int32) for _ in range(n_acc)] for _ in range(nslab)]
    for j in range(nblk):
        vj = jnp.broadcast_to(val[j:j + 1, :], (SUBLANES, tq))
        for r in range(nslab):
            lo = SUBLANES * r
            if lo > j:
                ahead = vj >= slabs[r]
            elif lo + SUBLANES - 1 <= j:
                ahead = vj > slabs[r]
            else:
                ahead = (vj > slabs[r]) | ((vj == slabs[r]) & (sub > j - lo))
            ranks[r][j % n_acc] = ranks[r][j % n_acc] + ahead.astype(jnp.int32)
    rank = jnp.concatenate([functools.reduce(lambda u, v: u + v, a) for a in ranks], axis=0)
    sel_t = (rank < N_SEL) & (blk <= cur)
    far_t = jnp.where(sel_t & (blk <= blk0 - 3), 0.0, NEG_INF)
    near_t = jnp.where(sel_t & (blk >= blk0 - 2), 0.0, NEG_INF)

    def q_with_mask(mask_t):
        mk = mask_t.T.astype(BF16)
        return jnp.concatenate([q, jnp.concatenate([mk] * hpg, axis=0)], axis=1)

    q_far = q_with_mask(far_t)
    q_near = q_with_mask(near_t)

    m_w = jnp.max(s_w, axis=-1, keepdims=True)
    e_w = jnp.exp2(s_w - m_w)
    y_cw = y_c + gated(finish(_dot(e_w, vw_ref[pl.ds(win0, nw), :])), 2, zw_ref)

    kt_sz = 2 * SEL_KEY_TILE
    n_far_keys = jnp.maximum(blk0 - 2, 0) * L_SLC
    n_tiles = (n_far_keys + kt_sz - 1) // kt_sz

    def far_scores(tile):
        start = pl.multiple_of(KV_PAD + tile * kt_sz, kt_sz)
        return _dot_nt(q_far, ks_ref[pl.ds(start, kt_sz), :])

    def far_values(tile):
        start = pl.multiple_of(KV_PAD + tile * kt_sz, kt_sz)
        return vs_ref[pl.ds(start, kt_sz), :]

    def update(carry, s_t, v_t):
        m_i, acc = carry
        m_n = jnp.maximum(m_i, jnp.max(s_t, axis=-1, keepdims=True))
        e_t = jnp.exp2(s_t - m_n)
        return m_n, jnp.exp2(m_i - m_n) * acc + _dot(e_t, v_t)

    def far_step(i, carry):
        return update(carry, far_scores(i), far_values(i))

    carry = (jnp.full((rows, 1), NEG_INF, F32), jnp.zeros((rows, 2 * dh), F32))
    carry = lax.fori_loop(0, n_tiles, far_step, carry)
    nk = (NSA_SUB + 2) * L_SLC
    near0 = pl.multiple_of(KV_PAD + q0 - 2 * L_SLC, L_SLC)
    s_n = _dot_nt(q_near, ks_ref[pl.ds(near0, nk), :]) + tsel_ref[...]
    _, acc = update(carry, s_n, vs_ref[pl.ds(near0, nk), :])
    y_ref[...] = (y_cw + gated(finish(acc), 1, zs_ref)).astype(y_ref.dtype)


def _gate_selectors(ng):
    ex = np.zeros((B_GROUPS, N_BRANCH, 2 * ng, B_HPG * B_HEAD_DIM), np.float32)
    for g in range(B_GROUPS):
        for br in range(N_BRANCH):
            for h in range(B_HPG):
                lane = br * B_HEADS + g * B_HPG + h
                ex[g, br, lane, h * B_HEAD_DIM:(h + 1) * B_HEAD_DIM] = 1.0
                ex[g, br, ng + lane, h * B_HEAD_DIM:(h + 1) * B_HEAD_DIM] = 1.0
    return ex


def _kv_templates(t, nblk):
    tp = KV_PAD + t
    ks_t = np.zeros((tp, 2 * B_HEAD_DIM), np.float32)
    ks_t[:, B_HEAD_DIM:] = _block_onehot(t, nblk)
    kw_t = np.zeros((tp, 2 * B_HEAD_DIM), np.float32)
    kw_t[:KV_PAD, B_HEAD_DIM] = 1.0
    v_t = np.zeros((tp, 2 * B_HEAD_DIM), np.float32)
    v_t[:, B_HEAD_DIM] = 1.0
    return [jnp.asarray(a, BF16) for a in (ks_t, kw_t, v_t)]


def _nsa(q, kcv, kvr, tcmp, tsel, twin, ov, z, gates):
    bn, t, _ = q.shape
    tq = NSA_Q_TILE
    gw = B_HPG * B_HEAD_DIM
    ncp = kcv.shape[2]
    tp = KV_PAD + t
    kvw = 2 * B_HEAD_DIM
    nblk = ov.shape[0]
    ng = gates.shape[2]
    rows = B_HPG * tq
    per_g = lambda b, g, i: (g, 0, 0)
    tmpl = pl.BlockSpec((tp, kvw), lambda b, g, i: (0, 0))
    ex = jnp.asarray(_gate_selectors(ng), BF16)
    kst, kwt, vt = _kv_templates(t, nblk)

    def z_spec(br):
        return pl.BlockSpec((None, tq, gw), lambda b, g, i: (b, i, br * B_GROUPS + g))

    return pl.pallas_call(
        _nsa_kernel,
        grid=(bn, B_GROUPS, t // tq),
        in_specs=[pl.BlockSpec((None, tq, gw), lambda b, g, i: (b, i, g)),
                  pl.BlockSpec((None, None, ncp, B_HEAD_DIM), lambda b, g, i: (b, g, 0, 0)),
                  pl.BlockSpec((None, None, ncp, B_HEAD_DIM), lambda b, g, i: (b, B_GROUPS + g, 0, 0)),
                  pl.BlockSpec((None, t, kvw), lambda b, g, i: (b, 0, g)),
                  pl.BlockSpec((None, t, kvw), lambda b, g, i: (b, 0, B_GROUPS + g)),
                  tmpl, tmpl, tmpl,
                  pl.BlockSpec((None, rows, tcmp.shape[2]), per_g),
                  pl.BlockSpec((None, rows, tsel.shape[2]), per_g),
                  pl.BlockSpec((None, rows, twin.shape[2]), per_g),
                  pl.BlockSpec(ov.shape, lambda b, g, i: (0, 0)),
                  z_spec(0), z_spec(1), z_spec(2),
                  pl.BlockSpec((None, tq, ng), lambda b, g, i: (b, i, 0)),
                  pl.BlockSpec((None,) + ex.shape[1:], lambda b, g, i: (g, 0, 0, 0))],
        out_specs=pl.BlockSpec((None, tq, gw), lambda b, g, i: (b, i, g)),
        out_shape=jax.ShapeDtypeStruct((bn, t, B_WIDTH), BF16),
        scratch_shapes=[pltpu.VMEM((tp, kvw), BF16)] * 4,
        compiler_params=_cparams("arbitrary", "arbitrary", "arbitrary"),
    )(q, kcv, kcv, kvr, kvr, kst, kwt, vt, tcmp, tsel, twin, ov, z, z, z, gates, ex)


def _final_kernel(y_ref, x1_ref, gate_ref, wo_ref, fg_ref, o_ref):
    x2 = x1_ref[...] + gate_ref[...] * jnp.dot(y_ref[...], wo_ref[...], preferred_element_type=F32)
    o_ref[...] = _rms(x2, fg_ref[...])


def _final(y, x1, gate1, wo, fg):
    bn, t, d = x1.shape
    tm = POST_ROW_TILE
    row = lambda b, i: (b, i, 0)
    return pl.pallas_call(
        _final_kernel,
        grid=(bn, t // tm),
        in_specs=[pl.BlockSpec((None, tm, B_WIDTH), row),
                  pl.BlockSpec((None, tm, d), row),
                  pl.BlockSpec((None, 1, d), lambda b, i: (b, 0, 0)),
                  pl.BlockSpec(wo.shape, lambda b, i: (0, 0)),
                  pl.BlockSpec((1, d), lambda b, i: (0, 0))],
        out_specs=pl.BlockSpec((None, tm, d), row),
        out_shape=jax.ShapeDtypeStruct((bn, t, d), F32),
        compiler_params=_cparams("arbitrary", "arbitrary"),
    )(y, x1, gate1, wo, fg)


def _overlap_matrix(ncp, n_cmp, n_slc, nblk):
    cells = np.arange(n_cmp)[:, None] + np.arange(L_CMP // CMP_STRIDE)[None, :]
    ov = (cells[:, None, :] // (L_SLC // CMP_STRIDE) == np.arange(n_slc)[None, :, None]).sum(-1)
    out = np.zeros((ncp, nblk), np.float32)
    out[:n_cmp, :n_slc] = ov
    return out


def _block_onehot(t, nblk):
    oh = np.zeros((KV_PAD + t, nblk), np.float32)
    oh[KV_PAD + np.arange(t), np.arange(t) // L_SLC] = 1.0
    oh[:KV_PAD, nblk - 1] = 1.0
    return oh


def kernel(x, c, rel_bias, ada_w, ada_b, norm_g, a_in_w, a_conv_w, a_A_log, a_dt_bias, a_onorm_g, a_out_w,
           kv_norm_g, kv_w, cmp_pos_k, cmp_pos_v, cmp_k_w1, cmp_k_w2, cmp_v_w1, cmp_v_w2,
           b_in_w, b_out_w, final_g):
    bn, t, d = x.shape
    assert ada_w.shape[0] == 2 and a_in_w.shape[0] == 1 and b_in_w.shape[0] == 1
    assert t % max(ROW_TILE, POST_ROW_TILE, 2 * SEL_KEY_TILE, NSA_Q_TILE) == 0
    n_slc = t // L_SLC
    nblk = SEL_MASK_LANES
    assert n_slc <= nblk
    n_cmp = (t - L_CMP) // CMP_STRIDE + 1
    ncp = t // CMP_STRIDE

    mod = _ada_modulation(c, ada_w, ada_b)
    shift = mod[:, :, None, :d]
    scale = mod[:, :, None, d:2 * d]
    gate = mod[:, :, None, 2 * d:]

    hb = GDN_HEADS_PER_STEP
    nhg = A_HEADS // hb
    w_in = a_in_w[0]
    wqkv = w_in[:, :3 * A_WIDTH].astype(BF16)
    wz = w_in[:, 3 * A_WIDTH:4 * A_WIDTH].astype(BF16)
    wb = w_in[:, 4 * A_WIDTH:4 * A_WIDTH + A_HEADS]
    wa = w_in[:, 4 * A_WIDTH + A_HEADS:]
    half = LANES // 2
    wba = jnp.zeros((d, nhg, LANES), F32)
    wba = wba.at[:, :, :hb].set(wb.reshape(d, nhg, hb)).at[:, :, half:half + hb].set(wa.reshape(d, nhg, hb))
    wba = wba.reshape(d, nhg * LANES).astype(BF16)
    lane_rows = lambda v: jnp.zeros((nhg, 1, LANES), F32).at[:, 0, half:half + hb].set(v.reshape(nhg, hb))
    q_a, k_a, v_a, z_a, ba = _in_proj_a(x, norm_g[0:1], scale[0], shift[0], wqkv, wz, wba, a_conv_w[0])
    og = _gdn(q_a, k_a, v_a, z_a, ba, lane_rows(a_A_log[0]), lane_rows(a_dt_bias[0]), a_onorm_g[0:1])

    ndh = B_GROUPS * B_HEAD_DIM
    wkvc = kv_w[:, :2 * ndh].astype(BF16)
    kv_rest = kv_w[:, 2 * ndh:].reshape(d, 4, B_GROUPS, B_HEAD_DIM)
    wkvr = jnp.concatenate([jnp.stack([kv_rest[:, 0], kv_rest[:, 1]], axis=2),
                            jnp.stack([kv_rest[:, 2], kv_rest[:, 3]], axis=2)], axis=1)
    wkvr = wkvr.reshape(d, 4 * ndh).astype(BF16)
    w_b = b_in_w[0]
    wq = w_b[:, :B_WIDTH].astype(BF16)
    wzb = w_b[:, B_WIDTH:4 * B_WIDTH].astype(BF16)
    wg = jnp.zeros((d, LANES), F32).at[:, :N_BRANCH * B_HEADS].set(w_b[:, 4 * B_WIDTH:]).astype(BF16)
    x1, kvc, kvr, q, z_b, gates = _post_a(x, og, gate[0], a_out_w[0].astype(BF16), kv_norm_g[None, :], wkvc, wkvr,
                                          norm_g[1:2], scale[1], shift[1], wq, wzb, wg)

    pos = jnp.stack([cmp_pos_k, cmp_pos_v])
    w1 = jnp.stack([cmp_k_w1, cmp_v_w1]).astype(BF16)
    w2 = jnp.stack([cmp_k_w2, cmp_v_w2]).astype(BF16)
    kcv = _compress(kvc, pos, w1, w2)

    twin, tsel, tcmp = _bias_tables(rel_bias)
    tc_hi = tcmp.astype(BF16)
    tc_lo = (tcmp - tc_hi.astype(F32)).astype(BF16)
    tcmp2 = jnp.concatenate([tc_hi, tc_lo], axis=-1)
    ov_t = _overlap_matrix(ncp, n_cmp, n_slc, nblk).T
    ov3 = jnp.asarray(np.concatenate([ov_t] * 3, axis=1), BF16)

    y = _nsa(q, kcv, kvr, tcmp2, tsel, twin, ov3, z_b, gates)

    return _final(y, x1, gate[1], b_out_w[0].astype(BF16), final_g[None, :])
```

```python
import functools
import math

import numpy as np
import jax
import jax.numpy as jnp
from jax import lax
from jax.experimental import pallas as pl
from jax.experimental.pallas import tpu as pltpu

F32 = jnp.float32
BF16 = jnp.bfloat16
HIGHEST = lax.Precision.HIGHEST

A_HEADS = 8
A_HEAD_DIM = 128
A_WIDTH = A_HEADS * A_HEAD_DIM
A_CONV = 4
A_CHUNK = 64
B_HEADS = 16
B_GROUPS = 2
B_HPG = B_HEADS // B_GROUPS
B_HEAD_DIM = 64
B_WIDTH = B_HEADS * B_HEAD_DIM
N_BRANCH = 3
L_CMP = 32
CMP_STRIDE = 16
L_SLC = 64
N_SEL = 16
WINDOW = 512
Q_BLOCK = 64
NUM_BUCKETS = 32
MAX_DISTANCE = 128
EPS = 1e-6
NEG_INF = -1e30
SEL_BOOST = 1e9
LOG2E = math.log2(math.e)
NSA_Q_SCALE = B_HEAD_DIM ** -0.5 * LOG2E

LANES = 128
SUBLANES = 8
VMEM_LIMIT_BYTES = 56 * 1024 * 1024

ROW_TILE = 512
POST_ROW_TILE = 512
GDN_HEADS_PER_STEP = 8
GDN_CHUNKS_PER_STEP = 4
SEL_KEY_TILE = 512
KV_PAD = WINDOW
NSA_Q_TILE = 128
NSA_SUB = NSA_Q_TILE // Q_BLOCK
SLC_SHIFT = L_SLC.bit_length() - 1
assert 1 << SLC_SHIFT == L_SLC and L_SLC == Q_BLOCK
CMP_LEAD = 12
CMP_NEAR = 32
assert CMP_NEAR >= CMP_LEAD + NSA_Q_TILE // CMP_STRIDE and NSA_Q_TILE % Q_BLOCK == 0
BLK16 = 16
SEL_MASK_LANES = B_HEAD_DIM
BIAS_COL_TILES = 8
RANK_PARTIALS = 4
LIVE_THRESHOLD = 0.1 * NEG_INF


def _cparams(*sem):
    return pltpu.CompilerParams(dimension_semantics=sem, vmem_limit_bytes=VMEM_LIMIT_BYTES)


def _sigmoid(x):
    return 1.0 / (1.0 + jnp.exp(-x))


def _silu(x):
    return x * _sigmoid(x)


def _dot(a, b):
    return jnp.dot(a.astype(BF16), b.astype(BF16), preferred_element_type=F32)


def _dot_nt(a, b):
    return lax.dot_general(a.astype(BF16), b.astype(BF16), (((1,), (1,)), ((), ())),
                           preferred_element_type=F32)


def _dot_f32(a, b):
    return jnp.dot(a, b, precision=HIGHEST, preferred_element_type=F32)


def _rms(x, g):
    ms = jnp.mean(x * x, axis=-1, keepdims=True)
    return x * lax.rsqrt(ms + EPS) * g


def _ada_kernel(c_ref, w_ref, b_ref, o_ref):
    o_ref[...] = _dot_f32(_silu(c_ref[...]), w_ref[...]) + b_ref[...]


def _ada_modulation(c, ada_w, ada_b):
    depth, d, d3 = ada_w.shape
    bn = c.shape[0]
    return pl.pallas_call(
        _ada_kernel,
        grid=(depth, d3 // d),
        in_specs=[pl.BlockSpec((bn, d), lambda l, j: (0, 0)),
                  pl.BlockSpec((None, d, d), lambda l, j: (l, 0, j)),
                  pl.BlockSpec((None, 1, d), lambda l, j: (l, 0, j))],
        out_specs=pl.BlockSpec((None, bn, d), lambda l, j: (l, 0, j)),
        out_shape=jax.ShapeDtypeStruct((depth, bn, d3), F32),
        compiler_params=_cparams("arbitrary", "arbitrary"),
    )(c, ada_w, ada_b.reshape(depth, 1, d3))


def _in_proj_a_kernel(x_ref, g_ref, scale_ref, shift_ref, wqkv_ref, wz_ref, wba_ref, cw_ref,
                      q_ref, k_ref, v_ref, z_ref, ba_ref, buf_ref):
    tm = x_ref.shape[0]
    halo = SUBLANES
    dh = A_HEAD_DIM

    @pl.when(pl.program_id(1) == 0)
    def _():
        buf_ref[...] = jnp.zeros(buf_ref.shape, F32)

    h = _rms(x_ref[...], g_ref[...]) * (1.0 + scale_ref[...]) + shift_ref[...]
    hb = h.astype(BF16)
    xp = jnp.concatenate([buf_ref[...], jnp.dot(hb, wqkv_ref[...], preferred_element_type=F32)], axis=0)
    cw = cw_ref[...]
    y = xp[halo:, :] * cw[A_CONV - 1:A_CONV, :]
    for kk in range(A_CONV - 1):
        y = y + pltpu.roll(xp, A_CONV - 1 - kk, axis=0)[halo:, :] * cw[kk:kk + 1, :]
    buf_ref[...] = xp[tm:, :]
    y = _silu(y)
    for i in range(A_HEADS):
        for which, o_ref, gain in ((0, q_ref, dh ** -0.5), (1, k_ref, 1.0)):
            xh = y[:, which * A_WIDTH + i * dh:which * A_WIDTH + (i + 1) * dh]
            inv = lax.rsqrt(jnp.sum(xh * xh, axis=-1, keepdims=True) + EPS) * gain
            o_ref[:, i * dh:(i + 1) * dh] = (xh * inv).astype(BF16)
    v_ref[...] = y[:, 2 * A_WIDTH:].astype(BF16)
    z_ref[...] = jnp.dot(hb, wz_ref[...], preferred_element_type=F32).astype(BF16)
    ba_ref[...] = jnp.dot(hb, wba_ref[...], preferred_element_type=F32)


def _in_proj_a(x, g, scale, shift, wqkv, wz, wba, conv_w):
    bn, t, d = x.shape
    tm = ROW_TILE
    row = lambda b, i: (b, i, 0)
    per_b = lambda b, i: (b, 0, 0)
    const = lambda b, i: (0, 0)
    nba = wba.shape[1]
    wide = pl.BlockSpec((None, tm, A_WIDTH), row)
    wide_sd = jax.ShapeDtypeStruct((bn, t, A_WIDTH), BF16)
    return pl.pallas_call(
        _in_proj_a_kernel,
        grid=(bn, t // tm),
        in_specs=[pl.BlockSpec((None, tm, d), row),
                  pl.BlockSpec((1, d), const),
                  pl.BlockSpec((None, 1, d), per_b),
                  pl.BlockSpec((None, 1, d), per_b),
                  pl.BlockSpec(wqkv.shape, const),
                  pl.BlockSpec(wz.shape, const),
                  pl.BlockSpec(wba.shape, const),
                  pl.BlockSpec(conv_w.shape, const)],
        out_specs=[wide, wide, wide, wide, pl.BlockSpec((None, tm, nba), row)],
        out_shape=[wide_sd, wide_sd, wide_sd, wide_sd, jax.ShapeDtypeStruct((bn, t, nba), F32)],
        scratch_shapes=[pltpu.VMEM((SUBLANES, 3 * A_WIDTH), F32)],
        compiler_params=_cparams("arbitrary", "arbitrary"),
    )(x, g, scale, shift, wqkv, wz, wba, conv_w)


def _cumsum_rows(x):
    n = x.shape[0]
    row = lax.broadcasted_iota(jnp.int32, x.shape, 0)
    s = 1
    while s < n:
        x = x + jnp.where(row >= s, pltpu.roll(x, s, axis=0), 0.0)
        s *= 2
    return x


def _unit_lower_inverse(ms):
    c = ms[0].shape[0]
    row = lax.broadcasted_iota(jnp.int32, (c, c), 0)
    col = lax.broadcasted_iota(jnp.int32, (c, c), 1)
    eye = (row == col).astype(F32)
    same_blk = (row & -BLK16) == (col & -BLK16)
    d = [jnp.where(same_blk, m, 0.0) for m in ms]
    mo = [m - x for m, x in zip(ms, d)]
    d2 = [_dot(x, x) for x in d]
    td = [eye - x for x in d]
    both = [_dot(jnp.concatenate([t, x], axis=0), x) for t, x in zip(td, d2)]
    td = [t + b[:c] for t, b in zip(td, both)]
    d4 = [b[c:] for b in both]
    both = [_dot(jnp.concatenate([t, x], axis=0), x) for t, x in zip(td, d4)]
    td = [t + b[:c] for t, b in zip(td, both)]
    d8 = [b[c:] for b in both]
    td = [t + _dot(t, x) for t, x in zip(td, d8)]
    n = [_dot(t, x) for t, x in zip(td, mo)]
    n2 = [_dot(x, x) for x in n]
    r = [eye - x for x in n]
    r = [a + _dot(a, x) for a, x in zip(r, n2)]
    return [_dot(a, t) for a, t in zip(r, td)]


def _gdn_kernel(q_ref, k_ref, v_ref, z_ref, ba_ref, alog_ref, dtb_ref, ong_ref, o_ref, s_ref, *, hb, nc):
    c = A_CHUNK
    dh = A_HEAD_DIM

    @pl.when(pl.program_id(2) == 0)
    def _():
        s_ref[...] = jnp.zeros(s_ref.shape, F32)

    q_all = q_ref[...]
    k_all = k_ref[...]
    v_all = v_ref[...]
    ba = ba_ref[...]
    beta_t = _sigmoid(ba)
    xa = ba + dtb_ref[...]
    softplus = jnp.maximum(xa, 0.0) + jnp.log(1.0 + jnp.exp(-jnp.abs(xa)))
    g_t = -jnp.exp(alog_ref[...]) * softplus
    gc_t = [_cumsum_rows(g_t[ci * c:(ci + 1) * c, :]) for ci in range(nc)]
    gc_tt = [x.T for x in gc_t]
    egc_t = [jnp.exp(x) for x in gc_t]
    ekd_t = [jnp.exp(x[c - 1:c, :] - x) for x in gc_t]
    egl_t = [jnp.exp(x[c - 1:c, :]) for x in gc_t]

    row = lax.broadcasted_iota(jnp.int32, (c, c), 0)
    col = lax.broadcasted_iota(jnp.int32, (c, c), 1)
    incl = row >= col
    strict = row > col
    heads = range(hb)
    jobs = [(ci, i) for ci in range(nc) for i in heads]
    la = LANES // 2

    def head(x, ci, i):
        return x[ci * c:(ci + 1) * c, i * dh:(i + 1) * dh]

    def lane(xs, ci, i):
        return xs[ci][:, la + i:la + i + 1]

    qnb = [head(q_all, ci, i) for ci, i in jobs]
    knb = [head(k_all, ci, i) for ci, i in jobs]
    qn = [x.astype(F32) for x in qnb]
    kn = [x.astype(F32) for x in knb]
    beta = [beta_t[ci * c:(ci + 1) * c, i:i + 1] for ci, i in jobs]
    kb = [x * y for x, y in zip(kn, beta)]
    decay = [jnp.where(incl, jnp.exp(jnp.where(incl, lane(gc_t, ci, i) - gc_tt[ci][la + i:la + i + 1, :], 0.0)), 0.0)
             for ci, i in jobs]
    kq = [_dot_nt(jnp.concatenate([x.astype(BF16), y], axis=0), z) for x, y, z in zip(kb, qnb, knb)]
    m = [jnp.where(strict, x[:c] * d, 0.0) for x, d in zip(kq, decay)]
    attn = [(x[c:] * d).astype(BF16) for x, d in zip(kq, decay)]
    rhs = [jnp.concatenate([head(v_all, ci, i).astype(F32) * beta[j], kb[j] * lane(egc_t, ci, i)],
                           axis=1).astype(BF16) for j, (ci, i) in enumerate(jobs)]
    qdec = [(qn[j] * lane(egc_t, ci, i)).astype(BF16) for j, (ci, i) in enumerate(jobs)]
    kdec_t = [(kn[j] * lane(ekd_t, ci, i)).T.astype(BF16) for j, (ci, i) in enumerate(jobs)]
    tinv = _unit_lower_inverse(m)
    uw = [_dot(x, y) for x, y in zip(tinv, rhs)]

    s_cur = [s_ref[i] for i in heads]
    for ci in range(nc):
        sb = [x.astype(BF16) for x in s_cur]
        job = [ci * hb + i for i in heads]
        ws = [jnp.dot(jnp.concatenate([uw[j][:, dh:].astype(BF16), qdec[j]], axis=0), sb[i],
                      preferred_element_type=F32) for i, j in zip(heads, job)]
        v_new = [uw[j][:, :dh] - ws[i][:c] for i, j in zip(heads, job)]
        vnb = [x.astype(BF16) for x in v_new]
        av = [jnp.dot(jnp.concatenate([attn[j], kdec_t[j]], axis=0), vnb[i], preferred_element_type=F32)
              for i, j in zip(heads, job)]
        s_cur = [s_cur[i] * lane(egl_t, ci, i) + av[i][c:] for i in heads]
        o = [ws[i][c:] + av[i][:c] for i in heads]
        for i in heads:
            rs = slice(ci * c, (ci + 1) * c)
            sl = slice(i * dh, (i + 1) * dh)
            o_ref[rs, sl] = (_rms(o[i], ong_ref[...]) * _silu(z_ref[rs, sl].astype(F32))).astype(o_ref.dtype)
    for i in heads:
        s_ref[i] = s_cur[i]


def _gdn(q, k, v, z, ba, alog_row, dtb_row, onorm_g):
    bn, t, _ = q.shape
    hb = GDN_HEADS_PER_STEP
    nc = GDN_CHUNKS_PER_STEP
    nhg = A_HEADS // hb
    c = nc * A_CHUNK
    w = hb * A_HEAD_DIM
    kern = functools.partial(_gdn_kernel, hb=hb, nc=nc)
    wide = pl.BlockSpec((None, c, w), lambda b, hg, n: (b, n, hg))
    hg_row = pl.BlockSpec((None, 1, LANES), lambda b, hg, n: (hg, 0, 0))
    return pl.pallas_call(
        kern,
        grid=(bn, nhg, t // c),
        in_specs=[wide, wide, wide, wide,
                  pl.BlockSpec((None, c, LANES), lambda b, hg, n: (b, n, hg)),
                  hg_row, hg_row,
                  pl.BlockSpec((1, A_HEAD_DIM), lambda b, hg, n: (0, 0))],
        out_specs=wide,
        out_shape=jax.ShapeDtypeStruct((bn, t, A_WIDTH), BF16),
        scratch_shapes=[pltpu.VMEM((hb, A_HEAD_DIM, A_HEAD_DIM), F32)],
        compiler_params=_cparams("arbitrary", "arbitrary", "arbitrary"),
    )(q, k, v, z, ba, alog_row, dtb_row, onorm_g)


def _post_a_kernel(x_ref, og_ref, gate_ref, wo_ref, kvg_ref, wkvc_ref, wkvr_ref, g1_ref, scale_ref,
                   shift_ref, wq_ref, wz_ref, wg_ref,
                   x1_ref, kvc_ref, kvr_ref, q_ref, z_ref, gates_ref):
    out = _dot(og_ref[...], wo_ref[...])
    x1 = x_ref[...] + gate_ref[...] * out
    x1_ref[...] = x1
    sb = _rms(x1, kvg_ref[...]).astype(BF16)
    kvc_ref[...] = jnp.dot(sb, wkvc_ref[...], preferred_element_type=F32)
    kvr_ref[...] = jnp.dot(sb, wkvr_ref[...], preferred_element_type=F32).astype(BF16)
    h = _rms(x1, g1_ref[...]) * (1.0 + scale_ref[...]) + shift_ref[...]
    hb = h.astype(BF16)
    q = jnp.dot(hb, wq_ref[...], preferred_element_type=F32) * NSA_Q_SCALE
    q_ref[...] = q.astype(BF16)
    z_ref[...] = jnp.dot(hb, wz_ref[...], preferred_element_type=F32).astype(BF16)
    gates_ref[...] = _sigmoid(jnp.dot(hb, wg_ref[...], preferred_element_type=F32))


def _post_a(x, og, gate0, wo, kvg, wkvc, wkvr, g1, scale1, shift1, wq, wz, wg):
    bn, t, d = x.shape
    tm = POST_ROW_TILE
    row = lambda b, i: (b, i, 0)
    per_b = lambda b, i: (b, 0, 0)
    const = lambda b, i: (0, 0)
    full = lambda a: pl.BlockSpec(a.shape, const)
    vec = pl.BlockSpec((1, d), const)
    bvec = pl.BlockSpec((None, 1, d), per_b)
    outs = [(d, F32), (wkvc.shape[1], F32), (wkvr.shape[1], BF16), (wq.shape[1], BF16),
            (wz.shape[1], BF16), (wg.shape[1], F32)]
    return pl.pallas_call(
        _post_a_kernel,
        grid=(bn, t // tm),
        in_specs=[pl.BlockSpec((None, tm, d), row), pl.BlockSpec((None, tm, A_WIDTH), row), bvec,
                  full(wo), vec, full(wkvc), full(wkvr), vec, bvec, bvec, full(wq), full(wz), full(wg)],
        out_specs=[pl.BlockSpec((None, tm, n), row) for n, _ in outs],
        out_shape=[jax.ShapeDtypeStruct((bn, t, n), dt) for n, dt in outs],
        compiler_params=_cparams("arbitrary", "arbitrary"),
    )(x, og, gate0, wo, kvg, wkvc, wkvr, g1, scale1, shift1, wq, wz, wg)


def _compress_kernel(kc_ref, vc_ref, pos_ref, w1_ref, w2_ref, o_ref):
    dh = B_HEAD_DIM
    ncp = kc_ref.shape[0] // CMP_STRIDE
    njob = 2 * B_GROUPS
    first = [None] * njob
    second = [None] * njob
    for l in range(CMP_STRIDE):
        x = [r[pl.ds(l, ncp, stride=CMP_STRIDE), :] for r in (kc_ref, vc_ref)]
        for j in range(njob):
            kind, g = divmod(j, B_GROUPS)
            xj = x[kind][:, g * dh:(g + 1) * dh]
            for half, acc in ((0, first), (1, second)):
                row = half * CMP_STRIDE + l
                term = _dot(xj + pos_ref[kind, row:row + 1, :], w1_ref[kind, row * dh:(row + 1) * dh, :])
                acc[j] = term if acc[j] is None else acc[j] + term
    for j in range(njob):
        hid = first[j] + pltpu.roll(second[j], ncp - 1, axis=0)
        o_ref[j] = _dot(_silu(hid), w2_ref[j // B_GROUPS]).astype(o_ref.dtype)


def _compress(kvc, pos, w1, w2):
    bn, t, wd = kvc.shape
    njob = wd // B_HEAD_DIM
    ncp = t // CMP_STRIDE
    full = lambda a: pl.BlockSpec(a.shape, lambda b: (0,) * a.ndim)
    return pl.pallas_call(
        _compress_kernel,
        grid=(bn,),
        in_specs=[pl.BlockSpec((None, t, wd // 2), lambda b: (b, 0, 0)),
                  pl.BlockSpec((None, t, wd // 2), lambda b: (b, 0, 1)), full(pos), full(w1), full(w2)],
        out_specs=pl.BlockSpec((None, njob, ncp, B_HEAD_DIM), lambda b: (b, 0, 0, 0)),
        out_shape=jax.ShapeDtypeStruct((bn, njob, ncp, B_HEAD_DIM), BF16),
        compiler_params=_cparams("arbitrary"),
    )(kvc, kvc, pos, w1, w2)


def _t5_bucket_np(dist):
    n = np.maximum(dist, 0)
    max_exact = NUM_BUCKETS // 2
    nf = np.maximum(n, 1).astype(np.float64)
    val = np.log(nf / max_exact) / math.log(MAX_DISTANCE / max_exact) * (NUM_BUCKETS - max_exact)
    frac = np.abs(val - np.round(val))
    safe = (frac > 1e-6) | (n <= max_exact) | (n >= MAX_DISTANCE)
    assert bool(np.all(safe)), "bucket boundary too close to an integer distance"
    large = np.minimum(max_exact + np.floor(np.maximum(val, 0.0)).astype(np.int64), NUM_BUCKETS - 1)
    return np.where(n < max_exact, n, large)


def _bias_onehot():
    far_sel = 2 * L_SLC + 1
    far_cmp = CMP_STRIDE * (CMP_LEAD + 1) - (L_CMP - 1)
    assert _t5_bucket_np(np.array([far_sel, far_cmp])).min() == NUM_BUCKETS - 1
    r = np.arange(NSA_Q_TILE)[:, None]
    tiles = []
    j = np.arange(WINDOW + NSA_Q_TILE)[None, :]
    d = r + WINDOW - j
    tiles.append((d, (d >= 0) & (d < WINDOW)))
    j = np.arange((NSA_SUB + 2) * L_SLC)[None, :]
    d = r + 2 * L_SLC - j
    tiles.append((d, d >= 0))
    j = np.arange(CMP_NEAR)[None, :]
    d = r - CMP_STRIDE * (j - CMP_LEAD) - (L_CMP - 1)
    tiles.append((d, d >= 0))
    cols = [np.where(valid, _t5_bucket_np(d), NUM_BUCKETS).reshape(-1) for d, valid in tiles]
    widths = [c.size for c in cols]
    return np.concatenate(cols).astype(np.int32)[None, :], widths


def _bias_kernel(rb_ref, bk_ref, o_ref):
    rb = rb_ref[...]
    lane = lax.broadcasted_iota(jnp.int32, rb.shape, 1)
    rbs = rb - rb[:, NUM_BUCKETS - 1:NUM_BUCKETS]
    rbs = jnp.where(lane < NUM_BUCKETS, rbs, jnp.where(lane == NUM_BUCKETS, NEG_INF, 0.0))
    bk = bk_ref[...]
    onehot = jnp.where(lax.broadcasted_iota(jnp.int32, (2 * NUM_BUCKETS, bk.shape[1]), 0) == bk, 1.0, 0.0)
    o_ref[...] = _dot_f32(rbs * LOG2E, onehot)


def _bias_tables(rel_bias):
    bk, widths = _bias_onehot()
    ncol = bk.shape[1]
    nt = BIAS_COL_TILES
    assert ncol % (nt * LANES) == 0
    tc = ncol // nt
    rb = jnp.concatenate([rel_bias.T, jnp.zeros((B_HEADS, NUM_BUCKETS), F32)], axis=1)
    flat = pl.pallas_call(
        _bias_kernel,
        grid=(nt,),
        in_specs=[pl.BlockSpec((B_HEADS, 2 * NUM_BUCKETS), lambda i: (0, 0)),
                  pl.BlockSpec((1, tc), lambda i: (0, i))],
        out_specs=pl.BlockSpec((B_HEADS, tc), lambda i: (0, i)),
        out_shape=jax.ShapeDtypeStruct((B_HEADS, ncol), F32),
        compiler_params=_cparams("arbitrary"),
    )(rb, jnp.asarray(bk))
    out, start = [], 0
    for wd in widths:
        tile = flat[:, start:start + wd].reshape(B_GROUPS, B_HPG * NSA_Q_TILE, wd // NSA_Q_TILE)
        out.append(tile)
        start += wd
    return out


def _nsa_kernel(q_ref, kc_ref, vc_ref, slc_ref, win_ref, kst_ref, kwt_ref, vt_ref, tcmp_ref, tsel_ref, twin_ref,
                ov_ref, zc_ref, zs_ref, zw_ref, gates_ref, ex_ref, y_ref,
                sa_ref, sb_ref, ks_ref, vs_ref, kw_ref, vw_ref):
    tq = NSA_Q_TILE
    dh = B_HEAD_DIM
    hpg = B_HPG
    rows = hpg * tq
    ti = pl.program_id(2)

    @pl.when(ti == 0)
    def _():
        ks_ref[...] = kst_ref[...]
        kw_ref[...] = kwt_ref[...]
        vs_ref[...] = vt_ref[...]
        vw_ref[...] = vt_ref[...]
        slc = slc_ref[...]
        win = win_ref[...]
        ks_ref[KV_PAD:, 0:dh] = slc[:, 0:dh]
        vs_ref[KV_PAD:, 0:dh] = slc[:, dh:]
        kw_ref[KV_PAD:, 0:dh] = win[:, 0:dh]
        vw_ref[KV_PAD:, 0:dh] = win[:, dh:]

    q0 = ti * tq
    blk0 = ti * NSA_SUB
    qt = q_ref[...]
    q = jnp.concatenate([qt[:, h * dh:(h + 1) * dh] for h in range(hpg)], axis=0)

    def to_tokens(o):
        return jnp.concatenate([o[h * tq:(h + 1) * tq, :] for h in range(hpg)], axis=1)

    def finish(pv):
        return to_tokens(pv[:, :dh] * (1.0 / pv[:, dh:dh + 1]))

    gt = gates_ref[...]
    g_hi = gt.astype(BF16)
    ghl = jnp.concatenate([g_hi, (gt - g_hi.astype(F32)).astype(BF16)], axis=1)

    def gated(o_tok, br, z_ref):
        gexp = jnp.dot(ghl, ex_ref[br], preferred_element_type=F32)
        return gexp * o_tok * _silu(z_ref[...].astype(F32))

    kc = kc_ref[...]
    ncp = kc.shape[0]
    nw = WINDOW + tq
    win0 = pl.multiple_of(q0, tq)
    first_near = (tq // CMP_STRIDE) * ti - CMP_LEAD
    cid = lax.broadcasted_iota(jnp.int32, (2 * CMP_NEAR, ncp), 1)
    jrow = lax.broadcasted_iota(jnp.int32, (2 * CMP_NEAR, ncp), 0) & (CMP_NEAR - 1)
    shift_eye = jnp.where(cid - first_near == jrow, 1.0, 0.0).astype(BF16)
    pad_col = jnp.where(lax.broadcasted_iota(jnp.int32, (rows, dh), 1) == 0, NEG_INF, 0.0).astype(BF16)
    q_win = jnp.concatenate([q, pad_col], axis=1)

    s = _dot_nt(q, kc) + jnp.dot(tcmp_ref[...], shift_eye, preferred_element_type=F32)
    s_w = _dot_nt(q_win, kw_ref[pl.ds(win0, nw), :]) + twin_ref[...]

    cvis = lax.broadcasted_iota(jnp.int32, (1, ncp), 1) < first_near + CMP_NEAR
    s = jnp.where(cvis, s, NEG_INF)
    live = s > LIVE_THRESHOLD
    m = jnp.max(s, axis=-1, keepdims=True)
    e = jnp.where(live, jnp.exp2(s - m), 0.0)
    p = e * (1.0 / jnp.maximum(jnp.sum(e, axis=-1, keepdims=True), 1e-30))
    y_c = gated(to_tokens(_dot(p, vc_ref[...])), 0, zc_ref)

    psum = p[0:tq, :]
    for h in range(1, hpg):
        psum = psum + p[h * tq:(h + 1) * tq, :]
    p_hi = psum.astype(BF16)
    p_r1 = psum - p_hi.astype(F32)
    p_mid = p_r1.astype(BF16)
    p_lo = (p_r1 - p_mid.astype(F32)).astype(BF16)
    p3 = jnp.concatenate([p_hi, p_mid, p_lo], axis=1)
    imp_t = _dot_nt(ov_ref[...], p3)

    nblk = imp_t.shape[0]
    blk = lax.broadcasted_iota(jnp.int32, (nblk, tq), 0)
    cur = blk0 + (lax.broadcasted_iota(jnp.int32, (nblk, tq), 1) >> SLC_SHIFT)
    forced = (blk == 0) | (blk == cur) | (blk == cur - 1)
    val = jnp.where(forced, SEL_BOOST, jnp.where(blk > cur, -SEL_BOOST, imp_t))
    nslab = nblk // SUBLANES
    slabs = [val[SUBLANES * r:SUBLANES * (r + 1), :] for r in range(nslab)]
    sub = lax.broadcasted_iota(jnp.int32, (SUBLANES, tq), 0)
    n_acc = RANK_PARTIALS
    ranks = [[jnp.zeros((SUBLANES, tq), jnp.int32) for _ in range(n_acc)] for _ in range(nslab)]
    for j in range(nblk):
        vj = jnp.broadcast_to(val[j:j + 1, :], (SUBLANES, tq))
        for r in range(nslab):
            lo = SUBLANES * r
            if lo > j:
                ahead = vj >= slabs[r]
            elif lo + SUBLANES - 1 <= j:
                ahead = vj > slabs[r]
            else:
                ahead = (vj > slabs[r]) | ((vj == slabs[r]) & (sub > j - lo))
            ranks[r][j % n_acc] = ranks[r][j % n_acc] + ahead.astype(jnp.int32)
    rank = jnp.concatenate([functools.reduce(lambda u, v: u + v, a) for a in ranks], axis=0)
    sel_t = (rank < N_SEL) & (blk <= cur)
    far_t = jnp.where(sel_t & (blk <= blk0 - 3), 0.0, NEG_INF)
    near_t = jnp.where(sel_t & (blk >= blk0 - 2), 0.0, NEG_INF)

    def q_with_mask(mask_t):
        mk = mask_t.T.astype(BF16)
        return jnp.concatenate([q, jnp.concatenate([mk] * hpg, axis=0)], axis=1)

    q_far = q_with_mask(far_t)
    q_near = q_with_mask(near_t)

    m_w = jnp.max(s_w, axis=-1, keepdims=True)
    e_w = jnp.exp2(s_w - m_w)
    y_cw = y_c + gated(finish(_dot(e_w, vw_ref[pl.ds(win0, nw), :])), 2, zw_ref)

    kt_sz = SEL_KEY_TILE
    n_far_keys = jnp.maximum(blk0 - 2, 0) * L_SLC
    n_pairs = (n_far_keys + 2 * kt_sz - 1) // (2 * kt_sz)

    def far_scores(tile):
        start = pl.multiple_of(KV_PAD + tile * kt_sz, kt_sz)
        return _dot_nt(q_far, ks_ref[pl.ds(start, kt_sz), :])

    def far_values(tile):
        start = pl.multiple_of(KV_PAD + tile * kt_sz, kt_sz)
        return vs_ref[pl.ds(start, kt_sz), :]

    def update(carry, s_t, v_t):
        m_i, acc = carry
        m_n = jnp.maximum(m_i, jnp.max(s_t, axis=-1, keepdims=True))
        e_t = jnp.exp2(s_t - m_n)
        return m_n, jnp.exp2(m_i - m_n) * acc + _dot(e_t, v_t)

    def pair_step(j, carry):
        sb_ref[...] = far_scores(2 * j + 1)
        carry = update(carry, sa_ref[...], far_values(2 * j))
        sa_ref[...] = far_scores(2 * j + 2)
        return update(carry, sb_ref[...], far_values(2 * j + 1))

    sa_ref[...] = far_scores(0)
    last = jnp.maximum(n_pairs, 1) - 1
    carry = (jnp.full((rows, 1), NEG_INF, F32), jnp.zeros((rows, 2 * dh), F32))
    carry = lax.fori_loop(0, last, pair_step, carry)
    sb_ref[...] = far_scores(2 * last + 1)
    carry = update(carry, sa_ref[...], far_values(2 * last))
    nk = (NSA_SUB + 2) * L_SLC
    near0 = pl.multiple_of(KV_PAD + q0 - 2 * L_SLC, L_SLC)
    s_n = _dot_nt(q_near, ks_ref[pl.ds(near0, nk), :]) + tsel_ref[...]
    carry = update(carry, sb_ref[...], far_values(2 * last + 1))
    _, acc = update(carry, s_n, vs_ref[pl.ds(near0, nk), :])
    y_ref[...] = (y_cw + gated(finish(acc), 1, zs_ref)).astype(y_ref.dtype)


def _gate_selectors(ng):
    ex = np.zeros((B_GROUPS, N_BRANCH, 2 * ng, B_HPG * B_HEAD_DIM), np.float32)
    for g in range(B_GROUPS):
        for br in range(N_BRANCH):
            for h in range(B_HPG):
                lane = br * B_HEADS + g * B_HPG + h
                ex[g, br, lane, h * B_HEAD_DIM:(h + 1) * B_HEAD_DIM] = 1.0
                ex[g, br, ng + lane, h * B_HEAD_DIM:(h + 1) * B_HEAD_DIM] = 1.0
    return ex


def _kv_templates(t, nblk):
    tp = KV_PAD + t
    ks_t = np.zeros((tp, 2 * B_HEAD_DIM), np.float32)
    ks_t[:, B_HEAD_DIM:] = _block_onehot(t, nblk)
    kw_t = np.zeros((tp, 2 * B_HEAD_DIM), np.float32)
    kw_t[:KV_PAD, B_HEAD_DIM] = 1.0
    v_t = np.zeros((tp, 2 * B_HEAD_DIM), np.float32)
    v_t[:, B_HEAD_DIM] = 1.0
    return [jnp.asarray(a, BF16) for a in (ks_t, kw_t, v_t)]


def _nsa(q, kcv, kvr, tcmp, tsel, twin, ov, z, gates):
    bn, t, _ = q.shape
    tq = NSA_Q_TILE
    gw = B_HPG * B_HEAD_DIM
    ncp = kcv.shape[2]
    tp = KV_PAD + t
    kvw = 2 * B_HEAD_DIM
    nblk = ov.shape[0]
    ng = gates.shape[2]
    rows = B_HPG * tq
    per_g = lambda b, g, i: (g, 0, 0)
    tmpl = pl.BlockSpec((tp, kvw), lambda b, g, i: (0, 0))
    ex = jnp.asarray(_gate_selectors(ng), BF16)
    kst, kwt, vt = _kv_templates(t, nblk)

    def z_spec(br):
        return pl.BlockSpec((None, tq, gw), lambda b, g, i: (b, i, br * B_GROUPS + g))

    return pl.pallas_call(
        _nsa_kernel,
        grid=(bn, B_GROUPS, t // tq),
        in_specs=[pl.BlockSpec((None, tq, gw), lambda b, g, i: (b, i, g)),
                  pl.BlockSpec((None, None, ncp, B_HEAD_DIM), lambda b, g, i: (b, g, 0, 0)),
                  pl.BlockSpec((None, None, ncp, B_HEAD_DIM), lambda b, g, i: (b, B_GROUPS + g, 0, 0)),
                  pl.BlockSpec((None, t, kvw), lambda b, g, i: (b, 0, g)),
                  pl.BlockSpec((None, t, kvw), lambda b, g, i: (b, 0, B_GROUPS + g)),
                  tmpl, tmpl, tmpl,
                  pl.BlockSpec((None, rows, tcmp.shape[2]), per_g),
                  pl.BlockSpec((None, rows, tsel.shape[2]), per_g),
                  pl.BlockSpec((None, rows, twin.shape[2]), per_g),
                  pl.BlockSpec(ov.shape, lambda b, g, i: (0, 0)),
                  z_spec(0), z_spec(1), z_spec(2),
                  pl.BlockSpec((None, tq, ng), lambda b, g, i: (b, i, 0)),
                  pl.BlockSpec((None,) + ex.shape[1:], lambda b, g, i: (g, 0, 0, 0))],
        out_specs=pl.BlockSpec((None, tq, gw), lambda b, g, i: (b, i, g)),
        out_shape=jax.ShapeDtypeStruct((bn, t, B_WIDTH), BF16),
        scratch_shapes=[pltpu.VMEM((rows, SEL_KEY_TILE), F32), pltpu.VMEM((rows, SEL_KEY_TILE), F32)]
        + [pltpu.VMEM((tp, kvw), BF16)] * 4,
        compiler_params=_cparams("arbitrary", "arbitrary", "arbitrary"),
    )(q, kcv, kcv, kvr, kvr, kst, kwt, vt, tcmp, tsel, twin, ov, z, z, z, gates, ex)


def _final_kernel(y_ref, x1_ref, gate_ref, wo_ref, fg_ref, o_ref):
    x2 = x1_ref[...] + gate_ref[...] * jnp.dot(y_ref[...], wo_ref[...], preferred_element_type=F32)
    o_ref[...] = _rms(x2, fg_ref[...])


def _final(y, x1, gate1, wo, fg):
    bn, t, d = x1.shape
    tm = POST_ROW_TILE
    row = lambda b, i: (b, i, 0)
    return pl.pallas_call(
        _final_kernel,
        grid=(bn, t // tm),
        in_specs=[pl.BlockSpec((None, tm, B_WIDTH), row),
                  pl.BlockSpec((None, tm, d), row),
                  pl.BlockSpec((None, 1, d), lambda b, i: (b, 0, 0)),
                  pl.BlockSpec(wo.shape, lambda b, i: (0, 0)),
                  pl.BlockSpec((1, d), lambda b, i: (0, 0))],
        out_specs=pl.BlockSpec((None, tm, d), row),
        out_shape=jax.ShapeDtypeStruct((bn, t, d), F32),
        compiler_params=_cparams("arbitrary", "arbitrary"),
    )(y, x1, gate1, wo, fg)


def _overlap_matrix(ncp, n_cmp, n_slc, nblk):
    cells = np.arange(n_cmp)[:, None] + np.arange(L_CMP // CMP_STRIDE)[None, :]
    ov = (cells[:, None, :] // (L_SLC // CMP_STRIDE) == np.arange(n_slc)[None, :, None]).sum(-1)
    out = np.zeros((ncp, nblk), np.float32)
    out[:n_cmp, :n_slc] = ov
    return out


def _block_onehot(t, nblk):
    oh = np.zeros((KV_PAD + t, nblk), np.float32)
    oh[KV_PAD + np.arange(t), np.arange(t) // L_SLC] = 1.0
    oh[:KV_PAD, nblk - 1] = 1.0
    return oh


def kernel(x, c, rel_bias, ada_w, ada_b, norm_g, a_in_w, a_conv_w, a_A_log, a_dt_bias, a_onorm_g, a_out_w,
           kv_norm_g, kv_w, cmp_pos_k, cmp_pos_v, cmp_k_w1, cmp_k_w2, cmp_v_w1, cmp_v_w2,
           b_in_w, b_out_w, final_g):
    bn, t, d = x.shape
    assert ada_w.shape[0] == 2 and a_in_w.shape[0] == 1 and b_in_w.shape[0] == 1
    assert t % max(ROW_TILE, POST_ROW_TILE, 2 * SEL_KEY_TILE, NSA_Q_TILE) == 0
    n_slc = t // L_SLC
    nblk = SEL_MASK_LANES
    assert n_slc <= nblk
    n_cmp = (t - L_CMP) // CMP_STRIDE + 1
    ncp = t // CMP_STRIDE

    mod = _ada_modulation(c, ada_w, ada_b)
    shift = mod[:, :, None, :d]
    scale = mod[:, :, None, d:2 * d]
    gate = mod[:, :, None, 2 * d:]

    hb = GDN_HEADS_PER_STEP
    nhg = A_HEADS // hb
    w_in = a_in_w[0]
    wqkv = w_in[:, :3 * A_WIDTH].astype(BF16)
    wz = w_in[:, 3 * A_WIDTH:4 * A_WIDTH].astype(BF16)
    wb = w_in[:, 4 * A_WIDTH:4 * A_WIDTH + A_HEADS]
    wa = w_in[:, 4 * A_WIDTH + A_HEADS:]
    half = LANES // 2
    wba = jnp.zeros((d, nhg, LANES), F32)
    wba = wba.at[:, :, :hb].set(wb.reshape(d, nhg, hb)).at[:, :, half:half + hb].set(wa.reshape(d, nhg, hb))
    wba = wba.reshape(d, nhg * LANES).astype(BF16)
    lane_rows = lambda v: jnp.zeros((nhg, 1, LANES), F32).at[:, 0, half:half + hb].set(v.reshape(nhg, hb))
    q_a, k_a, v_a, z_a, ba = _in_proj_a(x, norm_g[0:1], scale[0], shift[0], wqkv, wz, wba, a_conv_w[0])
    og = _gdn(q_a, k_a, v_a, z_a, ba, lane_rows(a_A_log[0]), lane_rows(a_dt_bias[0]), a_onorm_g[0:1])

    ndh = B_GROUPS * B_HEAD_DIM
    wkvc = kv_w[:, :2 * ndh].astype(BF16)
    kv_rest = kv_w[:, 2 * ndh:].reshape(d, 4, B_GROUPS, B_HEAD_DIM)
    wkvr = jnp.concatenate([jnp.stack([kv_rest[:, 0], kv_rest[:, 1]], axis=2),
                            jnp.stack([kv_rest[:, 2], kv_rest[:, 3]], axis=2)], axis=1)
    wkvr = wkvr.reshape(d, 4 * ndh).astype(BF16)
    w_b = b_in_w[0]
    wq = w_b[:, :B_WIDTH].astype(BF16)
    wzb = w_b[:, B_WIDTH:4 * B_WIDTH].astype(BF16)
    wg = jnp.zeros((d, LANES), F32).at[:, :N_BRANCH * B_HEADS].set(w_b[:, 4 * B_WIDTH:]).astype(BF16)
    x1, kvc, kvr, q, z_b, gates = _post_a(x, og, gate[0], a_out_w[0].astype(BF16), kv_norm_g[None, :], wkvc, wkvr,
                                          norm_g[1:2], scale[1], shift[1], wq, wzb, wg)

    pos = jnp.stack([cmp_pos_k, cmp_pos_v])
    w1 = jnp.stack([cmp_k_w1, cmp_v_w1]).astype(BF16)
    w2 = jnp.stack([cmp_k_w2, cmp_v_w2]).astype(BF16)
    kcv = _compress(kvc, pos, w1, w2)

    twin, tsel, tcmp = _bias_tables(rel_bias)
    tc_hi = tcmp.astype(BF16)
    tc_lo = (tcmp - tc_hi.astype(F32)).astype(BF16)
    tcmp2 = jnp.concatenate([tc_hi, tc_lo], axis=-1)
    ov_t = _overlap_matrix(ncp, n_cmp, n_slc, nblk).T
    ov3 = jnp.asarray(np.concatenate([ov_t] * 3, axis=1), BF16)

    y = _nsa(q, kcv, kvr, tcmp2, tsel, twin, ov3, z_b, gates)

    return _final(y, x1, gate[1], b_out_w[0].astype(BF16), final_g[None, :])
```
